```python
import jax
import jax.numpy as jnp
from jax import lax
import numpy as np

D_MODEL = 1024
BATCH = 8
SEQ = 4096
DEPTH = 2

MEM_LEN = 256
HEAD_DIM = 64
NEG_INF = -1e30
FORCE_SCORE = 1e6
LN_EPS = 1e-5
DN_ALPHA = (2.0 * DEPTH) ** 0.25
DN_BETA = (8.0 * DEPTH) ** -0.25

MIX_WIDTH = 512

NSA_HEADS = 8
NSA_KV_HEADS = 2
NSA_GROUP = NSA_HEADS // NSA_KV_HEADS
NSA_WIDTH = NSA_HEADS * HEAD_DIM
CMP_BLOCK = 32
CMP_STRIDE = 16
CMP_HIDDEN = 256
SLC_BLOCK = 64
SLC_TOPK = 16
WINDOW = 512
SLC_QCHUNK = 64
WIN_QCHUNK = 128

MOBA_HEADS = 8
MOBA_WIDTH = MOBA_HEADS * HEAD_DIM
MOBA_BLOCK = 256
MOBA_TOPK = 3
MOBA_QCHUNK = 32

CONV_CH = MIX_WIDTH
CONV_K = 3

N_BRANCH = 3
NSA_KV_COLS = 6 * NSA_KV_HEADS * HEAD_DIM
NSA_GATE_COLS = 3 * NSA_HEADS
MOBA_COLS = 3 * MOBA_WIDTH
CONV_COLS = 3 * CONV_CH
MERGE_COLS = N_BRANCH * D_MODEL
N_IN = NSA_WIDTH + NSA_KV_COLS + NSA_GATE_COLS + MOBA_COLS + CONV_COLS + MERGE_COLS

XATTN_HEADS = 4
XATTN_HEAD_DIM = 128
XATTN_WIDTH = XATTN_HEADS * XATTN_HEAD_DIM

N_EXPERTS = 256
TOP_K = 8
N_GROUPS = 8
TOPK_GROUPS = 4
EXPERT_DIM = 256
SHARED_DIM = 256
ROUTE_SCALE = 2.5
MOE_BLOCK = 128

kernel_name = 'hybrid_nsa_moba_shortconv_moe'


def layer_norm(x, g, b):
    xf = x.astype(jnp.float32)
    mu = jnp.mean(xf, axis=-1, keepdims=True)
    var = jnp.mean(jnp.square(xf - mu), axis=-1, keepdims=True)
    return ((xf - mu) * lax.rsqrt(var + LN_EPS) * g + b).astype(x.dtype)


def masked_softmax(s, mask):
    p = jax.nn.softmax(jnp.where(mask, s, NEG_INF), axis=-1)
    return jnp.where(mask, p, 0.0)


def alibi_slopes(n):
    return jnp.exp2(-8.0 * jnp.arange(1, n + 1, dtype=jnp.float32) / n)


def gather_blocks(blocks, idx):
    g = jax.vmap(jax.vmap(lambda bl, ix: bl[ix]))(blocks, idx)
    b, h, q, k, l, d = g.shape
    return g.reshape(b, h, q, k * l, d)


def compress_tokens(kv, pe, w1, w2):
    b, g, s, d = kv.shape
    nb = s // CMP_STRIDE
    sub = kv.reshape(b, g, nb, CMP_STRIDE, d)
    win = jnp.concatenate([sub[:, :, :-1], sub[:, :, 1:]], axis=3) + pe
    win = win.reshape(b, g, nb - 1, CMP_BLOCK * d)
    return jax.nn.gelu(win @ w1) @ w2


def nsa_attention(q, kv, gate_logits, cmp_pe, cmp_w1, cmp_w2):
    b, s, _, d = q.shape
    scale = d ** -0.5
    qg = q.reshape(b, s, NSA_KV_HEADS, NSA_GROUP, d).transpose(0, 2, 3, 1, 4)
    k_c, v_c, k_s, v_s, k_w, v_w = kv.transpose(2, 0, 3, 1, 4)
    slopes = alibi_slopes(NSA_HEADS).reshape(NSA_KV_HEADS, NSA_GROUP)[None, :, :, None, None]
    t = jnp.arange(s)

    kc = compress_tokens(k_c, cmp_pe[0], cmp_w1[0], cmp_w2[0])
    vc = compress_tokens(v_c, cmp_pe[1], cmp_w1[1], cmp_w2[1])
    nc = kc.shape[2]
    c_start = jnp.arange(nc) * CMP_STRIDE
    c_mask = (c_start + CMP_BLOCK - 1)[None, :] <= t[:, None]
    sc = jnp.einsum('bgrtd,bgcd->bgrtc', qg, kc).astype(jnp.float32) * scale
    p_cmp = masked_softmax(sc, c_mask)
    o_cmp = jnp.einsum('bgrtc,bgcd->bgrtd', p_cmp.astype(vc.dtype), vc)

    n_blk = s // SLC_BLOCK
    n_sel = min(SLC_TOPK, n_blk)
    b_start = jnp.arange(n_blk) * SLC_BLOCK
    overlap = ((c_start[:, None] < (b_start + SLC_BLOCK)[None, :])
               & ((c_start + CMP_BLOCK)[:, None] > b_start[None, :])).astype(jnp.float32)
    imp = jnp.einsum('bgrtc,cj->bgtj', p_cmp, overlap)
    cur = (t // SLC_BLOCK)[:, None]
    j = jnp.arange(n_blk)[None, :]
    forced = (j == 0) | (j == cur) | (j == cur - 1)
    imp = jnp.where(forced, FORCE_SCORE, jnp.where(j > cur, -FORCE_SCORE, imp))
    sel = lax.top_k(imp, n_sel)[1]
    ks_blk = k_s.reshape(b, NSA_KV_HEADS, n_blk, SLC_BLOCK, d)
    vs_blk = v_s.reshape(b, NSA_KV_HEADS, n_blk, SLC_BLOCK, d)

    def slc_chunk(ci):
        t0 = ci * SLC_QCHUNK
        qc = lax.dynamic_slice_in_dim(qg, t0, SLC_QCHUNK, axis=3)
        ic = lax.dynamic_slice_in_dim(sel, t0, SLC_QCHUNK, axis=2)
        kg = gather_blocks(ks_blk, ic)
        vg = gather_blocks(vs_blk, ic)
        pos = (ic[..., None] * SLC_BLOCK + jnp.arange(SLC_BLOCK)).reshape(b, NSA_KV_HEADS, SLC_QCHUNK, n_sel * SLC_BLOCK)
        dist = ((t0 + jnp.arange(SLC_QCHUNK))[:, None] - pos)[:, :, None]
        sl = jnp.einsum('bgrqd,bgqkd->bgrqk', qc, kg).astype(jnp.float32) * scale - slopes * dist
        p = masked_softmax(sl, dist >= 0)
        return jnp.einsum('bgrqk,bgqkd->bgrqd', p.astype(vg.dtype), vg)

    o_slc = lax.map(slc_chunk, jnp.arange(s // SLC_QCHUNK))
    o_slc = jnp.moveaxis(o_slc, 0, 3).reshape(b, NSA_KV_HEADS, NSA_GROUP, s, d)

    kw_p = jnp.pad(k_w, ((0, 0), (0, 0), (WINDOW, 0), (0, 0)))
    vw_p = jnp.pad(v_w, ((0, 0), (0, 0), (WINDOW, 0), (0, 0)))

    def win_chunk(ci):
        t0 = ci * WIN_QCHUNK
        qc = lax.dynamic_slice_in_dim(qg, t0, WIN_QCHUNK, axis=3)
        kb = lax.dynamic_slice_in_dim(kw_p, t0, WIN_QCHUNK + WINDOW, axis=2)
        vb = lax.dynamic_slice_in_dim(vw_p, t0, WIN_QCHUNK + WINDOW, axis=2)
        pos = t0 - WINDOW + jnp.arange(WIN_QCHUNK + WINDOW)
        dist = (t0 + jnp.arange(WIN_QCHUNK))[:, None] - pos[None, :]
        mask = (dist >= 0) & (dist < WINDOW) & (pos >= 0)[None, :]
        sw = jnp.einsum('bgrqd,bgkd->bgrqk', qc, kb).astype(jnp.float32) * scale - slopes * dist
        p = masked_softmax(sw, mask)
        return jnp.einsum('bgrqk,bgkd->bgrqd', p.astype(vb.dtype), vb)

    o_win = lax.map(win_chunk, jnp.arange(s // WIN_QCHUNK))
    o_win = jnp.moveaxis(o_win, 0, 3).reshape(b, NSA_KV_HEADS, NSA_GROUP, s, d)

    g = jax.nn.sigmoid(gate_logits).reshape(b, s, NSA_KV_HEADS, NSA_GROUP, 3).transpose(0, 2, 3, 1, 4)
    o = g[..., 0:1] * o_cmp + g[..., 1:2] * o_slc + g[..., 2:3] * o_win
    return o.transpose(0, 3, 1, 2, 4).reshape(b, s, NSA_WIDTH)


def moba_attention(q, k, v):
    b, s, h, d = q.shape
    scale = d ** -0.5
    q, k, v = (a.transpose(0, 2, 1, 3) for a in (q, k, v))
    n_blk = -(-s // MOBA_BLOCK)
    pad = n_blk * MOBA_BLOCK - s
    kb = jnp.pad(k, ((0, 0), (0, 0), (0, pad), (0, 0))).reshape(b, h, n_blk, MOBA_BLOCK, d)
    vb = jnp.pad(v, ((0, 0), (0, 0), (0, pad), (0, 0))).reshape(b, h, n_blk, MOBA_BLOCK, d)
    k_mean = jnp.mean(kb, axis=3)
    n_top = min(MOBA_TOPK, n_blk)
    n_s = n_top * MOBA_BLOCK
    slopes = alibi_slopes(h)[None, :, None, None]

    def chunk(ci):
        t0 = ci * MOBA_QCHUNK
        ib = t0 // MOBA_BLOCK
        qc = lax.dynamic_slice_in_dim(q, t0, MOBA_QCHUNK, axis=2)
        tq = t0 + jnp.arange(MOBA_QCHUNK)
        gate = jnp.einsum('bhqd,bhjd->bhqj', qc, k_mean).astype(jnp.float32)
        gate = jnp.where(jnp.arange(n_blk) < ib, gate, NEG_INF)
        sel = lax.top_k(gate, n_top)[1]
        valid = jnp.repeat(jnp.arange(n_top) < ib, MOBA_BLOCK)
        kg = gather_blocks(kb, sel)
        vg = gather_blocks(vb, sel)
        pos_sel = (sel[..., None] * MOBA_BLOCK + jnp.arange(MOBA_BLOCK)).reshape(b, h, MOBA_QCHUNK, n_s)
        s_sel = jnp.einsum('bhqd,bhqkd->bhqk', qc, kg).astype(jnp.float32) * scale - slopes * (tq[:, None] - pos_sel)
        k_own = lax.dynamic_index_in_dim(kb, ib, axis=2, keepdims=False)
        v_own = lax.dynamic_index_in_dim(vb, ib, axis=2, keepdims=False)
        d_own = tq[:, None] - (ib * MOBA_BLOCK + jnp.arange(MOBA_BLOCK))[None, :]
        s_own = jnp.einsum('bhqd,bhkd->bhqk', qc, k_own).astype(jnp.float32) * scale - slopes * d_own
        mask = jnp.concatenate([jnp.broadcast_to(valid, s_sel.shape), jnp.broadcast_to(d_own >= 0, s_own.shape)], axis=-1)
        p = masked_softmax(jnp.concatenate([s_sel, s_own], axis=-1), mask).astype(v.dtype)
        return (jnp.einsum('bhqk,bhqkd->bhqd', p[..., :n_s], vg)
                + jnp.einsum('bhqk,bhkd->bhqd', p[..., n_s:], v_own))

    o = lax.map(chunk, jnp.arange(s // MOBA_QCHUNK))
    o = jnp.moveaxis(o, 0, 2).reshape(b, h, s, d)
    return o.transpose(0, 2, 1, 3).reshape(b, s, h * d)


def short_conv_mixer(h, b_gate, c_gate, conv_w):
    u = c_gate * h
    y = lax.conv_general_dilated(u, conv_w[:, None, :], window_strides=(1,), padding=[(CONV_K - 1, 0)],
                                 dimension_numbers=('NWC', 'WIO', 'NWC'), feature_group_count=CONV_CH)
    return b_gate * y


def hybrid_mixer(x, w_in, cmp_pe, cmp_w1, cmp_w2, conv_w, w_branch, w_out):
    b, s, _ = x.shape
    z = x @ w_in
    sizes = [NSA_WIDTH, NSA_KV_COLS, NSA_GATE_COLS, MOBA_COLS, CONV_COLS, MERGE_COLS]
    nsa_q, nsa_kv, nsa_g, moba_qkv, conv_in, merge_g = jnp.split(z, np.cumsum(sizes)[:-1].tolist(), axis=-1)
    y_a = nsa_attention(nsa_q.reshape(b, s, NSA_HEADS, HEAD_DIM),
                        nsa_kv.reshape(b, s, 6, NSA_KV_HEADS, HEAD_DIM),
                        nsa_g.reshape(b, s, NSA_HEADS, 3), cmp_pe, cmp_w1, cmp_w2)
    qkv = moba_qkv.reshape(b, s, 3, MOBA_HEADS, HEAD_DIM)
    y_b = moba_attention(qkv[:, :, 0], qkv[:, :, 1], qkv[:, :, 2])
    h_c, b_c, c_c = jnp.split(conv_in, 3, axis=-1)
    y_c = short_conv_mixer(h_c, b_c, c_c, conv_w)
    gates = jax.nn.sigmoid(merge_g).reshape(b, s, N_BRANCH, D_MODEL)
    merged = (gates[:, :, 0] * (y_a @ w_branch[0])
              + gates[:, :, 1] * (y_b @ w_branch[1])
              + gates[:, :, 2] * (y_c @ w_branch[2]))
    return merged @ w_out


def memory_cross_attention(x, mem, wq, wkv, wo):
    b, s, _ = x.shape
    m = mem.shape[1]
    q = (x @ wq).reshape(b, s, XATTN_HEADS, XATTN_HEAD_DIM)
    kv = (mem @ wkv).reshape(b, m, 2, XATTN_HEADS, XATTN_HEAD_DIM)
    k, v = kv[:, :, 0], kv[:, :, 1]
    sc = jnp.einsum('bshd,bmhd->bhsm', q, k).astype(jnp.float32) * XATTN_HEAD_DIM ** -0.5
    p = jax.nn.softmax(sc, axis=-1).astype(v.dtype)
    o = jnp.einsum('bhsm,bmhd->bshd', p, v).reshape(b, s, XATTN_WIDTH)
    return o @ wo


def route(xf, router_w, router_b):
    t = xf.shape[0]
    s = jax.nn.sigmoid((xf @ router_w).astype(jnp.float32))
    sb = s + router_b.astype(jnp.float32)
    grp = sb.reshape(t, N_GROUPS, N_EXPERTS // N_GROUPS)
    g_score = jnp.sum(lax.top_k(grp, 2)[0], axis=-1)
    g_idx = lax.top_k(g_score, TOPK_GROUPS)[1]
    g_mask = jnp.any(g_idx[..., None] == jnp.arange(N_GROUPS), axis=-2)
    e_mask = jnp.repeat(g_mask, N_EXPERTS // N_GROUPS, axis=-1)
    idx = lax.top_k(jnp.where(e_mask, sb, NEG_INF), TOP_K)[1]
    w = jnp.take_along_axis(s, idx, axis=-1)
    w = w / jnp.sum(w, axis=-1, keepdims=True) * ROUTE_SCALE
    return idx, w.astype(xf.dtype)


def routed_experts(xf, idx, wts, exp_w13, exp_w2):
    t, dm = xf.shape
    n_a = t * TOP_K
    flat_e = idx.reshape(-1)
    flat_tok = jnp.arange(n_a) // TOP_K
    flat_w = wts.reshape(-1)
    order = jnp.argsort(flat_e)
    se, stok, sw = flat_e[order], flat_tok[order], flat_w[order]
    counts = jnp.bincount(flat_e, length=N_EXPERTS)
    pcounts = (counts + MOE_BLOCK - 1) // MOE_BLOCK * MOE_BLOCK
    starts = jnp.cumsum(counts) - counts
    pends = jnp.cumsum(pcounts)
    pstarts = pends - pcounts
    dest = pstarts[se] + jnp.arange(n_a) - starts[se]
    n_blocks = -(-n_a // MOE_BLOCK) + N_EXPERTS
    n_pad = n_blocks * MOE_BLOCK
    row_tok = jnp.full((n_pad,), t, jnp.int32).at[dest].set(stok.astype(jnp.int32))
    row_w = jnp.zeros((n_pad,), xf.dtype).at[dest].set(sw)
    block_e = jnp.minimum(jnp.searchsorted(pends, jnp.arange(n_blocks) * MOE_BLOCK, side='right'), N_EXPERTS - 1)
    x_pad = jnp.concatenate([xf, jnp.zeros((1, dm), xf.dtype)], axis=0)

    def run_block(args):
        tok, w, e = args
        xb = x_pad[tok]
        a, g = jnp.split(xb @ exp_w13[e], 2, axis=-1)
        return ((jax.nn.silu(a) * g) @ exp_w2[e]) * w[:, None]

    out = lax.map(run_block, (row_tok.reshape(n_blocks, MOE_BLOCK), row_w.reshape(n_blocks, MOE_BLOCK), block_e))
    y = jax.ops.segment_sum(out.reshape(n_pad, dm), row_tok, num_segments=t + 1)
    return y[:t]


def moe_ffn(x, router_w, router_b, exp_w13, exp_w2, shared_w13, shared_w2):
    b, s, dm = x.shape
    xf = x.reshape(b * s, dm)
    idx, wts = route(xf, router_w, router_b)
    routed = routed_experts(xf, idx, wts, exp_w13, exp_w2)
    a, g = jnp.split(xf @ shared_w13, 2, axis=-1)
    shared = (jax.nn.silu(a) * g) @ shared_w2
    return (routed + shared).reshape(b, s, dm)


def setup_inputs(seed: int = 0) -> dict:
    key = jax.random.key(seed)
    ks = jax.random.split(key, 32)
    L, D = DEPTH, D_MODEL

    def nrm(k, shape, scale):
        return jax.random.normal(k, shape, jnp.float32) * scale

    return {
        'x': nrm(ks[0], (BATCH, SEQ, D), 1.0),
        'mem': nrm(ks[1], (BATCH, MEM_LEN, D), 1.0),
        'w_in': nrm(ks[2], (L, D, N_IN), D ** -0.5),
        'cmp_pe': nrm(ks[3], (L, 2, CMP_BLOCK, HEAD_DIM), 0.02),
        'cmp_w1': nrm(ks[4], (L, 2, CMP_BLOCK * HEAD_DIM, CMP_HIDDEN), (CMP_BLOCK * HEAD_DIM) ** -0.5),
        'cmp_w2': nrm(ks[5], (L, 2, CMP_HIDDEN, HEAD_DIM), CMP_HIDDEN ** -0.5),
        'conv_w': nrm(ks[6], (L, CONV_K, CONV_CH), CONV_K ** -0.5),
        'w_branch': nrm(ks[7], (L, N_BRANCH, MIX_WIDTH, D), MIX_WIDTH ** -0.5 * DN_BETA),
        'w_out': nrm(ks[8], (L, D, D), D ** -0.5 * DN_BETA),
        'ln1_g': 1.0 + nrm(ks[9], (L, D), 0.02),
        'ln1_b': nrm(ks[10], (L, D), 0.02),
        'xattn_wq': nrm(ks[11], (L, D, XATTN_WIDTH), D ** -0.5),
        'xattn_wkv': jnp.concatenate([nrm(ks[12], (L, D, XATTN_WIDTH), D ** -0.5),
                                      nrm(ks[13], (L, D, XATTN_WIDTH), D ** -0.5 * DN_BETA)], axis=-1),
        'xattn_wo': nrm(ks[14], (L, XATTN_WIDTH, D), XATTN_WIDTH ** -0.5 * DN_BETA),
        'ln2_g': 1.0 + nrm(ks[15], (L, D), 0.02),
        'ln2_b': nrm(ks[16], (L, D), 0.02),
        'router_w': nrm(ks[17], (L, D, N_EXPERTS), D ** -0.5),
        'router_b': nrm(ks[18], (L, N_EXPERTS), 0.01),
        'exp_w13': nrm(ks[19], (L, N_EXPERTS, D, 2 * EXPERT_DIM), D ** -0.5),
        'exp_w2': nrm(ks[20], (L, N_EXPERTS, EXPERT_DIM, D), EXPERT_DIM ** -0.5 * DN_BETA),
        'shared_w13': nrm(ks[21], (L, D, 2 * SHARED_DIM), D ** -0.5),
        'shared_w2': nrm(ks[22], (L, SHARED_DIM, D), SHARED_DIM ** -0.5 * DN_BETA),
        'ln3_g': 1.0 + nrm(ks[23], (L, D), 0.02),
        'ln3_b': nrm(ks[24], (L, D), 0.02),
    }


def reference(x, mem, w_in, cmp_pe, cmp_w1, cmp_w2, conv_w, w_branch, w_out, ln1_g, ln1_b,
              xattn_wq, xattn_wkv, xattn_wo, ln2_g, ln2_b, router_w, router_b, exp_w13, exp_w2,
              shared_w13, shared_w2, ln3_g, ln3_b):
    h = x
    for l in range(DEPTH):
        h = layer_norm(DN_ALPHA * h + hybrid_mixer(h, w_in[l], cmp_pe[l], cmp_w1[l], cmp_w2[l], conv_w[l], w_branch[l], w_out[l]),
                       ln1_g[l], ln1_b[l])
        h = layer_norm(DN_ALPHA * h + memory_cross_attention(h, mem, xattn_wq[l], xattn_wkv[l], xattn_wo[l]),
                       ln2_g[l], ln2_b[l])
        h = layer_norm(DN_ALPHA * h + moe_ffn(h, router_w[l], router_b[l], exp_w13[l], exp_w2[l], shared_w13[l], shared_w2[l]),
                       ln3_g[l], ln3_b[l])
    return h
```

```python
import functools

import jax
import jax.numpy as jnp
import numpy as np
from jax import lax
from jax.experimental import pallas as pl
from jax.experimental.pallas import tpu as pltpu

F32 = jnp.float32
BF16 = jnp.bfloat16
I32 = jnp.int32

D_MODEL = 1024
HEAD_DIM = 64
SLOT = 128
NEG = -1e30
LN_EPS = 1e-5

NSA_HEADS = 8
NSA_KV_HEADS = 2
NSA_GROUP = 4
CMP_BLOCK = 32
CMP_STRIDE = 16
CMP_HIDDEN = 256
SLC_BLOCK = 64
SLC_TOPK = 16
WINDOW = 512
MOBA_HEADS = 8
MOBA_BLOCK = 256
MOBA_TOPK = 3
CONV_CH = 512
XATTN_HEADS = 4
XATTN_HEAD_DIM = 128
N_EXPERTS = 256
TOP_K = 8
N_GROUPS = 8
TOPK_GROUPS = 4
EXPERT_DIM = 256
ROUTE_SCALE = 2.5

TQ = 256
EXP_ROWS = 256
VMEM_LIMIT = 48 * 1024 * 1024

_OFF_NSA_Q = 0
_OFF_NSA_KV = 512
_OFF_NSA_G = 1280
_OFF_MOBA = 1304
_OFF_CONV = 2840
_OFF_MERGE = 4376


def _cparams(sem):
    return pltpu.CompilerParams(dimension_semantics=sem, vmem_limit_bytes=VMEM_LIMIT)


def _dot(a, b):
    return jnp.dot(a, b, preferred_element_type=F32)


def _dot_t(a, b):
    return lax.dot_general(a, b, (((1,), (1,)), ((), ())), preferred_element_type=F32)


def _layer_norm(x, g, b):
    mu = jnp.mean(x, axis=-1, keepdims=True)
    xc = x - mu
    var = jnp.mean(xc * xc, axis=-1, keepdims=True)
    return xc * lax.rsqrt(var + LN_EPS) * g + b


def _mm_kernel(x_ref, w_ref, o_ref, *, act):
    y = _dot(x_ref[...].astype(BF16), w_ref[...])
    if act == "sigmoid":
        y = jax.nn.sigmoid(y)
    o_ref[...] = y.astype(o_ref.dtype)


def _mm_feat_kernel(x_ref, w_ref, f_ref, o_ref):
    y = _dot(x_ref[...].astype(BF16), w_ref[...]) + f_ref[...]
    o_ref[...] = y.astype(o_ref.dtype)


def _mm(x, w, out_dtype, *, tm=1024, tn=512, act=None, feats=None, name):
    m, k = x.shape
    n = w.shape[1]
    tm = min(tm, m)
    tn = min(tn, n)
    assert m % tm == 0 and n % tn == 0, (m, n, tm, tn)
    in_specs = [pl.BlockSpec((tm, k), lambda i, j: (i, 0)),
                pl.BlockSpec((k, tn), lambda i, j: (0, j))]
    args = [x, w]
    if feats is None:
        body = functools.partial(_mm_kernel, act=act)
    else:
        per = feats.shape[0] // tm
        assert feats.shape[0] % tm == 0
        in_specs.append(pl.BlockSpec((tm, tn), lambda i, j: (i % per, j)))
        args.append(feats)
        body = _mm_feat_kernel
    return pl.pallas_call(
        body,
        out_shape=jax.ShapeDtypeStruct((m, n), out_dtype),
        grid=(m // tm, n // tn),
        in_specs=in_specs,
        out_specs=pl.BlockSpec((tm, tn), lambda i, j: (i, j)),
        compiler_params=_cparams(("parallel", "parallel")),
        name=name,
    )(*args)


def _cmp_kernel(sub_ref, pet_ref, peb_ref, wt_ref, wb_ref, w2_ref, o_ref):
    sub = sub_ref[0]
    nc = sub.shape[0]
    a = _dot((sub + pet_ref[...]).astype(BF16), wt_ref[...])
    b = _dot((sub + peb_ref[...]).astype(BF16), wb_ref[...])
    hid = jax.nn.gelu(a + pltpu.roll(b, nc - 1, 0))
    o_ref[0] = _dot(hid.astype(BF16), w2_ref[...]).astype(o_ref.dtype)


def _compress(sub, pet, peb, wt, wb, w2):
    bsz, nc, kk = sub.shape
    n_h = wt.shape[1]
    n_o = w2.shape[1]
    return pl.pallas_call(
        _cmp_kernel,
        out_shape=jax.ShapeDtypeStruct((bsz, nc, n_o), BF16),
        grid=(bsz,),
        in_specs=[pl.BlockSpec((1, nc, kk), lambda b: (b, 0, 0)),
                  pl.BlockSpec((1, kk), lambda b: (0, 0)),
                  pl.BlockSpec((1, kk), lambda b: (0, 0)),
                  pl.BlockSpec((kk, n_h), lambda b: (0, 0)),
                  pl.BlockSpec((kk, n_h), lambda b: (0, 0)),
                  pl.BlockSpec((n_h, n_o), lambda b: (0, 0))],
        out_specs=pl.BlockSpec((1, nc, n_o), lambda b: (b, 0, 0)),
        compiler_params=_cparams(("parallel",)),
        name="nsa_compress",
    )(sub, pet, peb, wt, wb, w2)


def _flash_reset(m_scr, l_scr, acc_scr):
    m_scr[...] = jnp.full(m_scr.shape, NEG, F32)
    l_scr[...] = jnp.zeros(l_scr.shape, F32)
    acc_scr[...] = jnp.zeros(acc_scr.shape, F32)


def _flash_update(s, v, m_scr, l_scr, acc_scr):
    m_prev = m_scr[...]
    m_new = jnp.maximum(m_prev, jnp.max(s, axis=1, keepdims=True))
    alpha = jnp.exp(m_prev - m_new)
    p = jnp.exp(s - m_new)
    l_scr[...] = alpha * l_scr[...] + jnp.sum(p, axis=1, keepdims=True)
    acc_scr[...] = alpha * acc_scr[...] + _dot(p.astype(BF16), v)
    m_scr[...] = m_new


def _tile(ref, j, tq):
    return ref[pl.ds(pl.multiple_of(j * tq, tq), tq), :]


def _nsa_kernel(q_ref, kc_ref, vc_ref, ks_ref, vs_ref, kw_ref, vw_ref, g_ref, ovt_ref, fq_ref,
                o_ref, m_scr, l_scr, acc_scr, imp_scr, *, tq, nsel):
    i = pl.program_id(2)
    r4 = NSA_GROUP
    rows = r4 * tq
    q4 = [q_ref[:, r * SLOT:(r + 1) * SLOT] for r in range(r4)]
    fq = fq_ref[0]
    tl = lax.broadcasted_iota(I32, (tq, tq), 0)
    pl_ = lax.broadcasted_iota(I32, (tq, tq), 1)
    causal = pl_ <= tl

    kc = kc_ref[0]
    vc = vc_ref[0]
    nc = kc.shape[0]
    qs = jnp.concatenate(q4, axis=0)
    s = _dot_t(qs, kc).reshape(r4, tq, nc)
    t_glob = i * tq + lax.broadcasted_iota(I32, (tq, nc), 0)
    c_idx = lax.broadcasted_iota(I32, (tq, nc), 1)
    cmask = (c_idx * CMP_STRIDE + (CMP_BLOCK - 1)) <= t_glob
    s = jnp.where(cmask[None], s, NEG)
    mx = jnp.max(s, axis=-1, keepdims=True)
    e = jnp.where(cmask[None], jnp.exp(s - mx), 0.0)
    lsum = jnp.sum(e, axis=-1, keepdims=True)
    p_cmp = (e / jnp.where(lsum > 0.0, lsum, 1.0)).astype(BF16)
    o_cmp = _dot(p_cmp.reshape(rows, nc), vc)

    ovt = ovt_ref[...]
    imp = _dot_t(ovt, p_cmp[0])
    for r in range(1, r4):
        imp = imp + _dot_t(ovt, p_cmp[r])
    nbp = imp.shape[0]
    jidx = lax.broadcasted_iota(I32, (nbp, tq), 0)
    cur = (i * tq + lax.broadcasted_iota(I32, (nbp, tq), 1)) // SLC_BLOCK
    forced = (jidx == 0) | (jidx == cur) | (jidx == cur - 1)
    imp = jnp.where(forced, 1e6, jnp.where(jidx > cur, -1e6, imp))
    imp_scr[...] = imp

    def rank_body(jp, rank):
        row = imp_scr[pl.ds(jp, 1), :]
        better = (row > imp) | ((row == imp) & (jp < jidx))
        return rank + better.astype(I32)

    n_live = (i + 1) * (tq // SLC_BLOCK)
    rank = lax.fori_loop(0, n_live, rank_body, jnp.zeros((nbp, tq), I32))
    sel = (rank < nsel) & (jidx <= cur)
    mask_t = jnp.where(sel, 0.0, NEG)
    feat = jnp.concatenate([jnp.zeros((SLOT - nbp, tq), F32), mask_t], axis=0).T

    qx = jnp.concatenate([q4[r] + (feat + fq[r:r + 1, :]).astype(BF16) for r in range(r4)], axis=0)
    _flash_reset(m_scr, l_scr, acc_scr)

    def slc_body(j, carry):
        _flash_update(_dot_t(qx, _tile(ks_ref, j, tq)), _tile(vs_ref, j, tq), m_scr, l_scr, acc_scr)
        return carry

    lax.fori_loop(0, i, slc_body, 0)
    s = _dot_t(qx, _tile(ks_ref, i, tq)).reshape(r4, tq, tq)
    s = jnp.where(causal[None], s, NEG).reshape(rows, tq)
    _flash_update(s, _tile(vs_ref, i, tq), m_scr, l_scr, acc_scr)
    o_slc = acc_scr[...] / l_scr[...]

    qx = jnp.concatenate([q4[r] + fq[r4 + r:r4 + r + 1, :].astype(BF16) for r in range(r4)], axis=0)
    _flash_reset(m_scr, l_scr, acc_scr)

    @pl.when(i >= 2)
    def _():
        s2 = _dot_t(qx, _tile(kw_ref, i - 2, tq)).reshape(r4, tq, tq)
        s2 = jnp.where((pl_ > tl)[None], s2, NEG).reshape(rows, tq)
        _flash_update(s2, _tile(vw_ref, i - 2, tq), m_scr, l_scr, acc_scr)

    @pl.when(i >= 1)
    def _():
        _flash_update(_dot_t(qx, _tile(kw_ref, i - 1, tq)), _tile(vw_ref, i - 1, tq), m_scr, l_scr, acc_scr)

    s = _dot_t(qx, _tile(kw_ref, i, tq)).reshape(r4, tq, tq)
    s = jnp.where(causal[None], s, NEG).reshape(rows, tq)
    _flash_update(s, _tile(vw_ref, i, tq), m_scr, l_scr, acc_scr)
    o_win = acc_scr[...] / l_scr[...]

    gate = jax.nn.sigmoid(g_ref[...])
    for r in range(r4):
        sl = slice(r * tq, (r + 1) * tq)
        o = (gate[:, 3 * r:3 * r + 1] * o_cmp[sl] + gate[:, 3 * r + 1:3 * r + 2] * o_slc[sl]
             + gate[:, 3 * r + 2:3 * r + 3] * o_win[sl])
        o_ref[:, r * SLOT:(r + 1) * SLOT] = o.astype(o_ref.dtype)


def _nsa_attention(za, zb, zc, kcv, ovt, fq, bsz, seq):
    tq = TQ
    ni = seq // tq
    nc = kcv.shape[1]
    nsel = min(SLC_TOPK, seq // SLC_BLOCK)
    assert seq // SLC_BLOCK <= 64 and WINDOW == 2 * tq
    rows = NSA_GROUP * tq
    gw = NSA_GROUP * SLOT
    in_specs = [
        pl.BlockSpec((tq, gw), lambda b, g, i: (b * ni + i, g)),
        pl.BlockSpec((1, nc, SLOT), lambda b, g, i: (b, 0, g)),
        pl.BlockSpec((1, nc, SLOT), lambda b, g, i: (b, 0, 2 + g)),
        pl.BlockSpec((seq, SLOT), lambda b, g, i: (b, g)),
        pl.BlockSpec((seq, SLOT), lambda b, g, i: (b, 8 + g)),
        pl.BlockSpec((seq, SLOT), lambda b, g, i: (b, 2 + g)),
        pl.BlockSpec((seq, SLOT), lambda b, g, i: (b, 10 + g)),
        pl.BlockSpec((tq, SLOT), lambda b, g, i: (b * ni + i, 2 + g)),
        pl.BlockSpec(ovt.shape, lambda b, g, i: (0, 0)),
        pl.BlockSpec((1, 8, SLOT), lambda b, g, i: (g, 0, 0)),
    ]
    return pl.pallas_call(
        functools.partial(_nsa_kernel, tq=tq, nsel=nsel),
        out_shape=jax.ShapeDtypeStruct((bsz * seq, NSA_HEADS * SLOT), BF16),
        grid=(bsz, NSA_KV_HEADS, ni),
        in_specs=in_specs,
        out_specs=pl.BlockSpec((tq, gw), lambda b, g, i: (b * ni + i, g)),
        scratch_shapes=[pltpu.VMEM((rows, 1), F32), pltpu.VMEM((rows, 1), F32),
                        pltpu.VMEM((rows, SLOT), F32), pltpu.VMEM((64, tq), F32)],
        compiler_params=_cparams(("parallel", "parallel", "arbitrary")),
        name="nsa_attention",
    )(za, kcv, kcv, zb, za, zb, za, zc, ovt, fq)


def _moba_kernel(q_ref, k_ref, v_ref, fq_ref, o_ref, m_scr, l_scr, acc_scr, km_scr, gate_scr,
                 *, tq, nblk, ntop):
    i = pl.program_id(2)

    @pl.when(i == 0)
    def _():
        km = jnp.mean(k_ref[...].astype(F32).reshape(nblk, tq, SLOT), axis=1)
        if nblk < 16:
            km = jnp.concatenate([km, jnp.zeros((16 - nblk, SLOT), F32)], axis=0)
        km_scr[...] = km

    q = q_ref[...]
    gate = _dot_t(km_scr[...].astype(BF16), q)
    gate_scr[...] = gate
    jidx = lax.broadcasted_iota(I32, (16, tq), 0)

    def rank_body(jp, rank):
        row = gate_scr[pl.ds(jp, 1), :]
        better = (row > gate) | ((row == gate) & (jp < jidx))
        return rank + better.astype(I32)

    rank = lax.fori_loop(0, i, rank_body, jnp.zeros((16, tq), I32))
    keep = ((jidx < i) & (rank < ntop)) | (jidx == i)
    mask_t = jnp.where(keep, 0.0, NEG)
    feat = jnp.concatenate([jnp.zeros((HEAD_DIM, tq), F32), mask_t,
                            jnp.zeros((SLOT - HEAD_DIM - 16, tq), F32)], axis=0).T
    qx = q + (feat + fq_ref[0][0:1, :]).astype(BF16)
    _flash_reset(m_scr, l_scr, acc_scr)

    def body(j, carry):
        _flash_update(_dot_t(qx, _tile(k_ref, j, tq)), _tile(v_ref, j, tq), m_scr, l_scr, acc_scr)
        return carry

    lax.fori_loop(0, i, body, 0)
    tl = lax.broadcasted_iota(I32, (tq, tq), 0)
    pl_ = lax.broadcasted_iota(I32, (tq, tq), 1)
    s = jnp.where(pl_ <= tl, _dot_t(qx, _tile(k_ref, i, tq)), NEG)
    _flash_update(s, _tile(v_ref, i, tq), m_scr, l_scr, acc_scr)
    o_ref[...] = (acc_scr[...] / l_scr[...]).astype(o_ref.dtype)


def _moba_attention(za, zb, fq, bsz, seq):
    tq = TQ
    assert tq == MOBA_BLOCK and seq % tq == 0
    ni = seq // tq
    assert ni <= 16
    ntop = min(MOBA_TOPK, ni)
    return pl.pallas_call(
        functools.partial(_moba_kernel, tq=tq, nblk=ni, ntop=ntop),
        out_shape=jax.ShapeDtypeStruct((bsz * seq, MOBA_HEADS * SLOT), BF16),
        grid=(bsz, MOBA_HEADS, ni),
        in_specs=[pl.BlockSpec((tq, SLOT), lambda b, h, i: (b * ni + i, 12 + h)),
                  pl.BlockSpec((seq, SLOT), lambda b, h, i: (b, 4 + h)),
                  pl.BlockSpec((seq, SLOT), lambda b, h, i: (b, 20 + h)),
                  pl.BlockSpec((1, 8, SLOT), lambda b, h, i: (h, 0, 0))],
        out_specs=pl.BlockSpec((tq, SLOT), lambda b, h, i: (b * ni + i, h)),
        scratch_shapes=[pltpu.VMEM((tq, 1), F32), pltpu.VMEM((tq, 1), F32), pltpu.VMEM((tq, SLOT), F32),
                        pltpu.VMEM((16, SLOT), F32), pltpu.VMEM((16, tq), F32)],
        compiler_params=_cparams(("parallel", "parallel", "arbitrary")),
        name="moba_attention",
    )(za, zb, za, fq)


def _merge_kernel(ya_ref, yb_ref, hc_ref, bc_ref, cc_ref, hcp_ref, ccp_ref, g0_ref, g1_ref, g2_ref,
                  h_ref, cw_ref, wa_ref, wb_ref, wc_ref, wo_ref, lg_ref, lb_ref, o_ref, *, tm, per, alpha):
    i = pl.program_id(0)
    u = cc_ref[...] * hc_ref[...]
    first = (i % per) == 0
    up = jnp.where(first, 0.0, ccp_ref[...] * hcp_ref[...])
    rowi = lax.broadcasted_iota(I32, u.shape, 0)
    u1 = jnp.where(rowi == 0, up[7:8, :], pltpu.roll(u, 1, 0))
    u2 = jnp.where(rowi == 0, up[6:7, :], jnp.where(rowi == 1, up[7:8, :], pltpu.roll(u, 2, 0)))
    cw = cw_ref[...]
    yc = bc_ref[...] * (cw[0:1, :] * u2 + cw[1:2, :] * u1 + cw[2:3, :] * u)
    merged = (jax.nn.sigmoid(g0_ref[...]) * _dot(ya_ref[...], wa_ref[...])
              + jax.nn.sigmoid(g1_ref[...]) * _dot(yb_ref[...], wb_ref[...])
              + jax.nn.sigmoid(g2_ref[...]) * _dot(yc.astype(BF16), wc_ref[...]))
    mix = _dot(merged.astype(BF16), wo_ref[...])
    o_ref[...] = _layer_norm(alpha * h_ref[...] + mix, lg_ref[...], lb_ref[...])


def _merge(ya, yb, zc, h, cw, wa, wb, wc, wo, lg, lb, seq, alpha):
    tm = 512
    t = h.shape[0]
    per = seq // tm
    d = D_MODEL
    row = lambda i: (i, 0)
    const = lambda i: (0, 0)
    prev = lambda c: (lambda i: (jnp.maximum(i * (tm // 8) - 1, 0), c))
    in_specs = [
        pl.BlockSpec((tm, d), row), pl.BlockSpec((tm, d), row),
        pl.BlockSpec((tm, CONV_CH), lambda i: (i, 1)), pl.BlockSpec((tm, CONV_CH), lambda i: (i, 2)),
        pl.BlockSpec((tm, CONV_CH), lambda i: (i, 3)),
        pl.BlockSpec((8, CONV_CH), prev(1)), pl.BlockSpec((8, CONV_CH), prev(3)),
        pl.BlockSpec((tm, d), lambda i: (i, 2)), pl.BlockSpec((tm, d), lambda i: (i, 3)),
        pl.BlockSpec((tm, d), lambda i: (i, 4)),
        pl.BlockSpec((tm, d), row),
        pl.BlockSpec((8, CONV_CH), const),
        pl.BlockSpec((d, d), const), pl.BlockSpec((d, d), const), pl.BlockSpec((CONV_CH, d), const),
        pl.BlockSpec((d, d), const), pl.BlockSpec((1, d), const), pl.BlockSpec((1, d), const),
    ]
    return pl.pallas_call(
        functools.partial(_merge_kernel, tm=tm, per=per, alpha=alpha),
        out_shape=jax.ShapeDtypeStruct((t, d), F32),
        grid=(t // tm,),
        in_specs=in_specs,
        out_specs=pl.BlockSpec((tm, d), row),
        compiler_params=_cparams(("parallel",)),
        name="mixer_merge",
    )(ya, yb, zc, zc, zc, zc, zc, zc, zc, zc, h, cw, wa, wb, wc, wo, lg, lb)


def _xattn_kernel(h_ref, kv_ref, wq_ref, wo_ref, lg_ref, lb_ref, o_ref, ob_ref, *, alpha):
    h = h_ref[...]
    q = _dot(h.astype(BF16), wq_ref[...]).astype(BF16)
    kv = kv_ref[...]
    nh = XATTN_HEADS
    hd = XATTN_HEAD_DIM
    outs = []
    for hh in range(nh):
        s = _dot_t(q[:, hh * hd:(hh + 1) * hd], kv[:, hh * hd:(hh + 1) * hd]) * (hd ** -0.5)
        s = s - jnp.max(s, axis=-1, keepdims=True)
        e = jnp.exp(s)
        p = e / jnp.sum(e, axis=-1, keepdims=True)
        outs.append(_dot(p.astype(BF16), kv[:, (nh + hh) * hd:(nh + hh + 1) * hd]))
    o = jnp.concatenate(outs, axis=-1).astype(BF16)
    y = _layer_norm(alpha * h + _dot(o, wo_ref[...]), lg_ref[...], lb_ref[...])
    o_ref[...] = y
    ob_ref[...] = y.astype(BF16)


def _xattn(h, kv, wq, wo, lg, lb, seq, mlen, alpha):
    tm = 512
    t = h.shape[0]
    per = seq // tm
    d = D_MODEL
    row = lambda i: (i, 0)
    const = lambda i: (0, 0)
    return pl.pallas_call(
        functools.partial(_xattn_kernel, alpha=alpha),
        out_shape=(jax.ShapeDtypeStruct((t, d), F32), jax.ShapeDtypeStruct((t, d), BF16)),
        grid=(t // tm,),
        in_specs=[pl.BlockSpec((tm, d), row), pl.BlockSpec((mlen, kv.shape[1]), lambda i: (i // per, 0)),
                  pl.BlockSpec(wq.shape, const), pl.BlockSpec(wo.shape, const),
                  pl.BlockSpec((1, d), const), pl.BlockSpec((1, d), const)],
        out_specs=(pl.BlockSpec((tm, d), row), pl.BlockSpec((tm, d), row)),
        compiler_params=_cparams(("parallel",)),
        name="mem_xattn",
    )(h, kv, wq, wo, lg, lb)


def _expert_kernel(be_ref, x_ref, w13_ref, w2_ref, rw_ref, o_ref):
    del be_ref
    hmid = _dot(x_ref[...], w13_ref[0].astype(BF16))
    a = hmid[:, :EXPERT_DIM]
    g = hmid[:, EXPERT_DIM:]
    act = (a * jax.nn.sigmoid(a) * g).astype(BF16)
    y = _dot(act, w2_ref[0].astype(BF16)) * rw_ref[...]
    o_ref[...] = y.astype(o_ref.dtype)


def _experts(xg, w13, w2, row_w, block_e):
    n_pad, d = xg.shape
    rows = EXP_ROWS
    nb = n_pad // rows
    grid_spec = pltpu.PrefetchScalarGridSpec(
        num_scalar_prefetch=1,
        grid=(nb,),
        in_specs=[pl.BlockSpec((rows, d), lambda i, be: (i, 0)),
                  pl.BlockSpec((1, d, 2 * EXPERT_DIM), lambda i, be: (be[i], 0, 0)),
                  pl.BlockSpec((1, EXPERT_DIM, d), lambda i, be: (be[i], 0, 0)),
                  pl.BlockSpec((rows, 1), lambda i, be: (i, 0))],
        out_specs=pl.BlockSpec((rows, d), lambda i, be: (i, 0)),
    )
    return pl.pallas_call(
        _expert_kernel,
        out_shape=jax.ShapeDtypeStruct((n_pad, d), BF16),
        grid_spec=grid_spec,
        compiler_params=_cparams(("arbitrary",)),
        name="moe_experts",
    )(block_e, xg, w13, w2, row_w)


def _moe_out_kernel(h_ref, r_ref, w13_ref, w2_ref, lg_ref, lb_ref, o_ref, *, alpha):
    h = h_ref[...]
    hmid = _dot(h.astype(BF16), w13_ref[...])
    a = hmid[:, :EXPERT_DIM]
    g = hmid[:, EXPERT_DIM:]
    shared = _dot((a * jax.nn.sigmoid(a) * g).astype(BF16), w2_ref[...])
    o_ref[...] = _layer_norm(alpha * h + r_ref[...] + shared, lg_ref[...], lb_ref[...])


def _moe_out(h, routed, w13, w2, lg, lb, alpha):
    tm = 512
    t, d = h.shape
    row = lambda i: (i, 0)
    const = lambda i: (0, 0)
    return pl.pallas_call(
        functools.partial(_moe_out_kernel, alpha=alpha),
        out_shape=jax.ShapeDtypeStruct((t, d), F32),
        grid=(t // tm,),
        in_specs=[pl.BlockSpec((tm, d), row), pl.BlockSpec((tm, d), row),
                  pl.BlockSpec(w13.shape, const), pl.BlockSpec(w2.shape, const),
                  pl.BlockSpec((1, d), const), pl.BlockSpec((1, d), const)],
        out_specs=pl.BlockSpec((tm, d), row),
        compiler_params=_cparams(("parallel",)),
        name="moe_shared_ln",
    )(h, routed, w13, w2, lg, lb)


def _route(s, router_b):
    t = s.shape[0]
    sb = s + router_b.astype(F32)
    grp = sb.reshape(t, N_GROUPS, N_EXPERTS // N_GROUPS)
    g_score = jnp.sum(lax.top_k(grp, 2)[0], axis=-1)
    g_idx = lax.top_k(g_score, TOPK_GROUPS)[1]
    g_mask = jnp.any(g_idx[..., None] == jnp.arange(N_GROUPS), axis=-2)
    e_mask = jnp.repeat(g_mask, N_EXPERTS // N_GROUPS, axis=-1)
    idx = lax.top_k(jnp.where(e_mask, sb, NEG), TOP_K)[1]
    w = jnp.take_along_axis(s, idx, axis=-1)
    w = w / jnp.sum(w, axis=-1, keepdims=True) * ROUTE_SCALE
    return idx, w


def _moe(h, hb, router_w, router_b, exp_w13, exp_w2, shared_w13, shared_w2, lg, lb, alpha):
    t, d = h.shape
    s = _mm(h, router_w, F32, tn=N_EXPERTS, act="sigmoid", name="moe_router")
    idx, wts = _route(s, router_b)
    rows = EXP_ROWS
    n_a = t * TOP_K
    flat_e = idx.reshape(-1)
    flat_w = wts.reshape(-1)
    order = jnp.argsort(flat_e)
    se = flat_e[order]
    counts = jnp.bincount(flat_e, length=N_EXPERTS)
    pcounts = (counts + rows - 1) // rows * rows
    starts = jnp.cumsum(counts) - counts
    pends = jnp.cumsum(pcounts)
    pstarts = pends - pcounts
    dest = (pstarts[se] + jnp.arange(n_a) - starts[se]).astype(I32)
    n_blocks = -(-n_a // rows) + N_EXPERTS
    n_pad = n_blocks * rows
    row_tok = jnp.full((n_pad,), t, I32).at[dest].set((order // TOP_K).astype(I32))
    row_w = jnp.zeros((n_pad,), F32).at[dest].set(flat_w[order])
    block_e = jnp.minimum(jnp.searchsorted(pends, jnp.arange(n_blocks) * rows, side='right'),
                          N_EXPERTS - 1).astype(I32)
    pos = jnp.zeros((n_a,), I32).at[order].set(dest)
    x_pad = jnp.concatenate([hb, jnp.zeros((1, d), hb.dtype)], axis=0)
    xg = jnp.take(x_pad, row_tok, axis=0)
    out = _experts(xg, exp_w13, exp_w2, row_w.reshape(n_pad, 1), block_e)
    routed = jnp.sum(jnp.take(out, pos, axis=0).reshape(t, TOP_K, d).astype(F32), axis=1)
    return _moe_out(h, routed, shared_w13, shared_w2, lg, lb, alpha)


def _pad_slots(w, scale=1.0):
    dm = w.shape[0]
    nh = w.shape[1] // HEAD_DIM
    w = (w * scale).reshape(dm, nh, HEAD_DIM)
    return jnp.concatenate([w, jnp.zeros_like(w)], axis=-1).reshape(dm, nh * SLOT)


def _pad_rows(w):
    n = w.shape[1]
    nh = w.shape[0] // HEAD_DIM
    w = w.reshape(nh, HEAD_DIM, n)
    return jnp.concatenate([w, jnp.zeros_like(w)], axis=1).reshape(nh * SLOT, n)


def _alibi(n):
    return np.exp2(-8.0 * np.arange(1, n + 1, dtype=np.float64) / n).astype(np.float32)


def _key_features(seq):
    p = np.arange(seq)
    slc = np.zeros((seq, SLOT), np.float32)
    slc[:, HEAD_DIM] = p % SLC_BLOCK
    blk = p // SLC_BLOCK
    nz = blk > 0
    slc[p[nz], HEAD_DIM + blk[nz]] = 1.0
    win = np.zeros((seq, SLOT), np.float32)
    win[:, HEAD_DIM] = p // 64
    win[:, HEAD_DIM + 1] = p % 64
    mob = np.zeros((seq, SLOT), np.float32)
    mob[p, HEAD_DIM + p // MOBA_BLOCK] = 1.0
    mob[:, HEAD_DIM + 16] = p % MOBA_BLOCK
    return np.concatenate([slc] * 2 + [win] * 2 + [mob] * MOBA_HEADS, axis=1)


def _query_features():
    sl = _alibi(NSA_HEADS)
    nsa = np.zeros((NSA_KV_HEADS, 8, SLOT), np.float32)
    for g in range(NSA_KV_HEADS):
        for r in range(NSA_GROUP):
            s = sl[g * NSA_GROUP + r]
            nsa[g, r, HEAD_DIM] = s
            nsa[g, r, HEAD_DIM + 1:] = s * SLC_BLOCK * np.arange(1, 64)
            nsa[g, NSA_GROUP + r, HEAD_DIM] = s * 64
            nsa[g, NSA_GROUP + r, HEAD_DIM + 1] = s
    sm = _alibi(MOBA_HEADS)
    mob = np.zeros((MOBA_HEADS, 8, SLOT), np.float32)
    for h in range(MOBA_HEADS):
        mob[h, 0, HEAD_DIM:HEAD_DIM + 16] = sm[h] * MOBA_BLOCK * np.arange(16)
        mob[h, 0, HEAD_DIM + 16] = sm[h]
    return nsa, mob


def _overlap_t(seq):
    nc = seq // CMP_STRIDE
    c_start = np.arange(nc) * CMP_STRIDE
    b_start = np.arange(64) * SLC_BLOCK
    ov = ((c_start[None, :] < (b_start + SLC_BLOCK)[:, None])
          & ((c_start + CMP_BLOCK)[None, :] > b_start[:, None])
          & (np.arange(nc) < nc - 1)[None, :] & (b_start < seq)[:, None])
    return ov.astype(np.float32)


def _compress_weights(pe, w1, w2):
    ty = np.array([0, 0, 1, 1])
    eye = jnp.eye(4, dtype=F32)
    w1r = w1.reshape(2, CMP_BLOCK, HEAD_DIM, CMP_HIDDEN)[ty]
    top = jnp.einsum('spdj,sS->psdSj', w1r[:, :CMP_STRIDE], eye).reshape(CMP_STRIDE * 256, 4 * CMP_HIDDEN)
    bot = jnp.einsum('spdj,sS->psdSj', w1r[:, CMP_STRIDE:], eye).reshape(CMP_STRIDE * 256, 4 * CMP_HIDDEN)
    per = pe[ty]
    pet = jnp.transpose(per[:, :CMP_STRIDE], (1, 0, 2)).reshape(1, CMP_STRIDE * 256)
    peb = jnp.transpose(per[:, CMP_STRIDE:], (1, 0, 2)).reshape(1, CMP_STRIDE * 256)
    w2p = jnp.concatenate([w2[ty], jnp.zeros((4, CMP_HIDDEN, SLOT - HEAD_DIM), F32)], axis=-1)
    w2b = jnp.einsum('sjd,sS->sjSd', w2p, eye).reshape(4 * CMP_HIDDEN, 4 * SLOT)
    return pet, peb, top.astype(BF16), bot.astype(BF16), w2b.astype(BF16)


def kernel(x, mem, w_in, cmp_pe, cmp_w1, cmp_w2, conv_w, w_branch, w_out, ln1_g, ln1_b,
           xattn_wq, xattn_wkv, xattn_wo, ln2_g, ln2_b, router_w, router_b, exp_w13, exp_w2,
           shared_w13, shared_w2, ln3_g, ln3_b):
    bsz, seq, d = x.shape
    mlen = mem.shape[1]
    depth = w_in.shape[0]
    alpha = (2.0 * depth) ** 0.25
    t = bsz * seq
    scale = HEAD_DIM ** -0.5

    kfeat = jnp.asarray(_key_features(seq))
    fq_nsa, fq_moba = (jnp.asarray(a) for a in _query_features())
    ovt = jnp.asarray(_overlap_t(seq)).astype(BF16)
    memf = mem.reshape(bsz * mlen, d)

    h = x.reshape(t, d)
    for l in range(depth):
        wi = w_in[l]
        kv6 = wi[:, _OFF_NSA_KV:_OFF_NSA_G].reshape(d, 6, NSA_KV_HEADS * HEAD_DIM)
        mq, mk, mv = (wi[:, _OFF_MOBA + j * 512:_OFF_MOBA + (j + 1) * 512] for j in range(3))
        w_a = jnp.concatenate([_pad_slots(wi[:, :512], scale), _pad_slots(kv6[:, 3]), _pad_slots(kv6[:, 5]),
                               _pad_slots(mq, scale), _pad_slots(mv)], axis=1).astype(BF16)
        w_b = jnp.concatenate([_pad_slots(kv6[:, 2]), _pad_slots(kv6[:, 4]), _pad_slots(mk)], axis=1).astype(BF16)
        wg = wi[:, _OFF_NSA_G:_OFF_MOBA].reshape(d, NSA_KV_HEADS, 12)
        wg = jnp.concatenate([wg, jnp.zeros((d, NSA_KV_HEADS, SLOT - 12), F32)], axis=-1).reshape(d, 2 * SLOT)
        w_c = jnp.concatenate([wi[:, _OFF_NSA_KV:_OFF_NSA_KV + 256], wg, wi[:, _OFF_CONV:]], axis=1).astype(BF16)

        za = _mm(h, w_a, BF16, name="proj_a")
        zb = _mm(h, w_b, BF16, feats=kfeat, name="proj_b")
        zc = _mm(h, w_c, F32, name="proj_c")

        pet, peb, wt, wb, w2b = _compress_weights(cmp_pe[l], cmp_w1[l], cmp_w2[l])
        sub = zc[:, :256].reshape(bsz, seq // CMP_STRIDE, CMP_STRIDE * 256)
        kcv = _compress(sub, pet, peb, wt, wb, w2b)

        ya = _nsa_attention(za, zb, zc, kcv, ovt, fq_nsa, bsz, seq)
        yb = _moba_attention(za, zb, fq_moba, bsz, seq)

        cw = jnp.concatenate([conv_w[l], jnp.zeros((5, CONV_CH), F32)], axis=0)
        h = _merge(ya, yb, zc, h, cw, _pad_rows(w_branch[l, 0]).astype(BF16),
                   _pad_rows(w_branch[l, 1]).astype(BF16), w_branch[l, 2].astype(BF16),
                   w_out[l].astype(BF16), ln1_g[l][None], ln1_b[l][None], seq, alpha)

        kv = _mm(memf, xattn_wkv[l].astype(BF16), BF16, tm=512, name="xattn_kv")
        h, hb = _xattn(h, kv, xattn_wq[l].astype(BF16), xattn_wo[l].astype(BF16),
                       ln2_g[l][None], ln2_b[l][None], seq, mlen, alpha)

        h = _moe(h, hb, router_w[l].astype(BF16), router_b[l], exp_w13[l], exp_w2[l],
                 shared_w13[l].astype(BF16), shared_w2[l].astype(BF16), ln3_g[l][None], ln3_b[l][None], alpha)
    return h.reshape(bsz, seq, d)
```

```python
import functools

import jax
import jax.numpy as jnp
import numpy as np
from jax import lax
from jax.experimental import pallas as pl
from jax.experimental.pallas import tpu as pltpu

F32 = jnp.float32
BF16 = jnp.bfloat16
I32 = jnp.int32

D_MODEL = 1024
HEAD_DIM = 64
SLOT = 128
NEG = -1e30
LN_EPS = 1e-5

NSA_HEADS = 8
NSA_KV_HEADS = 2
NSA_GROUP = 4
CMP_BLOCK = 32
CMP_STRIDE = 16
CMP_HIDDEN = 256
SLC_BLOCK = 64
SLC_TOPK = 16
WINDOW = 512
MOBA_HEADS = 8
MOBA_BLOCK = 256
MOBA_TOPK = 3
CONV_CH = 512
XATTN_HEADS = 4
XATTN_HEAD_DIM = 128
N_EXPERTS = 256
TOP_K = 8
N_GROUPS = 8
TOPK_GROUPS = 4
EXPERT_DIM = 256
ROUTE_SCALE = 2.5

TQ = 256
EXP_ROWS = 256
VMEM_LIMIT = 48 * 1024 * 1024

_OFF_NSA_Q = 0
_OFF_NSA_KV = 512
_OFF_NSA_G = 1280
_OFF_MOBA = 1304
_OFF_CONV = 2840
_OFF_MERGE = 4376


def _cparams(sem):
    return pltpu.CompilerParams(dimension_semantics=sem, vmem_limit_bytes=VMEM_LIMIT)


def _dot(a, b):
    return jnp.dot(a, b, preferred_element_type=F32)


def _dot_t(a, b):
    return lax.dot_general(a, b, (((1,), (1,)), ((), ())), preferred_element_type=F32)


def _layer_norm(x, g, b):
    mu = jnp.mean(x, axis=-1, keepdims=True)
    xc = x - mu
    var = jnp.mean(xc * xc, axis=-1, keepdims=True)
    return xc * lax.rsqrt(var + LN_EPS) * g + b


def _mm_kernel(x_ref, w_ref, o_ref, *, act):
    y = _dot(x_ref[...].astype(BF16), w_ref[...])
    if act == "sigmoid":
        y = jax.nn.sigmoid(y)
    o_ref[...] = y.astype(o_ref.dtype)


def _mm_feat_kernel(x_ref, w_ref, f_ref, o_ref):
    y = _dot(x_ref[...].astype(BF16), w_ref[...]) + f_ref[...]
    o_ref[...] = y.astype(o_ref.dtype)


def _mm(x, w, out_dtype, *, tm=1024, tn=512, act=None, feats=None, name):
    m, k = x.shape
    n = w.shape[1]
    tm = min(tm, m)
    tn = min(tn, n)
    assert m % tm == 0 and n % tn == 0, (m, n, tm, tn)
    in_specs = [pl.BlockSpec((tm, k), lambda i, j: (i, 0)),
                pl.BlockSpec((k, tn), lambda i, j: (0, j))]
    args = [x, w]
    if feats is None:
        body = functools.partial(_mm_kernel, act=act)
    else:
        per = feats.shape[0] // tm
        assert feats.shape[0] % tm == 0
        in_specs.append(pl.BlockSpec((tm, tn), lambda i, j: (i % per, j)))
        args.append(feats)
        body = _mm_feat_kernel
    return pl.pallas_call(
        body,
        out_shape=jax.ShapeDtypeStruct((m, n), out_dtype),
        grid=(m // tm, n // tn),
        in_specs=in_specs,
        out_specs=pl.BlockSpec((tm, tn), lambda i, j: (i, j)),
        compiler_params=_cparams(("parallel", "parallel")),
        name=name,
    )(*args)


def _cmp_kernel(sub_ref, pet_ref, peb_ref, wt_ref, wb_ref, w2_ref, o_ref):
    sub = sub_ref[0]
    nc = sub.shape[0]
    a = _dot((sub + pet_ref[...]).astype(BF16), wt_ref[...])
    b = _dot((sub + peb_ref[...]).astype(BF16), wb_ref[...])
    hid = jax.nn.gelu(a + pltpu.roll(b, nc - 1, 0))
    o_ref[0] = _dot(hid.astype(BF16), w2_ref[...]).astype(o_ref.dtype)


def _compress(sub, pet, peb, wt, wb, w2):
    bsz, nc, kk = sub.shape
    n_h = wt.shape[1]
    n_o = w2.shape[1]
    return pl.pallas_call(
        _cmp_kernel,
        out_shape=jax.ShapeDtypeStruct((bsz, nc, n_o), BF16),
        grid=(bsz,),
        in_specs=[pl.BlockSpec((1, nc, kk), lambda b: (b, 0, 0)),
                  pl.BlockSpec((1, kk), lambda b: (0, 0)),
                  pl.BlockSpec((1, kk), lambda b: (0, 0)),
                  pl.BlockSpec((kk, n_h), lambda b: (0, 0)),
                  pl.BlockSpec((kk, n_h), lambda b: (0, 0)),
                  pl.BlockSpec((n_h, n_o), lambda b: (0, 0))],
        out_specs=pl.BlockSpec((1, nc, n_o), lambda b: (b, 0, 0)),
        compiler_params=_cparams(("parallel",)),
        name="nsa_compress",
    )(sub, pet, peb, wt, wb, w2)


def _flash_reset(m_scr, l_scr, acc_scr):
    m_scr[...] = jnp.full(m_scr.shape, NEG, F32)
    l_scr[...] = jnp.zeros(l_scr.shape, F32)
    acc_scr[...] = jnp.zeros(acc_scr.shape, F32)


def _flash_update(s, v, m_scr, l_scr, acc_scr):
    m_prev = m_scr[...]
    m_new = jnp.maximum(m_prev, jnp.max(s, axis=1, keepdims=True))
    alpha = jnp.exp(m_prev - m_new)
    p = jnp.exp(s - m_new)
    l_scr[...] = alpha * l_scr[...] + jnp.sum(p, axis=1, keepdims=True)
    acc_scr[...] = alpha * acc_scr[...] + _dot(p.astype(BF16), v)
    m_scr[...] = m_new


def _tile(ref, j, tq):
    return ref[pl.ds(pl.multiple_of(j * tq, tq), tq), :]


def _nsa_kernel(q_ref, kc_ref, vc_ref, ks_ref, vs_ref, kw_ref, vw_ref, g_ref, ovt_ref, fq_ref,
                o_ref, m_scr, l_scr, acc_scr, imp_scr, *, tq, nsel):
    i = pl.program_id(2)
    r4 = NSA_GROUP
    rows = r4 * tq
    q4 = [q_ref[:, r * SLOT:(r + 1) * SLOT] for r in range(r4)]
    fq = fq_ref[0]
    tl = lax.broadcasted_iota(I32, (tq, tq), 0)
    pl_ = lax.broadcasted_iota(I32, (tq, tq), 1)
    causal = pl_ <= tl

    kc = kc_ref[0]
    vc = vc_ref[0]
    nc = kc.shape[0]
    qs = jnp.concatenate(q4, axis=0)
    s = _dot_t(qs, kc).reshape(r4, tq, nc)
    t_glob = i * tq + lax.broadcasted_iota(I32, (tq, nc), 0)
    c_idx = lax.broadcasted_iota(I32, (tq, nc), 1)
    cmask = (c_idx * CMP_STRIDE + (CMP_BLOCK - 1)) <= t_glob
    s = jnp.where(cmask[None], s, NEG)
    mx = jnp.max(s, axis=-1, keepdims=True)
    e = jnp.where(cmask[None], jnp.exp(s - mx), 0.0)
    lsum = jnp.sum(e, axis=-1, keepdims=True)
    p_cmp = (e / jnp.where(lsum > 0.0, lsum, 1.0)).astype(BF16)
    o_cmp = _dot(p_cmp.reshape(rows, nc), vc)

    ovt = ovt_ref[...]
    imp = _dot_t(ovt, p_cmp[0])
    for r in range(1, r4):
        imp = imp + _dot_t(ovt, p_cmp[r])
    nbp = imp.shape[0]
    jidx = lax.broadcasted_iota(I32, (nbp, tq), 0)
    cur = (i * tq + lax.broadcasted_iota(I32, (nbp, tq), 1)) // SLC_BLOCK
    forced = (jidx == 0) | (jidx == cur) | (jidx == cur - 1)
    imp = jnp.where(forced, 1e6, jnp.where(jidx > cur, -1e6, imp))
    imp_scr[...] = imp

    def rank_body(jp, rank):
        row = imp_scr[pl.ds(jp, 1), :]
        better = (row > imp) | ((row == imp) & (jp < jidx))
        return rank + better.astype(I32)

    n_live = (i + 1) * (tq // SLC_BLOCK)
    rank = lax.fori_loop(0, n_live, rank_body, jnp.zeros((nbp, tq), I32))
    sel = (rank < nsel) & (jidx <= cur)
    mask_t = jnp.where(sel, 0.0, NEG)
    feat = jnp.concatenate([jnp.zeros((SLOT - nbp, tq), F32), mask_t], axis=0).T

    qx = jnp.concatenate([q4[r] + (feat + fq[r:r + 1, :]).astype(BF16) for r in range(r4)], axis=0)
    _flash_reset(m_scr, l_scr, acc_scr)

    def slc_body(j, carry):
        _flash_update(_dot_t(qx, _tile(ks_ref, j, tq)), _tile(vs_ref, j, tq), m_scr, l_scr, acc_scr)
        return carry

    lax.fori_loop(0, i, slc_body, 0)
    s = _dot_t(qx, _tile(ks_ref, i, tq)).reshape(r4, tq, tq)
    s = jnp.where(causal[None], s, NEG).reshape(rows, tq)
    _flash_update(s, _tile(vs_ref, i, tq), m_scr, l_scr, acc_scr)
    o_slc = acc_scr[...] / l_scr[...]

    qx = jnp.concatenate([q4[r] + fq[r4 + r:r4 + r + 1, :].astype(BF16) for r in range(r4)], axis=0)
    _flash_reset(m_scr, l_scr, acc_scr)

    @pl.when(i >= 2)
    def _():
        s2 = _dot_t(qx, _tile(kw_ref, i - 2, tq)).reshape(r4, tq, tq)
        s2 = jnp.where((pl_ > tl)[None], s2, NEG).reshape(rows, tq)
        _flash_update(s2, _tile(vw_ref, i - 2, tq), m_scr, l_scr, acc_scr)

    @pl.when(i >= 1)
    def _():
        _flash_update(_dot_t(qx, _tile(kw_ref, i - 1, tq)), _tile(vw_ref, i - 1, tq), m_scr, l_scr, acc_scr)

    s = _dot_t(qx, _tile(kw_ref, i, tq)).reshape(r4, tq, tq)
    s = jnp.where(causal[None], s, NEG).reshape(rows, tq)
    _flash_update(s, _tile(vw_ref, i, tq), m_scr, l_scr, acc_scr)
    o_win = acc_scr[...] / l_scr[...]

    gate = jax.nn.sigmoid(g_ref[...])
    for r in range(r4):
        sl = slice(r * tq, (r + 1) * tq)
        o = (gate[:, 3 * r:3 * r + 1] * o_cmp[sl] + gate[:, 3 * r + 1:3 * r + 2] * o_slc[sl]
             + gate[:, 3 * r + 2:3 * r + 3] * o_win[sl])
        o_ref[:, r * SLOT:(r + 1) * SLOT] = o.astype(o_ref.dtype)


def _nsa_attention(za, zb, zc, kcv, ovt, fq, bsz, seq):
    tq = TQ
    ni = seq // tq
    nc = kcv.shape[1]
    nsel = min(SLC_TOPK, seq // SLC_BLOCK)
    assert seq // SLC_BLOCK <= 64 and WINDOW == 2 * tq
    rows = NSA_GROUP * tq
    gw = NSA_GROUP * SLOT
    in_specs = [
        pl.BlockSpec((tq, gw), lambda b, g, i: (b * ni + i, g)),
        pl.BlockSpec((1, nc, SLOT), lambda b, g, i: (b, 0, g)),
        pl.BlockSpec((1, nc, SLOT), lambda b, g, i: (b, 0, 2 + g)),
        pl.BlockSpec((seq, SLOT), lambda b, g, i: (b, g)),
        pl.BlockSpec((seq, SLOT), lambda b, g, i: (b, 8 + g)),
        pl.BlockSpec((seq, SLOT), lambda b, g, i: (b, 2 + g)),
        pl.BlockSpec((seq, SLOT), lambda b, g, i: (b, 10 + g)),
        pl.BlockSpec((tq, SLOT), lambda b, g, i: (b * ni + i, 2 + g)),
        pl.BlockSpec(ovt.shape, lambda b, g, i: (0, 0)),
        pl.BlockSpec((1, 8, SLOT), lambda b, g, i: (g, 0, 0)),
    ]
    return pl.pallas_call(
        functools.partial(_nsa_kernel, tq=tq, nsel=nsel),
        out_shape=jax.ShapeDtypeStruct((bsz * seq, NSA_HEADS * SLOT), BF16),
        grid=(bsz, NSA_KV_HEADS, ni),
        in_specs=in_specs,
        out_specs=pl.BlockSpec((tq, gw), lambda b, g, i: (b * ni + i, g)),
        scratch_shapes=[pltpu.VMEM((rows, 1), F32), pltpu.VMEM((rows, 1), F32),
                        pltpu.VMEM((rows, SLOT), F32), pltpu.VMEM((64, tq), F32)],
        compiler_params=_cparams(("parallel", "parallel", "arbitrary")),
        name="nsa_attention",
    )(za, kcv, kcv, zb, za, zb, za, zc, ovt, fq)


def _moba_kernel(q_ref, k_ref, v_ref, fq_ref, o_ref, m_scr, l_scr, acc_scr, km_scr, gate_scr,
                 *, tq, nblk, ntop):
    i = pl.program_id(2)

    @pl.when(i == 0)
    def _():
        km = jnp.mean(k_ref[...].astype(F32).reshape(nblk, tq, SLOT), axis=1)
        if nblk < 16:
            km = jnp.concatenate([km, jnp.zeros((16 - nblk, SLOT), F32)], axis=0)
        km_scr[...] = km

    q = q_ref[...]
    gate = _dot_t(km_scr[...].astype(BF16), q)
    gate_scr[...] = gate
    jidx = lax.broadcasted_iota(I32, (16, tq), 0)

    def rank_body(jp, rank):
        row = gate_scr[pl.ds(jp, 1), :]
        better = (row > gate) | ((row == gate) & (jp < jidx))
        return rank + better.astype(I32)

    rank = lax.fori_loop(0, i, rank_body, jnp.zeros((16, tq), I32))
    keep = ((jidx < i) & (rank < ntop)) | (jidx == i)
    mask_t = jnp.where(keep, 0.0, NEG)
    feat = jnp.concatenate([jnp.zeros((HEAD_DIM, tq), F32), mask_t,
                            jnp.zeros((SLOT - HEAD_DIM - 16, tq), F32)], axis=0).T
    qx = q + (feat + fq_ref[0][0:1, :]).astype(BF16)
    _flash_reset(m_scr, l_scr, acc_scr)

    def body(j, carry):
        _flash_update(_dot_t(qx, _tile(k_ref, j, tq)), _tile(v_ref, j, tq), m_scr, l_scr, acc_scr)
        return carry

    lax.fori_loop(0, i, body, 0)
    tl = lax.broadcasted_iota(I32, (tq, tq), 0)
    pl_ = lax.broadcasted_iota(I32, (tq, tq), 1)
    s = jnp.where(pl_ <= tl, _dot_t(qx, _tile(k_ref, i, tq)), NEG)
    _flash_update(s, _tile(v_ref, i, tq), m_scr, l_scr, acc_scr)
    o_ref[...] = (acc_scr[...] / l_scr[...]).astype(o_ref.dtype)


def _moba_attention(za, zb, fq, bsz, seq):
    tq = TQ
    assert tq == MOBA_BLOCK and seq % tq == 0
    ni = seq // tq
    assert ni <= 16
    ntop = min(MOBA_TOPK, ni)
    return pl.pallas_call(
        functools.partial(_moba_kernel, tq=tq, nblk=ni, ntop=ntop),
        out_shape=jax.ShapeDtypeStruct((bsz * seq, MOBA_HEADS * SLOT), BF16),
        grid=(bsz, MOBA_HEADS, ni),
        in_specs=[pl.BlockSpec((tq, SLOT), lambda b, h, i: (b * ni + i, 12 + h)),
                  pl.BlockSpec((seq, SLOT), lambda b, h, i: (b, 4 + h)),
                  pl.BlockSpec((seq, SLOT), lambda b, h, i: (b, 20 + h)),
                  pl.BlockSpec((1, 8, SLOT), lambda b, h, i: (h, 0, 0))],
        out_specs=pl.BlockSpec((tq, SLOT), lambda b, h, i: (b * ni + i, h)),
        scratch_shapes=[pltpu.VMEM((tq, 1), F32), pltpu.VMEM((tq, 1), F32), pltpu.VMEM((tq, SLOT), F32),
                        pltpu.VMEM((16, SLOT), F32), pltpu.VMEM((16, tq), F32)],
        compiler_params=_cparams(("parallel", "parallel", "arbitrary")),
        name="moba_attention",
    )(za, zb, za, fq)


def _merge_kernel(ya_ref, yb_ref, hc_ref, bc_ref, cc_ref, hcp_ref, ccp_ref, g0_ref, g1_ref, g2_ref,
                  h_ref, cw_ref, wa_ref, wb_ref, wc_ref, wo_ref, lg_ref, lb_ref, o_ref, *, tm, per, alpha):
    i = pl.program_id(0)
    u = cc_ref[...] * hc_ref[...]
    first = (i % per) == 0
    up = jnp.where(first, 0.0, ccp_ref[...] * hcp_ref[...])
    rowi = lax.broadcasted_iota(I32, u.shape, 0)
    u1 = jnp.where(rowi == 0, up[7:8, :], pltpu.roll(u, 1, 0))
    u2 = jnp.where(rowi == 0, up[6:7, :], jnp.where(rowi == 1, up[7:8, :], pltpu.roll(u, 2, 0)))
    cw = cw_ref[...]
    yc = bc_ref[...] * (cw[0:1, :] * u2 + cw[1:2, :] * u1 + cw[2:3, :] * u)
    merged = (jax.nn.sigmoid(g0_ref[...]) * _dot(ya_ref[...], wa_ref[...])
              + jax.nn.sigmoid(g1_ref[...]) * _dot(yb_ref[...], wb_ref[...])
              + jax.nn.sigmoid(g2_ref[...]) * _dot(yc.astype(BF16), wc_ref[...]))
    mix = _dot(merged.astype(BF16), wo_ref[...])
    o_ref[...] = _layer_norm(alpha * h_ref[...] + mix, lg_ref[...], lb_ref[...])


def _merge(ya, yb, zc, h, cw, wa, wb, wc, wo, lg, lb, seq, alpha):
    tm = 512
    t = h.shape[0]
    per = seq // tm
    d = D_MODEL
    row = lambda i: (i, 0)
    const = lambda i: (0, 0)
    prev = lambda c: (lambda i: (jnp.maximum(i * (tm // 8) - 1, 0), c))
    in_specs = [
        pl.BlockSpec((tm, d), row), pl.BlockSpec((tm, d), row),
        pl.BlockSpec((tm, CONV_CH), lambda i: (i, 1)), pl.BlockSpec((tm, CONV_CH), lambda i: (i, 2)),
        pl.BlockSpec((tm, CONV_CH), lambda i: (i, 3)),
        pl.BlockSpec((8, CONV_CH), prev(1)), pl.BlockSpec((8, CONV_CH), prev(3)),
        pl.BlockSpec((tm, d), lambda i: (i, 2)), pl.BlockSpec((tm, d), lambda i: (i, 3)),
        pl.BlockSpec((tm, d), lambda i: (i, 4)),
        pl.BlockSpec((tm, d), row),
        pl.BlockSpec((8, CONV_CH), const),
        pl.BlockSpec((d, d), const), pl.BlockSpec((d, d), const), pl.BlockSpec((CONV_CH, d), const),
        pl.BlockSpec((d, d), const), pl.BlockSpec((1, d), const), pl.BlockSpec((1, d), const),
    ]
    return pl.pallas_call(
        functools.partial(_merge_kernel, tm=tm, per=per, alpha=alpha),
        out_shape=jax.ShapeDtypeStruct((t, d), F32),
        grid=(t // tm,),
        in_specs=in_specs,
        out_specs=pl.BlockSpec((tm, d), row),
        compiler_params=_cparams(("parallel",)),
        name="mixer_merge",
    )(ya, yb, zc, zc, zc, zc, zc, zc, zc, zc, h, cw, wa, wb, wc, wo, lg, lb)


def _xattn_kernel(h_ref, kv_ref, wq_ref, wo_ref, lg_ref, lb_ref, o_ref, ob_ref, *, alpha):
    h = h_ref[...]
    q = _dot(h.astype(BF16), wq_ref[...]).astype(BF16)
    kv = kv_ref[...]
    nh = XATTN_HEADS
    hd = XATTN_HEAD_DIM
    outs = []
    for hh in range(nh):
        s = _dot_t(q[:, hh * hd:(hh + 1) * hd], kv[:, hh * hd:(hh + 1) * hd]) * (hd ** -0.5)
        s = s - jnp.max(s, axis=-1, keepdims=True)
        e = jnp.exp(s)
        p = e / jnp.sum(e, axis=-1, keepdims=True)
        outs.append(_dot(p.astype(BF16), kv[:, (nh + hh) * hd:(nh + hh + 1) * hd]))
    o = jnp.concatenate(outs, axis=-1).astype(BF16)
    y = _layer_norm(alpha * h + _dot(o, wo_ref[...]), lg_ref[...], lb_ref[...])
    o_ref[...] = y
    ob_ref[...] = y.astype(BF16)


def _xattn(h, kv, wq, wo, lg, lb, seq, mlen, alpha):
    tm = 512
    t = h.shape[0]
    per = seq // tm
    d = D_MODEL
    row = lambda i: (i, 0)
    const = lambda i: (0, 0)
    return pl.pallas_call(
        functools.partial(_xattn_kernel, alpha=alpha),
        out_shape=(jax.ShapeDtypeStruct((t, d), F32), jax.ShapeDtypeStruct((t, d), BF16)),
        grid=(t // tm,),
        in_specs=[pl.BlockSpec((tm, d), row), pl.BlockSpec((mlen, kv.shape[1]), lambda i: (i // per, 0)),
                  pl.BlockSpec(wq.shape, const), pl.BlockSpec(wo.shape, const),
                  pl.BlockSpec((1, d), const), pl.BlockSpec((1, d), const)],
        out_specs=(pl.BlockSpec((tm, d), row), pl.BlockSpec((tm, d), row)),
        compiler_params=_cparams(("parallel",)),
        name="mem_xattn",
    )(h, kv, wq, wo, lg, lb)


def _expert_kernel(be_ref, x_ref, w13_ref, w2_ref, o_ref):
    del be_ref
    hmid = _dot(x_ref[...], w13_ref[0].astype(BF16))
    a = hmid[:, :EXPERT_DIM]
    g = hmid[:, EXPERT_DIM:]
    act = (a * jax.nn.sigmoid(a) * g).astype(BF16)
    o_ref[...] = _dot(act, w2_ref[0].astype(BF16)).astype(o_ref.dtype)


def _experts(xg, w13, w2, block_e):
    n_pad, d = xg.shape
    rows = EXP_ROWS
    nb = n_pad // rows
    grid_spec = pltpu.PrefetchScalarGridSpec(
        num_scalar_prefetch=1,
        grid=(nb,),
        in_specs=[pl.BlockSpec((rows, d), lambda i, be: (i, 0)),
                  pl.BlockSpec((1, d, 2 * EXPERT_DIM), lambda i, be: (be[i], 0, 0)),
                  pl.BlockSpec((1, EXPERT_DIM, d), lambda i, be: (be[i], 0, 0))],
        out_specs=pl.BlockSpec((rows, d), lambda i, be: (i, 0)),
    )
    return pl.pallas_call(
        _expert_kernel,
        out_shape=jax.ShapeDtypeStruct((n_pad, d), BF16),
        grid_spec=grid_spec,
        compiler_params=_cparams(("arbitrary",)),
        name="moe_experts",
    )(block_e, xg, w13, w2)


def _moe_out_kernel(h_ref, r_ref, w13_ref, w2_ref, lg_ref, lb_ref, o_ref, *, alpha):
    h = h_ref[...]
    hmid = _dot(h.astype(BF16), w13_ref[...])
    a = hmid[:, :EXPERT_DIM]
    g = hmid[:, EXPERT_DIM:]
    shared = _dot((a * jax.nn.sigmoid(a) * g).astype(BF16), w2_ref[...])
    o_ref[...] = _layer_norm(alpha * h + r_ref[...] + shared, lg_ref[...], lb_ref[...])


def _moe_out(h, routed, w13, w2, lg, lb, alpha):
    tm = 512
    t, d = h.shape
    row = lambda i: (i, 0)
    const = lambda i: (0, 0)
    return pl.pallas_call(
        functools.partial(_moe_out_kernel, alpha=alpha),
        out_shape=jax.ShapeDtypeStruct((t, d), F32),
        grid=(t // tm,),
        in_specs=[pl.BlockSpec((tm, d), row), pl.BlockSpec((tm, d), row),
                  pl.BlockSpec(w13.shape, const), pl.BlockSpec(w2.shape, const),
                  pl.BlockSpec((1, d), const), pl.BlockSpec((1, d), const)],
        out_specs=pl.BlockSpec((tm, d), row),
        compiler_params=_cparams(("parallel",)),
        name="moe_shared_ln",
    )(h, routed, w13, w2, lg, lb)


def _router_kernel(h_ref, rwt_ref, rb_ref, tri_ref, idx_ref, w_ref, rank_ref, cnt_ref, carry_scr, *, tm):
    i = pl.program_id(0)

    @pl.when(i == 0)
    def _():
        carry_scr[...] = jnp.zeros(carry_scr.shape, F32)

    ne = N_EXPERTS
    gsz = ne // N_GROUPS
    s = jax.nn.sigmoid(_dot_t(rwt_ref[...], h_ref[...].astype(BF16)))
    sb = s + rb_ref[...]
    sb3 = sb.reshape(N_GROUPS, gsz, tm)
    li = lax.broadcasted_iota(I32, (N_GROUPS, gsz, tm), 1)
    m1 = jnp.max(sb3, axis=1, keepdims=True)
    first = jnp.min(jnp.where(sb3 == m1, li, gsz), axis=1, keepdims=True)
    m2 = jnp.max(jnp.where(li == first, -jnp.inf, sb3), axis=1, keepdims=True)
    gs = (m1 + m2).reshape(N_GROUPS, tm)
    gi = lax.broadcasted_iota(I32, (N_GROUPS, tm), 0)
    grank = jnp.zeros((N_GROUPS, tm), I32)
    for gp in range(N_GROUPS):
        row = gs[gp:gp + 1, :]
        grank = grank + ((row > gs) | ((row == gs) & (gp < gi))).astype(I32)
    gkeep = (grank < TOPK_GROUPS).astype(F32)
    ekeep = jnp.broadcast_to(gkeep[:, None, :], (N_GROUPS, gsz, tm)).reshape(ne, tm)
    cand = jnp.where(ekeep > 0.0, sb, NEG)
    eidx = lax.broadcasted_iota(I32, (ne, tm), 0)
    sel = jnp.zeros((ne, tm), F32)
    idxs, wts = [], []
    for _ in range(TOP_K):
        mx = jnp.max(cand, axis=0, keepdims=True)
        ik = jnp.min(jnp.where(cand == mx, eidx, ne), axis=0, keepdims=True)
        hit = eidx == ik
        wts.append(jnp.sum(jnp.where(hit, s, 0.0), axis=0, keepdims=True))
        idxs.append(ik)
        sel = jnp.where(hit, 1.0, sel)
        cand = jnp.where(hit, -jnp.inf, cand)
    before = _dot(sel.astype(BF16), tri_ref[...]) + carry_scr[...]
    ranks = [jnp.sum(jnp.where(eidx == ik, before, 0.0), axis=0, keepdims=True) for ik in idxs]
    carry_scr[...] = carry_scr[...] + jnp.sum(sel, axis=1, keepdims=True)
    w = jnp.concatenate(wts, axis=0)
    idx_ref[...] = jnp.concatenate(idxs, axis=0)
    w_ref[...] = w / jnp.sum(w, axis=0, keepdims=True) * ROUTE_SCALE
    rank_ref[...] = jnp.concatenate(ranks, axis=0).astype(I32)
    cnt_ref[...] = jnp.broadcast_to(carry_scr[...], cnt_ref.shape)


def _router(h, rwt, rb):
    tm = 256
    t, d = h.shape
    tri = jnp.asarray(np.triu(np.ones((tm, tm), np.float32), 1)).astype(BF16)
    const = lambda i: (0, 0)
    col = lambda i: (0, i)
    return pl.pallas_call(
        functools.partial(_router_kernel, tm=tm),
        out_shape=(jax.ShapeDtypeStruct((TOP_K, t), I32), jax.ShapeDtypeStruct((TOP_K, t), F32),
                   jax.ShapeDtypeStruct((TOP_K, t), I32), jax.ShapeDtypeStruct((N_EXPERTS, SLOT), F32)),
        grid=(t // tm,),
        in_specs=[pl.BlockSpec((tm, d), lambda i: (i, 0)), pl.BlockSpec((N_EXPERTS, d), const),
                  pl.BlockSpec((N_EXPERTS, 1), const), pl.BlockSpec((tm, tm), const)],
        out_specs=(pl.BlockSpec((TOP_K, tm), col), pl.BlockSpec((TOP_K, tm), col),
                   pl.BlockSpec((TOP_K, tm), col), pl.BlockSpec((N_EXPERTS, SLOT), const)),
        scratch_shapes=[pltpu.VMEM((N_EXPERTS, 1), F32)],
        compiler_params=_cparams(("arbitrary",)),
        name="moe_router",
    )(h, rwt, rb, tri)


def _moe(h, hb, router_w, router_b, exp_w13, exp_w2, shared_w13, shared_w2, lg, lb, alpha):
    t, d = h.shape
    idx, wts, rank, cnt = _router(h, router_w.T, router_b.astype(F32).reshape(N_EXPERTS, 1))
    rows = EXP_ROWS
    n_a = t * TOP_K
    counts = cnt[:, 0].astype(I32)
    pcounts = (counts + rows - 1) // rows * rows
    pends = jnp.cumsum(pcounts)
    pstarts = pends - pcounts
    dest = jnp.take(pstarts, idx, axis=0) + rank
    n_blocks = -(-n_a // rows) + N_EXPERTS
    n_pad = n_blocks * rows
    tok = jnp.broadcast_to(jnp.arange(t, dtype=I32)[None, :], (TOP_K, t))
    row_tok = jnp.full((n_pad,), t, I32).at[dest.reshape(-1)].set(tok.reshape(-1), unique_indices=True)
    block_e = jnp.minimum(jnp.searchsorted(pends, jnp.arange(n_blocks) * rows, side='right'),
                          N_EXPERTS - 1).astype(I32)
    x_pad = jnp.concatenate([hb, jnp.zeros((1, d), hb.dtype)], axis=0)
    xg = jnp.take(x_pad, row_tok, axis=0)
    out = _experts(xg, exp_w13, exp_w2, block_e)
    routed = jnp.einsum('kt,ktd->td', wts, jnp.take(out, dest, axis=0).astype(F32))
    return _moe_out(h, routed, shared_w13, shared_w2, lg, lb, alpha)


def _pad_slots(w, scale=1.0):
    dm = w.shape[0]
    nh = w.shape[1] // HEAD_DIM
    w = (w * scale).reshape(dm, nh, HEAD_DIM)
    return jnp.concatenate([w, jnp.zeros_like(w)], axis=-1).reshape(dm, nh * SLOT)


def _pad_rows(w):
    n = w.shape[1]
    nh = w.shape[0] // HEAD_DIM
    w = w.reshape(nh, HEAD_DIM, n)
    return jnp.concatenate([w, jnp.zeros_like(w)], axis=1).reshape(nh * SLOT, n)


def _alibi(n):
    return np.exp2(-8.0 * np.arange(1, n + 1, dtype=np.float64) / n).astype(np.float32)


def _key_features(seq):
    p = np.arange(seq)
    slc = np.zeros((seq, SLOT), np.float32)
    slc[:, HEAD_DIM] = p % SLC_BLOCK
    blk = p // SLC_BLOCK
    nz = blk > 0
    slc[p[nz], HEAD_DIM + blk[nz]] = 1.0
    win = np.zeros((seq, SLOT), np.float32)
    win[:, HEAD_DIM] = p // 64
    win[:, HEAD_DIM + 1] = p % 64
    mob = np.zeros((seq, SLOT), np.float32)
    mob[p, HEAD_DIM + p // MOBA_BLOCK] = 1.0
    mob[:, HEAD_DIM + 16] = p % MOBA_BLOCK
    return np.concatenate([slc] * 2 + [win] * 2 + [mob] * MOBA_HEADS, axis=1)


def _query_features():
    sl = _alibi(NSA_HEADS)
    nsa = np.zeros((NSA_KV_HEADS, 8, SLOT), np.float32)
    for g in range(NSA_KV_HEADS):
        for r in range(NSA_GROUP):
            s = sl[g * NSA_GROUP + r]
            nsa[g, r, HEAD_DIM] = s
            nsa[g, r, HEAD_DIM + 1:] = s * SLC_BLOCK * np.arange(1, 64)
            nsa[g, NSA_GROUP + r, HEAD_DIM] = s * 64
            nsa[g, NSA_GROUP + r, HEAD_DIM + 1] = s
    sm = _alibi(MOBA_HEADS)
    mob = np.zeros((MOBA_HEADS, 8, SLOT), np.float32)
    for h in range(MOBA_HEADS):
        mob[h, 0, HEAD_DIM:HEAD_DIM + 16] = sm[h] * MOBA_BLOCK * np.arange(16)
        mob[h, 0, HEAD_DIM + 16] = sm[h]
    return nsa, mob


def _overlap_t(seq):
    nc = seq // CMP_STRIDE
    c_start = np.arange(nc) * CMP_STRIDE
    b_start = np.arange(64) * SLC_BLOCK
    ov = ((c_start[None, :] < (b_start + SLC_BLOCK)[:, None])
          & ((c_start + CMP_BLOCK)[None, :] > b_start[:, None])
          & (np.arange(nc) < nc - 1)[None, :] & (b_start < seq)[:, None])
    return ov.astype(np.float32)


def _compress_weights(pe, w1, w2):
    ty = np.array([0, 0, 1, 1])
    eye = jnp.eye(4, dtype=F32)
    w1r = w1.reshape(2, CMP_BLOCK, HEAD_DIM, CMP_HIDDEN)[ty]
    top = jnp.einsum('spdj,sS->psdSj', w1r[:, :CMP_STRIDE], eye).reshape(CMP_STRIDE * 256, 4 * CMP_HIDDEN)
    bot = jnp.einsum('spdj,sS->psdSj', w1r[:, CMP_STRIDE:], eye).reshape(CMP_STRIDE * 256, 4 * CMP_HIDDEN)
    per = pe[ty]
    pet = jnp.transpose(per[:, :CMP_STRIDE], (1, 0, 2)).reshape(1, CMP_STRIDE * 256)
    peb = jnp.transpose(per[:, CMP_STRIDE:], (1, 0, 2)).reshape(1, CMP_STRIDE * 256)
    w2p = jnp.concatenate([w2[ty], jnp.zeros((4, CMP_HIDDEN, SLOT - HEAD_DIM), F32)], axis=-1)
    w2b = jnp.einsum('sjd,sS->sjSd', w2p, eye).reshape(4 * CMP_HIDDEN, 4 * SLOT)
    return pet, peb, top.astype(BF16), bot.astype(BF16), w2b.astype(BF16)


def kernel(x, mem, w_in, cmp_pe, cmp_w1, cmp_w2, conv_w, w_branch, w_out, ln1_g, ln1_b,
           xattn_wq, xattn_wkv, xattn_wo, ln2_g, ln2_b, router_w, router_b, exp_w13, exp_w2,
           shared_w13, shared_w2, ln3_g, ln3_b):
    bsz, seq, d = x.shape
    mlen = mem.shape[1]
    depth = w_in.shape[0]
    alpha = (2.0 * depth) ** 0.25
    t = bsz * seq
    scale = HEAD_DIM ** -0.5

    kfeat = jnp.asarray(_key_features(seq))
    fq_nsa, fq_moba = (jnp.asarray(a) for a in _query_features())
    ovt = jnp.asarray(_overlap_t(seq)).astype(BF16)
    memf = mem.reshape(bsz * mlen, d)

    h = x.reshape(t, d)
    for l in range(depth):
        wi = w_in[l]
        kv6 = wi[:, _OFF_NSA_KV:_OFF_NSA_G].reshape(d, 6, NSA_KV_HEADS * HEAD_DIM)
        mq, mk, mv = (wi[:, _OFF_MOBA + j * 512:_OFF_MOBA + (j + 1) * 512] for j in range(3))
        w_a = jnp.concatenate([_pad_slots(wi[:, :512], scale), _pad_slots(kv6[:, 3]), _pad_slots(kv6[:, 5]),
                               _pad_slots(mq, scale), _pad_slots(mv)], axis=1).astype(BF16)
        w_b = jnp.concatenate([_pad_slots(kv6[:, 2]), _pad_slots(kv6[:, 4]), _pad_slots(mk)], axis=1).astype(BF16)
        wg = wi[:, _OFF_NSA_G:_OFF_MOBA].reshape(d, NSA_KV_HEADS, 12)
        wg = jnp.concatenate([wg, jnp.zeros((d, NSA_KV_HEADS, SLOT - 12), F32)], axis=-1).reshape(d, 2 * SLOT)
        w_c = jnp.concatenate([wi[:, _OFF_NSA_KV:_OFF_NSA_KV + 256], wg, wi[:, _OFF_CONV:]], axis=1).astype(BF16)

        za = _mm(h, w_a, BF16, name="proj_a")
        zb = _mm(h, w_b, BF16, feats=kfeat, name="proj_b")
        zc = _mm(h, w_c, F32, name="proj_c")

        pet, peb, wt, wb, w2b = _compress_weights(cmp_pe[l], cmp_w1[l], cmp_w2[l])
        sub = zc[:, :256].reshape(bsz, seq // CMP_STRIDE, CMP_STRIDE * 256)
        kcv = _compress(sub, pet, peb, wt, wb, w2b)

        ya = _nsa_attention(za, zb, zc, kcv, ovt, fq_nsa, bsz, seq)
        yb = _moba_attention(za, zb, fq_moba, bsz, seq)

        cw = jnp.concatenate([conv_w[l], jnp.zeros((5, CONV_CH), F32)], axis=0)
        h = _merge(ya, yb, zc, h, cw, _pad_rows(w_branch[l, 0]).astype(BF16),
                   _pad_rows(w_branch[l, 1]).astype(BF16), w_branch[l, 2].astype(BF16),
                   w_out[l].astype(BF16), ln1_g[l][None], ln1_b[l][None], seq, alpha)

        kv = _mm(memf, xattn_wkv[l].astype(BF16), BF16, tm=512, name="xattn_kv")
        h, hb = _xattn(h, kv, xattn_wq[l].astype(BF16), xattn_wo[l].astype(BF16),
                       ln2_g[l][None], ln2_b[l][None], seq, mlen, alpha)

        h = _moe(h, hb, router_w[l].astype(BF16), router_b[l], exp_w13[l], exp_w2[l],
                 shared_w13[l].astype(BF16), shared_w2[l].astype(BF16), ln3_g[l][None], ln3_b[l][None], alpha)
    return h.reshape(bsz, seq, d)
```

```python
import functools

import jax
import jax.numpy as jnp
import numpy as np
from jax import lax
from jax.experimental import pallas as pl
from jax.experimental.pallas import tpu as pltpu

F32 = jnp.float32
BF16 = jnp.bfloat16
I32 = jnp.int32

D_MODEL = 1024
HEAD_DIM = 64
SLOT = 128
NEG = -1e30
LN_EPS = 1e-5

NSA_HEADS = 8
NSA_KV_HEADS = 2
NSA_GROUP = 4
CMP_BLOCK = 32
CMP_STRIDE = 16
CMP_HIDDEN = 256
SLC_BLOCK = 64
SLC_TOPK = 16
WINDOW = 512
MOBA_HEADS = 8
MOBA_BLOCK = 256
MOBA_TOPK = 3
CONV_CH = 512
XATTN_HEADS = 4
XATTN_HEAD_DIM = 128
N_EXPERTS = 256
TOP_K = 8
N_GROUPS = 8
TOPK_GROUPS = 4
EXPERT_DIM = 256
ROUTE_SCALE = 2.5

TQ = 256
MOBA_HPS = 4
EXP_ROWS = 256
VMEM_LIMIT = 48 * 1024 * 1024

_OFF_NSA_Q = 0
_OFF_NSA_KV = 512
_OFF_NSA_G = 1280
_OFF_MOBA = 1304
_OFF_CONV = 2840
_OFF_MERGE = 4376

_SLOT_NSA_Q = 0
_SLOT_SLC_V = 8
_SLOT_WIN_V = 10
_SLOT_MOBA_Q = 12
_SLOT_MOBA_V = 20
_N_SLOTS_T = 28


def _cparams(sem):
    return pltpu.CompilerParams(dimension_semantics=sem, vmem_limit_bytes=VMEM_LIMIT)


def _dot(a, b):
    return jnp.dot(a, b, preferred_element_type=F32)


def _dot_t(a, b):
    return lax.dot_general(a, b, (((1,), (1,)), ((), ())), preferred_element_type=F32)


def _layer_norm(x, g, b):
    mu = jnp.mean(x, axis=-1, keepdims=True)
    xc = x - mu
    var = jnp.mean(xc * xc, axis=-1, keepdims=True)
    return xc * lax.rsqrt(var + LN_EPS) * g + b


def _mm_kernel(x_ref, w_ref, o_ref):
    o_ref[...] = _dot(x_ref[...].astype(BF16), w_ref[...]).astype(o_ref.dtype)


def _mm_feat_kernel(x_ref, w_ref, f_ref, o_ref):
    y = _dot(x_ref[...].astype(BF16), w_ref[...]) + f_ref[...]
    o_ref[...] = y.astype(o_ref.dtype)


def _mm(x, w, out_dtype, *, tm=1024, tn=512, feats=None, name):
    m, k = x.shape
    n = w.shape[1]
    tm = min(tm, m)
    tn = min(tn, n)
    assert m % tm == 0 and n % tn == 0, (m, n, tm, tn)
    in_specs = [pl.BlockSpec((tm, k), lambda i, j: (i, 0)),
                pl.BlockSpec((k, tn), lambda i, j: (0, j))]
    args = [x, w]
    if feats is None:
        body = _mm_kernel
    else:
        per = feats.shape[0] // tm
        assert feats.shape[0] % tm == 0
        in_specs.append(pl.BlockSpec((tm, tn), lambda i, j: (i % per, j)))
        args.append(feats)
        body = _mm_feat_kernel
    return pl.pallas_call(
        body,
        out_shape=jax.ShapeDtypeStruct((m, n), out_dtype),
        grid=(m // tm, n // tn),
        in_specs=in_specs,
        out_specs=pl.BlockSpec((tm, tn), lambda i, j: (i, j)),
        compiler_params=_cparams(("parallel", "parallel")),
        name=name,
    )(*args)


def _cmp_kernel(sub_ref, pet_ref, peb_ref, wt_ref, wb_ref, w2k_ref, w2vt_ref, kc_ref, vct_ref):
    sub = sub_ref[0]
    nc = sub.shape[0]
    a = _dot((sub + pet_ref[...]).astype(BF16), wt_ref[...])
    b = _dot((sub + peb_ref[...]).astype(BF16), wb_ref[...])
    hid = jax.nn.gelu(a + pltpu.roll(b, nc - 1, 0)).astype(BF16)
    kc_ref[0] = _dot(hid, w2k_ref[...]).astype(kc_ref.dtype)
    vct_ref[0] = _dot_t(w2vt_ref[...], hid).astype(vct_ref.dtype)


def _compress(sub, pet, peb, wt, wb, w2k, w2vt):
    bsz, nc, kk = sub.shape
    n_h = wt.shape[1]
    n_o = w2k.shape[1]
    const = lambda b: (0, 0)
    return pl.pallas_call(
        _cmp_kernel,
        out_shape=(jax.ShapeDtypeStruct((bsz, nc, n_o), BF16), jax.ShapeDtypeStruct((bsz, n_o, nc), BF16)),
        grid=(bsz,),
        in_specs=[pl.BlockSpec((1, nc, kk), lambda b: (b, 0, 0)),
                  pl.BlockSpec((1, kk), const), pl.BlockSpec((1, kk), const),
                  pl.BlockSpec((kk, n_h), const), pl.BlockSpec((kk, n_h), const),
                  pl.BlockSpec((n_h, n_o), const), pl.BlockSpec((n_o, n_h), const)],
        out_specs=(pl.BlockSpec((1, nc, n_o), lambda b: (b, 0, 0)), pl.BlockSpec((1, n_o, nc), lambda b: (b, 0, 0))),
        compiler_params=_cparams(("parallel",)),
        name="nsa_compress",
    )(sub, pet, peb, wt, wb, w2k, w2vt)


def _flash_reset(m_scr, l_scr, acc_scr):
    m_scr[...] = jnp.full(m_scr.shape, NEG, F32)
    l_scr[...] = jnp.zeros(l_scr.shape, F32)
    acc_scr[...] = jnp.zeros(acc_scr.shape, F32)


def _flash_step(kget, vget, qxs, start, size, m_scr, l_scr, acc_scr, mask=None):
    nq = len(qxs)
    k0 = kget(0, start, size)
    if kget(1, start, size) is None:
        s = _dot(k0, jnp.concatenate(qxs, axis=1))
    else:
        s = jnp.concatenate([_dot(k0, qxs[0])] + [_dot(kget(c, start, size), qxs[c]) for c in range(1, nq)],
                            axis=1)
    if mask is not None:
        s = jnp.where(mask, s, NEG)
    m_prev = m_scr[...]
    m_new = jnp.maximum(m_prev, jnp.max(s, axis=0, keepdims=True))
    alpha = jnp.exp(m_prev - m_new)
    p = jnp.exp(s - m_new)
    l_scr[...] = alpha * l_scr[...] + jnp.sum(p, axis=0, keepdims=True)
    p = p.astype(BF16)
    v0 = vget(0, start, size)
    if vget(1, start, size) is None:
        pv = _dot(v0, p)
    else:
        pv = jnp.concatenate([_dot(v0, p[:, :TQ])] + [_dot(vget(c, start, size), p[:, c * TQ:(c + 1) * TQ])
                                                      for c in range(1, nq)], axis=1)
    acc_scr[...] = alpha * acc_scr[...] + pv
    m_scr[...] = m_new


def _rows(ref, start, size, lane0=0):
    return ref[pl.ds(pl.multiple_of(start, TQ), size), lane0:lane0 + SLOT]


def _cols(ref, start, size, row0=0):
    return ref[row0:row0 + SLOT, pl.ds(pl.multiple_of(start, TQ), size)]


def _flash_past(kget, vget, qxs, n_tiles, m_scr, l_scr, acc_scr):
    tq = TQ

    def body(j, carry):
        _flash_step(kget, vget, qxs, j * 2 * tq, 2 * tq, m_scr, l_scr, acc_scr)
        return carry

    lax.fori_loop(0, n_tiles // 2, body, 0)

    @pl.when(n_tiles % 2 == 1)
    def _():
        _flash_step(kget, vget, qxs, (n_tiles - 1) * tq, tq, m_scr, l_scr, acc_scr)


def _nsa_kernel(q_ref, kc_ref, vct_ref, ks_ref, vst_ref, kw_ref, vwt_ref, g_ref, ovt_ref, fq_ref,
                o_ref, m_scr, l_scr, acc_scr, imp_scr, *, tq, nsel):
    i = pl.program_id(2)
    r4 = NSA_GROUP
    rr = r4 * tq
    q4 = [q_ref[r * SLOT:(r + 1) * SLOT, :] for r in range(r4)]
    fq = fq_ref[0]
    p_loc = lax.broadcasted_iota(I32, (tq, rr), 0)
    t_loc = lax.broadcasted_iota(I32, (tq, rr), 1) & (tq - 1)
    causal = p_loc <= t_loc

    kc = kc_ref[0]
    nc = kc.shape[0]
    qs = jnp.concatenate(q4, axis=1)
    s = _dot(kc, qs)
    t_glob = i * tq + (lax.broadcasted_iota(I32, (nc, rr), 1) & (tq - 1))
    c_idx = lax.broadcasted_iota(I32, (nc, rr), 0)
    cmask = (c_idx * CMP_STRIDE + (CMP_BLOCK - 1)) <= t_glob
    s = jnp.where(cmask, s, NEG)
    mx = jnp.max(s, axis=0, keepdims=True)
    e = jnp.where(cmask, jnp.exp(s - mx), 0.0)
    lsum = jnp.sum(e, axis=0, keepdims=True)
    p_cmp = (e / jnp.where(lsum > 0.0, lsum, 1.0)).astype(BF16)
    o_cmp = _dot(vct_ref[0], p_cmp)

    ovt = ovt_ref[...]
    imp = _dot(ovt, p_cmp[:, 0:tq])
    for r in range(1, r4):
        imp = imp + _dot(ovt, p_cmp[:, r * tq:(r + 1) * tq])
    nbp = imp.shape[0]
    jidx = lax.broadcasted_iota(I32, (nbp, tq), 0)
    cur = (i * tq + lax.broadcasted_iota(I32, (nbp, tq), 1)) // SLC_BLOCK
    forced = (jidx == 0) | (jidx == cur) | (jidx == cur - 1)
    imp = jnp.where(forced, 1e6, jnp.where(jidx > cur, -1e6, imp))
    imp_scr[...] = imp

    def rank_body(jp, rank):
        row = imp_scr[pl.ds(jp, 1), :]
        better = (row > imp) | ((row == imp) & (jp < jidx))
        return rank + better.astype(I32)

    n_live = (i + 1) * (tq // SLC_BLOCK)
    rank = lax.fori_loop(0, n_live, rank_body, jnp.zeros((nbp, tq), I32))
    sel = (rank < nsel) & (jidx <= cur)
    mask_t = jnp.where(sel, 0.0, NEG)
    feat = jnp.concatenate([jnp.zeros((SLOT - nbp, tq), F32), mask_t], axis=0)

    qxs = [q4[r] + (feat + fq[:, r:r + 1]).astype(BF16) for r in range(r4)]
    kget = lambda c, start, size: _rows(ks_ref, start, size) if c == 0 else None
    vget = lambda c, start, size: _cols(vst_ref, start, size) if c == 0 else None
    _flash_reset(m_scr, l_scr, acc_scr)
    _flash_past(kget, vget, qxs, i, m_scr, l_scr, acc_scr)
    _flash_step(kget, vget, qxs, i * tq, tq, m_scr, l_scr, acc_scr, mask=causal)
    o_slc = acc_scr[...] / l_scr[...]

    qxs = [q4[r] + fq[:, r4 + r:r4 + r + 1].astype(BF16) for r in range(r4)]
    kget = lambda c, start, size: _rows(kw_ref, start, size) if c == 0 else None
    vget = lambda c, start, size: _cols(vwt_ref, start, size) if c == 0 else None
    _flash_reset(m_scr, l_scr, acc_scr)

    @pl.when(i >= 2)
    def _():
        _flash_step(kget, vget, qxs, (i - 2) * tq, tq, m_scr, l_scr, acc_scr, mask=p_loc > t_loc)

    @pl.when(i >= 1)
    def _():
        _flash_step(kget, vget, qxs, (i - 1) * tq, tq, m_scr, l_scr, acc_scr)

    _flash_step(kget, vget, qxs, i * tq, tq, m_scr, l_scr, acc_scr, mask=causal)
    o_win = acc_scr[...] / l_scr[...]

    gate = jax.nn.sigmoid(g_ref[...])
    for r in range(r4):
        sl = slice(r * tq, (r + 1) * tq)
        o = (gate[3 * r:3 * r + 1, :] * o_cmp[:, sl] + gate[3 * r + 1:3 * r + 2, :] * o_slc[:, sl]
             + gate[3 * r + 2:3 * r + 3, :] * o_win[:, sl])
        o_ref[:, r * SLOT:(r + 1) * SLOT] = o.T.astype(o_ref.dtype)


def _nsa_attention(zat, zb, gt, kc, vct, ovt, fq, bsz, seq):
    tq = TQ
    ni = seq // tq
    nc = kc.shape[1]
    nsel = min(SLC_TOPK, seq // SLC_BLOCK)
    assert seq // SLC_BLOCK <= 64 and WINDOW == 2 * tq
    rr = NSA_GROUP * tq
    gw = NSA_GROUP * SLOT
    in_specs = [
        pl.BlockSpec((gw, tq), lambda b, g, i: (g, b * ni + i)),
        pl.BlockSpec((1, nc, SLOT), lambda b, g, i: (b, 0, g)),
        pl.BlockSpec((1, SLOT, nc), lambda b, g, i: (b, g, 0)),
        pl.BlockSpec((seq, SLOT), lambda b, g, i: (b, g)),
        pl.BlockSpec((SLOT, seq), lambda b, g, i: (_SLOT_SLC_V + g, b)),
        pl.BlockSpec((seq, SLOT), lambda b, g, i: (b, 2 + g)),
        pl.BlockSpec((SLOT, seq), lambda b, g, i: (_SLOT_WIN_V + g, b)),
        pl.BlockSpec((SLOT, tq), lambda b, g, i: (g, b * ni + i)),
        pl.BlockSpec(ovt.shape, lambda b, g, i: (0, 0)),
        pl.BlockSpec((1, SLOT, 8), lambda b, g, i: (g, 0, 0)),
    ]
    return pl.pallas_call(
        functools.partial(_nsa_kernel, tq=tq, nsel=nsel),
        out_shape=jax.ShapeDtypeStruct((bsz * seq, NSA_HEADS * SLOT), BF16),
        grid=(bsz, NSA_KV_HEADS, ni),
        in_specs=in_specs,
        out_specs=pl.BlockSpec((tq, gw), lambda b, g, i: (b * ni + i, g)),
        scratch_shapes=[pltpu.VMEM((1, rr), F32), pltpu.VMEM((1, rr), F32),
                        pltpu.VMEM((SLOT, rr), F32), pltpu.VMEM((64, tq), F32)],
        compiler_params=_cparams(("parallel", "parallel", "arbitrary")),
        name="nsa_attention",
    )(zat, kc, vct, zb, zat, zb, zat, gt, ovt, fq)


def _moba_kernel(q_ref, k_ref, vt_ref, fq_ref, o_ref, m_scr, l_scr, acc_scr, km_scr, gate_scr,
                 *, tq, nblk, ntop):
    i = pl.program_id(2)
    nh = MOBA_HPS

    @pl.when(i == 0)
    def _():
        for hh in range(nh):
            kh = k_ref[:, hh * SLOT:(hh + 1) * SLOT].astype(F32)
            km = jnp.mean(kh.reshape(nblk, tq, SLOT), axis=1)
            if nblk < 16:
                km = jnp.concatenate([km, jnp.zeros((16 - nblk, SLOT), F32)], axis=0)
            km_scr[hh] = km

    qs = [q_ref[hh * SLOT:(hh + 1) * SLOT, :] for hh in range(nh)]
    gates = [_dot(km_scr[hh].astype(BF16), qs[hh]) for hh in range(nh)]
    for hh in range(nh):
        gate_scr[hh] = gates[hh]
    jidx = lax.broadcasted_iota(I32, (16, tq), 0)

    def rank_body(jp, ranks):
        out = []
        for hh in range(nh):
            row = gate_scr[hh, pl.ds(jp, 1), :]
            better = (row > gates[hh]) | ((row == gates[hh]) & (jp < jidx))
            out.append(ranks[hh] + better.astype(I32))
        return tuple(out)

    ranks = lax.fori_loop(0, i, rank_body, tuple(jnp.zeros((16, tq), I32) for _ in range(nh)))
    qxs = []
    for hh in range(nh):
        keep = ((jidx < i) & (ranks[hh] < ntop)) | (jidx == i)
        mask_t = jnp.where(keep, 0.0, NEG)
        feat = jnp.concatenate([jnp.zeros((HEAD_DIM, tq), F32), mask_t,
                                jnp.zeros((SLOT - HEAD_DIM - 16, tq), F32)], axis=0)
        qxs.append(qs[hh] + (feat + fq_ref[hh][:, 0:1]).astype(BF16))
    kget = lambda c, start, size: _rows(k_ref, start, size, c * SLOT)
    vget = lambda c, start, size: _cols(vt_ref, start, size, c * SLOT)
    _flash_reset(m_scr, l_scr, acc_scr)
    _flash_past(kget, vget, qxs, i, m_scr, l_scr, acc_scr)
    p_loc = lax.broadcasted_iota(I32, (tq, nh * tq), 0)
    t_loc = lax.broadcasted_iota(I32, (tq, nh * tq), 1) & (tq - 1)
    _flash_step(kget, vget, qxs, i * tq, tq, m_scr, l_scr, acc_scr, mask=p_loc <= t_loc)
    o = acc_scr[...] / l_scr[...]
    for hh in range(nh):
        o_ref[:, hh * SLOT:(hh + 1) * SLOT] = o[:, hh * tq:(hh + 1) * tq].T.astype(o_ref.dtype)


def _moba_attention(zat, zb, fq, bsz, seq):
    tq = TQ
    assert tq == MOBA_BLOCK and seq % tq == 0
    ni = seq // tq
    assert ni <= 16
    ntop = min(MOBA_TOPK, ni)
    nh = MOBA_HPS
    hw = nh * SLOT
    q0 = _SLOT_MOBA_Q // nh
    v0 = _SLOT_MOBA_V // nh
    return pl.pallas_call(
        functools.partial(_moba_kernel, tq=tq, nblk=ni, ntop=ntop),
        out_shape=jax.ShapeDtypeStruct((bsz * seq, MOBA_HEADS * SLOT), BF16),
        grid=(bsz, MOBA_HEADS // nh, ni),
        in_specs=[pl.BlockSpec((hw, tq), lambda b, h, i: (q0 + h, b * ni + i)),
                  pl.BlockSpec((seq, hw), lambda b, h, i: (b, 1 + h)),
                  pl.BlockSpec((hw, seq), lambda b, h, i: (v0 + h, b)),
                  pl.BlockSpec((nh, SLOT, 8), lambda b, h, i: (h, 0, 0))],
        out_specs=pl.BlockSpec((tq, hw), lambda b, h, i: (b * ni + i, h)),
        scratch_shapes=[pltpu.VMEM((1, nh * tq), F32), pltpu.VMEM((1, nh * tq), F32),
                        pltpu.VMEM((SLOT, nh * tq), F32),
                        pltpu.VMEM((nh, 16, SLOT), F32), pltpu.VMEM((nh, 16, tq), F32)],
        compiler_params=_cparams(("parallel", "parallel", "arbitrary")),
        name="moba_attention",
    )(zat, zb, zat, fq)


def _merge_kernel(ya_ref, yb_ref, hc_ref, bc_ref, cc_ref, hcp_ref, ccp_ref, g0_ref, g1_ref, g2_ref,
                  h_ref, cw_ref, wa_ref, wb_ref, wc_ref, wo_ref, lg_ref, lb_ref, o_ref, *, tm, per, alpha):
    i = pl.program_id(0)
    u = cc_ref[...] * hc_ref[...]
    first = (i % per) == 0
    up = jnp.where(first, 0.0, ccp_ref[...] * hcp_ref[...])
    rowi = lax.broadcasted_iota(I32, u.shape, 0)
    u1 = jnp.where(rowi == 0, up[7:8, :], pltpu.roll(u, 1, 0))
    u2 = jnp.where(rowi == 0, up[6:7, :], jnp.where(rowi == 1, up[7:8, :], pltpu.roll(u, 2, 0)))
    cw = cw_ref[...]
    yc = bc_ref[...] * (cw[0:1, :] * u2 + cw[1:2, :] * u1 + cw[2:3, :] * u)
    merged = (jax.nn.sigmoid(g0_ref[...]) * _dot(ya_ref[...], wa_ref[...])
              + jax.nn.sigmoid(g1_ref[...]) * _dot(yb_ref[...], wb_ref[...])
              + jax.nn.sigmoid(g2_ref[...]) * _dot(yc.astype(BF16), wc_ref[...]))
    mix = _dot(merged.astype(BF16), wo_ref[...])
    o_ref[...] = _layer_norm(alpha * h_ref[...] + mix, lg_ref[...], lb_ref[...])


def _merge(ya, yb, zc, h, cw, wa, wb, wc, wo, lg, lb, seq, alpha):
    tm = 512
    t = h.shape[0]
    per = seq // tm
    d = D_MODEL
    row = lambda i: (i, 0)
    const = lambda i: (0, 0)
    prev = lambda c: (lambda i: (jnp.maximum(i * (tm // 8) - 1, 0), c))
    in_specs = [
        pl.BlockSpec((tm, d), row), pl.BlockSpec((tm, d), row),
        pl.BlockSpec((tm, CONV_CH), lambda i: (i, 1)), pl.BlockSpec((tm, CONV_CH), lambda i: (i, 2)),
        pl.BlockSpec((tm, CONV_CH), lambda i: (i, 3)),
        pl.BlockSpec((8, CONV_CH), prev(1)), pl.BlockSpec((8, CONV_CH), prev(3)),
        pl.BlockSpec((tm, d), lambda i: (i, 2)), pl.BlockSpec((tm, d), lambda i: (i, 3)),
        pl.BlockSpec((tm, d), lambda i: (i, 4)),
        pl.BlockSpec((tm, d), row),
        pl.BlockSpec((8, CONV_CH), const),
        pl.BlockSpec((d, d), const), pl.BlockSpec((d, d), const), pl.BlockSpec((CONV_CH, d), const),
        pl.BlockSpec((d, d), const), pl.BlockSpec((1, d), const), pl.BlockSpec((1, d), const),
    ]
    return pl.pallas_call(
        functools.partial(_merge_kernel, tm=tm, per=per, alpha=alpha),
        out_shape=jax.ShapeDtypeStruct((t, d), F32),
        grid=(t // tm,),
        in_specs=in_specs,
        out_specs=pl.BlockSpec((tm, d), row),
        compiler_params=_cparams(("parallel",)),
        name="mixer_merge",
    )(ya, yb, zc, zc, zc, zc, zc, zc, zc, zc, h, cw, wa, wb, wc, wo, lg, lb)


def _xattn_kernel(h_ref, kv_ref, wq_ref, wo_ref, lg_ref, lb_ref, o_ref, ob_ref, *, alpha):
    h = h_ref[...]
    q = _dot(h.astype(BF16), wq_ref[...]).astype(BF16)
    kv = kv_ref[...]
    nh = XATTN_HEADS
    hd = XATTN_HEAD_DIM
    outs = []
    for hh in range(nh):
        s = _dot_t(q[:, hh * hd:(hh + 1) * hd], kv[:, hh * hd:(hh + 1) * hd]) * (hd ** -0.5)
        s = s - jnp.max(s, axis=-1, keepdims=True)
        e = jnp.exp(s)
        p = e / jnp.sum(e, axis=-1, keepdims=True)
        outs.append(_dot(p.astype(BF16), kv[:, (nh + hh) * hd:(nh + hh + 1) * hd]))
    o = jnp.concatenate(outs, axis=-1).astype(BF16)
    y = _layer_norm(alpha * h + _dot(o, wo_ref[...]), lg_ref[...], lb_ref[...])
    o_ref[...] = y
    ob_ref[...] = y.astype(BF16)


def _xattn(h, kv, wq, wo, lg, lb, seq, mlen, alpha):
    tm = 512
    t = h.shape[0]
    per = seq // tm
    d = D_MODEL
    row = lambda i: (i, 0)
    const = lambda i: (0, 0)
    return pl.pallas_call(
        functools.partial(_xattn_kernel, alpha=alpha),
        out_shape=(jax.ShapeDtypeStruct((t, d), F32), jax.ShapeDtypeStruct((t, d), BF16)),
        grid=(t // tm,),
        in_specs=[pl.BlockSpec((tm, d), row), pl.BlockSpec((mlen, kv.shape[1]), lambda i: (i // per, 0)),
                  pl.BlockSpec(wq.shape, const), pl.BlockSpec(wo.shape, const),
                  pl.BlockSpec((1, d), const), pl.BlockSpec((1, d), const)],
        out_specs=(pl.BlockSpec((tm, d), row), pl.BlockSpec((tm, d), row)),
        compiler_params=_cparams(("parallel",)),
        name="mem_xattn",
    )(h, kv, wq, wo, lg, lb)


def _expert_kernel(be_ref, x_ref, w13_ref, w2_ref, o_ref):
    del be_ref
    hmid = _dot(x_ref[...], w13_ref[0, 0].astype(BF16))
    a = hmid[:, :EXPERT_DIM]
    g = hmid[:, EXPERT_DIM:]
    act = (a * jax.nn.sigmoid(a) * g).astype(BF16)
    o_ref[...] = _dot(act, w2_ref[0, 0].astype(BF16)).astype(o_ref.dtype)


def _experts(xg, w13, w2, block_e, layer):
    n_pad, d = xg.shape
    rows = EXP_ROWS
    nb = n_pad // rows
    grid_spec = pltpu.PrefetchScalarGridSpec(
        num_scalar_prefetch=1,
        grid=(nb,),
        in_specs=[pl.BlockSpec((rows, d), lambda i, be: (i, 0)),
                  pl.BlockSpec((1, 1, d, 2 * EXPERT_DIM), lambda i, be: (layer, be[i], 0, 0)),
                  pl.BlockSpec((1, 1, EXPERT_DIM, d), lambda i, be: (layer, be[i], 0, 0))],
        out_specs=pl.BlockSpec((rows, d), lambda i, be: (i, 0)),
    )
    return pl.pallas_call(
        _expert_kernel,
        out_shape=jax.ShapeDtypeStruct((n_pad, d), BF16),
        grid_spec=grid_spec,
        compiler_params=_cparams(("arbitrary",)),
        name="moe_experts",
    )(block_e, xg, w13, w2)


def _moe_out_kernel(h_ref, y8_ref, wk_ref, w13_ref, w2_ref, lg_ref, lb_ref, o_ref, *, alpha):
    h = h_ref[...]
    hmid = _dot(h.astype(BF16), w13_ref[...])
    a = hmid[:, :EXPERT_DIM]
    g = hmid[:, EXPERT_DIM:]
    acc = alpha * h + _dot((a * jax.nn.sigmoid(a) * g).astype(BF16), w2_ref[...])
    wk = wk_ref[...]
    for k in range(TOP_K):
        acc = acc + wk[:, k:k + 1] * y8_ref[k].astype(F32)
    o_ref[...] = _layer_norm(acc, lg_ref[...], lb_ref[...])


def _moe_out(h, y8, wk, w13, w2, lg, lb, alpha):
    tm = 256
    t, d = h.shape
    row = lambda i: (i, 0)
    const = lambda i: (0, 0)
    return pl.pallas_call(
        functools.partial(_moe_out_kernel, alpha=alpha),
        out_shape=jax.ShapeDtypeStruct((t, d), F32),
        grid=(t // tm,),
        in_specs=[pl.BlockSpec((tm, d), row), pl.BlockSpec((TOP_K, tm, d), lambda i: (0, i, 0)),
                  pl.BlockSpec((tm, TOP_K), row),
                  pl.BlockSpec(w13.shape, const), pl.BlockSpec(w2.shape, const),
                  pl.BlockSpec((1, d), const), pl.BlockSpec((1, d), const)],
        out_specs=pl.BlockSpec((tm, d), row),
        compiler_params=_cparams(("parallel",)),
        name="moe_shared_ln",
    )(h, y8, wk, w13, w2, lg, lb)


def _dest_kernel(idx_ref, rank_ref, ps_ref, o_ref):
    idx = idx_ref[...]
    ps = ps_ref[...]
    tm = idx.shape[1]
    eidx = lax.broadcasted_iota(I32, (N_EXPERTS, tm), 0)
    rows = [jnp.sum(jnp.where(eidx == idx[k:k + 1, :], ps, 0.0), axis=0, keepdims=True) for k in range(TOP_K)]
    o_ref[...] = jnp.concatenate(rows, axis=0).astype(I32) + rank_ref[...]


def _dest(idx, rank, pstarts):
    tm = 1024
    t = idx.shape[1]
    tm = min(tm, t)
    col = lambda i: (0, i)
    return pl.pallas_call(
        _dest_kernel,
        out_shape=jax.ShapeDtypeStruct((TOP_K, t), I32),
        grid=(t // tm,),
        in_specs=[pl.BlockSpec((TOP_K, tm), col), pl.BlockSpec((TOP_K, tm), col),
                  pl.BlockSpec((N_EXPERTS, 1), lambda i: (0, 0))],
        out_specs=pl.BlockSpec((TOP_K, tm), col),
        compiler_params=_cparams(("parallel",)),
        name="moe_dest",
    )(idx, rank, pstarts)


def _router_kernel(h_ref, rwt_ref, rb_ref, tri_ref, idx_ref, w_ref, rank_ref, cnt_ref, carry_scr, *, tm):
    i = pl.program_id(0)

    @pl.when(i == 0)
    def _():
        carry_scr[...] = jnp.zeros(carry_scr.shape, F32)

    ne = N_EXPERTS
    gsz = ne // N_GROUPS
    s = jax.nn.sigmoid(_dot_t(rwt_ref[...], h_ref[...].astype(BF16)))
    sb = s + rb_ref[...]
    sb3 = sb.reshape(N_GROUPS, gsz, tm)
    li = lax.broadcasted_iota(I32, (N_GROUPS, gsz, tm), 1)
    m1 = jnp.max(sb3, axis=1, keepdims=True)
    first = jnp.min(jnp.where(sb3 == m1, li, gsz), axis=1, keepdims=True)
    m2 = jnp.max(jnp.where(li == first, -jnp.inf, sb3), axis=1, keepdims=True)
    gs = (m1 + m2).reshape(N_GROUPS, tm)
    gi = lax.broadcasted_iota(I32, (N_GROUPS, tm), 0)
    grank = jnp.zeros((N_GROUPS, tm), I32)
    for gp in range(N_GROUPS):
        row = gs[gp:gp + 1, :]
        grank = grank + ((row > gs) | ((row == gs) & (gp < gi))).astype(I32)
    gkeep = (grank < TOPK_GROUPS).astype(F32)
    ekeep = jnp.broadcast_to(gkeep[:, None, :], (N_GROUPS, gsz, tm)).reshape(ne, tm)
    cand = jnp.where(ekeep > 0.0, sb, NEG)
    eidx = lax.broadcasted_iota(I32, (ne, tm), 0)
    sel = jnp.zeros((ne, tm), F32)
    idxs, wts = [], []
    for _ in range(TOP_K):
        mx = jnp.max(cand, axis=0, keepdims=True)
        ik = jnp.min(jnp.where(cand == mx, eidx, ne), axis=0, keepdims=True)
        hit = eidx == ik
        wts.append(jnp.sum(jnp.where(hit, s, 0.0), axis=0, keepdims=True))
        idxs.append(ik)
        sel = jnp.where(hit, 1.0, sel)
        cand = jnp.where(hit, -jnp.inf, cand)
    before = _dot(sel.astype(BF16), tri_ref[...]) + carry_scr[...]
    ranks = [jnp.sum(jnp.where(eidx == ik, before, 0.0), axis=0, keepdims=True) for ik in idxs]
    carry_scr[...] = carry_scr[...] + jnp.sum(sel, axis=1, keepdims=True)
    w = jnp.concatenate(wts, axis=0)
    idx_ref[...] = jnp.concatenate(idxs, axis=0)
    w_ref[...] = w / jnp.sum(w, axis=0, keepdims=True) * ROUTE_SCALE
    rank_ref[...] = jnp.concatenate(ranks, axis=0).astype(I32)
    cnt_ref[...] = jnp.broadcast_to(carry_scr[...], cnt_ref.shape)


def _router(h, rwt, rb):
    tm = 256
    t, d = h.shape
    tri = jnp.asarray(np.triu(np.ones((tm, tm), np.float32), 1)).astype(BF16)
    const = lambda i: (0, 0)
    col = lambda i: (0, i)
    return pl.pallas_call(
        functools.partial(_router_kernel, tm=tm),
        out_shape=(jax.ShapeDtypeStruct((TOP_K, t), I32), jax.ShapeDtypeStruct((TOP_K, t), F32),
                   jax.ShapeDtypeStruct((TOP_K, t), I32), jax.ShapeDtypeStruct((N_EXPERTS, SLOT), F32)),
        grid=(t // tm,),
        in_specs=[pl.BlockSpec((tm, d), lambda i: (i, 0)), pl.BlockSpec((N_EXPERTS, d), const),
                  pl.BlockSpec((N_EXPERTS, 1), const), pl.BlockSpec((tm, tm), const)],
        out_specs=(pl.BlockSpec((TOP_K, tm), col), pl.BlockSpec((TOP_K, tm), col),
                   pl.BlockSpec((TOP_K, tm), col), pl.BlockSpec((N_EXPERTS, SLOT), const)),
        scratch_shapes=[pltpu.VMEM((N_EXPERTS, 1), F32)],
        compiler_params=_cparams(("arbitrary",)),
        name="moe_router",
    )(h, rwt, rb, tri)


def _moe(h, hb, router_w, router_b, exp_w13, exp_w2, layer, shared_w13, shared_w2, lg, lb, alpha):
    t, d = h.shape
    idx, wts, rank, cnt = _router(h, router_w.T, router_b.astype(F32).reshape(N_EXPERTS, 1))
    rows = EXP_ROWS
    n_a = t * TOP_K
    counts = cnt[:, 0].astype(I32)
    pcounts = (counts + rows - 1) // rows * rows
    pends = jnp.cumsum(pcounts)
    pstarts = pends - pcounts
    dest = _dest(idx, rank, pstarts.astype(F32).reshape(N_EXPERTS, 1))
    n_blocks = -(-n_a // rows) + N_EXPERTS
    n_pad = n_blocks * rows
    tok = jnp.broadcast_to(jnp.arange(t, dtype=I32)[None, :], (TOP_K, t))
    row_tok = (jnp.arange(n_pad, dtype=I32) % t).at[dest.reshape(-1)].set(tok.reshape(-1), unique_indices=True)
    block_e = jnp.minimum(jnp.searchsorted(pends, jnp.arange(n_blocks) * rows, side='right'),
                          N_EXPERTS - 1).astype(I32)
    xg = jnp.take(hb, row_tok, axis=0, mode="clip")
    out = _experts(xg, exp_w13, exp_w2, block_e, layer)
    y8 = jnp.take(out, dest, axis=0, mode="clip")
    return _moe_out(h, y8, wts.T, shared_w13, shared_w2, lg, lb, alpha)


def _pad_slots(w, scale=1.0):
    dm = w.shape[0]
    nh = w.shape[1] // HEAD_DIM
    w = (w * scale).reshape(dm, nh, HEAD_DIM)
    return jnp.concatenate([w, jnp.zeros_like(w)], axis=-1).reshape(dm, nh * SLOT)


def _pad_rows(w):
    n = w.shape[1]
    nh = w.shape[0] // HEAD_DIM
    w = w.reshape(nh, HEAD_DIM, n)
    return jnp.concatenate([w, jnp.zeros_like(w)], axis=1).reshape(nh * SLOT, n)


def _alibi(n):
    return np.exp2(-8.0 * np.arange(1, n + 1, dtype=np.float64) / n).astype(np.float32)


def _key_features(seq):
    p = np.arange(seq)
    slc = np.zeros((seq, SLOT), np.float32)
    slc[:, HEAD_DIM] = p % SLC_BLOCK
    blk = p // SLC_BLOCK
    nz = blk > 0
    slc[p[nz], HEAD_DIM + blk[nz]] = 1.0
    win = np.zeros((seq, SLOT), np.float32)
    win[:, HEAD_DIM] = p // 64
    win[:, HEAD_DIM + 1] = p % 64
    mob = np.zeros((seq, SLOT), np.float32)
    mob[p, HEAD_DIM + p // MOBA_BLOCK] = 1.0
    mob[:, HEAD_DIM + 16] = p % MOBA_BLOCK
    return np.concatenate([slc] * 2 + [win] * 2 + [mob] * MOBA_HEADS, axis=1)


def _query_features():
    sl = _alibi(NSA_HEADS)
    nsa = np.zeros((NSA_KV_HEADS, SLOT, 8), np.float32)
    for g in range(NSA_KV_HEADS):
        for r in range(NSA_GROUP):
            s = sl[g * NSA_GROUP + r]
            nsa[g, HEAD_DIM, r] = s
            nsa[g, HEAD_DIM + 1:, r] = s * SLC_BLOCK * np.arange(1, 64)
            nsa[g, HEAD_DIM, NSA_GROUP + r] = s * 64
            nsa[g, HEAD_DIM + 1, NSA_GROUP + r] = s
    sm = _alibi(MOBA_HEADS)
    mob = np.zeros((MOBA_HEADS, SLOT, 8), np.float32)
    for h in range(MOBA_HEADS):
        mob[h, HEAD_DIM:HEAD_DIM + 16, 0] = sm[h] * MOBA_BLOCK * np.arange(16)
        mob[h, HEAD_DIM + 16, 0] = sm[h]
    return nsa, mob


def _overlap_t(seq):
    nc = seq // CMP_STRIDE
    c_start = np.arange(nc) * CMP_STRIDE
    b_start = np.arange(64) * SLC_BLOCK
    ov = ((c_start[None, :] < (b_start + SLC_BLOCK)[:, None])
          & ((c_start + CMP_BLOCK)[None, :] > b_start[:, None])
          & (np.arange(nc) < nc - 1)[None, :] & (b_start < seq)[:, None])
    return ov.astype(np.float32)


def _compress_weights(pe, w1, w2):
    ty = np.array([0, 0, 1, 1])
    eye = jnp.eye(4, dtype=F32)
    w1r = w1.reshape(2, CMP_BLOCK, HEAD_DIM, CMP_HIDDEN)[ty]
    top = jnp.einsum('spdj,sS->psdSj', w1r[:, :CMP_STRIDE], eye).reshape(CMP_STRIDE * 256, 4 * CMP_HIDDEN)
    bot = jnp.einsum('spdj,sS->psdSj', w1r[:, CMP_STRIDE:], eye).reshape(CMP_STRIDE * 256, 4 * CMP_HIDDEN)
    per = pe[ty]
    pet = jnp.transpose(per[:, :CMP_STRIDE], (1, 0, 2)).reshape(1, CMP_STRIDE * 256)
    peb = jnp.transpose(per[:, CMP_STRIDE:], (1, 0, 2)).reshape(1, CMP_STRIDE * 256)
    w2p = jnp.concatenate([w2[ty], jnp.zeros((4, CMP_HIDDEN, SLOT - HEAD_DIM), F32)], axis=-1)
    w2b = jnp.einsum('sjd,sS->sjSd', w2p, eye).reshape(4 * CMP_HIDDEN, 4 * SLOT)
    w2k = w2b[:, :2 * SLOT].astype(BF16)
    w2vt = w2b[:, 2 * SLOT:].T.astype(BF16)
    return pet, peb, top.astype(BF16), bot.astype(BF16), w2k, w2vt


def kernel(x, mem, w_in, cmp_pe, cmp_w1, cmp_w2, conv_w, w_branch, w_out, ln1_g, ln1_b,
           xattn_wq, xattn_wkv, xattn_wo, ln2_g, ln2_b, router_w, router_b, exp_w13, exp_w2,
           shared_w13, shared_w2, ln3_g, ln3_b):
    bsz, seq, d = x.shape
    mlen = mem.shape[1]
    depth = w_in.shape[0]
    alpha = (2.0 * depth) ** 0.25
    t = bsz * seq
    scale = HEAD_DIM ** -0.5

    kfeat = jnp.asarray(_key_features(seq))
    fq_nsa, fq_moba = (jnp.asarray(a) for a in _query_features())
    ovt = jnp.asarray(_overlap_t(seq)).astype(BF16)
    memf = mem.reshape(bsz * mlen, d)

    h = x.reshape(t, d)
    ht = h.astype(BF16).T
    for l in range(depth):
        wi = w_in[l]
        kv6 = wi[:, _OFF_NSA_KV:_OFF_NSA_G].reshape(d, 6, NSA_KV_HEADS * HEAD_DIM)
        mq, mk, mv = (wi[:, _OFF_MOBA + j * 512:_OFF_MOBA + (j + 1) * 512] for j in range(3))
        w_at = jnp.concatenate([_pad_slots(wi[:, :512], scale), _pad_slots(kv6[:, 3]), _pad_slots(kv6[:, 5]),
                                _pad_slots(mq, scale), _pad_slots(mv)], axis=1).T.astype(BF16)
        w_b = jnp.concatenate([_pad_slots(kv6[:, 2]), _pad_slots(kv6[:, 4]), _pad_slots(mk)], axis=1).astype(BF16)
        wg = wi[:, _OFF_NSA_G:_OFF_MOBA].reshape(d, NSA_KV_HEADS, 12)
        wgt = jnp.concatenate([wg, jnp.zeros((d, NSA_KV_HEADS, SLOT - 12), F32)], axis=-1)
        wgt = wgt.reshape(d, NSA_KV_HEADS * SLOT).T.astype(BF16)
        w_c = jnp.concatenate([wi[:, _OFF_NSA_KV:_OFF_NSA_KV + 256], jnp.zeros((d, 256), F32),
                               wi[:, _OFF_CONV:]], axis=1).astype(BF16)

        zat = _mm(w_at, ht, BF16, tm=896, tn=1024, name="proj_at")
        gt = _mm(wgt, ht, F32, tm=256, tn=1024, name="proj_gate")
        zb = _mm(h, w_b, BF16, feats=kfeat, name="proj_b")
        zc = _mm(h, w_c, F32, name="proj_c")

        pet, peb, wt, wb, w2k, w2vt = _compress_weights(cmp_pe[l], cmp_w1[l], cmp_w2[l])
        sub = zc[:, :256].reshape(bsz, seq // CMP_STRIDE, CMP_STRIDE * 256)
        kc, vct = _compress(sub, pet, peb, wt, wb, w2k, w2vt)

        ya = _nsa_attention(zat, zb, gt, kc, vct, ovt, fq_nsa, bsz, seq)
        yb = _moba_attention(zat, zb, fq_moba, bsz, seq)

        cw = jnp.concatenate([conv_w[l], jnp.zeros((5, CONV_CH), F32)], axis=0)
        h = _merge(ya, yb, zc, h, cw, _pad_rows(w_branch[l, 0]).astype(BF16),
                   _pad_rows(w_branch[l, 1]).astype(BF16), w_branch[l, 2].astype(BF16),
                   w_out[l].astype(BF16), ln1_g[l][None], ln1_b[l][None], seq, alpha)

        kv = _mm(memf, xattn_wkv[l].astype(BF16), BF16, tm=512, name="xattn_kv")
        h, hb = _xattn(h, kv, xattn_wq[l].astype(BF16), xattn_wo[l].astype(BF16),
                       ln2_g[l][None], ln2_b[l][None], seq, mlen, alpha)

        h = _moe(h, hb, router_w[l].astype(BF16), router_b[l], exp_w13, exp_w2, l,
                 shared_w13[l].astype(BF16), shared_w2[l].astype(BF16), ln3_g[l][None], ln3_b[l][None], alpha)
        if l + 1 < depth:
            ht = h.astype(BF16).T
    return h.reshape(bsz, seq, d)
```

```python
import functools

import jax
import jax.numpy as jnp
import numpy as np
from jax import lax
from jax.experimental import pallas as pl
from jax.experimental.pallas import tpu as pltpu

F32 = jnp.float32
BF16 = jnp.bfloat16
I32 = jnp.int32

D_MODEL = 1024
HEAD_DIM = 64
SLOT = 128
NEG = -1e30
LN_EPS = 1e-5

NSA_HEADS = 8
NSA_KV_HEADS = 2
NSA_GROUP = 4
CMP_BLOCK = 32
CMP_STRIDE = 16
CMP_HIDDEN = 256
SLC_BLOCK = 64
SLC_TOPK = 16
WINDOW = 512
MOBA_HEADS = 8
MOBA_BLOCK = 256
MOBA_TOPK = 3
CONV_CH = 512
XATTN_HEADS = 4
XATTN_HEAD_DIM = 128
N_EXPERTS = 256
TOP_K = 8
N_GROUPS = 8
TOPK_GROUPS = 4
EXPERT_DIM = 256
ROUTE_SCALE = 2.5

TQ = 256
MOBA_HPS = 4
EXP_ROWS = 256
VMEM_LIMIT = 48 * 1024 * 1024

_OFF_NSA_Q = 0
_OFF_NSA_KV = 512
_OFF_NSA_G = 1280
_OFF_MOBA = 1304
_OFF_CONV = 2840
_OFF_MERGE = 4376

_SLOT_NSA_Q = 0
_SLOT_SLC_V = 8
_SLOT_WIN_V = 10
_SLOT_MOBA_Q = 12
_SLOT_MOBA_V = 20
_N_SLOTS_T = 28


def _cparams(sem):
    return pltpu.CompilerParams(dimension_semantics=sem, vmem_limit_bytes=VMEM_LIMIT)


def _dot(a, b):
    return jnp.dot(a, b, preferred_element_type=F32)


def _dot_t(a, b):
    return lax.dot_general(a, b, (((1,), (1,)), ((), ())), preferred_element_type=F32)


def _layer_norm(x, g, b):
    mu = jnp.mean(x, axis=-1, keepdims=True)
    xc = x - mu
    var = jnp.mean(xc * xc, axis=-1, keepdims=True)
    return xc * lax.rsqrt(var + LN_EPS) * g + b


def _mm_kernel(x_ref, w_ref, o_ref):
    o_ref[...] = _dot(x_ref[...].astype(BF16), w_ref[...]).astype(o_ref.dtype)


def _mm_feat_kernel(x_ref, w_ref, f_ref, o_ref):
    y = _dot(x_ref[...].astype(BF16), w_ref[...]) + f_ref[...]
    o_ref[...] = y.astype(o_ref.dtype)


def _mm(x, w, out_dtype, *, tm=1024, tn=512, feats=None, name):
    m, k = x.shape
    n = w.shape[1]
    tm = min(tm, m)
    tn = min(tn, n)
    assert m % tm == 0 and n % tn == 0, (m, n, tm, tn)
    in_specs = [pl.BlockSpec((tm, k), lambda i, j: (i, 0)),
                pl.BlockSpec((k, tn), lambda i, j: (0, j))]
    args = [x, w]
    if feats is None:
        body = _mm_kernel
    else:
        per = feats.shape[0] // tm
        assert feats.shape[0] % tm == 0
        in_specs.append(pl.BlockSpec((tm, tn), lambda i, j: (i % per, j)))
        args.append(feats)
        body = _mm_feat_kernel
    return pl.pallas_call(
        body,
        out_shape=jax.ShapeDtypeStruct((m, n), out_dtype),
        grid=(m // tm, n // tn),
        in_specs=in_specs,
        out_specs=pl.BlockSpec((tm, tn), lambda i, j: (i, j)),
        compiler_params=_cparams(("parallel", "parallel")),
        name=name,
    )(*args)


def _cmp_kernel(sub_ref, pet_ref, peb_ref, wt_ref, wb_ref, w2k_ref, w2vt_ref, kc_ref, vct_ref):
    sub = sub_ref[0]
    nc = sub.shape[0]
    a = _dot((sub + pet_ref[...]).astype(BF16), wt_ref[...])
    b = _dot((sub + peb_ref[...]).astype(BF16), wb_ref[...])
    hid = jax.nn.gelu(a + pltpu.roll(b, nc - 1, 0)).astype(BF16)
    kc_ref[0] = _dot(hid, w2k_ref[...]).astype(kc_ref.dtype)
    vct_ref[0] = _dot_t(w2vt_ref[...], hid).astype(vct_ref.dtype)


def _compress(sub, pet, peb, wt, wb, w2k, w2vt):
    bsz, nc, kk = sub.shape
    n_h = wt.shape[1]
    n_o = w2k.shape[1]
    const = lambda b: (0, 0)
    return pl.pallas_call(
        _cmp_kernel,
        out_shape=(jax.ShapeDtypeStruct((bsz, nc, n_o), BF16), jax.ShapeDtypeStruct((bsz, n_o, nc), BF16)),
        grid=(bsz,),
        in_specs=[pl.BlockSpec((1, nc, kk), lambda b: (b, 0, 0)),
                  pl.BlockSpec((1, kk), const), pl.BlockSpec((1, kk), const),
                  pl.BlockSpec((kk, n_h), const), pl.BlockSpec((kk, n_h), const),
                  pl.BlockSpec((n_h, n_o), const), pl.BlockSpec((n_o, n_h), const)],
        out_specs=(pl.BlockSpec((1, nc, n_o), lambda b: (b, 0, 0)), pl.BlockSpec((1, n_o, nc), lambda b: (b, 0, 0))),
        compiler_params=_cparams(("parallel",)),
        name="nsa_compress",
    )(sub, pet, peb, wt, wb, w2k, w2vt)


def _flash_reset(m_scr, l_scr, acc_scr):
    m_scr[...] = jnp.full(m_scr.shape, NEG, F32)
    l_scr[...] = jnp.zeros(l_scr.shape, F32)
    acc_scr[...] = jnp.zeros(acc_scr.shape, F32)


def _qk(kget, qxs, start, size):
    k0 = kget(0, start, size)
    if kget(1, start, size) is None:
        return _dot(k0, jnp.concatenate(qxs, axis=1))
    return jnp.concatenate([_dot(k0, qxs[0])] + [_dot(kget(c, start, size), qxs[c])
                                                 for c in range(1, len(qxs))], axis=1)


def _softmax_pv(s, vget, nq, start, size, m_scr, l_scr, acc_scr, mask=None):
    if mask is not None:
        s = jnp.where(mask, s, NEG)
    m_prev = m_scr[...]
    m_new = jnp.maximum(m_prev, jnp.max(s, axis=0, keepdims=True))
    alpha = jnp.exp(m_prev - m_new)
    p = jnp.exp(s - m_new)
    l_scr[...] = alpha * l_scr[...] + jnp.sum(p, axis=0, keepdims=True)
    p = p.astype(BF16)
    v0 = vget(0, start, size)
    if vget(1, start, size) is None:
        pv = _dot(v0, p)
    else:
        pv = jnp.concatenate([_dot(v0, p[:, :TQ])] + [_dot(vget(c, start, size), p[:, c * TQ:(c + 1) * TQ])
                                                      for c in range(1, nq)], axis=1)
    acc_scr[...] = alpha * acc_scr[...] + pv
    m_scr[...] = m_new


def _flash_step(kget, vget, qxs, start, size, m_scr, l_scr, acc_scr, mask=None):
    _softmax_pv(_qk(kget, qxs, start, size), vget, len(qxs), start, size, m_scr, l_scr, acc_scr, mask)


def _rows(ref, start, size, lane0=0):
    return ref[pl.ds(pl.multiple_of(start, TQ), size), lane0:lane0 + SLOT]


def _cols(ref, start, size, row0=0):
    return ref[row0:row0 + SLOT, pl.ds(pl.multiple_of(start, TQ), size)]


def _flash_causal(kget, vget, qxs, i, sa_scr, sb_scr, m_scr, l_scr, acc_scr):
    tq = TQ
    tk = 2 * tq
    nq = len(qxs)
    rr = nq * tq
    n_steps = i // 2 + 1
    sa_scr[...] = _qk(kget, qxs, 0, tk)

    def body(jj, carry):
        j = 2 * jj
        s = sa_scr[...]
        sb_scr[...] = _qk(kget, qxs, (j + 1) * tk, tk)
        _softmax_pv(s, vget, nq, j * tk, tk, m_scr, l_scr, acc_scr)
        s = sb_scr[...]
        sa_scr[...] = _qk(kget, qxs, (j + 2) * tk, tk)
        _softmax_pv(s, vget, nq, (j + 1) * tk, tk, m_scr, l_scr, acc_scr)
        return carry

    lax.fori_loop(0, (n_steps - 1) // 2, body, 0)
    last = n_steps - 1
    valid = (last * tk + lax.broadcasted_iota(I32, (tk, rr), 0)
             <= i * tq + (lax.broadcasted_iota(I32, (tk, rr), 1) & (tq - 1)))

    @pl.when(last % 2 == 0)
    def _():
        _softmax_pv(sa_scr[...], vget, nq, last * tk, tk, m_scr, l_scr, acc_scr, mask=valid)

    @pl.when(last % 2 == 1)
    def _():
        s = sa_scr[...]
        sb_scr[...] = _qk(kget, qxs, last * tk, tk)
        _softmax_pv(s, vget, nq, (last - 1) * tk, tk, m_scr, l_scr, acc_scr)
        _softmax_pv(sb_scr[...], vget, nq, last * tk, tk, m_scr, l_scr, acc_scr, mask=valid)


def _nsa_kernel(q_ref, kc_ref, vct_ref, ks_ref, vst_ref, kw_ref, vwt_ref, g_ref, ovt_ref, fq_ref,
                o_ref, m_scr, l_scr, acc_scr, imp_scr, sa_scr, sb_scr, *, tq, nsel):
    i = pl.program_id(2)
    r4 = NSA_GROUP
    rr = r4 * tq
    q4 = [q_ref[r * SLOT:(r + 1) * SLOT, :] for r in range(r4)]
    fq = fq_ref[0]
    p_loc = lax.broadcasted_iota(I32, (tq, rr), 0)
    t_loc = lax.broadcasted_iota(I32, (tq, rr), 1) & (tq - 1)
    causal = p_loc <= t_loc

    kc = kc_ref[0]
    nc = kc.shape[0]
    qs = jnp.concatenate(q4, axis=1)
    s = _dot(kc, qs)
    t_glob = i * tq + (lax.broadcasted_iota(I32, (nc, rr), 1) & (tq - 1))
    c_idx = lax.broadcasted_iota(I32, (nc, rr), 0)
    cmask = (c_idx * CMP_STRIDE + (CMP_BLOCK - 1)) <= t_glob
    s = jnp.where(cmask, s, NEG)
    mx = jnp.max(s, axis=0, keepdims=True)
    e = jnp.where(cmask, jnp.exp(s - mx), 0.0)
    lsum = jnp.sum(e, axis=0, keepdims=True)
    p_cmp = (e / jnp.where(lsum > 0.0, lsum, 1.0)).astype(BF16)
    o_cmp = _dot(vct_ref[0], p_cmp)

    ovt = ovt_ref[...]
    imp = _dot(ovt, p_cmp[:, 0:tq])
    for r in range(1, r4):
        imp = imp + _dot(ovt, p_cmp[:, r * tq:(r + 1) * tq])
    nbp = imp.shape[0]
    jidx = lax.broadcasted_iota(I32, (nbp, tq), 0)
    cur = (i * tq + lax.broadcasted_iota(I32, (nbp, tq), 1)) // SLC_BLOCK
    forced = (jidx == 0) | (jidx == cur) | (jidx == cur - 1)
    imp = jnp.where(forced, 1e6, jnp.where(jidx > cur, -1e6, imp))
    imp_scr[...] = imp

    def rank_body(jp, rank):
        row = imp_scr[pl.ds(jp, 1), :]
        better = (row > imp) | ((row == imp) & (jp < jidx))
        return rank + better.astype(I32)

    n_live = (i + 1) * (tq // SLC_BLOCK)
    rank = lax.fori_loop(0, n_live, rank_body, jnp.zeros((nbp, tq), I32))
    sel = (rank < nsel) & (jidx <= cur)
    mask_t = jnp.where(sel, 0.0, NEG)
    feat = jnp.concatenate([jnp.zeros((SLOT - nbp, tq), F32), mask_t], axis=0)

    qxs = [q4[r] + (feat + fq[:, r:r + 1]).astype(BF16) for r in range(r4)]
    kget = lambda c, start, size: _rows(ks_ref, start, size) if c == 0 else None
    vget = lambda c, start, size: _cols(vst_ref, start, size) if c == 0 else None
    _flash_reset(m_scr, l_scr, acc_scr)
    _flash_causal(kget, vget, qxs, i, sa_scr, sb_scr, m_scr, l_scr, acc_scr)
    o_slc = acc_scr[...] / l_scr[...]

    qxs = [q4[r] + fq[:, r4 + r:r4 + r + 1].astype(BF16) for r in range(r4)]
    kget = lambda c, start, size: _rows(kw_ref, start, size) if c == 0 else None
    vget = lambda c, start, size: _cols(vwt_ref, start, size) if c == 0 else None
    _flash_reset(m_scr, l_scr, acc_scr)

    @pl.when(i >= 2)
    def _():
        _flash_step(kget, vget, qxs, (i - 2) * tq, tq, m_scr, l_scr, acc_scr, mask=p_loc > t_loc)

    @pl.when(i >= 1)
    def _():
        _flash_step(kget, vget, qxs, (i - 1) * tq, tq, m_scr, l_scr, acc_scr)

    _flash_step(kget, vget, qxs, i * tq, tq, m_scr, l_scr, acc_scr, mask=causal)
    o_win = acc_scr[...] / l_scr[...]

    gate = jax.nn.sigmoid(g_ref[...])
    for r in range(r4):
        sl = slice(r * tq, (r + 1) * tq)
        o = (gate[3 * r:3 * r + 1, :] * o_cmp[:, sl] + gate[3 * r + 1:3 * r + 2, :] * o_slc[:, sl]
             + gate[3 * r + 2:3 * r + 3, :] * o_win[:, sl])
        o_ref[:, r * SLOT:(r + 1) * SLOT] = o.T.astype(o_ref.dtype)


def _nsa_attention(zat, zb, gt, kc, vct, ovt, fq, bsz, seq):
    tq = TQ
    ni = seq // tq
    nc = kc.shape[1]
    nsel = min(SLC_TOPK, seq // SLC_BLOCK)
    assert seq // SLC_BLOCK <= 64 and WINDOW == 2 * tq and ni % 2 == 0
    rr = NSA_GROUP * tq
    gw = NSA_GROUP * SLOT
    in_specs = [
        pl.BlockSpec((gw, tq), lambda b, g, i: (g, b * ni + i)),
        pl.BlockSpec((1, nc, SLOT), lambda b, g, i: (b, 0, g)),
        pl.BlockSpec((1, SLOT, nc), lambda b, g, i: (b, g, 0)),
        pl.BlockSpec((seq, SLOT), lambda b, g, i: (b, g)),
        pl.BlockSpec((SLOT, seq), lambda b, g, i: (_SLOT_SLC_V + g, b)),
        pl.BlockSpec((seq, SLOT), lambda b, g, i: (b, 2 + g)),
        pl.BlockSpec((SLOT, seq), lambda b, g, i: (_SLOT_WIN_V + g, b)),
        pl.BlockSpec((SLOT, tq), lambda b, g, i: (g, b * ni + i)),
        pl.BlockSpec(ovt.shape, lambda b, g, i: (0, 0)),
        pl.BlockSpec((1, SLOT, 8), lambda b, g, i: (g, 0, 0)),
    ]
    return pl.pallas_call(
        functools.partial(_nsa_kernel, tq=tq, nsel=nsel),
        out_shape=jax.ShapeDtypeStruct((bsz * seq, NSA_HEADS * SLOT), BF16),
        grid=(bsz, NSA_KV_HEADS, ni),
        in_specs=in_specs,
        out_specs=pl.BlockSpec((tq, gw), lambda b, g, i: (b * ni + i, g)),
        scratch_shapes=[pltpu.VMEM((1, rr), F32), pltpu.VMEM((1, rr), F32),
                        pltpu.VMEM((SLOT, rr), F32), pltpu.VMEM((64, tq), F32),
                        pltpu.VMEM((2 * tq, rr), F32), pltpu.VMEM((2 * tq, rr), F32)],
        compiler_params=_cparams(("parallel", "parallel", "arbitrary")),
        name="nsa_attention",
    )(zat, kc, vct, zb, zat, zb, zat, gt, ovt, fq)


def _moba_kernel(q_ref, k_ref, vt_ref, fq_ref, o_ref, m_scr, l_scr, acc_scr, km_scr, gate_scr,
                 sa_scr, sb_scr, *, tq, nblk, ntop):
    i = pl.program_id(2)
    nh = MOBA_HPS

    @pl.when(i == 0)
    def _():
        for hh in range(nh):
            kh = k_ref[:, hh * SLOT:(hh + 1) * SLOT].astype(F32)
            km = jnp.mean(kh.reshape(nblk, tq, SLOT), axis=1)
            if nblk < 16:
                km = jnp.concatenate([km, jnp.zeros((16 - nblk, SLOT), F32)], axis=0)
            km_scr[hh] = km

    qs = [q_ref[hh * SLOT:(hh + 1) * SLOT, :] for hh in range(nh)]
    gates = [_dot(km_scr[hh].astype(BF16), qs[hh]) for hh in range(nh)]
    for hh in range(nh):
        gate_scr[hh] = gates[hh]
    jidx = lax.broadcasted_iota(I32, (16, tq), 0)

    def rank_body(jp, ranks):
        out = []
        for hh in range(nh):
            row = gate_scr[hh, pl.ds(jp, 1), :]
            better = (row > gates[hh]) | ((row == gates[hh]) & (jp < jidx))
            out.append(ranks[hh] + better.astype(I32))
        return tuple(out)

    ranks = lax.fori_loop(0, i, rank_body, tuple(jnp.zeros((16, tq), I32) for _ in range(nh)))
    qxs = []
    for hh in range(nh):
        keep = ((jidx < i) & (ranks[hh] < ntop)) | (jidx == i)
        mask_t = jnp.where(keep, 0.0, NEG)
        feat = jnp.concatenate([jnp.zeros((HEAD_DIM, tq), F32), mask_t,
                                jnp.zeros((SLOT - HEAD_DIM - 16, tq), F32)], axis=0)
        qxs.append(qs[hh] + (feat + fq_ref[hh][:, 0:1]).astype(BF16))
    kget = lambda c, start, size: _rows(k_ref, start, size, c * SLOT)
    vget = lambda c, start, size: _cols(vt_ref, start, size, c * SLOT)
    _flash_reset(m_scr, l_scr, acc_scr)
    _flash_causal(kget, vget, qxs, i, sa_scr, sb_scr, m_scr, l_scr, acc_scr)
    o = acc_scr[...] / l_scr[...]
    for hh in range(nh):
        o_ref[:, hh * SLOT:(hh + 1) * SLOT] = o[:, hh * tq:(hh + 1) * tq].T.astype(o_ref.dtype)


def _moba_attention(zat, zb, fq, bsz, seq):
    tq = TQ
    assert tq == MOBA_BLOCK and seq % tq == 0
    ni = seq // tq
    assert ni <= 16 and ni % 2 == 0
    ntop = min(MOBA_TOPK, ni)
    nh = MOBA_HPS
    hw = nh * SLOT
    q0 = _SLOT_MOBA_Q // nh
    v0 = _SLOT_MOBA_V // nh
    return pl.pallas_call(
        functools.partial(_moba_kernel, tq=tq, nblk=ni, ntop=ntop),
        out_shape=jax.ShapeDtypeStruct((bsz * seq, MOBA_HEADS * SLOT), BF16),
        grid=(bsz, MOBA_HEADS // nh, ni),
        in_specs=[pl.BlockSpec((hw, tq), lambda b, h, i: (q0 + h, b * ni + i)),
                  pl.BlockSpec((seq, hw), lambda b, h, i: (b, 1 + h)),
                  pl.BlockSpec((hw, seq), lambda b, h, i: (v0 + h, b)),
                  pl.BlockSpec((nh, SLOT, 8), lambda b, h, i: (h, 0, 0))],
        out_specs=pl.BlockSpec((tq, hw), lambda b, h, i: (b * ni + i, h)),
        scratch_shapes=[pltpu.VMEM((1, nh * tq), F32), pltpu.VMEM((1, nh * tq), F32),
                        pltpu.VMEM((SLOT, nh * tq), F32),
                        pltpu.VMEM((nh, 16, SLOT), F32), pltpu.VMEM((nh, 16, tq), F32),
                        pltpu.VMEM((2 * tq, nh * tq), F32), pltpu.VMEM((2 * tq, nh * tq), F32)],
        compiler_params=_cparams(("parallel", "parallel", "arbitrary")),
        name="moba_attention",
    )(zat, zb, zat, fq)


def _merge_kernel(ya_ref, yb_ref, hc_ref, bc_ref, cc_ref, hcp_ref, ccp_ref, g0_ref, g1_ref, g2_ref,
                  h_ref, cw_ref, wa_ref, wb_ref, wc_ref, wo_ref, lg_ref, lb_ref, o_ref, *, tm, per, alpha):
    i = pl.program_id(0)
    u = cc_ref[...] * hc_ref[...]
    first = (i % per) == 0
    up = jnp.where(first, 0.0, ccp_ref[...] * hcp_ref[...])
    rowi = lax.broadcasted_iota(I32, u.shape, 0)
    u1 = jnp.where(rowi == 0, up[7:8, :], pltpu.roll(u, 1, 0))
    u2 = jnp.where(rowi == 0, up[6:7, :], jnp.where(rowi == 1, up[7:8, :], pltpu.roll(u, 2, 0)))
    cw = cw_ref[...]
    yc = bc_ref[...] * (cw[0:1, :] * u2 + cw[1:2, :] * u1 + cw[2:3, :] * u)
    merged = (jax.nn.sigmoid(g0_ref[...]) * _dot(ya_ref[...], wa_ref[...])
              + jax.nn.sigmoid(g1_ref[...]) * _dot(yb_ref[...], wb_ref[...])
              + jax.nn.sigmoid(g2_ref[...]) * _dot(yc.astype(BF16), wc_ref[...]))
    mix = _dot(merged.astype(BF16), wo_ref[...])
    o_ref[...] = _layer_norm(alpha * h_ref[...] + mix, lg_ref[...], lb_ref[...])


def _merge(ya, yb, zc, h, cw, wa, wb, wc, wo, lg, lb, seq, alpha):
    tm = 512
    t = h.shape[0]
    per = seq // tm
    d = D_MODEL
    row = lambda i: (i, 0)
    const = lambda i: (0, 0)
    prev = lambda c: (lambda i: (jnp.maximum(i * (tm // 8) - 1, 0), c))
    in_specs = [
        pl.BlockSpec((tm, d), row), pl.BlockSpec((tm, d), row),
        pl.BlockSpec((tm, CONV_CH), lambda i: (i, 1)), pl.BlockSpec((tm, CONV_CH), lambda i: (i, 2)),
        pl.BlockSpec((tm, CONV_CH), lambda i: (i, 3)),
        pl.BlockSpec((8, CONV_CH), prev(1)), pl.BlockSpec((8, CONV_CH), prev(3)),
        pl.BlockSpec((tm, d), lambda i: (i, 2)), pl.BlockSpec((tm, d), lambda i: (i, 3)),
        pl.BlockSpec((tm, d), lambda i: (i, 4)),
        pl.BlockSpec((tm, d), row),
        pl.BlockSpec((8, CONV_CH), const),
        pl.BlockSpec((d, d), const), pl.BlockSpec((d, d), const), pl.BlockSpec((CONV_CH, d), const),
        pl.BlockSpec((d, d), const), pl.BlockSpec((1, d), const), pl.BlockSpec((1, d), const),
    ]
    return pl.pallas_call(
        functools.partial(_merge_kernel, tm=tm, per=per, alpha=alpha),
        out_shape=jax.ShapeDtypeStruct((t, d), F32),
        grid=(t // tm,),
        in_specs=in_specs,
        out_specs=pl.BlockSpec((tm, d), row),
        compiler_params=_cparams(("parallel",)),
        name="mixer_merge",
    )(ya, yb, zc, zc, zc, zc, zc, zc, zc, zc, h, cw, wa, wb, wc, wo, lg, lb)


def _xattn_kernel(h_ref, kv_ref, wq_ref, wo_ref, lg_ref, lb_ref, o_ref, ob_ref, *, alpha):
    h = h_ref[...]
    q = _dot(h.astype(BF16), wq_ref[...]).astype(BF16)
    kv = kv_ref[...]
    nh = XATTN_HEADS
    hd = XATTN_HEAD_DIM
    outs = []
    for hh in range(nh):
        s = _dot_t(q[:, hh * hd:(hh + 1) * hd], kv[:, hh * hd:(hh + 1) * hd]) * (hd ** -0.5)
        s = s - jnp.max(s, axis=-1, keepdims=True)
        e = jnp.exp(s)
        p = e / jnp.sum(e, axis=-1, keepdims=True)
        outs.append(_dot(p.astype(BF16), kv[:, (nh + hh) * hd:(nh + hh + 1) * hd]))
    o = jnp.concatenate(outs, axis=-1).astype(BF16)
    y = _layer_norm(alpha * h + _dot(o, wo_ref[...]), lg_ref[...], lb_ref[...])
    o_ref[...] = y
    ob_ref[...] = y.astype(BF16)


def _xattn(h, kv, wq, wo, lg, lb, seq, mlen, alpha):
    tm = 512
    t = h.shape[0]
    per = seq // tm
    d = D_MODEL
    row = lambda i: (i, 0)
    const = lambda i: (0, 0)
    return pl.pallas_call(
        functools.partial(_xattn_kernel, alpha=alpha),
        out_shape=(jax.ShapeDtypeStruct((t, d), F32), jax.ShapeDtypeStruct((t, d), BF16)),
        grid=(t // tm,),
        in_specs=[pl.BlockSpec((tm, d), row), pl.BlockSpec((mlen, kv.shape[1]), lambda i: (i // per, 0)),
                  pl.BlockSpec(wq.shape, const), pl.BlockSpec(wo.shape, const),
                  pl.BlockSpec((1, d), const), pl.BlockSpec((1, d), const)],
        out_specs=(pl.BlockSpec((tm, d), row), pl.BlockSpec((tm, d), row)),
        compiler_params=_cparams(("parallel",)),
        name="mem_xattn",
    )(h, kv, wq, wo, lg, lb)


def _expert_kernel(be_ref, x_ref, w13_ref, w2_ref, o_ref, w13b_scr, w2b_scr):
    i = pl.program_id(0)

    @pl.when((i == 0) | (be_ref[i] != be_ref[jnp.maximum(i - 1, 0)]))
    def _():
        w13b_scr[...] = w13_ref[0, 0].astype(BF16)
        w2b_scr[...] = w2_ref[0, 0].astype(BF16)

    hmid = _dot(x_ref[...], w13b_scr[...])
    a = hmid[:, :EXPERT_DIM]
    g = hmid[:, EXPERT_DIM:]
    act = (a * jax.nn.sigmoid(a) * g).astype(BF16)
    o_ref[...] = _dot(act, w2b_scr[...]).astype(o_ref.dtype)


def _experts(xg, w13, w2, block_e, layer):
    n_pad, d = xg.shape
    rows = EXP_ROWS
    nb = n_pad // rows
    grid_spec = pltpu.PrefetchScalarGridSpec(
        num_scalar_prefetch=1,
        grid=(nb,),
        in_specs=[pl.BlockSpec((rows, d), lambda i, be: (i, 0)),
                  pl.BlockSpec((1, 1, d, 2 * EXPERT_DIM), lambda i, be: (layer, be[i], 0, 0)),
                  pl.BlockSpec((1, 1, EXPERT_DIM, d), lambda i, be: (layer, be[i], 0, 0))],
        out_specs=pl.BlockSpec((rows, d), lambda i, be: (i, 0)),
        scratch_shapes=[pltpu.VMEM((d, 2 * EXPERT_DIM), BF16), pltpu.VMEM((EXPERT_DIM, d), BF16)],
    )
    return pl.pallas_call(
        _expert_kernel,
        out_shape=jax.ShapeDtypeStruct((n_pad, d), BF16),
        grid_spec=grid_spec,
        compiler_params=_cparams(("arbitrary",)),
        name="moe_experts",
    )(block_e, xg, w13, w2)


def _moe_out_kernel(h_ref, y8_ref, wk_ref, w13_ref, w2_ref, lg_ref, lb_ref, o_ref, *, alpha):
    h = h_ref[...]
    hmid = _dot(h.astype(BF16), w13_ref[...])
    a = hmid[:, :EXPERT_DIM]
    g = hmid[:, EXPERT_DIM:]
    acc = alpha * h + _dot((a * jax.nn.sigmoid(a) * g).astype(BF16), w2_ref[...])
    wk = wk_ref[...]
    for k in range(TOP_K):
        acc = acc + wk[:, k:k + 1] * y8_ref[k].astype(F32)
    o_ref[...] = _layer_norm(acc, lg_ref[...], lb_ref[...])


def _moe_out(h, y8, wk, w13, w2, lg, lb, alpha):
    tm = 256
    t, d = h.shape
    row = lambda i: (i, 0)
    const = lambda i: (0, 0)
    return pl.pallas_call(
        functools.partial(_moe_out_kernel, alpha=alpha),
        out_shape=jax.ShapeDtypeStruct((t, d), F32),
        grid=(t // tm,),
        in_specs=[pl.BlockSpec((tm, d), row), pl.BlockSpec((TOP_K, tm, d), lambda i: (0, i, 0)),
                  pl.BlockSpec((tm, TOP_K), row),
                  pl.BlockSpec(w13.shape, const), pl.BlockSpec(w2.shape, const),
                  pl.BlockSpec((1, d), const), pl.BlockSpec((1, d), const)],
        out_specs=pl.BlockSpec((tm, d), row),
        compiler_params=_cparams(("parallel",)),
        name="moe_shared_ln",
    )(h, y8, wk, w13, w2, lg, lb)


def _dest_kernel(idx_ref, rank_ref, ps_ref, o_ref):
    idx = idx_ref[...]
    ps = ps_ref[...]
    tm = idx.shape[1]
    eidx = lax.broadcasted_iota(I32, (N_EXPERTS, tm), 0)
    rows = [jnp.sum(jnp.where(eidx == idx[k:k + 1, :], ps, 0.0), axis=0, keepdims=True) for k in range(TOP_K)]
    o_ref[...] = jnp.concatenate(rows, axis=0).astype(I32) + rank_ref[...]


def _dest(idx, rank, pstarts):
    tm = 1024
    t = idx.shape[1]
    tm = min(tm, t)
    col = lambda i: (0, i)
    return pl.pallas_call(
        _dest_kernel,
        out_shape=jax.ShapeDtypeStruct((TOP_K, t), I32),
        grid=(t // tm,),
        in_specs=[pl.BlockSpec((TOP_K, tm), col), pl.BlockSpec((TOP_K, tm), col),
                  pl.BlockSpec((N_EXPERTS, 1), lambda i: (0, 0))],
        out_specs=pl.BlockSpec((TOP_K, tm), col),
        compiler_params=_cparams(("parallel",)),
        name="moe_dest",
    )(idx, rank, pstarts)


def _router_kernel(h_ref, rwt_ref, rb_ref, tri_ref, idx_ref, w_ref, rank_ref, cnt_ref, carry_scr, *, tm):
    i = pl.program_id(0)

    @pl.when(i == 0)
    def _():
        carry_scr[...] = jnp.zeros(carry_scr.shape, F32)

    ne = N_EXPERTS
    gsz = ne // N_GROUPS
    s = jax.nn.sigmoid(_dot_t(rwt_ref[...], h_ref[...].astype(BF16)))
    sb = s + rb_ref[...]
    sb3 = sb.reshape(N_GROUPS, gsz, tm)
    li = lax.broadcasted_iota(I32, (N_GROUPS, gsz, tm), 1)
    m1 = jnp.max(sb3, axis=1, keepdims=True)
    first = jnp.min(jnp.where(sb3 == m1, li, gsz), axis=1, keepdims=True)
    m2 = jnp.max(jnp.where(li == first, -jnp.inf, sb3), axis=1, keepdims=True)
    gs = (m1 + m2).reshape(N_GROUPS, tm)
    gi = lax.broadcasted_iota(I32, (N_GROUPS, tm), 0)
    grank = jnp.zeros((N_GROUPS, tm), I32)
    for gp in range(N_GROUPS):
        row = gs[gp:gp + 1, :]
        grank = grank + ((row > gs) | ((row == gs) & (gp < gi))).astype(I32)
    gkeep = (grank < TOPK_GROUPS).astype(F32)
    ekeep = jnp.broadcast_to(gkeep[:, None, :], (N_GROUPS, gsz, tm)).reshape(ne, tm)
    cand = jnp.where(ekeep > 0.0, sb, NEG)
    eidx = lax.broadcasted_iota(I32, (ne, tm), 0)
    sel = jnp.zeros((ne, tm), F32)
    idxs, wts = [], []
    for _ in range(TOP_K):
        mx = jnp.max(cand, axis=0, keepdims=True)
        ik = jnp.min(jnp.where(cand == mx, eidx, ne), axis=0, keepdims=True)
        hit = eidx == ik
        wts.append(jnp.sum(jnp.where(hit, s, 0.0), axis=0, keepdims=True))
        idxs.append(ik)
        sel = jnp.where(hit, 1.0, sel)
        cand = jnp.where(hit, -jnp.inf, cand)
    before = _dot(sel.astype(BF16), tri_ref[...]) + carry_scr[...]
    ranks = [jnp.sum(jnp.where(eidx == ik, before, 0.0), axis=0, keepdims=True) for ik in idxs]
    carry_scr[...] = carry_scr[...] + jnp.sum(sel, axis=1, keepdims=True)
    w = jnp.concatenate(wts, axis=0)
    idx_ref[...] = jnp.concatenate(idxs, axis=0)
    w_ref[...] = w / jnp.sum(w, axis=0, keepdims=True) * ROUTE_SCALE
    rank_ref[...] = jnp.concatenate(ranks, axis=0).astype(I32)
    cnt_ref[...] = jnp.broadcast_to(carry_scr[...], cnt_ref.shape)


def _router(h, rwt, rb):
    tm = 256
    t, d = h.shape
    tri = jnp.asarray(np.triu(np.ones((tm, tm), np.float32), 1)).astype(BF16)
    const = lambda i: (0, 0)
    col = lambda i: (0, i)
    return pl.pallas_call(
        functools.partial(_router_kernel, tm=tm),
        out_shape=(jax.ShapeDtypeStruct((TOP_K, t), I32), jax.ShapeDtypeStruct((TOP_K, t), F32),
                   jax.ShapeDtypeStruct((TOP_K, t), I32), jax.ShapeDtypeStruct((N_EXPERTS, SLOT), F32)),
        grid=(t // tm,),
        in_specs=[pl.BlockSpec((tm, d), lambda i: (i, 0)), pl.BlockSpec((N_EXPERTS, d), const),
                  pl.BlockSpec((N_EXPERTS, 1), const), pl.BlockSpec((tm, tm), const)],
        out_specs=(pl.BlockSpec((TOP_K, tm), col), pl.BlockSpec((TOP_K, tm), col),
                   pl.BlockSpec((TOP_K, tm), col), pl.BlockSpec((N_EXPERTS, SLOT), const)),
        scratch_shapes=[pltpu.VMEM((N_EXPERTS, 1), F32)],
        compiler_params=_cparams(("arbitrary",)),
        name="moe_router",
    )(h, rwt, rb, tri)


def _moe(h, hb, router_w, router_b, exp_w13, exp_w2, layer, shared_w13, shared_w2, lg, lb, alpha):
    t, d = h.shape
    idx, wts, rank, cnt = _router(h, router_w.T, router_b.astype(F32).reshape(N_EXPERTS, 1))
    rows = EXP_ROWS
    n_a = t * TOP_K
    counts = cnt[:, 0].astype(I32)
    pcounts = (counts + rows - 1) // rows * rows
    pends = jnp.cumsum(pcounts)
    pstarts = pends - pcounts
    dest = _dest(idx, rank, pstarts.astype(F32).reshape(N_EXPERTS, 1))
    n_blocks = -(-n_a // rows) + N_EXPERTS
    n_pad = n_blocks * rows
    tok = jnp.broadcast_to(jnp.arange(t, dtype=I32)[None, :], (TOP_K, t))
    row_tok = (jnp.arange(n_pad, dtype=I32) % t).at[dest.reshape(-1)].set(tok.reshape(-1), unique_indices=True)
    block_e = jnp.minimum(jnp.searchsorted(pends, jnp.arange(n_blocks) * rows, side='right'),
                          N_EXPERTS - 1).astype(I32)
    xg = jnp.take(hb, row_tok, axis=0, mode="clip")
    out = _experts(xg, exp_w13, exp_w2, block_e, layer)
    y8 = jnp.take(out, dest, axis=0, mode="clip")
    return _moe_out(h, y8, wts.T, shared_w13, shared_w2, lg, lb, alpha)


def _pad_slots(w, scale=1.0):
    dm = w.shape[0]
    nh = w.shape[1] // HEAD_DIM
    w = (w * scale).reshape(dm, nh, HEAD_DIM)
    return jnp.concatenate([w, jnp.zeros_like(w)], axis=-1).reshape(dm, nh * SLOT)


def _pad_rows(w):
    n = w.shape[1]
    nh = w.shape[0] // HEAD_DIM
    w = w.reshape(nh, HEAD_DIM, n)
    return jnp.concatenate([w, jnp.zeros_like(w)], axis=1).reshape(nh * SLOT, n)


def _alibi(n):
    return np.exp2(-8.0 * np.arange(1, n + 1, dtype=np.float64) / n).astype(np.float32)


def _key_features(seq):
    p = np.arange(seq)
    slc = np.zeros((seq, SLOT), np.float32)
    slc[:, HEAD_DIM] = p % SLC_BLOCK
    blk = p // SLC_BLOCK
    nz = blk > 0
    slc[p[nz], HEAD_DIM + blk[nz]] = 1.0
    win = np.zeros((seq, SLOT), np.float32)
    win[:, HEAD_DIM] = p // 64
    win[:, HEAD_DIM + 1] = p % 64
    mob = np.zeros((seq, SLOT), np.float32)
    mob[p, HEAD_DIM + p // MOBA_BLOCK] = 1.0
    mob[:, HEAD_DIM + 16] = p % MOBA_BLOCK
    return np.concatenate([slc] * 2 + [win] * 2 + [mob] * MOBA_HEADS, axis=1)


def _query_features():
    sl = _alibi(NSA_HEADS)
    nsa = np.zeros((NSA_KV_HEADS, SLOT, 8), np.float32)
    for g in range(NSA_KV_HEADS):
        for r in range(NSA_GROUP):
            s = sl[g * NSA_GROUP + r]
            nsa[g, HEAD_DIM, r] = s
            nsa[g, HEAD_DIM + 1:, r] = s * SLC_BLOCK * np.arange(1, 64)
            nsa[g, HEAD_DIM, NSA_GROUP + r] = s * 64
            nsa[g, HEAD_DIM + 1, NSA_GROUP + r] = s
    sm = _alibi(MOBA_HEADS)
    mob = np.zeros((MOBA_HEADS, SLOT, 8), np.float32)
    for h in range(MOBA_HEADS):
        mob[h, HEAD_DIM:HEAD_DIM + 16, 0] = sm[h] * MOBA_BLOCK * np.arange(16)
        mob[h, HEAD_DIM + 16, 0] = sm[h]
    return nsa, mob


def _overlap_t(seq):
    nc = seq // CMP_STRIDE
    c_start = np.arange(nc) * CMP_STRIDE
    b_start = np.arange(64) * SLC_BLOCK
    ov = ((c_start[None, :] < (b_start + SLC_BLOCK)[:, None])
          & ((c_start + CMP_BLOCK)[None, :] > b_start[:, None])
          & (np.arange(nc) < nc - 1)[None, :] & (b_start < seq)[:, None])
    return ov.astype(np.float32)


def _compress_weights(pe, w1, w2):
    ty = np.array([0, 0, 1, 1])
    eye = jnp.eye(4, dtype=F32)
    w1r = w1.reshape(2, CMP_BLOCK, HEAD_DIM, CMP_HIDDEN)[ty]
    top = jnp.einsum('spdj,sS->psdSj', w1r[:, :CMP_STRIDE], eye).reshape(CMP_STRIDE * 256, 4 * CMP_HIDDEN)
    bot = jnp.einsum('spdj,sS->psdSj', w1r[:, CMP_STRIDE:], eye).reshape(CMP_STRIDE * 256, 4 * CMP_HIDDEN)
    per = pe[ty]
    pet = jnp.transpose(per[:, :CMP_STRIDE], (1, 0, 2)).reshape(1, CMP_STRIDE * 256)
    peb = jnp.transpose(per[:, CMP_STRIDE:], (1, 0, 2)).reshape(1, CMP_STRIDE * 256)
    w2p = jnp.concatenate([w2[ty], jnp.zeros((4, CMP_HIDDEN, SLOT - HEAD_DIM), F32)], axis=-1)
    w2b = jnp.einsum('sjd,sS->sjSd', w2p, eye).reshape(4 * CMP_HIDDEN, 4 * SLOT)
    w2k = w2b[:, :2 * SLOT].astype(BF16)
    w2vt = w2b[:, 2 * SLOT:].T.astype(BF16)
    return pet, peb, top.astype(BF16), bot.astype(BF16), w2k, w2vt


def kernel(x, mem, w_in, cmp_pe, cmp_w1, cmp_w2, conv_w, w_branch, w_out, ln1_g, ln1_b,
           xattn_wq, xattn_wkv, xattn_wo, ln2_g, ln2_b, router_w, router_b, exp_w13, exp_w2,
           shared_w13, shared_w2, ln3_g, ln3_b):
    bsz, seq, d = x.shape
    mlen = mem.shape[1]
    depth = w_in.shape[0]
    alpha = (2.0 * depth) ** 0.25
    t = bsz * seq
    scale = HEAD_DIM ** -0.5

    kfeat = jnp.asarray(_key_features(seq))
    fq_nsa, fq_moba = (jnp.asarray(a) for a in _query_features())
    ovt = jnp.asarray(_overlap_t(seq)).astype(BF16)
    memf = mem.reshape(bsz * mlen, d)

    h = x.reshape(t, d)
    ht = h.astype(BF16).T
    for l in range(depth):
        wi = w_in[l]
        kv6 = wi[:, _OFF_NSA_KV:_OFF_NSA_G].reshape(d, 6, NSA_KV_HEADS * HEAD_DIM)
        mq, mk, mv = (wi[:, _OFF_MOBA + j * 512:_OFF_MOBA + (j + 1) * 512] for j in range(3))
        w_at = jnp.concatenate([_pad_slots(wi[:, :512], scale), _pad_slots(kv6[:, 3]), _pad_slots(kv6[:, 5]),
                                _pad_slots(mq, scale), _pad_slots(mv)], axis=1).T.astype(BF16)
        w_b = jnp.concatenate([_pad_slots(kv6[:, 2]), _pad_slots(kv6[:, 4]), _pad_slots(mk)], axis=1).astype(BF16)
        wg = wi[:, _OFF_NSA_G:_OFF_MOBA].reshape(d, NSA_KV_HEADS, 12)
        wgt = jnp.concatenate([wg, jnp.zeros((d, NSA_KV_HEADS, SLOT - 12), F32)], axis=-1)
        wgt = wgt.reshape(d, NSA_KV_HEADS * SLOT).T.astype(BF16)
        w_c = jnp.concatenate([wi[:, _OFF_NSA_KV:_OFF_NSA_KV + 256], jnp.zeros((d, 256), F32),
                               wi[:, _OFF_CONV:]], axis=1).astype(BF16)

        zat = _mm(w_at, ht, BF16, tm=896, tn=1024, name="proj_at")
        gt = _mm(wgt, ht, F32, tm=256, tn=1024, name="proj_gate")
        zb = _mm(h, w_b, BF16, feats=kfeat, name="proj_b")
        zc = _mm(h, w_c, F32, name="proj_c")

        pet, peb, wt, wb, w2k, w2vt = _compress_weights(cmp_pe[l], cmp_w1[l], cmp_w2[l])
        sub = zc[:, :256].reshape(bsz, seq // CMP_STRIDE, CMP_STRIDE * 256)
        kc, vct = _compress(sub, pet, peb, wt, wb, w2k, w2vt)

        ya = _nsa_attention(zat, zb, gt, kc, vct, ovt, fq_nsa, bsz, seq)
        yb = _moba_attention(zat, zb, fq_moba, bsz, seq)

        cw = jnp.concatenate([conv_w[l], jnp.zeros((5, CONV_CH), F32)], axis=0)
        h = _merge(ya, yb, zc, h, cw, _pad_rows(w_branch[l, 0]).astype(BF16),
                   _pad_rows(w_branch[l, 1]).astype(BF16), w_branch[l, 2].astype(BF16),
                   w_out[l].astype(BF16), ln1_g[l][None], ln1_b[l][None], seq, alpha)

        kv = _mm(memf, xattn_wkv[l].astype(BF16), BF16, tm=512, name="xattn_kv")
        h, hb = _xattn(h, kv, xattn_wq[l].astype(BF16), xattn_wo[l].astype(BF16),
                       ln2_g[l][None], ln2_b[l][None], seq, mlen, alpha)

        h = _moe(h, hb, router_w[l].astype(BF16), router_b[l], exp_w13, exp_w2, l,
                 shared_w13[l].astype(BF16), shared_w2[l].astype(BF16), ln3_g[l][None], ln3_b[l][None], alpha)
        if l + 1 < depth:
            ht = h.astype(BF16).T
    return h.reshape(bsz, seq, d)
```

```python
import functools

import jax
import jax.numpy as jnp
import numpy as np
from jax import lax
from jax.experimental import pallas as pl
from jax.experimental.pallas import tpu as pltpu

F32 = jnp.float32
BF16 = jnp.bfloat16
I32 = jnp.int32

D_MODEL = 1024
HEAD_DIM = 64
SLOT = 128
NEG = -1e30
LN_EPS = 1e-5

NSA_HEADS = 8
NSA_KV_HEADS = 2
NSA_GROUP = 4
CMP_BLOCK = 32
CMP_STRIDE = 16
CMP_HIDDEN = 256
SLC_BLOCK = 64
SLC_TOPK = 16
WINDOW = 512
MOBA_HEADS = 8
MOBA_BLOCK = 256
MOBA_TOPK = 3
CONV_CH = 512
XATTN_HEADS = 4
XATTN_HEAD_DIM = 128
N_EXPERTS = 256
TOP_K = 8
N_GROUPS = 8
TOPK_GROUPS = 4
EXPERT_DIM = 256
ROUTE_SCALE = 2.5

TQ = 256
MOBA_HPS = 4
EXP_ROWS = 512
VMEM_LIMIT = 48 * 1024 * 1024

_OFF_NSA_Q = 0
_OFF_NSA_KV = 512
_OFF_NSA_G = 1280
_OFF_MOBA = 1304
_OFF_CONV = 2840
_OFF_MERGE = 4376

_SLOT_NSA_Q = 0
_SLOT_SLC_V = 8
_SLOT_WIN_V = 10
_SLOT_MOBA_Q = 12
_SLOT_MOBA_V = 20
_N_SLOTS_T = 28


def _cparams(sem):
    return pltpu.CompilerParams(dimension_semantics=sem, vmem_limit_bytes=VMEM_LIMIT)


def _dot(a, b):
    return jnp.dot(a, b, preferred_element_type=F32)


def _dot_t(a, b):
    return lax.dot_general(a, b, (((1,), (1,)), ((), ())), preferred_element_type=F32)


def _layer_norm(x, g, b):
    mu = jnp.mean(x, axis=-1, keepdims=True)
    xc = x - mu
    var = jnp.mean(xc * xc, axis=-1, keepdims=True)
    return xc * lax.rsqrt(var + LN_EPS) * g + b


def _mm_kernel(x_ref, w_ref, o_ref):
    o_ref[...] = _dot(x_ref[...].astype(BF16), w_ref[...]).astype(o_ref.dtype)


def _mm_feat_kernel(x_ref, w_ref, f_ref, o_ref):
    y = _dot(x_ref[...].astype(BF16), w_ref[...]) + f_ref[...]
    o_ref[...] = y.astype(o_ref.dtype)


def _mm(x, w, out_dtype, *, tm=1024, tn=512, feats=None, name):
    m, k = x.shape
    n = w.shape[1]
    tm = min(tm, m)
    tn = min(tn, n)
    assert m % tm == 0 and n % tn == 0, (m, n, tm, tn)
    in_specs = [pl.BlockSpec((tm, k), lambda i, j: (i, 0)),
                pl.BlockSpec((k, tn), lambda i, j: (0, j))]
    args = [x, w]
    if feats is None:
        body = _mm_kernel
    else:
        per = feats.shape[0] // tm
        assert feats.shape[0] % tm == 0
        in_specs.append(pl.BlockSpec((tm, tn), lambda i, j: (i % per, j)))
        args.append(feats)
        body = _mm_feat_kernel
    return pl.pallas_call(
        body,
        out_shape=jax.ShapeDtypeStruct((m, n), out_dtype),
        grid=(m // tm, n // tn),
        in_specs=in_specs,
        out_specs=pl.BlockSpec((tm, tn), lambda i, j: (i, j)),
        compiler_params=_cparams(("parallel", "parallel")),
        name=name,
    )(*args)


def _cmp_kernel(sub_ref, pet_ref, peb_ref, wt_ref, wb_ref, w2k_ref, w2vt_ref, kc_ref, vct_ref):
    sub = sub_ref[0].astype(F32)
    nc = sub.shape[0]
    a = _dot((sub + pet_ref[...]).astype(BF16), wt_ref[...])
    b = _dot((sub + peb_ref[...]).astype(BF16), wb_ref[...])
    hid = jax.nn.gelu(a + pltpu.roll(b, nc - 1, 0)).astype(BF16)
    kc_ref[0] = _dot(hid, w2k_ref[...]).astype(kc_ref.dtype)
    vct_ref[0] = _dot_t(w2vt_ref[...], hid).astype(vct_ref.dtype)


def _compress(sub, pet, peb, wt, wb, w2k, w2vt):
    bsz, nc, kk = sub.shape
    n_h = wt.shape[1]
    n_o = w2k.shape[1]
    const = lambda b: (0, 0)
    return pl.pallas_call(
        _cmp_kernel,
        out_shape=(jax.ShapeDtypeStruct((bsz, nc, n_o), BF16), jax.ShapeDtypeStruct((bsz, n_o, nc), BF16)),
        grid=(bsz,),
        in_specs=[pl.BlockSpec((1, nc, kk), lambda b: (b, 0, 0)),
                  pl.BlockSpec((1, kk), const), pl.BlockSpec((1, kk), const),
                  pl.BlockSpec((kk, n_h), const), pl.BlockSpec((kk, n_h), const),
                  pl.BlockSpec((n_h, n_o), const), pl.BlockSpec((n_o, n_h), const)],
        out_specs=(pl.BlockSpec((1, nc, n_o), lambda b: (b, 0, 0)), pl.BlockSpec((1, n_o, nc), lambda b: (b, 0, 0))),
        compiler_params=_cparams(("parallel",)),
        name="nsa_compress",
    )(sub, pet, peb, wt, wb, w2k, w2vt)


def _flash_reset(m_scr, l_scr, acc_scr):
    m_scr[...] = jnp.full(m_scr.shape, NEG, F32)
    l_scr[...] = jnp.zeros(l_scr.shape, F32)
    acc_scr[...] = jnp.zeros(acc_scr.shape, F32)


def _qk(kget, qxs, start, size):
    k0 = kget(0, start, size)
    if kget(1, start, size) is None:
        return _dot(k0, jnp.concatenate(qxs, axis=1))
    return jnp.concatenate([_dot(k0, qxs[0])] + [_dot(kget(c, start, size), qxs[c])
                                                 for c in range(1, len(qxs))], axis=1)


def _softmax_pv(s, vget, nq, start, size, m_scr, l_scr, acc_scr, mask=None):
    if mask is not None:
        s = jnp.where(mask, s, NEG)
    m_prev = m_scr[...]
    m_new = jnp.maximum(m_prev, jnp.max(s, axis=0, keepdims=True))
    alpha = jnp.exp(m_prev - m_new)
    p = jnp.exp(s - m_new)
    l_scr[...] = alpha * l_scr[...] + jnp.sum(p, axis=0, keepdims=True)
    p = p.astype(BF16)
    v0 = vget(0, start, size)
    if vget(1, start, size) is None:
        pv = _dot(v0, p)
    else:
        pv = jnp.concatenate([_dot(v0, p[:, :TQ])] + [_dot(vget(c, start, size), p[:, c * TQ:(c + 1) * TQ])
                                                      for c in range(1, nq)], axis=1)
    acc_scr[...] = alpha * acc_scr[...] + pv
    m_scr[...] = m_new


def _flash_step(kget, vget, qxs, start, size, m_scr, l_scr, acc_scr, mask=None):
    _softmax_pv(_qk(kget, qxs, start, size), vget, len(qxs), start, size, m_scr, l_scr, acc_scr, mask)


def _rows(ref, start, size, lane0=0):
    return ref[pl.ds(pl.multiple_of(start, TQ), size), lane0:lane0 + SLOT]


def _cols(ref, start, size, row0=0):
    return ref[row0:row0 + SLOT, pl.ds(pl.multiple_of(start, TQ), size)]


def _flash_causal(kget, vget, qxs, i, sa_scr, sb_scr, m_scr, l_scr, acc_scr):
    tq = TQ
    tk = 2 * tq
    nq = len(qxs)
    rr = nq * tq
    n_steps = i // 2 + 1
    sa_scr[...] = _qk(kget, qxs, 0, tk)

    def body(jj, carry):
        j = 2 * jj
        s = sa_scr[...]
        sb_scr[...] = _qk(kget, qxs, (j + 1) * tk, tk)
        _softmax_pv(s, vget, nq, j * tk, tk, m_scr, l_scr, acc_scr)
        s = sb_scr[...]
        sa_scr[...] = _qk(kget, qxs, (j + 2) * tk, tk)
        _softmax_pv(s, vget, nq, (j + 1) * tk, tk, m_scr, l_scr, acc_scr)
        return carry

    lax.fori_loop(0, (n_steps - 1) // 2, body, 0)
    last = n_steps - 1
    valid = (last * tk + lax.broadcasted_iota(I32, (tk, rr), 0)
             <= i * tq + (lax.broadcasted_iota(I32, (tk, rr), 1) & (tq - 1)))

    @pl.when(last % 2 == 0)
    def _():
        _softmax_pv(sa_scr[...], vget, nq, last * tk, tk, m_scr, l_scr, acc_scr, mask=valid)

    @pl.when(last % 2 == 1)
    def _():
        s = sa_scr[...]
        sb_scr[...] = _qk(kget, qxs, last * tk, tk)
        _softmax_pv(s, vget, nq, (last - 1) * tk, tk, m_scr, l_scr, acc_scr)
        _softmax_pv(sb_scr[...], vget, nq, last * tk, tk, m_scr, l_scr, acc_scr, mask=valid)


def _nsa_kernel(q_ref, kc_ref, vct_ref, ks_ref, vst_ref, kw_ref, vwt_ref, g_ref, ovt_ref, fq_ref,
                o_ref, m_scr, l_scr, acc_scr, imp_scr, sa_scr, sb_scr, *, tq, nsel):
    i = pl.program_id(2)
    r4 = NSA_GROUP
    rr = r4 * tq
    q4 = [q_ref[r * SLOT:(r + 1) * SLOT, :] for r in range(r4)]
    fq = fq_ref[0]

    kc = kc_ref[0]
    nc = kc.shape[0]
    qs = jnp.concatenate(q4, axis=1)
    s = _dot(kc, qs)
    t_glob = i * tq + (lax.broadcasted_iota(I32, (nc, rr), 1) & (tq - 1))
    c_idx = lax.broadcasted_iota(I32, (nc, rr), 0)
    cmask = (c_idx * CMP_STRIDE + (CMP_BLOCK - 1)) <= t_glob
    s = jnp.where(cmask, s, NEG)
    mx = jnp.max(s, axis=0, keepdims=True)
    e = jnp.where(cmask, jnp.exp(s - mx), 0.0)
    lsum = jnp.sum(e, axis=0, keepdims=True)
    p_cmp = (e / jnp.where(lsum > 0.0, lsum, 1.0)).astype(BF16)
    o_cmp = _dot(vct_ref[0], p_cmp)

    ovt = ovt_ref[...]
    imp = _dot(ovt, p_cmp[:, 0:tq])
    for r in range(1, r4):
        imp = imp + _dot(ovt, p_cmp[:, r * tq:(r + 1) * tq])
    nbp = imp.shape[0]
    jidx = lax.broadcasted_iota(I32, (nbp, tq), 0)
    cur = (i * tq + lax.broadcasted_iota(I32, (nbp, tq), 1)) // SLC_BLOCK
    forced = (jidx == 0) | (jidx == cur) | (jidx == cur - 1)
    imp = jnp.where(forced, 1e6, jnp.where(jidx > cur, -1e6, imp))
    imp_scr[...] = imp

    def rank_body(jp, rank):
        row = imp_scr[pl.ds(jp, 1), :]
        better = (row > imp) | ((row == imp) & (jp < jidx))
        return rank + better.astype(I32)

    n_live = (i + 1) * (tq // SLC_BLOCK)
    rank = lax.fori_loop(0, n_live, rank_body, jnp.zeros((nbp, tq), I32))
    sel = (rank < nsel) & (jidx <= cur)
    mask_t = jnp.where(sel, 0.0, NEG)
    feat = jnp.concatenate([jnp.zeros((SLOT - nbp, tq), F32), mask_t], axis=0)

    qxs = [q4[r] + (feat + fq[:, r:r + 1]).astype(BF16) for r in range(r4)]
    kget = lambda c, start, size: _rows(ks_ref, start, size) if c == 0 else None
    vget = lambda c, start, size: _cols(vst_ref, start, size) if c == 0 else None
    _flash_reset(m_scr, l_scr, acc_scr)
    _flash_causal(kget, vget, qxs, i, sa_scr, sb_scr, m_scr, l_scr, acc_scr)
    o_slc = acc_scr[...] / l_scr[...]

    qx = jnp.concatenate([q4[r] + fq[:, r4 + r:r4 + r + 1].astype(BF16) for r in range(r4)], axis=1)
    w0 = jnp.maximum(i - 2, 0) * tq
    s = _dot(_rows(kw_ref, w0, 3 * tq), qx)
    dist = (i * tq + (lax.broadcasted_iota(I32, (3 * tq, rr), 1) & (tq - 1))
            - (w0 + lax.broadcasted_iota(I32, (3 * tq, rr), 0)))
    s = jnp.where(dist >= 0, jnp.where(dist < WINDOW, s, NEG), NEG)
    e = jnp.exp(s - jnp.max(s, axis=0, keepdims=True))
    o_win = _dot(_cols(vwt_ref, w0, 3 * tq), e.astype(BF16)) / jnp.sum(e, axis=0, keepdims=True)

    gate = jax.nn.sigmoid(g_ref[...])
    for r in range(r4):
        sl = slice(r * tq, (r + 1) * tq)
        o = (gate[3 * r:3 * r + 1, :] * o_cmp[:, sl] + gate[3 * r + 1:3 * r + 2, :] * o_slc[:, sl]
             + gate[3 * r + 2:3 * r + 3, :] * o_win[:, sl])
        o_ref[:, r * SLOT:(r + 1) * SLOT] = o.T.astype(o_ref.dtype)


def _nsa_attention(zat, zb, gt, kc, vct, ovt, fq, bsz, seq):
    tq = TQ
    ni = seq // tq
    nc = kc.shape[1]
    nsel = min(SLC_TOPK, seq // SLC_BLOCK)
    assert seq // SLC_BLOCK <= 64 and WINDOW == 2 * tq and ni % 2 == 0 and ni >= 3
    rr = NSA_GROUP * tq
    gw = NSA_GROUP * SLOT
    in_specs = [
        pl.BlockSpec((gw, tq), lambda b, g, i: (g, b * ni + i)),
        pl.BlockSpec((1, nc, SLOT), lambda b, g, i: (b, 0, g)),
        pl.BlockSpec((1, SLOT, nc), lambda b, g, i: (b, g, 0)),
        pl.BlockSpec((seq, SLOT), lambda b, g, i: (b, g)),
        pl.BlockSpec((SLOT, seq), lambda b, g, i: (_SLOT_SLC_V + g, b)),
        pl.BlockSpec((seq, SLOT), lambda b, g, i: (b, 2 + g)),
        pl.BlockSpec((SLOT, seq), lambda b, g, i: (_SLOT_WIN_V + g, b)),
        pl.BlockSpec((SLOT, tq), lambda b, g, i: (g, b * ni + i)),
        pl.BlockSpec(ovt.shape, lambda b, g, i: (0, 0)),
        pl.BlockSpec((1, SLOT, 8), lambda b, g, i: (g, 0, 0)),
    ]
    return pl.pallas_call(
        functools.partial(_nsa_kernel, tq=tq, nsel=nsel),
        out_shape=jax.ShapeDtypeStruct((bsz * seq, NSA_HEADS * SLOT), BF16),
        grid=(bsz, NSA_KV_HEADS, ni),
        in_specs=in_specs,
        out_specs=pl.BlockSpec((tq, gw), lambda b, g, i: (b * ni + i, g)),
        scratch_shapes=[pltpu.VMEM((1, rr), F32), pltpu.VMEM((1, rr), F32),
                        pltpu.VMEM((SLOT, rr), F32), pltpu.VMEM((64, tq), F32),
                        pltpu.VMEM((2 * tq, rr), F32), pltpu.VMEM((2 * tq, rr), F32)],
        compiler_params=_cparams(("parallel", "parallel", "arbitrary")),
        name="nsa_attention",
    )(zat, kc, vct, zb, zat, zb, zat, gt, ovt, fq)


def _moba_kernel(q_ref, k_ref, vt_ref, fq_ref, o_ref, m_scr, l_scr, acc_scr, km_scr, gate_scr,
                 sa_scr, sb_scr, *, tq, nblk, ntop):
    i = pl.program_id(2)
    nh = MOBA_HPS

    @pl.when(i == 0)
    def _():
        for hh in range(nh):
            kh = k_ref[:, hh * SLOT:(hh + 1) * SLOT].astype(F32)
            km = jnp.mean(kh.reshape(nblk, tq, SLOT), axis=1)
            if nblk < 16:
                km = jnp.concatenate([km, jnp.zeros((16 - nblk, SLOT), F32)], axis=0)
            km_scr[hh] = km

    qs = [q_ref[hh * SLOT:(hh + 1) * SLOT, :] for hh in range(nh)]
    gates = [_dot(km_scr[hh].astype(BF16), qs[hh]) for hh in range(nh)]
    for hh in range(nh):
        gate_scr[hh] = gates[hh]
    jidx = lax.broadcasted_iota(I32, (16, tq), 0)

    def rank_body(jp, ranks):
        out = []
        for hh in range(nh):
            row = gate_scr[hh, pl.ds(jp, 1), :]
            better = (row > gates[hh]) | ((row == gates[hh]) & (jp < jidx))
            out.append(ranks[hh] + better.astype(I32))
        return tuple(out)

    ranks = lax.fori_loop(0, i, rank_body, tuple(jnp.zeros((16, tq), I32) for _ in range(nh)))
    qxs = []
    for hh in range(nh):
        keep = ((jidx < i) & (ranks[hh] < ntop)) | (jidx == i)
        mask_t = jnp.where(keep, 0.0, NEG)
        feat = jnp.concatenate([jnp.zeros((HEAD_DIM, tq), F32), mask_t,
                                jnp.zeros((SLOT - HEAD_DIM - 16, tq), F32)], axis=0)
        qxs.append(qs[hh] + (feat + fq_ref[hh][:, 0:1]).astype(BF16))
    kget = lambda c, start, size: _rows(k_ref, start, size, c * SLOT)
    vget = lambda c, start, size: _cols(vt_ref, start, size, c * SLOT)
    _flash_reset(m_scr, l_scr, acc_scr)
    _flash_causal(kget, vget, qxs, i, sa_scr, sb_scr, m_scr, l_scr, acc_scr)
    o = acc_scr[...] / l_scr[...]
    for hh in range(nh):
        o_ref[:, hh * SLOT:(hh + 1) * SLOT] = o[:, hh * tq:(hh + 1) * tq].T.astype(o_ref.dtype)


def _moba_attention(zat, zb, fq, bsz, seq):
    tq = TQ
    assert tq == MOBA_BLOCK and seq % tq == 0
    ni = seq // tq
    assert ni <= 16 and ni % 2 == 0
    ntop = min(MOBA_TOPK, ni)
    nh = MOBA_HPS
    hw = nh * SLOT
    q0 = _SLOT_MOBA_Q // nh
    v0 = _SLOT_MOBA_V // nh
    return pl.pallas_call(
        functools.partial(_moba_kernel, tq=tq, nblk=ni, ntop=ntop),
        out_shape=jax.ShapeDtypeStruct((bsz * seq, MOBA_HEADS * SLOT), BF16),
        grid=(bsz, MOBA_HEADS // nh, ni),
        in_specs=[pl.BlockSpec((hw, tq), lambda b, h, i: (q0 + h, b * ni + i)),
                  pl.BlockSpec((seq, hw), lambda b, h, i: (b, 1 + h)),
                  pl.BlockSpec((hw, seq), lambda b, h, i: (v0 + h, b)),
                  pl.BlockSpec((nh, SLOT, 8), lambda b, h, i: (h, 0, 0))],
        out_specs=pl.BlockSpec((tq, hw), lambda b, h, i: (b * ni + i, h)),
        scratch_shapes=[pltpu.VMEM((1, nh * tq), F32), pltpu.VMEM((1, nh * tq), F32),
                        pltpu.VMEM((SLOT, nh * tq), F32),
                        pltpu.VMEM((nh, 16, SLOT), F32), pltpu.VMEM((nh, 16, tq), F32),
                        pltpu.VMEM((2 * tq, nh * tq), F32), pltpu.VMEM((2 * tq, nh * tq), F32)],
        compiler_params=_cparams(("parallel", "parallel", "arbitrary")),
        name="moba_attention",
    )(zat, zb, zat, fq)


def _merge_kernel(ya_ref, yb_ref, hc_ref, bc_ref, cc_ref, hcp_ref, ccp_ref, g0_ref, g1_ref, g2_ref,
                  h_ref, cw_ref, wa_ref, wb_ref, wc_ref, wo_ref, lg_ref, lb_ref, o_ref, *, tm, per, alpha):
    i = pl.program_id(0)
    u = cc_ref[...].astype(F32) * hc_ref[...].astype(F32)
    first = (i % per) == 0
    up = jnp.where(first, 0.0, ccp_ref[...].astype(F32) * hcp_ref[...].astype(F32))
    rowi = lax.broadcasted_iota(I32, u.shape, 0)
    u1 = jnp.where(rowi == 0, up[15:16, :], pltpu.roll(u, 1, 0))
    u2 = jnp.where(rowi == 0, up[14:15, :], jnp.where(rowi == 1, up[15:16, :], pltpu.roll(u, 2, 0)))
    cw = cw_ref[...]
    yc = bc_ref[...].astype(F32) * (cw[0:1, :] * u2 + cw[1:2, :] * u1 + cw[2:3, :] * u)
    merged = (jax.nn.sigmoid(g0_ref[...].astype(F32)) * _dot(ya_ref[...], wa_ref[...])
              + jax.nn.sigmoid(g1_ref[...].astype(F32)) * _dot(yb_ref[...], wb_ref[...])
              + jax.nn.sigmoid(g2_ref[...].astype(F32)) * _dot(yc.astype(BF16), wc_ref[...]))
    mix = _dot(merged.astype(BF16), wo_ref[...])
    o_ref[...] = _layer_norm(alpha * h_ref[...] + mix, lg_ref[...], lb_ref[...])


def _merge(ya, yb, zc, h, cw, wa, wb, wc, wo, lg, lb, seq, alpha):
    tm = 512
    t = h.shape[0]
    per = seq // tm
    d = D_MODEL
    row = lambda i: (i, 0)
    const = lambda i: (0, 0)
    prev = lambda c: (lambda i: (jnp.maximum(i * (tm // 16) - 1, 0), c))
    in_specs = [
        pl.BlockSpec((tm, d), row), pl.BlockSpec((tm, d), row),
        pl.BlockSpec((tm, CONV_CH), lambda i: (i, 1)), pl.BlockSpec((tm, CONV_CH), lambda i: (i, 2)),
        pl.BlockSpec((tm, CONV_CH), lambda i: (i, 3)),
        pl.BlockSpec((16, CONV_CH), prev(1)), pl.BlockSpec((16, CONV_CH), prev(3)),
        pl.BlockSpec((tm, d), lambda i: (i, 2)), pl.BlockSpec((tm, d), lambda i: (i, 3)),
        pl.BlockSpec((tm, d), lambda i: (i, 4)),
        pl.BlockSpec((tm, d), row),
        pl.BlockSpec((8, CONV_CH), const),
        pl.BlockSpec((d, d), const), pl.BlockSpec((d, d), const), pl.BlockSpec((CONV_CH, d), const),
        pl.BlockSpec((d, d), const), pl.BlockSpec((1, d), const), pl.BlockSpec((1, d), const),
    ]
    return pl.pallas_call(
        functools.partial(_merge_kernel, tm=tm, per=per, alpha=alpha),
        out_shape=jax.ShapeDtypeStruct((t, d), F32),
        grid=(t // tm,),
        in_specs=in_specs,
        out_specs=pl.BlockSpec((tm, d), row),
        compiler_params=_cparams(("parallel",)),
        name="mixer_merge",
    )(ya, yb, zc, zc, zc, zc, zc, zc, zc, zc, h, cw, wa, wb, wc, wo, lg, lb)


def _xattn_kernel(h_ref, kv_ref, wq_ref, wo_ref, lg_ref, lb_ref, o_ref, ob_ref, *, alpha):
    h = h_ref[...]
    q = _dot(h.astype(BF16), wq_ref[...]).astype(BF16)
    kv = kv_ref[...]
    nh = XATTN_HEADS
    hd = XATTN_HEAD_DIM
    outs = []
    for hh in range(nh):
        s = _dot_t(q[:, hh * hd:(hh + 1) * hd], kv[:, hh * hd:(hh + 1) * hd]) * (hd ** -0.5)
        s = s - jnp.max(s, axis=-1, keepdims=True)
        e = jnp.exp(s)
        p = e / jnp.sum(e, axis=-1, keepdims=True)
        outs.append(_dot(p.astype(BF16), kv[:, (nh + hh) * hd:(nh + hh + 1) * hd]))
    o = jnp.concatenate(outs, axis=-1).astype(BF16)
    y = _layer_norm(alpha * h + _dot(o, wo_ref[...]), lg_ref[...], lb_ref[...])
    o_ref[...] = y
    ob_ref[...] = y.astype(BF16)


def _xattn(h, kv, wq, wo, lg, lb, seq, mlen, alpha):
    tm = 512
    t = h.shape[0]
    per = seq // tm
    d = D_MODEL
    row = lambda i: (i, 0)
    const = lambda i: (0, 0)
    return pl.pallas_call(
        functools.partial(_xattn_kernel, alpha=alpha),
        out_shape=(jax.ShapeDtypeStruct((t, d), F32), jax.ShapeDtypeStruct((t, d), BF16)),
        grid=(t // tm,),
        in_specs=[pl.BlockSpec((tm, d), row), pl.BlockSpec((mlen, kv.shape[1]), lambda i: (i // per, 0)),
                  pl.BlockSpec(wq.shape, const), pl.BlockSpec(wo.shape, const),
                  pl.BlockSpec((1, d), const), pl.BlockSpec((1, d), const)],
        out_specs=(pl.BlockSpec((tm, d), row), pl.BlockSpec((tm, d), row)),
        compiler_params=_cparams(("parallel",)),
        name="mem_xattn",
    )(h, kv, wq, wo, lg, lb)


def _expert_kernel(be_ref, x_ref, w13_ref, w2_ref, o_ref, w13b_scr, w2b_scr):
    i = pl.program_id(0)

    @pl.when((i == 0) | (be_ref[i] != be_ref[jnp.maximum(i - 1, 0)]))
    def _():
        w13b_scr[...] = w13_ref[0, 0].astype(BF16)
        w2b_scr[...] = w2_ref[0, 0].astype(BF16)

    hmid = _dot(x_ref[...], w13b_scr[...])
    a = hmid[:, :EXPERT_DIM]
    g = hmid[:, EXPERT_DIM:]
    act = (a * jax.nn.sigmoid(a) * g).astype(BF16)
    o_ref[...] = _dot(act, w2b_scr[...]).astype(o_ref.dtype)


def _experts(xg, w13, w2, block_e, layer):
    n_pad, d = xg.shape
    rows = EXP_ROWS
    nb = n_pad // rows
    grid_spec = pltpu.PrefetchScalarGridSpec(
        num_scalar_prefetch=1,
        grid=(nb,),
        in_specs=[pl.BlockSpec((rows, d), lambda i, be: (i, 0)),
                  pl.BlockSpec((1, 1, d, 2 * EXPERT_DIM), lambda i, be: (layer, be[i], 0, 0)),
                  pl.BlockSpec((1, 1, EXPERT_DIM, d), lambda i, be: (layer, be[i], 0, 0))],
        out_specs=pl.BlockSpec((rows, d), lambda i, be: (i, 0)),
        scratch_shapes=[pltpu.VMEM((d, 2 * EXPERT_DIM), BF16), pltpu.VMEM((EXPERT_DIM, d), BF16)],
    )
    return pl.pallas_call(
        _expert_kernel,
        out_shape=jax.ShapeDtypeStruct((n_pad, d), BF16),
        grid_spec=grid_spec,
        compiler_params=_cparams(("arbitrary",)),
        name="moe_experts",
    )(block_e, xg, w13, w2)


def _moe_out_kernel(h_ref, y8_ref, wk_ref, w13_ref, w2_ref, lg_ref, lb_ref, o_ref, *, alpha):
    h = h_ref[...]
    hmid = _dot(h.astype(BF16), w13_ref[...])
    a = hmid[:, :EXPERT_DIM]
    g = hmid[:, EXPERT_DIM:]
    acc = alpha * h + _dot((a * jax.nn.sigmoid(a) * g).astype(BF16), w2_ref[...])
    wk = wk_ref[...]
    for k in range(TOP_K):
        acc = acc + wk[:, k:k + 1] * y8_ref[k].astype(F32)
    o_ref[...] = _layer_norm(acc, lg_ref[...], lb_ref[...])


def _moe_out(h, y8, wk, w13, w2, lg, lb, alpha):
    tm = 256
    t, d = h.shape
    row = lambda i: (i, 0)
    const = lambda i: (0, 0)
    return pl.pallas_call(
        functools.partial(_moe_out_kernel, alpha=alpha),
        out_shape=jax.ShapeDtypeStruct((t, d), F32),
        grid=(t // tm,),
        in_specs=[pl.BlockSpec((tm, d), row), pl.BlockSpec((TOP_K, tm, d), lambda i: (0, i, 0)),
                  pl.BlockSpec((tm, TOP_K), row),
                  pl.BlockSpec(w13.shape, const), pl.BlockSpec(w2.shape, const),
                  pl.BlockSpec((1, d), const), pl.BlockSpec((1, d), const)],
        out_specs=pl.BlockSpec((tm, d), row),
        compiler_params=_cparams(("parallel",)),
        name="moe_shared_ln",
    )(h, y8, wk, w13, w2, lg, lb)


def _dest_kernel(idx_ref, rank_ref, ps_ref, o_ref):
    idx = idx_ref[...]
    ps = ps_ref[...]
    tm = idx.shape[1]
    eidx = lax.broadcasted_iota(I32, (N_EXPERTS, tm), 0)
    rows = [jnp.sum(jnp.where(eidx == idx[k:k + 1, :], ps, 0.0), axis=0, keepdims=True) for k in range(TOP_K)]
    o_ref[...] = jnp.concatenate(rows, axis=0).astype(I32) + rank_ref[...]


def _dest(idx, rank, pstarts):
    tm = 1024
    t = idx.shape[1]
    tm = min(tm, t)
    col = lambda i: (0, i)
    return pl.pallas_call(
        _dest_kernel,
        out_shape=jax.ShapeDtypeStruct((TOP_K, t), I32),
        grid=(t // tm,),
        in_specs=[pl.BlockSpec((TOP_K, tm), col), pl.BlockSpec((TOP_K, tm), col),
                  pl.BlockSpec((N_EXPERTS, 1), lambda i: (0, 0))],
        out_specs=pl.BlockSpec((TOP_K, tm), col),
        compiler_params=_cparams(("parallel",)),
        name="moe_dest",
    )(idx, rank, pstarts)


def _router_kernel(h_ref, rwt_ref, rb_ref, tri_ref, idx_ref, w_ref, rank_ref, cnt_ref, carry_scr, *, tm):
    i = pl.program_id(0)

    @pl.when(i == 0)
    def _():
        carry_scr[...] = jnp.zeros(carry_scr.shape, F32)

    ne = N_EXPERTS
    gsz = ne // N_GROUPS
    s = jax.nn.sigmoid(_dot_t(rwt_ref[...], h_ref[...].astype(BF16)))
    sb = s + rb_ref[...]
    sb3 = sb.reshape(N_GROUPS, gsz, tm)
    li = lax.broadcasted_iota(I32, (N_GROUPS, gsz, tm), 1)
    m1 = jnp.max(sb3, axis=1, keepdims=True)
    first = jnp.min(jnp.where(sb3 == m1, li, gsz), axis=1, keepdims=True)
    m2 = jnp.max(jnp.where(li == first, -jnp.inf, sb3), axis=1, keepdims=True)
    gs = (m1 + m2).reshape(N_GROUPS, tm)
    gi = lax.broadcasted_iota(I32, (N_GROUPS, tm), 0)
    grank = jnp.zeros((N_GROUPS, tm), I32)
    for gp in range(N_GROUPS):
        row = gs[gp:gp + 1, :]
        grank = grank + ((row > gs) | ((row == gs) & (gp < gi))).astype(I32)
    gkeep = (grank < TOPK_GROUPS).astype(F32)
    ekeep = jnp.broadcast_to(gkeep[:, None, :], (N_GROUPS, gsz, tm)).reshape(ne, tm)
    cand = jnp.where(ekeep > 0.0, sb, NEG)
    eidx = lax.broadcasted_iota(I32, (ne, tm), 0)
    sel = jnp.zeros((ne, tm), F32)
    idxs, wts = [], []
    for _ in range(TOP_K):
        mx = jnp.max(cand, axis=0, keepdims=True)
        ik = jnp.min(jnp.where(cand == mx, eidx, ne), axis=0, keepdims=True)
        hit = eidx == ik
        wts.append(jnp.sum(jnp.where(hit, s, 0.0), axis=0, keepdims=True))
        idxs.append(ik)
        sel = jnp.where(hit, 1.0, sel)
        cand = jnp.where(hit, -jnp.inf, cand)
    before = _dot(sel.astype(BF16), tri_ref[...]) + carry_scr[...]
    ranks = [jnp.sum(jnp.where(eidx == ik, before, 0.0), axis=0, keepdims=True) for ik in idxs]
    carry_scr[...] = carry_scr[...] + jnp.sum(sel, axis=1, keepdims=True)
    w = jnp.concatenate(wts, axis=0)
    idx_ref[...] = jnp.concatenate(idxs, axis=0)
    w_ref[...] = w / jnp.sum(w, axis=0, keepdims=True) * ROUTE_SCALE
    rank_ref[...] = jnp.concatenate(ranks, axis=0).astype(I32)
    cnt_ref[...] = jnp.broadcast_to(carry_scr[...], cnt_ref.shape)


def _router(h, rwt, rb):
    tm = 256
    t, d = h.shape
    tri = jnp.asarray(np.triu(np.ones((tm, tm), np.float32), 1)).astype(BF16)
    const = lambda i: (0, 0)
    col = lambda i: (0, i)
    return pl.pallas_call(
        functools.partial(_router_kernel, tm=tm),
        out_shape=(jax.ShapeDtypeStruct((TOP_K, t), I32), jax.ShapeDtypeStruct((TOP_K, t), F32),
                   jax.ShapeDtypeStruct((TOP_K, t), I32), jax.ShapeDtypeStruct((N_EXPERTS, SLOT), F32)),
        grid=(t // tm,),
        in_specs=[pl.BlockSpec((tm, d), lambda i: (i, 0)), pl.BlockSpec((N_EXPERTS, d), const),
                  pl.BlockSpec((N_EXPERTS, 1), const), pl.BlockSpec((tm, tm), const)],
        out_specs=(pl.BlockSpec((TOP_K, tm), col), pl.BlockSpec((TOP_K, tm), col),
                   pl.BlockSpec((TOP_K, tm), col), pl.BlockSpec((N_EXPERTS, SLOT), const)),
        scratch_shapes=[pltpu.VMEM((N_EXPERTS, 1), F32)],
        compiler_params=_cparams(("arbitrary",)),
        name="moe_router",
    )(h, rwt, rb, tri)


def _moe(h, hb, router_w, router_b, exp_w13, exp_w2, layer, shared_w13, shared_w2, lg, lb, alpha):
    t, d = h.shape
    idx, wts, rank, cnt = _router(h, router_w.T, router_b.astype(F32).reshape(N_EXPERTS, 1))
    rows = EXP_ROWS
    n_a = t * TOP_K
    counts = cnt[:, 0].astype(I32)
    pcounts = (counts + rows - 1) // rows * rows
    pends = jnp.cumsum(pcounts)
    pstarts = pends - pcounts
    dest = _dest(idx, rank, pstarts.astype(F32).reshape(N_EXPERTS, 1))
    n_blocks = -(-n_a // rows) + N_EXPERTS
    n_pad = n_blocks * rows
    tok = jnp.broadcast_to(jnp.arange(t, dtype=I32)[None, :], (TOP_K, t))
    row_tok = (jnp.arange(n_pad, dtype=I32) % t).at[dest.reshape(-1)].set(tok.reshape(-1), unique_indices=True)
    block_e = jnp.minimum(jnp.searchsorted(pends, jnp.arange(n_blocks) * rows, side='right'),
                          N_EXPERTS - 1).astype(I32)
    xg = jnp.take(hb, row_tok, axis=0, mode="clip")
    out = _experts(xg, exp_w13, exp_w2, block_e, layer)
    y8 = jnp.take(out, dest, axis=0, mode="clip")
    return _moe_out(h, y8, wts.T, shared_w13, shared_w2, lg, lb, alpha)


def _pad_slots(w, scale=1.0):
    dm = w.shape[0]
    nh = w.shape[1] // HEAD_DIM
    w = (w * scale).reshape(dm, nh, HEAD_DIM)
    return jnp.concatenate([w, jnp.zeros_like(w)], axis=-1).reshape(dm, nh * SLOT)


def _pad_rows(w):
    n = w.shape[1]
    nh = w.shape[0] // HEAD_DIM
    w = w.reshape(nh, HEAD_DIM, n)
    return jnp.concatenate([w, jnp.zeros_like(w)], axis=1).reshape(nh * SLOT, n)


def _alibi(n):
    return np.exp2(-8.0 * np.arange(1, n + 1, dtype=np.float64) / n).astype(np.float32)


def _key_features(seq):
    p = np.arange(seq)
    slc = np.zeros((seq, SLOT), np.float32)
    slc[:, HEAD_DIM] = p % SLC_BLOCK
    blk = p // SLC_BLOCK
    nz = blk > 0
    slc[p[nz], HEAD_DIM + blk[nz]] = 1.0
    win = np.zeros((seq, SLOT), np.float32)
    win[:, HEAD_DIM] = p // 64
    win[:, HEAD_DIM + 1] = p % 64
    mob = np.zeros((seq, SLOT), np.float32)
    mob[p, HEAD_DIM + p // MOBA_BLOCK] = 1.0
    mob[:, HEAD_DIM + 16] = p % MOBA_BLOCK
    return np.concatenate([slc] * 2 + [win] * 2 + [mob] * MOBA_HEADS, axis=1)


def _query_features():
    sl = _alibi(NSA_HEADS)
    nsa = np.zeros((NSA_KV_HEADS, SLOT, 8), np.float32)
    for g in range(NSA_KV_HEADS):
        for r in range(NSA_GROUP):
            s = sl[g * NSA_GROUP + r]
            nsa[g, HEAD_DIM, r] = s
            nsa[g, HEAD_DIM + 1:, r] = s * SLC_BLOCK * np.arange(1, 64)
            nsa[g, HEAD_DIM, NSA_GROUP + r] = s * 64
            nsa[g, HEAD_DIM + 1, NSA_GROUP + r] = s
    sm = _alibi(MOBA_HEADS)
    mob = np.zeros((MOBA_HEADS, SLOT, 8), np.float32)
    for h in range(MOBA_HEADS):
        mob[h, HEAD_DIM:HEAD_DIM + 16, 0] = sm[h] * MOBA_BLOCK * np.arange(16)
        mob[h, HEAD_DIM + 16, 0] = sm[h]
    return nsa, mob


def _overlap_t(seq):
    nc = seq // CMP_STRIDE
    c_start = np.arange(nc) * CMP_STRIDE
    b_start = np.arange(64) * SLC_BLOCK
    ov = ((c_start[None, :] < (b_start + SLC_BLOCK)[:, None])
          & ((c_start + CMP_BLOCK)[None, :] > b_start[:, None])
          & (np.arange(nc) < nc - 1)[None, :] & (b_start < seq)[:, None])
    return ov.astype(np.float32)


def _compress_weights(pe, w1, w2):
    ty = np.array([0, 0, 1, 1])
    eye = jnp.eye(4, dtype=F32)
    w1r = w1.reshape(2, CMP_BLOCK, HEAD_DIM, CMP_HIDDEN)[ty]
    top = jnp.einsum('spdj,sS->psdSj', w1r[:, :CMP_STRIDE], eye).reshape(CMP_STRIDE * 256, 4 * CMP_HIDDEN)
    bot = jnp.einsum('spdj,sS->psdSj', w1r[:, CMP_STRIDE:], eye).reshape(CMP_STRIDE * 256, 4 * CMP_HIDDEN)
    per = pe[ty]
    pet = jnp.transpose(per[:, :CMP_STRIDE], (1, 0, 2)).reshape(1, CMP_STRIDE * 256)
    peb = jnp.transpose(per[:, CMP_STRIDE:], (1, 0, 2)).reshape(1, CMP_STRIDE * 256)
    w2p = jnp.concatenate([w2[ty], jnp.zeros((4, CMP_HIDDEN, SLOT - HEAD_DIM), F32)], axis=-1)
    w2b = jnp.einsum('sjd,sS->sjSd', w2p, eye).reshape(4 * CMP_HIDDEN, 4 * SLOT)
    w2k = w2b[:, :2 * SLOT].astype(BF16)
    w2vt = w2b[:, 2 * SLOT:].T.astype(BF16)
    return pet, peb, top.astype(BF16), bot.astype(BF16), w2k, w2vt


def kernel(x, mem, w_in, cmp_pe, cmp_w1, cmp_w2, conv_w, w_branch, w_out, ln1_g, ln1_b,
           xattn_wq, xattn_wkv, xattn_wo, ln2_g, ln2_b, router_w, router_b, exp_w13, exp_w2,
           shared_w13, shared_w2, ln3_g, ln3_b):
    bsz, seq, d = x.shape
    mlen = mem.shape[1]
    depth = w_in.shape[0]
    alpha = (2.0 * depth) ** 0.25
    t = bsz * seq
    scale = HEAD_DIM ** -0.5

    kfeat = jnp.asarray(_key_features(seq))
    fq_nsa, fq_moba = (jnp.asarray(a) for a in _query_features())
    ovt = jnp.asarray(_overlap_t(seq)).astype(BF16)
    memf = mem.reshape(bsz * mlen, d)

    h = x.reshape(t, d)
    ht = h.astype(BF16).T
    for l in range(depth):
        wi = w_in[l]
        kv6 = wi[:, _OFF_NSA_KV:_OFF_NSA_G].reshape(d, 6, NSA_KV_HEADS * HEAD_DIM)
        mq, mk, mv = (wi[:, _OFF_MOBA + j * 512:_OFF_MOBA + (j + 1) * 512] for j in range(3))
        w_at = jnp.concatenate([_pad_slots(wi[:, :512], scale), _pad_slots(kv6[:, 3]), _pad_slots(kv6[:, 5]),
                                _pad_slots(mq, scale), _pad_slots(mv)], axis=1).T.astype(BF16)
        w_b = jnp.concatenate([_pad_slots(kv6[:, 2]), _pad_slots(kv6[:, 4]), _pad_slots(mk)], axis=1).astype(BF16)
        wg = wi[:, _OFF_NSA_G:_OFF_MOBA].reshape(d, NSA_KV_HEADS, 12)
        wgt = jnp.concatenate([wg, jnp.zeros((d, NSA_KV_HEADS, SLOT - 12), F32)], axis=-1)
        wgt = wgt.reshape(d, NSA_KV_HEADS * SLOT).T.astype(BF16)
        w_c = jnp.concatenate([wi[:, _OFF_NSA_KV:_OFF_NSA_KV + 256], jnp.zeros((d, 256), F32),
                               wi[:, _OFF_CONV:]], axis=1).astype(BF16)

        zat = _mm(w_at, ht, BF16, tm=896, tn=1024, name="proj_at")
        gt = _mm(wgt, ht, F32, tm=256, tn=1024, name="proj_gate")
        zb = _mm(h, w_b, BF16, feats=kfeat, name="proj_b")
        zc = _mm(h, w_c, BF16, name="proj_c")

        pet, peb, wt, wb, w2k, w2vt = _compress_weights(cmp_pe[l], cmp_w1[l], cmp_w2[l])
        sub = zc[:, :256].reshape(bsz, seq // CMP_STRIDE, CMP_STRIDE * 256)
        kc, vct = _compress(sub, pet, peb, wt, wb, w2k, w2vt)

        ya = _nsa_attention(zat, zb, gt, kc, vct, ovt, fq_nsa, bsz, seq)
        yb = _moba_attention(zat, zb, fq_moba, bsz, seq)

        cw = jnp.concatenate([conv_w[l], jnp.zeros((5, CONV_CH), F32)], axis=0)
        h = _merge(ya, yb, zc, h, cw, _pad_rows(w_branch[l, 0]).astype(BF16),
                   _pad_rows(w_branch[l, 1]).astype(BF16), w_branch[l, 2].astype(BF16),
                   w_out[l].astype(BF16), ln1_g[l][None], ln1_b[l][None], seq, alpha)

        kv = _mm(memf, xattn_wkv[l].astype(BF16), BF16, tm=512, name="xattn_kv")
        h, hb = _xattn(h, kv, xattn_wq[l].astype(BF16), xattn_wo[l].astype(BF16),
                       ln2_g[l][None], ln2_b[l][None], seq, mlen, alpha)

        h = _moe(h, hb, router_w[l].astype(BF16), router_b[l], exp_w13, exp_w2, l,
                 shared_w13[l].astype(BF16), shared_w2[l].astype(BF16), ln3_g[l][None], ln3_b[l][None], alpha)
        if l + 1 < depth:
            ht = h.astype(BF16).T
    return h.reshape(bsz, seq, d)
```

```python
import functools

import jax
import jax.numpy as jnp
import numpy as np
from jax import lax
from jax.experimental import pallas as pl
from jax.experimental.pallas import tpu as pltpu

F32 = jnp.float32
BF16 = jnp.bfloat16
I32 = jnp.int32

D_MODEL = 1024
HEAD_DIM = 64
SLOT = 128
NEG = -1e30
LN_EPS = 1e-5

NSA_HEADS = 8
NSA_KV_HEADS = 2
NSA_GROUP = 4
CMP_BLOCK = 32
CMP_STRIDE = 16
CMP_HIDDEN = 256
SLC_BLOCK = 64
SLC_TOPK = 16
WINDOW = 512
MOBA_HEADS = 8
MOBA_BLOCK = 256
MOBA_TOPK = 3
CONV_CH = 512
XATTN_HEADS = 4
XATTN_HEAD_DIM = 128
N_EXPERTS = 256
TOP_K = 8
N_GROUPS = 8
TOPK_GROUPS = 4
EXPERT_DIM = 256
ROUTE_SCALE = 2.5

TQ = 256
MOBA_HPS = 4
EXP_ROWS = 512
VMEM_LIMIT = 48 * 1024 * 1024

_OFF_NSA_Q = 0
_OFF_NSA_KV = 512
_OFF_NSA_G = 1280
_OFF_MOBA = 1304
_OFF_CONV = 2840
_OFF_MERGE = 4376

_SLOT_NSA_Q = 0
_SLOT_SLC_V = 8
_SLOT_WIN_V = 10
_SLOT_MOBA_Q = 12
_SLOT_MOBA_V = 20
_N_SLOTS_T = 28


def _cparams(sem):
    return pltpu.CompilerParams(dimension_semantics=sem, vmem_limit_bytes=VMEM_LIMIT)


def _dot(a, b):
    return jnp.dot(a, b, preferred_element_type=F32)


def _dot_t(a, b):
    return lax.dot_general(a, b, (((1,), (1,)), ((), ())), preferred_element_type=F32)


def _layer_norm(x, g, b):
    mu = jnp.mean(x, axis=-1, keepdims=True)
    xc = x - mu
    var = jnp.mean(xc * xc, axis=-1, keepdims=True)
    return xc * lax.rsqrt(var + LN_EPS) * g + b


def _mm_kernel(x_ref, w_ref, o_ref):
    o_ref[...] = _dot(x_ref[...].astype(BF16), w_ref[...]).astype(o_ref.dtype)


def _mm_feat_kernel(x_ref, w_ref, f_ref, o_ref):
    y = _dot(x_ref[...].astype(BF16), w_ref[...]) + f_ref[...]
    o_ref[...] = y.astype(o_ref.dtype)


def _mm(x, w, out_dtype, *, tm=1024, tn=512, feats=None, name):
    m, k = x.shape
    n = w.shape[1]
    tm = min(tm, m)
    tn = min(tn, n)
    assert m % tm == 0 and n % tn == 0, (m, n, tm, tn)
    in_specs = [pl.BlockSpec((tm, k), lambda i, j: (i, 0)),
                pl.BlockSpec((k, tn), lambda i, j: (0, j))]
    args = [x, w]
    if feats is None:
        body = _mm_kernel
    else:
        per = feats.shape[0] // tm
        assert feats.shape[0] % tm == 0
        in_specs.append(pl.BlockSpec((tm, tn), lambda i, j: (i % per, j)))
        args.append(feats)
        body = _mm_feat_kernel
    return pl.pallas_call(
        body,
        out_shape=jax.ShapeDtypeStruct((m, n), out_dtype),
        grid=(m // tm, n // tn),
        in_specs=in_specs,
        out_specs=pl.BlockSpec((tm, tn), lambda i, j: (i, j)),
        compiler_params=_cparams(("parallel", "parallel")),
        name=name,
    )(*args)


def _proj_t_kernel(w_ref, ht_ref, o_ref):
    acc = _dot(w_ref[...], ht_ref[...])
    tn = acc.shape[1]
    zero = jnp.zeros((SLOT - HEAD_DIM, tn), o_ref.dtype)
    for s in range(w_ref.shape[0] // HEAD_DIM):
        o_ref[s * SLOT:s * SLOT + HEAD_DIM, :] = acc[s * HEAD_DIM:(s + 1) * HEAD_DIM, :].astype(o_ref.dtype)
        o_ref[s * SLOT + HEAD_DIM:(s + 1) * SLOT, :] = zero


def _proj_t(w, ht):
    m, k = w.shape
    t = ht.shape[1]
    tn = min(512, t)
    n_out = m // HEAD_DIM * SLOT
    return pl.pallas_call(
        _proj_t_kernel,
        out_shape=jax.ShapeDtypeStruct((n_out, t), BF16),
        grid=(t // tn,),
        in_specs=[pl.BlockSpec((m, k), lambda j: (0, 0)), pl.BlockSpec((k, tn), lambda j: (0, j))],
        out_specs=pl.BlockSpec((n_out, tn), lambda j: (0, j)),
        compiler_params=_cparams(("parallel",)),
        name="proj_at",
    )(w, ht)


def _cmp_kernel(sub_ref, pet_ref, peb_ref, wt_ref, wb_ref, w2k_ref, w2vt_ref, kc_ref, vct_ref):
    sub = sub_ref[0].astype(F32)
    nc = sub.shape[0]
    a = _dot((sub + pet_ref[...]).astype(BF16), wt_ref[...])
    b = _dot((sub + peb_ref[...]).astype(BF16), wb_ref[...])
    hid = jax.nn.gelu(a + pltpu.roll(b, nc - 1, 0)).astype(BF16)
    kc_ref[0] = _dot(hid, w2k_ref[...]).astype(kc_ref.dtype)
    vct_ref[0] = _dot_t(w2vt_ref[...], hid).astype(vct_ref.dtype)


def _compress(sub, pet, peb, wt, wb, w2k, w2vt):
    bsz, nc, kk = sub.shape
    n_h = wt.shape[1]
    n_o = w2k.shape[1]
    const = lambda b: (0, 0)
    return pl.pallas_call(
        _cmp_kernel,
        out_shape=(jax.ShapeDtypeStruct((bsz, nc, n_o), BF16), jax.ShapeDtypeStruct((bsz, n_o, nc), BF16)),
        grid=(bsz,),
        in_specs=[pl.BlockSpec((1, nc, kk), lambda b: (b, 0, 0)),
                  pl.BlockSpec((1, kk), const), pl.BlockSpec((1, kk), const),
                  pl.BlockSpec((kk, n_h), const), pl.BlockSpec((kk, n_h), const),
                  pl.BlockSpec((n_h, n_o), const), pl.BlockSpec((n_o, n_h), const)],
        out_specs=(pl.BlockSpec((1, nc, n_o), lambda b: (b, 0, 0)), pl.BlockSpec((1, n_o, nc), lambda b: (b, 0, 0))),
        compiler_params=_cparams(("parallel",)),
        name="nsa_compress",
    )(sub, pet, peb, wt, wb, w2k, w2vt)


def _flash_reset(m_scr, l_scr, acc_scr):
    m_scr[...] = jnp.full(m_scr.shape, NEG, F32)
    l_scr[...] = jnp.zeros(l_scr.shape, F32)
    acc_scr[...] = jnp.zeros(acc_scr.shape, F32)


def _qk(kget, qxs, start, size):
    k0 = kget(0, start, size)
    if kget(1, start, size) is None:
        return _dot(k0, jnp.concatenate(qxs, axis=1))
    return jnp.concatenate([_dot(k0, qxs[0])] + [_dot(kget(c, start, size), qxs[c])
                                                 for c in range(1, len(qxs))], axis=1)


def _softmax_pv(s, vget, nq, start, size, m_scr, l_scr, acc_scr, mask=None):
    if mask is not None:
        s = jnp.where(mask, s, NEG)
    m_prev = m_scr[...]
    m_new = jnp.maximum(m_prev, jnp.max(s, axis=0, keepdims=True))
    alpha = jnp.exp(m_prev - m_new)
    p = jnp.exp(s - m_new)
    l_scr[...] = alpha * l_scr[...] + jnp.sum(p, axis=0, keepdims=True)
    p = p.astype(BF16)
    v0 = vget(0, start, size)
    if vget(1, start, size) is None:
        pv = _dot(v0, p)
    else:
        pv = jnp.concatenate([_dot(v0, p[:, :TQ])] + [_dot(vget(c, start, size), p[:, c * TQ:(c + 1) * TQ])
                                                      for c in range(1, nq)], axis=1)
    acc_scr[...] = alpha * acc_scr[...] + pv
    m_scr[...] = m_new


def _flash_step(kget, vget, qxs, start, size, m_scr, l_scr, acc_scr, mask=None):
    _softmax_pv(_qk(kget, qxs, start, size), vget, len(qxs), start, size, m_scr, l_scr, acc_scr, mask)


def _rows(ref, start, size, lane0=0):
    return ref[pl.ds(pl.multiple_of(start, TQ), size), lane0:lane0 + SLOT]


def _cols(ref, start, size, row0=0):
    return ref[row0:row0 + SLOT, pl.ds(pl.multiple_of(start, TQ), size)]


def _flash_causal(kget, vget, qxs, i, sa_scr, sb_scr, m_scr, l_scr, acc_scr):
    tq = TQ
    tk = 2 * tq
    nq = len(qxs)
    rr = nq * tq
    n_steps = i // 2 + 1
    sa_scr[...] = _qk(kget, qxs, 0, tk)

    def body(jj, carry):
        j = 2 * jj
        s = sa_scr[...]
        sb_scr[...] = _qk(kget, qxs, (j + 1) * tk, tk)
        _softmax_pv(s, vget, nq, j * tk, tk, m_scr, l_scr, acc_scr)
        s = sb_scr[...]
        sa_scr[...] = _qk(kget, qxs, (j + 2) * tk, tk)
        _softmax_pv(s, vget, nq, (j + 1) * tk, tk, m_scr, l_scr, acc_scr)
        return carry

    lax.fori_loop(0, (n_steps - 1) // 2, body, 0)
    last = n_steps - 1
    valid = (last * tk + lax.broadcasted_iota(I32, (tk, rr), 0)
             <= i * tq + (lax.broadcasted_iota(I32, (tk, rr), 1) & (tq - 1)))

    @pl.when(last % 2 == 0)
    def _():
        _softmax_pv(sa_scr[...], vget, nq, last * tk, tk, m_scr, l_scr, acc_scr, mask=valid)

    @pl.when(last % 2 == 1)
    def _():
        s = sa_scr[...]
        sb_scr[...] = _qk(kget, qxs, last * tk, tk)
        _softmax_pv(s, vget, nq, (last - 1) * tk, tk, m_scr, l_scr, acc_scr)
        _softmax_pv(sb_scr[...], vget, nq, last * tk, tk, m_scr, l_scr, acc_scr, mask=valid)


def _nsa_kernel(q_ref, kc_ref, vct_ref, ks_ref, vst_ref, kw_ref, vwt_ref, g_ref, ovt_ref, fq_ref,
                o_ref, m_scr, l_scr, acc_scr, imp_scr, sa_scr, sb_scr, *, tq, nsel):
    i = pl.program_id(2)
    r4 = NSA_GROUP
    rr = r4 * tq
    q4 = [q_ref[r * SLOT:(r + 1) * SLOT, :] for r in range(r4)]
    fq = fq_ref[0]

    kc = kc_ref[0]
    nc = kc.shape[0]
    qs = jnp.concatenate(q4, axis=1)
    s = _dot(kc, qs)
    t_glob = i * tq + (lax.broadcasted_iota(I32, (nc, rr), 1) & (tq - 1))
    c_idx = lax.broadcasted_iota(I32, (nc, rr), 0)
    cmask = (c_idx * CMP_STRIDE + (CMP_BLOCK - 1)) <= t_glob
    s = jnp.where(cmask, s, NEG)
    mx = jnp.max(s, axis=0, keepdims=True)
    e = jnp.where(cmask, jnp.exp(s - mx), 0.0)
    lsum = jnp.sum(e, axis=0, keepdims=True)
    p_cmp = (e / jnp.where(lsum > 0.0, lsum, 1.0)).astype(BF16)
    o_cmp = _dot(vct_ref[0], p_cmp)

    ovt = ovt_ref[...]
    imp = _dot(ovt, p_cmp[:, 0:tq])
    for r in range(1, r4):
        imp = imp + _dot(ovt, p_cmp[:, r * tq:(r + 1) * tq])
    nbp = imp.shape[0]
    jidx = lax.broadcasted_iota(I32, (nbp, tq), 0)
    cur = (i * tq + lax.broadcasted_iota(I32, (nbp, tq), 1)) // SLC_BLOCK
    forced = (jidx == 0) | (jidx == cur) | (jidx == cur - 1)
    imp = jnp.where(forced, 1e6, jnp.where(jidx > cur, -1e6, imp))
    imp_scr[...] = imp

    def rank_body(jp, rank):
        row = imp_scr[pl.ds(jp, 1), :]
        better = (row > imp) | ((row == imp) & (jp < jidx))
        return rank + better.astype(I32)

    n_live = (i + 1) * (tq // SLC_BLOCK)
    rank = lax.fori_loop(0, n_live, rank_body, jnp.zeros((nbp, tq), I32))
    sel = (rank < nsel) & (jidx <= cur)
    mask_t = jnp.where(sel, 0.0, NEG)
    feat = jnp.concatenate([jnp.zeros((SLOT - nbp, tq), F32), mask_t], axis=0)

    qxs = [q4[r] + (feat + fq[:, r:r + 1]).astype(BF16) for r in range(r4)]
    kget = lambda c, start, size: _rows(ks_ref, start, size) if c == 0 else None
    vget = lambda c, start, size: _cols(vst_ref, start, size) if c == 0 else None
    _flash_reset(m_scr, l_scr, acc_scr)
    _flash_causal(kget, vget, qxs, i, sa_scr, sb_scr, m_scr, l_scr, acc_scr)
    o_slc = acc_scr[...] / l_scr[...]

    qx = jnp.concatenate([q4[r] + fq[:, r4 + r:r4 + r + 1].astype(BF16) for r in range(r4)], axis=1)
    w0 = jnp.maximum(i - 2, 0) * tq
    s = _dot(_rows(kw_ref, w0, 3 * tq), qx)
    dist = (i * tq + (lax.broadcasted_iota(I32, (3 * tq, rr), 1) & (tq - 1))
            - (w0 + lax.broadcasted_iota(I32, (3 * tq, rr), 0)))
    s = jnp.where(dist >= 0, jnp.where(dist < WINDOW, s, NEG), NEG)
    e = jnp.exp(s - jnp.max(s, axis=0, keepdims=True))
    o_win = _dot(_cols(vwt_ref, w0, 3 * tq), e.astype(BF16)) / jnp.sum(e, axis=0, keepdims=True)

    gate = jax.nn.sigmoid(g_ref[...])
    for r in range(r4):
        sl = slice(r * tq, (r + 1) * tq)
        o = (gate[3 * r:3 * r + 1, :] * o_cmp[:, sl] + gate[3 * r + 1:3 * r + 2, :] * o_slc[:, sl]
             + gate[3 * r + 2:3 * r + 3, :] * o_win[:, sl])
        o_ref[:, r * SLOT:(r + 1) * SLOT] = o.T.astype(o_ref.dtype)


def _nsa_attention(zat, zb, gt, kc, vct, ovt, fq, bsz, seq):
    tq = TQ
    ni = seq // tq
    nc = kc.shape[1]
    nsel = min(SLC_TOPK, seq // SLC_BLOCK)
    assert seq // SLC_BLOCK <= 64 and WINDOW == 2 * tq and ni % 2 == 0 and ni >= 3
    rr = NSA_GROUP * tq
    gw = NSA_GROUP * SLOT
    in_specs = [
        pl.BlockSpec((gw, tq), lambda b, g, i: (g, b * ni + i)),
        pl.BlockSpec((1, nc, SLOT), lambda b, g, i: (b, 0, g)),
        pl.BlockSpec((1, SLOT, nc), lambda b, g, i: (b, g, 0)),
        pl.BlockSpec((seq, SLOT), lambda b, g, i: (b, g)),
        pl.BlockSpec((SLOT, seq), lambda b, g, i: (_SLOT_SLC_V + g, b)),
        pl.BlockSpec((seq, SLOT), lambda b, g, i: (b, 2 + g)),
        pl.BlockSpec((SLOT, seq), lambda b, g, i: (_SLOT_WIN_V + g, b)),
        pl.BlockSpec((SLOT, tq), lambda b, g, i: (g, b * ni + i)),
        pl.BlockSpec(ovt.shape, lambda b, g, i: (0, 0)),
        pl.BlockSpec((1, SLOT, 8), lambda b, g, i: (g, 0, 0)),
    ]
    return pl.pallas_call(
        functools.partial(_nsa_kernel, tq=tq, nsel=nsel),
        out_shape=jax.ShapeDtypeStruct((bsz * seq, NSA_HEADS * SLOT), BF16),
        grid=(bsz, NSA_KV_HEADS, ni),
        in_specs=in_specs,
        out_specs=pl.BlockSpec((tq, gw), lambda b, g, i: (b * ni + i, g)),
        scratch_shapes=[pltpu.VMEM((1, rr), F32), pltpu.VMEM((1, rr), F32),
                        pltpu.VMEM((SLOT, rr), F32), pltpu.VMEM((64, tq), F32),
                        pltpu.VMEM((2 * tq, rr), F32), pltpu.VMEM((2 * tq, rr), F32)],
        compiler_params=_cparams(("parallel", "parallel", "arbitrary")),
        name="nsa_attention",
    )(zat, kc, vct, zb, zat, zb, zat, gt, ovt, fq)


def _moba_kernel(q_ref, k_ref, vt_ref, fq_ref, o_ref, m_scr, l_scr, acc_scr, km_scr, gate_scr,
                 sa_scr, sb_scr, *, tq, nblk, ntop):
    i = pl.program_id(2)
    nh = MOBA_HPS

    @pl.when(i == 0)
    def _():
        for hh in range(nh):
            kh = k_ref[:, hh * SLOT:(hh + 1) * SLOT].astype(F32)
            km = jnp.mean(kh.reshape(nblk, tq, SLOT), axis=1)
            if nblk < 16:
                km = jnp.concatenate([km, jnp.zeros((16 - nblk, SLOT), F32)], axis=0)
            km_scr[hh] = km

    qs = [q_ref[hh * SLOT:(hh + 1) * SLOT, :] for hh in range(nh)]
    gates = [_dot(km_scr[hh].astype(BF16), qs[hh]) for hh in range(nh)]
    for hh in range(nh):
        gate_scr[hh] = gates[hh]
    jidx = lax.broadcasted_iota(I32, (16, tq), 0)

    def rank_body(jp, ranks):
        out = []
        for hh in range(nh):
            row = gate_scr[hh, pl.ds(jp, 1), :]
            better = (row > gates[hh]) | ((row == gates[hh]) & (jp < jidx))
            out.append(ranks[hh] + better.astype(I32))
        return tuple(out)

    ranks = lax.fori_loop(0, i, rank_body, tuple(jnp.zeros((16, tq), I32) for _ in range(nh)))
    qxs = []
    for hh in range(nh):
        keep = ((jidx < i) & (ranks[hh] < ntop)) | (jidx == i)
        mask_t = jnp.where(keep, 0.0, NEG)
        feat = jnp.concatenate([jnp.zeros((HEAD_DIM, tq), F32), mask_t,
                                jnp.zeros((SLOT - HEAD_DIM - 16, tq), F32)], axis=0)
        qxs.append(qs[hh] + (feat + fq_ref[hh][:, 0:1]).astype(BF16))
    kget = lambda c, start, size: _rows(k_ref, start, size, c * SLOT)
    vget = lambda c, start, size: _cols(vt_ref, start, size, c * SLOT)
    _flash_reset(m_scr, l_scr, acc_scr)
    _flash_causal(kget, vget, qxs, i, sa_scr, sb_scr, m_scr, l_scr, acc_scr)
    o = acc_scr[...] / l_scr[...]
    for hh in range(nh):
        o_ref[:, hh * SLOT:(hh + 1) * SLOT] = o[:, hh * tq:(hh + 1) * tq].T.astype(o_ref.dtype)


def _moba_attention(zat, zb, fq, bsz, seq):
    tq = TQ
    assert tq == MOBA_BLOCK and seq % tq == 0
    ni = seq // tq
    assert ni <= 16 and ni % 2 == 0
    ntop = min(MOBA_TOPK, ni)
    nh = MOBA_HPS
    hw = nh * SLOT
    q0 = _SLOT_MOBA_Q // nh
    v0 = _SLOT_MOBA_V // nh
    return pl.pallas_call(
        functools.partial(_moba_kernel, tq=tq, nblk=ni, ntop=ntop),
        out_shape=jax.ShapeDtypeStruct((bsz * seq, MOBA_HEADS * SLOT), BF16),
        grid=(bsz, MOBA_HEADS // nh, ni),
        in_specs=[pl.BlockSpec((hw, tq), lambda b, h, i: (q0 + h, b * ni + i)),
                  pl.BlockSpec((seq, hw), lambda b, h, i: (b, 1 + h)),
                  pl.BlockSpec((hw, seq), lambda b, h, i: (v0 + h, b)),
                  pl.BlockSpec((nh, SLOT, 8), lambda b, h, i: (h, 0, 0))],
        out_specs=pl.BlockSpec((tq, hw), lambda b, h, i: (b * ni + i, h)),
        scratch_shapes=[pltpu.VMEM((1, nh * tq), F32), pltpu.VMEM((1, nh * tq), F32),
                        pltpu.VMEM((SLOT, nh * tq), F32),
                        pltpu.VMEM((nh, 16, SLOT), F32), pltpu.VMEM((nh, 16, tq), F32),
                        pltpu.VMEM((2 * tq, nh * tq), F32), pltpu.VMEM((2 * tq, nh * tq), F32)],
        compiler_params=_cparams(("parallel", "parallel", "arbitrary")),
        name="moba_attention",
    )(zat, zb, zat, fq)


def _merge_kernel(ya_ref, yb_ref, hc_ref, bc_ref, cc_ref, hcp_ref, ccp_ref, g0_ref, g1_ref, g2_ref,
                  h_ref, cw_ref, wa_ref, wb_ref, wc_ref, wo_ref, lg_ref, lb_ref, o_ref, *, tm, per, alpha):
    i = pl.program_id(0)
    u = cc_ref[...].astype(F32) * hc_ref[...].astype(F32)
    first = (i % per) == 0
    up = jnp.where(first, 0.0, ccp_ref[...].astype(F32) * hcp_ref[...].astype(F32))
    rowi = lax.broadcasted_iota(I32, u.shape, 0)
    u1 = jnp.where(rowi == 0, up[15:16, :], pltpu.roll(u, 1, 0))
    u2 = jnp.where(rowi == 0, up[14:15, :], jnp.where(rowi == 1, up[15:16, :], pltpu.roll(u, 2, 0)))
    cw = cw_ref[...]
    yc = bc_ref[...].astype(F32) * (cw[0:1, :] * u2 + cw[1:2, :] * u1 + cw[2:3, :] * u)
    merged = (jax.nn.sigmoid(g0_ref[...].astype(F32)) * _dot(ya_ref[...], wa_ref[...])
              + jax.nn.sigmoid(g1_ref[...].astype(F32)) * _dot(yb_ref[...], wb_ref[...])
              + jax.nn.sigmoid(g2_ref[...].astype(F32)) * _dot(yc.astype(BF16), wc_ref[...]))
    mix = _dot(merged.astype(BF16), wo_ref[...])
    o_ref[...] = _layer_norm(alpha * h_ref[...] + mix, lg_ref[...], lb_ref[...])


def _merge(ya, yb, zc, h, cw, wa, wb, wc, wo, lg, lb, seq, alpha):
    tm = 512
    t = h.shape[0]
    per = seq // tm
    d = D_MODEL
    row = lambda i: (i, 0)
    const = lambda i: (0, 0)
    prev = lambda c: (lambda i: (jnp.maximum(i * (tm // 16) - 1, 0), c))
    in_specs = [
        pl.BlockSpec((tm, d), row), pl.BlockSpec((tm, d), row),
        pl.BlockSpec((tm, CONV_CH), lambda i: (i, 1)), pl.BlockSpec((tm, CONV_CH), lambda i: (i, 2)),
        pl.BlockSpec((tm, CONV_CH), lambda i: (i, 3)),
        pl.BlockSpec((16, CONV_CH), prev(1)), pl.BlockSpec((16, CONV_CH), prev(3)),
        pl.BlockSpec((tm, d), lambda i: (i, 2)), pl.BlockSpec((tm, d), lambda i: (i, 3)),
        pl.BlockSpec((tm, d), lambda i: (i, 4)),
        pl.BlockSpec((tm, d), row),
        pl.BlockSpec((8, CONV_CH), const),
        pl.BlockSpec((d, d), const), pl.BlockSpec((d, d), const), pl.BlockSpec((CONV_CH, d), const),
        pl.BlockSpec((d, d), const), pl.BlockSpec((1, d), const), pl.BlockSpec((1, d), const),
    ]
    return pl.pallas_call(
        functools.partial(_merge_kernel, tm=tm, per=per, alpha=alpha),
        out_shape=jax.ShapeDtypeStruct((t, d), F32),
        grid=(t // tm,),
        in_specs=in_specs,
        out_specs=pl.BlockSpec((tm, d), row),
        compiler_params=_cparams(("parallel",)),
        name="mixer_merge",
    )(ya, yb, zc, zc, zc, zc, zc, zc, zc, zc, h, cw, wa, wb, wc, wo, lg, lb)


def _xattn_kernel(h_ref, kv_ref, wq_ref, wo_ref, lg_ref, lb_ref, o_ref, ob_ref, *, alpha):
    h = h_ref[...]
    q = _dot(h.astype(BF16), wq_ref[...]).astype(BF16)
    kv = kv_ref[...]
    nh = XATTN_HEADS
    hd = XATTN_HEAD_DIM
    outs = []
    for hh in range(nh):
        s = _dot_t(q[:, hh * hd:(hh + 1) * hd], kv[:, hh * hd:(hh + 1) * hd]) * (hd ** -0.5)
        s = s - jnp.max(s, axis=-1, keepdims=True)
        e = jnp.exp(s)
        p = e / jnp.sum(e, axis=-1, keepdims=True)
        outs.append(_dot(p.astype(BF16), kv[:, (nh + hh) * hd:(nh + hh + 1) * hd]))
    o = jnp.concatenate(outs, axis=-1).astype(BF16)
    y = _layer_norm(alpha * h + _dot(o, wo_ref[...]), lg_ref[...], lb_ref[...])
    o_ref[...] = y
    ob_ref[...] = y.astype(BF16)


def _xattn(h, kv, wq, wo, lg, lb, seq, mlen, alpha):
    tm = 512
    t = h.shape[0]
    per = seq // tm
    d = D_MODEL
    row = lambda i: (i, 0)
    const = lambda i: (0, 0)
    return pl.pallas_call(
        functools.partial(_xattn_kernel, alpha=alpha),
        out_shape=(jax.ShapeDtypeStruct((t, d), F32), jax.ShapeDtypeStruct((t, d), BF16)),
        grid=(t // tm,),
        in_specs=[pl.BlockSpec((tm, d), row), pl.BlockSpec((mlen, kv.shape[1]), lambda i: (i // per, 0)),
                  pl.BlockSpec(wq.shape, const), pl.BlockSpec(wo.shape, const),
                  pl.BlockSpec((1, d), const), pl.BlockSpec((1, d), const)],
        out_specs=(pl.BlockSpec((tm, d), row), pl.BlockSpec((tm, d), row)),
        compiler_params=_cparams(("parallel",)),
        name="mem_xattn",
    )(h, kv, wq, wo, lg, lb)


def _expert_kernel(be_ref, nu_ref, x_ref, w13_ref, w2_ref, o_ref, w13b_scr, w2b_scr):
    i = pl.program_id(0)

    @pl.when((i == 0) | (be_ref[i] != be_ref[jnp.maximum(i - 1, 0)]))
    def _():
        w13b_scr[...] = w13_ref[0, 0].astype(BF16)
        w2b_scr[...] = w2_ref[0, 0].astype(BF16)

    @pl.when(i < nu_ref[0])
    def _():
        hmid = _dot(x_ref[...], w13b_scr[...])
        a = hmid[:, :EXPERT_DIM]
        g = hmid[:, EXPERT_DIM:]
        act = (a * jax.nn.sigmoid(a) * g).astype(BF16)
        o_ref[...] = _dot(act, w2b_scr[...]).astype(o_ref.dtype)


def _experts(xg, w13, w2, block_e, n_used, layer):
    n_pad, d = xg.shape
    rows = EXP_ROWS
    nb = n_pad // rows
    blk = lambda i, be, nu: (jnp.minimum(i, nu[0] - 1), 0)
    grid_spec = pltpu.PrefetchScalarGridSpec(
        num_scalar_prefetch=2,
        grid=(nb,),
        in_specs=[pl.BlockSpec((rows, d), blk),
                  pl.BlockSpec((1, 1, d, 2 * EXPERT_DIM), lambda i, be, nu: (layer, be[i], 0, 0)),
                  pl.BlockSpec((1, 1, EXPERT_DIM, d), lambda i, be, nu: (layer, be[i], 0, 0))],
        out_specs=pl.BlockSpec((rows, d), blk),
        scratch_shapes=[pltpu.VMEM((d, 2 * EXPERT_DIM), BF16), pltpu.VMEM((EXPERT_DIM, d), BF16)],
    )
    return pl.pallas_call(
        _expert_kernel,
        out_shape=jax.ShapeDtypeStruct((n_pad, d), BF16),
        grid_spec=grid_spec,
        compiler_params=_cparams(("arbitrary",)),
        name="moe_experts",
    )(block_e, n_used, xg, w13, w2)


def _moe_out_kernel(h_ref, y8_ref, wk_ref, w13_ref, w2_ref, lg_ref, lb_ref, o_ref, ob_ref, *, alpha):
    h = h_ref[...]
    hmid = _dot(h.astype(BF16), w13_ref[...])
    a = hmid[:, :EXPERT_DIM]
    g = hmid[:, EXPERT_DIM:]
    acc = alpha * h + _dot((a * jax.nn.sigmoid(a) * g).astype(BF16), w2_ref[...])
    wk = wk_ref[...]
    for k in range(TOP_K):
        acc = acc + wk[:, k:k + 1] * y8_ref[k].astype(F32)
    y = _layer_norm(acc, lg_ref[...], lb_ref[...])
    o_ref[...] = y
    ob_ref[...] = y.astype(BF16)


def _moe_out(h, y8, wk, w13, w2, lg, lb, alpha):
    tm = 256
    t, d = h.shape
    row = lambda i: (i, 0)
    const = lambda i: (0, 0)
    return pl.pallas_call(
        functools.partial(_moe_out_kernel, alpha=alpha),
        out_shape=(jax.ShapeDtypeStruct((t, d), F32), jax.ShapeDtypeStruct((t, d), BF16)),
        grid=(t // tm,),
        in_specs=[pl.BlockSpec((tm, d), row), pl.BlockSpec((TOP_K, tm, d), lambda i: (0, i, 0)),
                  pl.BlockSpec((tm, TOP_K), row),
                  pl.BlockSpec(w13.shape, const), pl.BlockSpec(w2.shape, const),
                  pl.BlockSpec((1, d), const), pl.BlockSpec((1, d), const)],
        out_specs=(pl.BlockSpec((tm, d), row), pl.BlockSpec((tm, d), row)),
        compiler_params=_cparams(("parallel",)),
        name="moe_shared_ln",
    )(h, y8, wk, w13, w2, lg, lb)


def _dest_kernel(idx_ref, rank_ref, ps_ref, o_ref):
    idx = idx_ref[...]
    ps = ps_ref[...]
    tm = idx.shape[1]
    eidx = lax.broadcasted_iota(I32, (N_EXPERTS, tm), 0)
    rows = [jnp.sum(jnp.where(eidx == idx[k:k + 1, :], ps, 0.0), axis=0, keepdims=True) for k in range(TOP_K)]
    o_ref[...] = jnp.concatenate(rows, axis=0).astype(I32) + rank_ref[...]


def _dest(idx, rank, pstarts):
    tm = 1024
    t = idx.shape[1]
    tm = min(tm, t)
    col = lambda i: (0, i)
    return pl.pallas_call(
        _dest_kernel,
        out_shape=jax.ShapeDtypeStruct((TOP_K, t), I32),
        grid=(t // tm,),
        in_specs=[pl.BlockSpec((TOP_K, tm), col), pl.BlockSpec((TOP_K, tm), col),
                  pl.BlockSpec((N_EXPERTS, 1), lambda i: (0, 0))],
        out_specs=pl.BlockSpec((TOP_K, tm), col),
        compiler_params=_cparams(("parallel",)),
        name="moe_dest",
    )(idx, rank, pstarts)


def _router_kernel(h_ref, rwt_ref, rb_ref, tri_ref, idx_ref, w_ref, rank_ref, cnt_ref, carry_scr, *, tm):
    i = pl.program_id(0)

    @pl.when(i == 0)
    def _():
        carry_scr[...] = jnp.zeros(carry_scr.shape, F32)

    ne = N_EXPERTS
    gsz = ne // N_GROUPS
    s = jax.nn.sigmoid(_dot_t(rwt_ref[...], h_ref[...].astype(BF16)))
    sb = s + rb_ref[...]
    sb3 = sb.reshape(N_GROUPS, gsz, tm)
    li = lax.broadcasted_iota(I32, (N_GROUPS, gsz, tm), 1)
    m1 = jnp.max(sb3, axis=1, keepdims=True)
    first = jnp.min(jnp.where(sb3 == m1, li, gsz), axis=1, keepdims=True)
    m2 = jnp.max(jnp.where(li == first, -jnp.inf, sb3), axis=1, keepdims=True)
    gs = (m1 + m2).reshape(N_GROUPS, tm)
    gi = lax.broadcasted_iota(I32, (N_GROUPS, tm), 0)
    grank = jnp.zeros((N_GROUPS, tm), I32)
    for gp in range(N_GROUPS):
        row = gs[gp:gp + 1, :]
        grank = grank + ((row > gs) | ((row == gs) & (gp < gi))).astype(I32)
    gkeep = (grank < TOPK_GROUPS).astype(F32)
    ekeep = jnp.broadcast_to(gkeep[:, None, :], (N_GROUPS, gsz, tm)).reshape(ne, tm)
    cand = jnp.where(ekeep > 0.0, sb, NEG)
    eidx = lax.broadcasted_iota(I32, (ne, tm), 0)
    sel = jnp.zeros((ne, tm), F32)
    idxs, wts = [], []
    for _ in range(TOP_K):
        mx = jnp.max(cand, axis=0, keepdims=True)
        ik = jnp.min(jnp.where(cand == mx, eidx, ne), axis=0, keepdims=True)
        hit = eidx == ik
        wts.append(jnp.sum(jnp.where(hit, s, 0.0), axis=0, keepdims=True))
        idxs.append(ik)
        sel = jnp.where(hit, 1.0, sel)
        cand = jnp.where(hit, -jnp.inf, cand)
    before = _dot(sel.astype(BF16), tri_ref[...]) + carry_scr[...]
    ranks = [jnp.sum(jnp.where(eidx == ik, before, 0.0), axis=0, keepdims=True) for ik in idxs]
    carry_scr[...] = carry_scr[...] + jnp.sum(sel, axis=1, keepdims=True)
    w = jnp.concatenate(wts, axis=0)
    idx_ref[...] = jnp.concatenate(idxs, axis=0)
    w_ref[...] = w / jnp.sum(w, axis=0, keepdims=True) * ROUTE_SCALE
    rank_ref[...] = jnp.concatenate(ranks, axis=0).astype(I32)
    cnt_ref[...] = jnp.broadcast_to(carry_scr[...], cnt_ref.shape)


def _router(h, rwt, rb):
    tm = 256
    t, d = h.shape
    tri = jnp.asarray(np.triu(np.ones((tm, tm), np.float32), 1)).astype(BF16)
    const = lambda i: (0, 0)
    col = lambda i: (0, i)
    return pl.pallas_call(
        functools.partial(_router_kernel, tm=tm),
        out_shape=(jax.ShapeDtypeStruct((TOP_K, t), I32), jax.ShapeDtypeStruct((TOP_K, t), F32),
                   jax.ShapeDtypeStruct((TOP_K, t), I32), jax.ShapeDtypeStruct((N_EXPERTS, SLOT), F32)),
        grid=(t // tm,),
        in_specs=[pl.BlockSpec((tm, d), lambda i: (i, 0)), pl.BlockSpec((N_EXPERTS, d), const),
                  pl.BlockSpec((N_EXPERTS, 1), const), pl.BlockSpec((tm, tm), const)],
        out_specs=(pl.BlockSpec((TOP_K, tm), col), pl.BlockSpec((TOP_K, tm), col),
                   pl.BlockSpec((TOP_K, tm), col), pl.BlockSpec((N_EXPERTS, SLOT), const)),
        scratch_shapes=[pltpu.VMEM((N_EXPERTS, 1), F32)],
        compiler_params=_cparams(("arbitrary",)),
        name="moe_router",
    )(h, rwt, rb, tri)


def _moe(h, hb, router_w, router_b, exp_w13, exp_w2, layer, shared_w13, shared_w2, lg, lb, alpha):
    t, d = h.shape
    idx, wts, rank, cnt = _router(h, router_w.T, router_b.astype(F32).reshape(N_EXPERTS, 1))
    rows = EXP_ROWS
    n_a = t * TOP_K
    counts = cnt[:, 0].astype(I32)
    pcounts = (counts + rows - 1) // rows * rows
    pends = jnp.cumsum(pcounts)
    pstarts = pends - pcounts
    dest = _dest(idx, rank, pstarts.astype(F32).reshape(N_EXPERTS, 1))
    n_blocks = -(-n_a // rows) + N_EXPERTS
    n_pad = n_blocks * rows
    tok = jnp.broadcast_to(jnp.arange(t, dtype=I32)[None, :], (TOP_K, t))
    row_tok = (jnp.arange(n_pad, dtype=I32) % t).at[dest.reshape(-1)].set(tok.reshape(-1), unique_indices=True)
    n_used = (pends[-1:] // rows).astype(I32)
    first_row = jnp.minimum(jnp.arange(n_blocks, dtype=I32), n_used - 1) * rows
    block_e = jnp.sum((pends[None, :] <= first_row[:, None]).astype(I32), axis=1)
    xg = jnp.take(hb, row_tok, axis=0, mode="clip")
    out = _experts(xg, exp_w13, exp_w2, block_e, n_used, layer)
    y8 = jnp.take(out, dest, axis=0, mode="clip")
    return _moe_out(h, y8, wts.T, shared_w13, shared_w2, lg, lb, alpha)


def _pad_slots(w, scale=1.0):
    dm = w.shape[0]
    nh = w.shape[1] // HEAD_DIM
    w = (w * scale).reshape(dm, nh, HEAD_DIM)
    return jnp.concatenate([w, jnp.zeros_like(w)], axis=-1).reshape(dm, nh * SLOT)


def _pad_rows(w):
    n = w.shape[1]
    nh = w.shape[0] // HEAD_DIM
    w = w.reshape(nh, HEAD_DIM, n)
    return jnp.concatenate([w, jnp.zeros_like(w)], axis=1).reshape(nh * SLOT, n)


def _alibi(n):
    return np.exp2(-8.0 * np.arange(1, n + 1, dtype=np.float64) / n).astype(np.float32)


def _key_features(seq):
    p = np.arange(seq)
    slc = np.zeros((seq, SLOT), np.float32)
    slc[:, HEAD_DIM] = p % SLC_BLOCK
    blk = p // SLC_BLOCK
    nz = blk > 0
    slc[p[nz], HEAD_DIM + blk[nz]] = 1.0
    win = np.zeros((seq, SLOT), np.float32)
    win[:, HEAD_DIM] = p // 64
    win[:, HEAD_DIM + 1] = p % 64
    mob = np.zeros((seq, SLOT), np.float32)
    mob[p, HEAD_DIM + p // MOBA_BLOCK] = 1.0
    mob[:, HEAD_DIM + 16] = p % MOBA_BLOCK
    return np.concatenate([slc] * 2 + [win] * 2 + [mob] * MOBA_HEADS, axis=1)


def _query_features():
    sl = _alibi(NSA_HEADS)
    nsa = np.zeros((NSA_KV_HEADS, SLOT, 8), np.float32)
    for g in range(NSA_KV_HEADS):
        for r in range(NSA_GROUP):
            s = sl[g * NSA_GROUP + r]
            nsa[g, HEAD_DIM, r] = s
            nsa[g, HEAD_DIM + 1:, r] = s * SLC_BLOCK * np.arange(1, 64)
            nsa[g, HEAD_DIM, NSA_GROUP + r] = s * 64
            nsa[g, HEAD_DIM + 1, NSA_GROUP + r] = s
    sm = _alibi(MOBA_HEADS)
    mob = np.zeros((MOBA_HEADS, SLOT, 8), np.float32)
    for h in range(MOBA_HEADS):
        mob[h, HEAD_DIM:HEAD_DIM + 16, 0] = sm[h] * MOBA_BLOCK * np.arange(16)
        mob[h, HEAD_DIM + 16, 0] = sm[h]
    return nsa, mob


def _overlap_t(seq):
    nc = seq // CMP_STRIDE
    c_start = np.arange(nc) * CMP_STRIDE
    b_start = np.arange(64) * SLC_BLOCK
    ov = ((c_start[None, :] < (b_start + SLC_BLOCK)[:, None])
          & ((c_start + CMP_BLOCK)[None, :] > b_start[:, None])
          & (np.arange(nc) < nc - 1)[None, :] & (b_start < seq)[:, None])
    return ov.astype(np.float32)


def _compress_weights(pe, w1, w2):
    ty = np.array([0, 0, 1, 1])
    eye = jnp.eye(4, dtype=F32)
    w1r = w1.reshape(2, CMP_BLOCK, HEAD_DIM, CMP_HIDDEN)[ty]
    top = jnp.einsum('spdj,sS->psdSj', w1r[:, :CMP_STRIDE], eye).reshape(CMP_STRIDE * 256, 4 * CMP_HIDDEN)
    bot = jnp.einsum('spdj,sS->psdSj', w1r[:, CMP_STRIDE:], eye).reshape(CMP_STRIDE * 256, 4 * CMP_HIDDEN)
    per = pe[ty]
    pet = jnp.transpose(per[:, :CMP_STRIDE], (1, 0, 2)).reshape(1, CMP_STRIDE * 256)
    peb = jnp.transpose(per[:, CMP_STRIDE:], (1, 0, 2)).reshape(1, CMP_STRIDE * 256)
    w2p = jnp.concatenate([w2[ty], jnp.zeros((4, CMP_HIDDEN, SLOT - HEAD_DIM), F32)], axis=-1)
    w2b = jnp.einsum('sjd,sS->sjSd', w2p, eye).reshape(4 * CMP_HIDDEN, 4 * SLOT)
    w2k = w2b[:, :2 * SLOT].astype(BF16)
    w2vt = w2b[:, 2 * SLOT:].T.astype(BF16)
    return pet, peb, top.astype(BF16), bot.astype(BF16), w2k, w2vt


def kernel(x, mem, w_in, cmp_pe, cmp_w1, cmp_w2, conv_w, w_branch, w_out, ln1_g, ln1_b,
           xattn_wq, xattn_wkv, xattn_wo, ln2_g, ln2_b, router_w, router_b, exp_w13, exp_w2,
           shared_w13, shared_w2, ln3_g, ln3_b):
    bsz, seq, d = x.shape
    mlen = mem.shape[1]
    depth = w_in.shape[0]
    alpha = (2.0 * depth) ** 0.25
    t = bsz * seq
    scale = HEAD_DIM ** -0.5

    kfeat = jnp.asarray(_key_features(seq))
    fq_nsa, fq_moba = (jnp.asarray(a) for a in _query_features())
    ovt = jnp.asarray(_overlap_t(seq)).astype(BF16)
    memf = mem.reshape(bsz * mlen, d)

    h = x.reshape(t, d)
    hb = h.astype(BF16)
    for l in range(depth):
        ht = hb.T
        wi = w_in[l]
        kv6 = wi[:, _OFF_NSA_KV:_OFF_NSA_G].reshape(d, 6, NSA_KV_HEADS * HEAD_DIM)
        mq, mk, mv = (wi[:, _OFF_MOBA + j * 512:_OFF_MOBA + (j + 1) * 512] for j in range(3))
        w_at = jnp.concatenate([wi[:, :512] * scale, kv6[:, 3], kv6[:, 5], mq * scale, mv],
                               axis=1).T.astype(BF16)
        w_b = jnp.concatenate([_pad_slots(kv6[:, 2]), _pad_slots(kv6[:, 4]), _pad_slots(mk)], axis=1).astype(BF16)
        wg = wi[:, _OFF_NSA_G:_OFF_MOBA].reshape(d, NSA_KV_HEADS, 12)
        wgt = jnp.concatenate([wg, jnp.zeros((d, NSA_KV_HEADS, SLOT - 12), F32)], axis=-1)
        wgt = wgt.reshape(d, NSA_KV_HEADS * SLOT).T.astype(BF16)
        w_c = jnp.concatenate([wi[:, _OFF_NSA_KV:_OFF_NSA_KV + 256], jnp.zeros((d, 256), F32),
                               wi[:, _OFF_CONV:]], axis=1).astype(BF16)

        zat = _proj_t(w_at, ht)
        gt = _mm(wgt, ht, F32, tm=256, tn=1024, name="proj_gate")
        zb = _mm(hb, w_b, BF16, feats=kfeat, name="proj_b")
        zc = _mm(hb, w_c, BF16, name="proj_c")

        pet, peb, wt, wb, w2k, w2vt = _compress_weights(cmp_pe[l], cmp_w1[l], cmp_w2[l])
        sub = zc[:, :256].reshape(bsz, seq // CMP_STRIDE, CMP_STRIDE * 256)
        kc, vct = _compress(sub, pet, peb, wt, wb, w2k, w2vt)

        ya = _nsa_attention(zat, zb, gt, kc, vct, ovt, fq_nsa, bsz, seq)
        yb = _moba_attention(zat, zb, fq_moba, bsz, seq)

        cw = jnp.concatenate([conv_w[l], jnp.zeros((5, CONV_CH), F32)], axis=0)
        h = _merge(ya, yb, zc, h, cw, _pad_rows(w_branch[l, 0]).astype(BF16),
                   _pad_rows(w_branch[l, 1]).astype(BF16), w_branch[l, 2].astype(BF16),
                   w_out[l].astype(BF16), ln1_g[l][None], ln1_b[l][None], seq, alpha)

        kv = _mm(memf, xattn_wkv[l].astype(BF16), BF16, tm=512, name="xattn_kv")
        h, hb = _xattn(h, kv, xattn_wq[l].astype(BF16), xattn_wo[l].astype(BF16),
                       ln2_g[l][None], ln2_b[l][None], seq, mlen, alpha)

        h, hb = _moe(h, hb, router_w[l].astype(BF16), router_b[l], exp_w13, exp_w2, l,
                     shared_w13[l].astype(BF16), shared_w2[l].astype(BF16), ln3_g[l][None], ln3_b[l][None], alpha)
    return h.reshape(bsz, seq, d)
```

```python
import functools

import jax
import jax.numpy as jnp
import numpy as np
from jax import lax
from jax.experimental import pallas as pl
from jax.experimental.pallas import tpu as pltpu
from jax.experimental.pallas import tpu_sc as plsc

F32 = jnp.float32
BF16 = jnp.bfloat16
I32 = jnp.int32

D_MODEL = 1024
HEAD_DIM = 64
SLOT = 128
NEG = -1e30
LN_EPS = 1e-5

NSA_HEADS = 8
NSA_KV_HEADS = 2
NSA_GROUP = 4
CMP_BLOCK = 32
CMP_STRIDE = 16
CMP_HIDDEN = 256
SLC_BLOCK = 64
SLC_TOPK = 16
WINDOW = 512
MOBA_HEADS = 8
MOBA_BLOCK = 256
MOBA_TOPK = 3
CONV_CH = 512
XATTN_HEADS = 4
XATTN_HEAD_DIM = 128
N_EXPERTS = 256
TOP_K = 8
N_GROUPS = 8
TOPK_GROUPS = 4
EXPERT_DIM = 256
ROUTE_SCALE = 2.5

TQ = 256
MOBA_HPS = 4
EXP_ROWS = 512
VMEM_LIMIT = 48 * 1024 * 1024
SC_CORES = 2
SC_SUBCORES = 16
SC_CHUNK = 64

_OFF_NSA_Q = 0
_OFF_NSA_KV = 512
_OFF_NSA_G = 1280
_OFF_MOBA = 1304
_OFF_CONV = 2840
_OFF_MERGE = 4376

_SLOT_NSA_Q = 0
_SLOT_SLC_V = 8
_SLOT_WIN_V = 10
_SLOT_MOBA_Q = 12
_SLOT_MOBA_V = 20
_N_SLOTS_T = 28


def _cparams(sem):
    return pltpu.CompilerParams(dimension_semantics=sem, vmem_limit_bytes=VMEM_LIMIT)


def _dot(a, b):
    return jnp.dot(a, b, preferred_element_type=F32)


def _dot_t(a, b):
    return lax.dot_general(a, b, (((1,), (1,)), ((), ())), preferred_element_type=F32)


def _layer_norm(x, g, b):
    mu = jnp.mean(x, axis=-1, keepdims=True)
    xc = x - mu
    var = jnp.mean(xc * xc, axis=-1, keepdims=True)
    return xc * lax.rsqrt(var + LN_EPS) * g + b


def _mm_kernel(x_ref, w_ref, o_ref):
    o_ref[...] = _dot(x_ref[...].astype(BF16), w_ref[...]).astype(o_ref.dtype)


def _mm_feat_kernel(x_ref, w_ref, f_ref, o_ref):
    y = _dot(x_ref[...].astype(BF16), w_ref[...]) + f_ref[...]
    o_ref[...] = y.astype(o_ref.dtype)


def _mm(x, w, out_dtype, *, tm=1024, tn=512, feats=None, name):
    m, k = x.shape
    n = w.shape[1]
    tm = min(tm, m)
    tn = min(tn, n)
    assert m % tm == 0 and n % tn == 0, (m, n, tm, tn)
    in_specs = [pl.BlockSpec((tm, k), lambda i, j: (i, 0)),
                pl.BlockSpec((k, tn), lambda i, j: (0, j))]
    args = [x, w]
    if feats is None:
        body = _mm_kernel
    else:
        per = feats.shape[0] // tm
        assert feats.shape[0] % tm == 0
        in_specs.append(pl.BlockSpec((tm, tn), lambda i, j: (i % per, j)))
        args.append(feats)
        body = _mm_feat_kernel
    return pl.pallas_call(
        body,
        out_shape=jax.ShapeDtypeStruct((m, n), out_dtype),
        grid=(m // tm, n // tn),
        in_specs=in_specs,
        out_specs=pl.BlockSpec((tm, tn), lambda i, j: (i, j)),
        compiler_params=_cparams(("parallel", "parallel")),
        name=name,
    )(*args)


def _proj_t_kernel(w_ref, ht_ref, o_ref):
    acc = _dot(w_ref[...], ht_ref[...])
    tn = acc.shape[1]
    zero = jnp.zeros((SLOT - HEAD_DIM, tn), o_ref.dtype)
    for s in range(w_ref.shape[0] // HEAD_DIM):
        o_ref[s * SLOT:s * SLOT + HEAD_DIM, :] = acc[s * HEAD_DIM:(s + 1) * HEAD_DIM, :].astype(o_ref.dtype)
        o_ref[s * SLOT + HEAD_DIM:(s + 1) * SLOT, :] = zero


def _proj_t(w, ht):
    m, k = w.shape
    t = ht.shape[1]
    tn = min(512, t)
    n_out = m // HEAD_DIM * SLOT
    return pl.pallas_call(
        _proj_t_kernel,
        out_shape=jax.ShapeDtypeStruct((n_out, t), BF16),
        grid=(t // tn,),
        in_specs=[pl.BlockSpec((m, k), lambda j: (0, 0)), pl.BlockSpec((k, tn), lambda j: (0, j))],
        out_specs=pl.BlockSpec((n_out, tn), lambda j: (0, j)),
        compiler_params=_cparams(("parallel",)),
        name="proj_at",
    )(w, ht)


def _cmp_kernel(sub_ref, pet_ref, peb_ref, wt_ref, wb_ref, w2k_ref, w2vt_ref, kc_ref, vct_ref):
    sub = sub_ref[0].astype(F32)
    nc = sub.shape[0]
    a = _dot((sub + pet_ref[...]).astype(BF16), wt_ref[...])
    b = _dot((sub + peb_ref[...]).astype(BF16), wb_ref[...])
    hid = jax.nn.gelu(a + pltpu.roll(b, nc - 1, 0)).astype(BF16)
    kc_ref[0] = _dot(hid, w2k_ref[...]).astype(kc_ref.dtype)
    vct_ref[0] = _dot_t(w2vt_ref[...], hid).astype(vct_ref.dtype)


def _compress(sub, pet, peb, wt, wb, w2k, w2vt):
    bsz, nc, kk = sub.shape
    n_h = wt.shape[1]
    n_o = w2k.shape[1]
    const = lambda b: (0, 0)
    return pl.pallas_call(
        _cmp_kernel,
        out_shape=(jax.ShapeDtypeStruct((bsz, nc, n_o), BF16), jax.ShapeDtypeStruct((bsz, n_o, nc), BF16)),
        grid=(bsz,),
        in_specs=[pl.BlockSpec((1, nc, kk), lambda b: (b, 0, 0)),
                  pl.BlockSpec((1, kk), const), pl.BlockSpec((1, kk), const),
                  pl.BlockSpec((kk, n_h), const), pl.BlockSpec((kk, n_h), const),
                  pl.BlockSpec((n_h, n_o), const), pl.BlockSpec((n_o, n_h), const)],
        out_specs=(pl.BlockSpec((1, nc, n_o), lambda b: (b, 0, 0)), pl.BlockSpec((1, n_o, nc), lambda b: (b, 0, 0))),
        compiler_params=_cparams(("parallel",)),
        name="nsa_compress",
    )(sub, pet, peb, wt, wb, w2k, w2vt)


def _flash_reset(m_scr, l_scr, acc_scr):
    m_scr[...] = jnp.full(m_scr.shape, NEG, F32)
    l_scr[...] = jnp.zeros(l_scr.shape, F32)
    acc_scr[...] = jnp.zeros(acc_scr.shape, F32)


def _qk(kget, qxs, start, size):
    k0 = kget(0, start, size)
    if kget(1, start, size) is None:
        return _dot(k0, jnp.concatenate(qxs, axis=1))
    return jnp.concatenate([_dot(k0, qxs[0])] + [_dot(kget(c, start, size), qxs[c])
                                                 for c in range(1, len(qxs))], axis=1)


def _softmax_pv(s, vget, nq, start, size, m_scr, l_scr, acc_scr, mask=None):
    if mask is not None:
        s = jnp.where(mask, s, NEG)
    m_prev = m_scr[...]
    m_new = jnp.maximum(m_prev, jnp.max(s, axis=0, keepdims=True))
    alpha = jnp.exp(m_prev - m_new)
    p = jnp.exp(s - m_new)
    l_scr[...] = alpha * l_scr[...] + jnp.sum(p, axis=0, keepdims=True)
    p = p.astype(BF16)
    v0 = vget(0, start, size)
    if vget(1, start, size) is None:
        pv = _dot(v0, p)
    else:
        pv = jnp.concatenate([_dot(v0, p[:, :TQ])] + [_dot(vget(c, start, size), p[:, c * TQ:(c + 1) * TQ])
                                                      for c in range(1, nq)], axis=1)
    acc_scr[...] = alpha * acc_scr[...] + pv
    m_scr[...] = m_new


def _flash_step(kget, vget, qxs, start, size, m_scr, l_scr, acc_scr, mask=None):
    _softmax_pv(_qk(kget, qxs, start, size), vget, len(qxs), start, size, m_scr, l_scr, acc_scr, mask)


def _rows(ref, start, size, lane0=0):
    return ref[pl.ds(pl.multiple_of(start, TQ), size), lane0:lane0 + SLOT]


def _cols(ref, start, size, row0=0):
    return ref[row0:row0 + SLOT, pl.ds(pl.multiple_of(start, TQ), size)]


def _flash_causal(kget, vget, qxs, i, sa_scr, sb_scr, m_scr, l_scr, acc_scr):
    tq = TQ
    tk = 2 * tq
    nq = len(qxs)
    rr = nq * tq
    n_steps = i // 2 + 1
    sa_scr[...] = _qk(kget, qxs, 0, tk)

    def body(jj, carry):
        j = 2 * jj
        s = sa_scr[...]
        sb_scr[...] = _qk(kget, qxs, (j + 1) * tk, tk)
        _softmax_pv(s, vget, nq, j * tk, tk, m_scr, l_scr, acc_scr)
        s = sb_scr[...]
        sa_scr[...] = _qk(kget, qxs, (j + 2) * tk, tk)
        _softmax_pv(s, vget, nq, (j + 1) * tk, tk, m_scr, l_scr, acc_scr)
        return carry

    lax.fori_loop(0, (n_steps - 1) // 2, body, 0)
    last = n_steps - 1
    valid = (last * tk + lax.broadcasted_iota(I32, (tk, rr), 0)
             <= i * tq + (lax.broadcasted_iota(I32, (tk, rr), 1) & (tq - 1)))

    @pl.when(last % 2 == 0)
    def _():
        _softmax_pv(sa_scr[...], vget, nq, last * tk, tk, m_scr, l_scr, acc_scr, mask=valid)

    @pl.when(last % 2 == 1)
    def _():
        s = sa_scr[...]
        sb_scr[...] = _qk(kget, qxs, last * tk, tk)
        _softmax_pv(s, vget, nq, (last - 1) * tk, tk, m_scr, l_scr, acc_scr)
        _softmax_pv(sb_scr[...], vget, nq, last * tk, tk, m_scr, l_scr, acc_scr, mask=valid)


def _nsa_kernel(q_ref, kc_ref, vct_ref, ks_ref, vst_ref, kw_ref, vwt_ref, g_ref, ovt_ref, fq_ref,
                o_ref, m_scr, l_scr, acc_scr, imp_scr, sa_scr, sb_scr, *, tq, nsel):
    i = pl.program_id(2)
    r4 = NSA_GROUP
    rr = r4 * tq
    q4 = [q_ref[r * SLOT:(r + 1) * SLOT, :] for r in range(r4)]
    fq = fq_ref[0]

    kc = kc_ref[0]
    nc = kc.shape[0]
    qs = jnp.concatenate(q4, axis=1)
    s = _dot(kc, qs)
    t_glob = i * tq + (lax.broadcasted_iota(I32, (nc, rr), 1) & (tq - 1))
    c_idx = lax.broadcasted_iota(I32, (nc, rr), 0)
    cmask = (c_idx * CMP_STRIDE + (CMP_BLOCK - 1)) <= t_glob
    s = jnp.where(cmask, s, NEG)
    mx = jnp.max(s, axis=0, keepdims=True)
    e = jnp.where(cmask, jnp.exp(s - mx), 0.0)
    lsum = jnp.sum(e, axis=0, keepdims=True)
    p_cmp = (e / jnp.where(lsum > 0.0, lsum, 1.0)).astype(BF16)
    o_cmp = _dot(vct_ref[0], p_cmp)

    ovt = ovt_ref[...]
    imp = _dot(ovt, p_cmp[:, 0:tq])
    for r in range(1, r4):
        imp = imp + _dot(ovt, p_cmp[:, r * tq:(r + 1) * tq])
    nbp = imp.shape[0]
    jidx = lax.broadcasted_iota(I32, (nbp, tq), 0)
    cur = (i * tq + lax.broadcasted_iota(I32, (nbp, tq), 1)) // SLC_BLOCK
    forced = (jidx == 0) | (jidx == cur) | (jidx == cur - 1)
    imp = jnp.where(forced, 1e6, jnp.where(jidx > cur, -1e6, imp))
    imp_scr[...] = imp

    def rank_body(jp, rank):
        row = imp_scr[pl.ds(jp, 1), :]
        better = (row > imp) | ((row == imp) & (jp < jidx))
        return rank + better.astype(I32)

    n_live = (i + 1) * (tq // SLC_BLOCK)
    rank = lax.fori_loop(0, n_live, rank_body, jnp.zeros((nbp, tq), I32))
    sel = (rank < nsel) & (jidx <= cur)
    mask_t = jnp.where(sel, 0.0, NEG)
    feat = jnp.concatenate([jnp.zeros((SLOT - nbp, tq), F32), mask_t], axis=0)

    qxs = [q4[r] + (feat + fq[:, r:r + 1]).astype(BF16) for r in range(r4)]
    kget = lambda c, start, size: _rows(ks_ref, start, size) if c == 0 else None
    vget = lambda c, start, size: _cols(vst_ref, start, size) if c == 0 else None
    _flash_reset(m_scr, l_scr, acc_scr)
    _flash_causal(kget, vget, qxs, i, sa_scr, sb_scr, m_scr, l_scr, acc_scr)
    o_slc = acc_scr[...] / l_scr[...]

    qx = jnp.concatenate([q4[r] + fq[:, r4 + r:r4 + r + 1].astype(BF16) for r in range(r4)], axis=1)
    w0 = jnp.maximum(i - 2, 0) * tq
    s = _dot(_rows(kw_ref, w0, 3 * tq), qx)
    dist = (i * tq + (lax.broadcasted_iota(I32, (3 * tq, rr), 1) & (tq - 1))
            - (w0 + lax.broadcasted_iota(I32, (3 * tq, rr), 0)))
    s = jnp.where(dist >= 0, jnp.where(dist < WINDOW, s, NEG), NEG)
    e = jnp.exp(s - jnp.max(s, axis=0, keepdims=True))
    o_win = _dot(_cols(vwt_ref, w0, 3 * tq), e.astype(BF16)) / jnp.sum(e, axis=0, keepdims=True)

    gate = jax.nn.sigmoid(g_ref[...])
    for r in range(r4):
        sl = slice(r * tq, (r + 1) * tq)
        o = (gate[3 * r:3 * r + 1, :] * o_cmp[:, sl] + gate[3 * r + 1:3 * r + 2, :] * o_slc[:, sl]
             + gate[3 * r + 2:3 * r + 3, :] * o_win[:, sl])
        o_ref[:, r * SLOT:(r + 1) * SLOT] = o.T.astype(o_ref.dtype)


def _nsa_attention(zat, zb, gt, kc, vct, ovt, fq, bsz, seq):
    tq = TQ
    ni = seq // tq
    nc = kc.shape[1]
    nsel = min(SLC_TOPK, seq // SLC_BLOCK)
    assert seq // SLC_BLOCK <= 64 and WINDOW == 2 * tq and ni % 2 == 0 and ni >= 3
    rr = NSA_GROUP * tq
    gw = NSA_GROUP * SLOT
    in_specs = [
        pl.BlockSpec((gw, tq), lambda b, g, i: (g, b * ni + i)),
        pl.BlockSpec((1, nc, SLOT), lambda b, g, i: (b, 0, g)),
        pl.BlockSpec((1, SLOT, nc), lambda b, g, i: (b, g, 0)),
        pl.BlockSpec((seq, SLOT), lambda b, g, i: (b, g)),
        pl.BlockSpec((SLOT, seq), lambda b, g, i: (_SLOT_SLC_V + g, b)),
        pl.BlockSpec((seq, SLOT), lambda b, g, i: (b, 2 + g)),
        pl.BlockSpec((SLOT, seq), lambda b, g, i: (_SLOT_WIN_V + g, b)),
        pl.BlockSpec((SLOT, tq), lambda b, g, i: (g, b * ni + i)),
        pl.BlockSpec(ovt.shape, lambda b, g, i: (0, 0)),
        pl.BlockSpec((1, SLOT, 8), lambda b, g, i: (g, 0, 0)),
    ]
    return pl.pallas_call(
        functools.partial(_nsa_kernel, tq=tq, nsel=nsel),
        out_shape=jax.ShapeDtypeStruct((bsz * seq, NSA_HEADS * SLOT), BF16),
        grid=(bsz, NSA_KV_HEADS, ni),
        in_specs=in_specs,
        out_specs=pl.BlockSpec((tq, gw), lambda b, g, i: (b * ni + i, g)),
        scratch_shapes=[pltpu.VMEM((1, rr), F32), pltpu.VMEM((1, rr), F32),
                        pltpu.VMEM((SLOT, rr), F32), pltpu.VMEM((64, tq), F32),
                        pltpu.VMEM((2 * tq, rr), F32), pltpu.VMEM((2 * tq, rr), F32)],
        compiler_params=_cparams(("parallel", "parallel", "arbitrary")),
        name="nsa_attention",
    )(zat, kc, vct, zb, zat, zb, zat, gt, ovt, fq)


def _moba_kernel(q_ref, k_ref, vt_ref, fq_ref, o_ref, m_scr, l_scr, acc_scr, km_scr, gate_scr,
                 sa_scr, sb_scr, *, tq, nblk, ntop):
    i = pl.program_id(2)
    nh = MOBA_HPS

    @pl.when(i == 0)
    def _():
        for hh in range(nh):
            kh = k_ref[:, hh * SLOT:(hh + 1) * SLOT].astype(F32)
            km = jnp.mean(kh.reshape(nblk, tq, SLOT), axis=1)
            if nblk < 16:
                km = jnp.concatenate([km, jnp.zeros((16 - nblk, SLOT), F32)], axis=0)
            km_scr[hh] = km

    qs = [q_ref[hh * SLOT:(hh + 1) * SLOT, :] for hh in range(nh)]
    gates = [_dot(km_scr[hh].astype(BF16), qs[hh]) for hh in range(nh)]
    for hh in range(nh):
        gate_scr[hh] = gates[hh]
    jidx = lax.broadcasted_iota(I32, (16, tq), 0)

    def rank_body(jp, ranks):
        out = []
        for hh in range(nh):
            row = gate_scr[hh, pl.ds(jp, 1), :]
            better = (row > gates[hh]) | ((row == gates[hh]) & (jp < jidx))
            out.append(ranks[hh] + better.astype(I32))
        return tuple(out)

    ranks = lax.fori_loop(0, i, rank_body, tuple(jnp.zeros((16, tq), I32) for _ in range(nh)))
    qxs = []
    for hh in range(nh):
        keep = ((jidx < i) & (ranks[hh] < ntop)) | (jidx == i)
        mask_t = jnp.where(keep, 0.0, NEG)
        feat = jnp.concatenate([jnp.zeros((HEAD_DIM, tq), F32), mask_t,
                                jnp.zeros((SLOT - HEAD_DIM - 16, tq), F32)], axis=0)
        qxs.append(qs[hh] + (feat + fq_ref[hh][:, 0:1]).astype(BF16))
    kget = lambda c, start, size: _rows(k_ref, start, size, c * SLOT)
    vget = lambda c, start, size: _cols(vt_ref, start, size, c * SLOT)
    _flash_reset(m_scr, l_scr, acc_scr)
    _flash_causal(kget, vget, qxs, i, sa_scr, sb_scr, m_scr, l_scr, acc_scr)
    o = acc_scr[...] / l_scr[...]
    for hh in range(nh):
        o_ref[:, hh * SLOT:(hh + 1) * SLOT] = o[:, hh * tq:(hh + 1) * tq].T.astype(o_ref.dtype)


def _moba_attention(zat, zb, fq, bsz, seq):
    tq = TQ
    assert tq == MOBA_BLOCK and seq % tq == 0
    ni = seq // tq
    assert ni <= 16 and ni % 2 == 0
    ntop = min(MOBA_TOPK, ni)
    nh = MOBA_HPS
    hw = nh * SLOT
    q0 = _SLOT_MOBA_Q // nh
    v0 = _SLOT_MOBA_V // nh
    return pl.pallas_call(
        functools.partial(_moba_kernel, tq=tq, nblk=ni, ntop=ntop),
        out_shape=jax.ShapeDtypeStruct((bsz * seq, MOBA_HEADS * SLOT), BF16),
        grid=(bsz, MOBA_HEADS // nh, ni),
        in_specs=[pl.BlockSpec((hw, tq), lambda b, h, i: (q0 + h, b * ni + i)),
                  pl.BlockSpec((seq, hw), lambda b, h, i: (b, 1 + h)),
                  pl.BlockSpec((hw, seq), lambda b, h, i: (v0 + h, b)),
                  pl.BlockSpec((nh, SLOT, 8), lambda b, h, i: (h, 0, 0))],
        out_specs=pl.BlockSpec((tq, hw), lambda b, h, i: (b * ni + i, h)),
        scratch_shapes=[pltpu.VMEM((1, nh * tq), F32), pltpu.VMEM((1, nh * tq), F32),
                        pltpu.VMEM((SLOT, nh * tq), F32),
                        pltpu.VMEM((nh, 16, SLOT), F32), pltpu.VMEM((nh, 16, tq), F32),
                        pltpu.VMEM((2 * tq, nh * tq), F32), pltpu.VMEM((2 * tq, nh * tq), F32)],
        compiler_params=_cparams(("parallel", "parallel", "arbitrary")),
        name="moba_attention",
    )(zat, zb, zat, fq)


def _merge_kernel(ya_ref, yb_ref, hc_ref, bc_ref, cc_ref, hcp_ref, ccp_ref, g0_ref, g1_ref, g2_ref,
                  h_ref, cw_ref, wa_ref, wb_ref, wc_ref, wo_ref, lg_ref, lb_ref, o_ref, *, tm, per, alpha):
    i = pl.program_id(0)
    u = cc_ref[...].astype(F32) * hc_ref[...].astype(F32)
    first = (i % per) == 0
    up = jnp.where(first, 0.0, ccp_ref[...].astype(F32) * hcp_ref[...].astype(F32))
    rowi = lax.broadcasted_iota(I32, u.shape, 0)
    u1 = jnp.where(rowi == 0, up[15:16, :], pltpu.roll(u, 1, 0))
    u2 = jnp.where(rowi == 0, up[14:15, :], jnp.where(rowi == 1, up[15:16, :], pltpu.roll(u, 2, 0)))
    cw = cw_ref[...]
    yc = bc_ref[...].astype(F32) * (cw[0:1, :] * u2 + cw[1:2, :] * u1 + cw[2:3, :] * u)
    merged = (jax.nn.sigmoid(g0_ref[...].astype(F32)) * _dot(ya_ref[...], wa_ref[...])
              + jax.nn.sigmoid(g1_ref[...].astype(F32)) * _dot(yb_ref[...], wb_ref[...])
              + jax.nn.sigmoid(g2_ref[...].astype(F32)) * _dot(yc.astype(BF16), wc_ref[...]))
    mix = _dot(merged.astype(BF16), wo_ref[...])
    o_ref[...] = _layer_norm(alpha * h_ref[...] + mix, lg_ref[...], lb_ref[...])


def _merge(ya, yb, zc, h, cw, wa, wb, wc, wo, lg, lb, seq, alpha):
    tm = 512
    t = h.shape[0]
    per = seq // tm
    d = D_MODEL
    row = lambda i: (i, 0)
    const = lambda i: (0, 0)
    prev = lambda c: (lambda i: (jnp.maximum(i * (tm // 16) - 1, 0), c))
    in_specs = [
        pl.BlockSpec((tm, d), row), pl.BlockSpec((tm, d), row),
        pl.BlockSpec((tm, CONV_CH), lambda i: (i, 1)), pl.BlockSpec((tm, CONV_CH), lambda i: (i, 2)),
        pl.BlockSpec((tm, CONV_CH), lambda i: (i, 3)),
        pl.BlockSpec((16, CONV_CH), prev(1)), pl.BlockSpec((16, CONV_CH), prev(3)),
        pl.BlockSpec((tm, d), lambda i: (i, 2)), pl.BlockSpec((tm, d), lambda i: (i, 3)),
        pl.BlockSpec((tm, d), lambda i: (i, 4)),
        pl.BlockSpec((tm, d), row),
        pl.BlockSpec((8, CONV_CH), const),
        pl.BlockSpec((d, d), const), pl.BlockSpec((d, d), const), pl.BlockSpec((CONV_CH, d), const),
        pl.BlockSpec((d, d), const), pl.BlockSpec((1, d), const), pl.BlockSpec((1, d), const),
    ]
    return pl.pallas_call(
        functools.partial(_merge_kernel, tm=tm, per=per, alpha=alpha),
        out_shape=jax.ShapeDtypeStruct((t, d), F32),
        grid=(t // tm,),
        in_specs=in_specs,
        out_specs=pl.BlockSpec((tm, d), row),
        compiler_params=_cparams(("parallel",)),
        name="mixer_merge",
    )(ya, yb, zc, zc, zc, zc, zc, zc, zc, zc, h, cw, wa, wb, wc, wo, lg, lb)


def _xattn_kernel(h_ref, kv_ref, wq_ref, wo_ref, lg_ref, lb_ref, o_ref, ob_ref, *, alpha):
    h = h_ref[...]
    q = _dot(h.astype(BF16), wq_ref[...]).astype(BF16)
    kv = kv_ref[...]
    nh = XATTN_HEADS
    hd = XATTN_HEAD_DIM
    outs = []
    for hh in range(nh):
        s = _dot_t(q[:, hh * hd:(hh + 1) * hd], kv[:, hh * hd:(hh + 1) * hd]) * (hd ** -0.5)
        s = s - jnp.max(s, axis=-1, keepdims=True)
        e = jnp.exp(s)
        p = e / jnp.sum(e, axis=-1, keepdims=True)
        outs.append(_dot(p.astype(BF16), kv[:, (nh + hh) * hd:(nh + hh + 1) * hd]))
    o = jnp.concatenate(outs, axis=-1).astype(BF16)
    y = _layer_norm(alpha * h + _dot(o, wo_ref[...]), lg_ref[...], lb_ref[...])
    o_ref[...] = y
    bits = lax.bitcast_convert_type(y.astype(BF16).astype(F32), I32)
    half = y.shape[1] // 2
    ob_ref[...] = (bits[:, half:] & -65536) | lax.shift_right_logical(bits[:, :half], 16)


def _unpack_pairs(words):
    lo = lax.bitcast_convert_type(lax.shift_left(words, 16), F32).astype(BF16)
    hi = lax.bitcast_convert_type(words & -65536, F32).astype(BF16)
    return lo, hi


def _xattn(h, kv, wq, wo, lg, lb, seq, mlen, alpha):
    tm = 512
    t = h.shape[0]
    per = seq // tm
    d = D_MODEL
    row = lambda i: (i, 0)
    const = lambda i: (0, 0)
    return pl.pallas_call(
        functools.partial(_xattn_kernel, alpha=alpha),
        out_shape=(jax.ShapeDtypeStruct((t, d), F32), jax.ShapeDtypeStruct((t, d // 2), I32)),
        grid=(t // tm,),
        in_specs=[pl.BlockSpec((tm, d), row), pl.BlockSpec((mlen, kv.shape[1]), lambda i: (i // per, 0)),
                  pl.BlockSpec(wq.shape, const), pl.BlockSpec(wo.shape, const),
                  pl.BlockSpec((1, d), const), pl.BlockSpec((1, d), const)],
        out_specs=(pl.BlockSpec((tm, d), row), pl.BlockSpec((tm, d // 2), row)),
        compiler_params=_cparams(("parallel",)),
        name="mem_xattn",
    )(h, kv, wq, wo, lg, lb)


def _expert_kernel(be_ref, nu_ref, x_ref, w13_ref, w2_ref, o_ref, w13b_scr, w2b_scr):
    i = pl.program_id(0)

    @pl.when((i == 0) | (be_ref[i] != be_ref[jnp.maximum(i - 1, 0)]))
    def _():
        w13b_scr[...] = w13_ref[0, 0].astype(BF16)
        w2b_scr[...] = w2_ref[0, 0].astype(BF16)

    @pl.when(i < nu_ref[0])
    def _():
        x_lo, x_hi = _unpack_pairs(x_ref[...])
        half = x_lo.shape[1]
        hmid = _dot(x_lo, w13b_scr[:half, :]) + _dot(x_hi, w13b_scr[half:, :])
        a = hmid[:, :EXPERT_DIM]
        g = hmid[:, EXPERT_DIM:]
        act = (a * jax.nn.sigmoid(a) * g).astype(BF16)
        o_ref[...] = _dot(act, w2b_scr[...]).astype(o_ref.dtype)


def _experts(xg, w13, w2, block_e, n_used, layer):
    n_pad = xg.shape[0]
    d = 2 * xg.shape[1]
    rows = EXP_ROWS
    nb = n_pad // rows
    blk = lambda i, be, nu: (jnp.minimum(i, nu[0] - 1), 0)
    grid_spec = pltpu.PrefetchScalarGridSpec(
        num_scalar_prefetch=2,
        grid=(nb,),
        in_specs=[pl.BlockSpec((rows, d // 2), blk),
                  pl.BlockSpec((1, 1, d, 2 * EXPERT_DIM), lambda i, be, nu: (layer, be[i], 0, 0)),
                  pl.BlockSpec((1, 1, EXPERT_DIM, d), lambda i, be, nu: (layer, be[i], 0, 0))],
        out_specs=pl.BlockSpec((rows, d), blk),
        scratch_shapes=[pltpu.VMEM((d, 2 * EXPERT_DIM), BF16), pltpu.VMEM((EXPERT_DIM, d), BF16)],
    )
    return pl.pallas_call(
        _expert_kernel,
        out_shape=jax.ShapeDtypeStruct((n_pad, d), BF16),
        grid_spec=grid_spec,
        compiler_params=_cparams(("arbitrary",)),
        name="moe_experts",
    )(block_e, n_used, xg, w13, w2)


def _moe_out_kernel(h_ref, y8_ref, wk_ref, w13_ref, w2_ref, lg_ref, lb_ref, o_ref, ob_ref, *, alpha):
    h = h_ref[...]
    hmid = _dot(h.astype(BF16), w13_ref[...])
    a = hmid[:, :EXPERT_DIM]
    g = hmid[:, EXPERT_DIM:]
    acc = alpha * h + _dot((a * jax.nn.sigmoid(a) * g).astype(BF16), w2_ref[...])
    wk = wk_ref[...]
    for k in range(TOP_K):
        acc = acc + wk[:, k:k + 1] * y8_ref[k].astype(F32)
    y = _layer_norm(acc, lg_ref[...], lb_ref[...])
    o_ref[...] = y
    ob_ref[...] = y.astype(BF16)


def _moe_out(h, y8, wk, w13, w2, lg, lb, alpha):
    tm = 256
    t, d = h.shape
    row = lambda i: (i, 0)
    const = lambda i: (0, 0)
    return pl.pallas_call(
        functools.partial(_moe_out_kernel, alpha=alpha),
        out_shape=(jax.ShapeDtypeStruct((t, d), F32), jax.ShapeDtypeStruct((t, d), BF16)),
        grid=(t // tm,),
        in_specs=[pl.BlockSpec((tm, d), row), pl.BlockSpec((TOP_K, tm, d), lambda i: (0, i, 0)),
                  pl.BlockSpec((tm, TOP_K), row),
                  pl.BlockSpec(w13.shape, const), pl.BlockSpec(w2.shape, const),
                  pl.BlockSpec((1, d), const), pl.BlockSpec((1, d), const)],
        out_specs=(pl.BlockSpec((tm, d), row), pl.BlockSpec((tm, d), row)),
        compiler_params=_cparams(("parallel",)),
        name="moe_shared_ln",
    )(h, y8, wk, w13, w2, lg, lb)


def _dest_kernel(idx_ref, rank_ref, ps_ref, o_ref):
    idx = idx_ref[...]
    ps = ps_ref[...]
    tm = idx.shape[1]
    eidx = lax.broadcasted_iota(I32, (N_EXPERTS, tm), 0)
    rows = [jnp.sum(jnp.where(eidx == idx[k:k + 1, :], ps, 0.0), axis=0, keepdims=True) for k in range(TOP_K)]
    o_ref[...] = jnp.concatenate(rows, axis=0).astype(I32) + rank_ref[...]


def _dest(idx, rank, pstarts):
    tm = 1024
    t = idx.shape[1]
    tm = min(tm, t)
    col = lambda i: (0, i)
    return pl.pallas_call(
        _dest_kernel,
        out_shape=jax.ShapeDtypeStruct((TOP_K, t), I32),
        grid=(t // tm,),
        in_specs=[pl.BlockSpec((TOP_K, tm), col), pl.BlockSpec((TOP_K, tm), col),
                  pl.BlockSpec((N_EXPERTS, 1), lambda i: (0, 0))],
        out_specs=pl.BlockSpec((TOP_K, tm), col),
        compiler_params=_cparams(("parallel",)),
        name="moe_dest",
    )(idx, rank, pstarts)


def _router_kernel(h_ref, rwt_ref, rb_ref, tri_ref, idx_ref, w_ref, rank_ref, cnt_ref, carry_scr, *, tm):
    i = pl.program_id(0)

    @pl.when(i == 0)
    def _():
        carry_scr[...] = jnp.zeros(carry_scr.shape, F32)

    ne = N_EXPERTS
    gsz = ne // N_GROUPS
    s = jax.nn.sigmoid(_dot_t(rwt_ref[...], h_ref[...].astype(BF16)))
    sb = s + rb_ref[...]
    sb3 = sb.reshape(N_GROUPS, gsz, tm)
    li = lax.broadcasted_iota(I32, (N_GROUPS, gsz, tm), 1)
    m1 = jnp.max(sb3, axis=1, keepdims=True)
    first = jnp.min(jnp.where(sb3 == m1, li, gsz), axis=1, keepdims=True)
    m2 = jnp.max(jnp.where(li == first, -jnp.inf, sb3), axis=1, keepdims=True)
    gs = (m1 + m2).reshape(N_GROUPS, tm)
    gi = lax.broadcasted_iota(I32, (N_GROUPS, tm), 0)
    grank = jnp.zeros((N_GROUPS, tm), I32)
    for gp in range(N_GROUPS):
        row = gs[gp:gp + 1, :]
        grank = grank + ((row > gs) | ((row == gs) & (gp < gi))).astype(I32)
    gkeep = (grank < TOPK_GROUPS).astype(F32)
    ekeep = jnp.broadcast_to(gkeep[:, None, :], (N_GROUPS, gsz, tm)).reshape(ne, tm)
    cand = jnp.where(ekeep > 0.0, sb, NEG)
    eidx = lax.broadcasted_iota(I32, (ne, tm), 0)
    sel = jnp.zeros((ne, tm), F32)
    idxs, wts = [], []
    for _ in range(TOP_K):
        mx = jnp.max(cand, axis=0, keepdims=True)
        ik = jnp.min(jnp.where(cand == mx, eidx, ne), axis=0, keepdims=True)
        hit = eidx == ik
        wts.append(jnp.sum(jnp.where(hit, s, 0.0), axis=0, keepdims=True))
        idxs.append(ik)
        sel = jnp.where(hit, 1.0, sel)
        cand = jnp.where(hit, -jnp.inf, cand)
    before = _dot(sel.astype(BF16), tri_ref[...]) + carry_scr[...]
    ranks = [jnp.sum(jnp.where(eidx == ik, before, 0.0), axis=0, keepdims=True) for ik in idxs]
    carry_scr[...] = carry_scr[...] + jnp.sum(sel, axis=1, keepdims=True)
    w = jnp.concatenate(wts, axis=0)
    idx_ref[...] = jnp.concatenate(idxs, axis=0)
    w_ref[...] = w / jnp.sum(w, axis=0, keepdims=True) * ROUTE_SCALE
    rank_ref[...] = jnp.concatenate(ranks, axis=0).astype(I32)
    cnt_ref[...] = jnp.broadcast_to(carry_scr[...], cnt_ref.shape)


def _router(h, rwt, rb):
    tm = 256
    t, d = h.shape
    tri = jnp.asarray(np.triu(np.ones((tm, tm), np.float32), 1)).astype(BF16)
    const = lambda i: (0, 0)
    col = lambda i: (0, i)
    return pl.pallas_call(
        functools.partial(_router_kernel, tm=tm),
        out_shape=(jax.ShapeDtypeStruct((TOP_K, t), I32), jax.ShapeDtypeStruct((TOP_K, t), F32),
                   jax.ShapeDtypeStruct((TOP_K, t), I32), jax.ShapeDtypeStruct((N_EXPERTS, SLOT), F32)),
        grid=(t // tm,),
        in_specs=[pl.BlockSpec((tm, d), lambda i: (i, 0)), pl.BlockSpec((N_EXPERTS, d), const),
                  pl.BlockSpec((N_EXPERTS, 1), const), pl.BlockSpec((tm, tm), const)],
        out_specs=(pl.BlockSpec((TOP_K, tm), col), pl.BlockSpec((TOP_K, tm), col),
                   pl.BlockSpec((TOP_K, tm), col), pl.BlockSpec((N_EXPERTS, SLOT), const)),
        scratch_shapes=[pltpu.VMEM((N_EXPERTS, 1), F32)],
        compiler_params=_cparams(("arbitrary",)),
        name="moe_router",
    )(h, rwt, rb, tri)


def _dispatch(hp, dest_flat, n_pad):
    t, w = hp.shape
    n_a = dest_flat.shape[0]
    per_w = n_a // (SC_CORES * SC_SUBCORES)
    assert per_w * SC_CORES * SC_SUBCORES == n_a and per_w % SC_CHUNK == 0 and t % per_w == 0
    mesh = plsc.VectorSubcoreMesh(core_axis_name="c", subcore_axis_name="s")

    @functools.partial(
        pl.kernel, out_type=jax.ShapeDtypeStruct((n_pad, w), I32), mesh=mesh,
        scratch_types=[pltpu.VMEM((SC_CHUNK,), I32), pltpu.VMEM((SC_CHUNK, w), I32), pltpu.SemaphoreType.DMA],
        name="moe_dispatch")
    def scatter_rows(hp_hbm, dest_hbm, xg_hbm, idx_v, rows_v, sem):
        wid = lax.axis_index("s") * SC_CORES + lax.axis_index("c")
        base = wid * per_w

        @pl.loop(0, per_w // SC_CHUNK)
        def _(j):
            a0 = pl.multiple_of(base + j * SC_CHUNK, SC_CHUNK)
            t0 = pl.multiple_of(lax.rem(a0, t), SC_CHUNK)
            pltpu.sync_copy(dest_hbm.at[pl.ds(a0, SC_CHUNK)], idx_v)
            pltpu.sync_copy(hp_hbm.at[pl.ds(t0, SC_CHUNK)], rows_v)
            pltpu.async_copy(rows_v, xg_hbm.at[idx_v], sem).wait()

    return scatter_rows(hp, dest_flat)


def _moe(h, hp, router_w, router_b, exp_w13, exp_w2, layer, shared_w13, shared_w2, lg, lb, alpha):
    t, d = h.shape
    idx, wts, rank, cnt = _router(h, router_w.T, router_b.astype(F32).reshape(N_EXPERTS, 1))
    rows = EXP_ROWS
    n_a = t * TOP_K
    counts = cnt[:, 0].astype(I32)
    pcounts = (counts + rows - 1) // rows * rows
    pends = jnp.cumsum(pcounts)
    pstarts = pends - pcounts
    dest = _dest(idx, rank, pstarts.astype(F32).reshape(N_EXPERTS, 1))
    n_blocks = -(-n_a // rows) + N_EXPERTS
    n_pad = n_blocks * rows
    n_used = (pends[-1:] // rows).astype(I32)
    first_row = jnp.minimum(jnp.arange(n_blocks, dtype=I32), n_used - 1) * rows
    block_e = jnp.sum((pends[None, :] <= first_row[:, None]).astype(I32), axis=1)
    xg = _dispatch(hp, dest.reshape(-1), n_pad)
    out = _experts(xg, exp_w13, exp_w2, block_e, n_used, layer)
    y8 = jnp.take(out, dest, axis=0, mode="clip")
    return _moe_out(h, y8, wts.T, shared_w13, shared_w2, lg, lb, alpha)


def _pad_slots(w, scale=1.0):
    dm = w.shape[0]
    nh = w.shape[1] // HEAD_DIM
    w = (w * scale).reshape(dm, nh, HEAD_DIM)
    return jnp.concatenate([w, jnp.zeros_like(w)], axis=-1).reshape(dm, nh * SLOT)


def _pad_rows(w):
    n = w.shape[1]
    nh = w.shape[0] // HEAD_DIM
    w = w.reshape(nh, HEAD_DIM, n)
    return jnp.concatenate([w, jnp.zeros_like(w)], axis=1).reshape(nh * SLOT, n)


def _alibi(n):
    return np.exp2(-8.0 * np.arange(1, n + 1, dtype=np.float64) / n).astype(np.float32)


def _key_features(seq):
    p = np.arange(seq)
    slc = np.zeros((seq, SLOT), np.float32)
    slc[:, HEAD_DIM] = p % SLC_BLOCK
    blk = p // SLC_BLOCK
    nz = blk > 0
    slc[p[nz], HEAD_DIM + blk[nz]] = 1.0
    win = np.zeros((seq, SLOT), np.float32)
    win[:, HEAD_DIM] = p // 64
    win[:, HEAD_DIM + 1] = p % 64
    mob = np.zeros((seq, SLOT), np.float32)
    mob[p, HEAD_DIM + p // MOBA_BLOCK] = 1.0
    mob[:, HEAD_DIM + 16] = p % MOBA_BLOCK
    return np.concatenate([slc] * 2 + [win] * 2 + [mob] * MOBA_HEADS, axis=1)


def _query_features():
    sl = _alibi(NSA_HEADS)
    nsa = np.zeros((NSA_KV_HEADS, SLOT, 8), np.float32)
    for g in range(NSA_KV_HEADS):
        for r in range(NSA_GROUP):
            s = sl[g * NSA_GROUP + r]
            nsa[g, HEAD_DIM, r] = s
            nsa[g, HEAD_DIM + 1:, r] = s * SLC_BLOCK * np.arange(1, 64)
            nsa[g, HEAD_DIM, NSA_GROUP + r] = s * 64
            nsa[g, HEAD_DIM + 1, NSA_GROUP + r] = s
    sm = _alibi(MOBA_HEADS)
    mob = np.zeros((MOBA_HEADS, SLOT, 8), np.float32)
    for h in range(MOBA_HEADS):
        mob[h, HEAD_DIM:HEAD_DIM + 16, 0] = sm[h] * MOBA_BLOCK * np.arange(16)
        mob[h, HEAD_DIM + 16, 0] = sm[h]
    return nsa, mob


def _overlap_t(seq):
    nc = seq // CMP_STRIDE
    c_start = np.arange(nc) * CMP_STRIDE
    b_start = np.arange(64) * SLC_BLOCK
    ov = ((c_start[None, :] < (b_start + SLC_BLOCK)[:, None])
          & ((c_start + CMP_BLOCK)[None, :] > b_start[:, None])
          & (np.arange(nc) < nc - 1)[None, :] & (b_start < seq)[:, None])
    return ov.astype(np.float32)


def _compress_weights(pe, w1, w2):
    ty = np.array([0, 0, 1, 1])
    eye = jnp.eye(4, dtype=F32)
    w1r = w1.reshape(2, CMP_BLOCK, HEAD_DIM, CMP_HIDDEN)[ty]
    top = jnp.einsum('spdj,sS->psdSj', w1r[:, :CMP_STRIDE], eye).reshape(CMP_STRIDE * 256, 4 * CMP_HIDDEN)
    bot = jnp.einsum('spdj,sS->psdSj', w1r[:, CMP_STRIDE:], eye).reshape(CMP_STRIDE * 256, 4 * CMP_HIDDEN)
    per = pe[ty]
    pet = jnp.transpose(per[:, :CMP_STRIDE], (1, 0, 2)).reshape(1, CMP_STRIDE * 256)
    peb = jnp.transpose(per[:, CMP_STRIDE:], (1, 0, 2)).reshape(1, CMP_STRIDE * 256)
    w2p = jnp.concatenate([w2[ty], jnp.zeros((4, CMP_HIDDEN, SLOT - HEAD_DIM), F32)], axis=-1)
    w2b = jnp.einsum('sjd,sS->sjSd', w2p, eye).reshape(4 * CMP_HIDDEN, 4 * SLOT)
    w2k = w2b[:, :2 * SLOT].astype(BF16)
    w2vt = w2b[:, 2 * SLOT:].T.astype(BF16)
    return pet, peb, top.astype(BF16), bot.astype(BF16), w2k, w2vt


def kernel(x, mem, w_in, cmp_pe, cmp_w1, cmp_w2, conv_w, w_branch, w_out, ln1_g, ln1_b,
           xattn_wq, xattn_wkv, xattn_wo, ln2_g, ln2_b, router_w, router_b, exp_w13, exp_w2,
           shared_w13, shared_w2, ln3_g, ln3_b):
    bsz, seq, d = x.shape
    mlen = mem.shape[1]
    depth = w_in.shape[0]
    alpha = (2.0 * depth) ** 0.25
    t = bsz * seq
    scale = HEAD_DIM ** -0.5

    kfeat = jnp.asarray(_key_features(seq))
    fq_nsa, fq_moba = (jnp.asarray(a) for a in _query_features())
    ovt = jnp.asarray(_overlap_t(seq)).astype(BF16)
    memf = mem.reshape(bsz * mlen, d)

    h = x.reshape(t, d)
    hb = h.astype(BF16)
    for l in range(depth):
        ht = hb.T
        wi = w_in[l]
        kv6 = wi[:, _OFF_NSA_KV:_OFF_NSA_G].reshape(d, 6, NSA_KV_HEADS * HEAD_DIM)
        mq, mk, mv = (wi[:, _OFF_MOBA + j * 512:_OFF_MOBA + (j + 1) * 512] for j in range(3))
        w_at = jnp.concatenate([wi[:, :512] * scale, kv6[:, 3], kv6[:, 5], mq * scale, mv],
                               axis=1).T.astype(BF16)
        w_b = jnp.concatenate([_pad_slots(kv6[:, 2]), _pad_slots(kv6[:, 4]), _pad_slots(mk)], axis=1).astype(BF16)
        wg = wi[:, _OFF_NSA_G:_OFF_MOBA].reshape(d, NSA_KV_HEADS, 12)
        wgt = jnp.concatenate([wg, jnp.zeros((d, NSA_KV_HEADS, SLOT - 12), F32)], axis=-1)
        wgt = wgt.reshape(d, NSA_KV_HEADS * SLOT).T.astype(BF16)
        w_c = jnp.concatenate([wi[:, _OFF_NSA_KV:_OFF_NSA_KV + 256], jnp.zeros((d, 256), F32),
                               wi[:, _OFF_CONV:]], axis=1).astype(BF16)

        zat = _proj_t(w_at, ht)
        gt = _mm(wgt, ht, F32, tm=256, tn=1024, name="proj_gate")
        zb = _mm(hb, w_b, BF16, feats=kfeat, name="proj_b")
        zc = _mm(hb, w_c, BF16, name="proj_c")

        pet, peb, wt, wb, w2k, w2vt = _compress_weights(cmp_pe[l], cmp_w1[l], cmp_w2[l])
        sub = zc[:, :256].reshape(bsz, seq // CMP_STRIDE, CMP_STRIDE * 256)
        kc, vct = _compress(sub, pet, peb, wt, wb, w2k, w2vt)

        ya = _nsa_attention(zat, zb, gt, kc, vct, ovt, fq_nsa, bsz, seq)
        yb = _moba_attention(zat, zb, fq_moba, bsz, seq)

        cw = jnp.concatenate([conv_w[l], jnp.zeros((5, CONV_CH), F32)], axis=0)
        h = _merge(ya, yb, zc, h, cw, _pad_rows(w_branch[l, 0]).astype(BF16),
                   _pad_rows(w_branch[l, 1]).astype(BF16), w_branch[l, 2].astype(BF16),
                   w_out[l].astype(BF16), ln1_g[l][None], ln1_b[l][None], seq, alpha)

        kv = _mm(memf, xattn_wkv[l].astype(BF16), BF16, tm=512, name="xattn_kv")
        h, hp = _xattn(h, kv, xattn_wq[l].astype(BF16), xattn_wo[l].astype(BF16),
                       ln2_g[l][None], ln2_b[l][None], seq, mlen, alpha)

        h, hb = _moe(h, hp, router_w[l].astype(BF16), router_b[l], exp_w13, exp_w2, l,
                     shared_w13[l].astype(BF16), shared_w2[l].astype(BF16), ln3_g[l][None], ln3_b[l][None], alpha)
    return h.reshape(bsz, seq, d)
```

```python
import functools

import jax
import jax.numpy as jnp
import numpy as np
from jax import lax
from jax.experimental import pallas as pl
from jax.experimental.pallas import tpu as pltpu
from jax.experimental.pallas import tpu_sc as plsc

F32 = jnp.float32
BF16 = jnp.bfloat16
I32 = jnp.int32

D_MODEL = 1024
HEAD_DIM = 64
SLOT = 128
NEG = -1e30
LN_EPS = 1e-5

NSA_HEADS = 8
NSA_KV_HEADS = 2
NSA_GROUP = 4
CMP_BLOCK = 32
CMP_STRIDE = 16
CMP_HIDDEN = 256
SLC_BLOCK = 64
SLC_TOPK = 16
WINDOW = 512
MOBA_HEADS = 8
MOBA_BLOCK = 256
MOBA_TOPK = 3
CONV_CH = 512
XATTN_HEADS = 4
XATTN_HEAD_DIM = 128
N_EXPERTS = 256
TOP_K = 8
N_GROUPS = 8
TOPK_GROUPS = 4
EXPERT_DIM = 256
ROUTE_SCALE = 2.5

TQ = 256
MOBA_HPS = 4
EXP_ROWS = 512
VMEM_LIMIT = 48 * 1024 * 1024
SC_CORES = 2
SC_SUBCORES = 16
SC_CHUNK = 64

_OFF_NSA_Q = 0
_OFF_NSA_KV = 512
_OFF_NSA_G = 1280
_OFF_MOBA = 1304
_OFF_CONV = 2840
_OFF_MERGE = 4376

_SLOT_NSA_Q = 0
_SLOT_SLC_V = 8
_SLOT_WIN_V = 10
_SLOT_MOBA_Q = 12
_SLOT_MOBA_V = 20
_N_SLOTS_T = 28


def _cparams(sem):
    return pltpu.CompilerParams(dimension_semantics=sem, vmem_limit_bytes=VMEM_LIMIT)


def _dot(a, b):
    return jnp.dot(a, b, preferred_element_type=F32)


def _dot_t(a, b):
    return lax.dot_general(a, b, (((1,), (1,)), ((), ())), preferred_element_type=F32)


def _layer_norm(x, g, b):
    mu = jnp.mean(x, axis=-1, keepdims=True)
    xc = x - mu
    var = jnp.mean(xc * xc, axis=-1, keepdims=True)
    return xc * lax.rsqrt(var + LN_EPS) * g + b


def _mm_kernel(x_ref, w_ref, o_ref):
    o_ref[...] = _dot(x_ref[...].astype(BF16), w_ref[...]).astype(o_ref.dtype)


def _mm_feat_kernel(x_ref, w_ref, f_ref, o_ref):
    y = _dot(x_ref[...].astype(BF16), w_ref[...]) + f_ref[...]
    o_ref[...] = y.astype(o_ref.dtype)


def _mm(x, w, out_dtype, *, tm=1024, tn=512, feats=None, name):
    m, k = x.shape
    n = w.shape[1]
    tm = min(tm, m)
    tn = min(tn, n)
    assert m % tm == 0 and n % tn == 0, (m, n, tm, tn)
    in_specs = [pl.BlockSpec((tm, k), lambda i, j: (i, 0)),
                pl.BlockSpec((k, tn), lambda i, j: (0, j))]
    args = [x, w]
    if feats is None:
        body = _mm_kernel
    else:
        per = feats.shape[0] // tm
        assert feats.shape[0] % tm == 0
        in_specs.append(pl.BlockSpec((tm, tn), lambda i, j: (i % per, j)))
        args.append(feats)
        body = _mm_feat_kernel
    return pl.pallas_call(
        body,
        out_shape=jax.ShapeDtypeStruct((m, n), out_dtype),
        grid=(m // tm, n // tn),
        in_specs=in_specs,
        out_specs=pl.BlockSpec((tm, tn), lambda i, j: (i, j)),
        compiler_params=_cparams(("parallel", "parallel")),
        name=name,
    )(*args)


def _proj_t_kernel(w_ref, ht_ref, o_ref, *, value_slots):
    acc = _dot(w_ref[...], ht_ref[...])
    tn = acc.shape[1]
    zero = jnp.zeros((SLOT - HEAD_DIM, tn), o_ref.dtype)
    ones_row = (lax.broadcasted_iota(I32, (SLOT - HEAD_DIM, tn), 0) == 0).astype(o_ref.dtype)
    for s in range(w_ref.shape[0] // HEAD_DIM):
        o_ref[s * SLOT:s * SLOT + HEAD_DIM, :] = acc[s * HEAD_DIM:(s + 1) * HEAD_DIM, :].astype(o_ref.dtype)
        o_ref[s * SLOT + HEAD_DIM:(s + 1) * SLOT, :] = ones_row if s in value_slots else zero


def _proj_t(w, ht, value_slots):
    m, k = w.shape
    t = ht.shape[1]
    tn = min(512, t)
    n_out = m // HEAD_DIM * SLOT
    return pl.pallas_call(
        functools.partial(_proj_t_kernel, value_slots=value_slots),
        out_shape=jax.ShapeDtypeStruct((n_out, t), BF16),
        grid=(t // tn,),
        in_specs=[pl.BlockSpec((m, k), lambda j: (0, 0)), pl.BlockSpec((k, tn), lambda j: (0, j))],
        out_specs=pl.BlockSpec((n_out, tn), lambda j: (0, j)),
        compiler_params=_cparams(("parallel",)),
        name="proj_at",
    )(w, ht)


def _cmp_kernel(sub_ref, pet_ref, peb_ref, wt_ref, wb_ref, w2k_ref, w2vt_ref, kc_ref, vct_ref):
    sub = sub_ref[0].astype(F32)
    nc = sub.shape[0]
    a = _dot((sub + pet_ref[...]).astype(BF16), wt_ref[...])
    b = _dot((sub + peb_ref[...]).astype(BF16), wb_ref[...])
    hid = jax.nn.gelu(a + pltpu.roll(b, nc - 1, 0)).astype(BF16)
    kc_ref[0] = _dot(hid, w2k_ref[...]).astype(kc_ref.dtype)
    vct_ref[0] = _dot_t(w2vt_ref[...], hid).astype(vct_ref.dtype)


def _compress(sub, pet, peb, wt, wb, w2k, w2vt):
    bsz, nc, kk = sub.shape
    n_h = wt.shape[1]
    n_o = w2k.shape[1]
    const = lambda b: (0, 0)
    return pl.pallas_call(
        _cmp_kernel,
        out_shape=(jax.ShapeDtypeStruct((bsz, nc, n_o), BF16), jax.ShapeDtypeStruct((bsz, n_o, nc), BF16)),
        grid=(bsz,),
        in_specs=[pl.BlockSpec((1, nc, kk), lambda b: (b, 0, 0)),
                  pl.BlockSpec((1, kk), const), pl.BlockSpec((1, kk), const),
                  pl.BlockSpec((kk, n_h), const), pl.BlockSpec((kk, n_h), const),
                  pl.BlockSpec((n_h, n_o), const), pl.BlockSpec((n_o, n_h), const)],
        out_specs=(pl.BlockSpec((1, nc, n_o), lambda b: (b, 0, 0)), pl.BlockSpec((1, n_o, nc), lambda b: (b, 0, 0))),
        compiler_params=_cparams(("parallel",)),
        name="nsa_compress",
    )(sub, pet, peb, wt, wb, w2k, w2vt)


def _flash_reset(m_scr, acc_scr):
    m_scr[...] = jnp.full(m_scr.shape, NEG, F32)
    acc_scr[...] = jnp.zeros(acc_scr.shape, F32)


def _qk(kget, qxs, start, size):
    k0 = kget(0, start, size)
    if kget(1, start, size) is None:
        return _dot(k0, jnp.concatenate(qxs, axis=1))
    return jnp.concatenate([_dot(k0, qxs[0])] + [_dot(kget(c, start, size), qxs[c])
                                                 for c in range(1, len(qxs))], axis=1)


def _softmax_pv(s, vget, nq, start, size, m_scr, acc_scr, mask=None):
    if mask is not None:
        s = jnp.where(mask, s, NEG)
    m_prev = m_scr[...]
    m_new = jnp.maximum(m_prev, jnp.max(s, axis=0, keepdims=True))
    alpha = jnp.exp(m_prev - m_new)
    p = jnp.exp(s - m_new).astype(BF16)
    v0 = vget(0, start, size)
    if vget(1, start, size) is None:
        pv = _dot(v0, p)
    else:
        pv = jnp.concatenate([_dot(v0, p[:, :TQ])] + [_dot(vget(c, start, size), p[:, c * TQ:(c + 1) * TQ])
                                                      for c in range(1, nq)], axis=1)
    acc_scr[...] = alpha * acc_scr[...] + pv
    m_scr[...] = m_new


def _rows(ref, start, size, lane0=0):
    return ref[pl.ds(pl.multiple_of(start, TQ), size), lane0:lane0 + SLOT]


def _cols(ref, start, size, row0=0):
    return ref[row0:row0 + SLOT, pl.ds(pl.multiple_of(start, TQ), size)]


def _flash_causal(kget, vget, qxs, i, sa_scr, sb_scr, m_scr, acc_scr):
    tq = TQ
    tk = 2 * tq
    nq = len(qxs)
    rr = nq * tq
    n_steps = i // 2 + 1
    sa_scr[...] = _qk(kget, qxs, 0, tk)

    def body(jj, carry):
        j = 2 * jj
        s = sa_scr[...]
        sb_scr[...] = _qk(kget, qxs, (j + 1) * tk, tk)
        _softmax_pv(s, vget, nq, j * tk, tk, m_scr, acc_scr)
        s = sb_scr[...]
        sa_scr[...] = _qk(kget, qxs, (j + 2) * tk, tk)
        _softmax_pv(s, vget, nq, (j + 1) * tk, tk, m_scr, acc_scr)
        return carry

    lax.fori_loop(0, (n_steps - 1) // 2, body, 0)
    last = n_steps - 1
    valid = (last * tk + lax.broadcasted_iota(I32, (tk, rr), 0)
             <= i * tq + (lax.broadcasted_iota(I32, (tk, rr), 1) & (tq - 1)))

    @pl.when(last % 2 == 0)
    def _():
        _softmax_pv(sa_scr[...], vget, nq, last * tk, tk, m_scr, acc_scr, mask=valid)

    @pl.when(last % 2 == 1)
    def _():
        s = sa_scr[...]
        sb_scr[...] = _qk(kget, qxs, last * tk, tk)
        _softmax_pv(s, vget, nq, (last - 1) * tk, tk, m_scr, acc_scr)
        _softmax_pv(sb_scr[...], vget, nq, last * tk, tk, m_scr, acc_scr, mask=valid)


def _nsa_kernel(q_ref, kc_ref, vct_ref, ks_ref, vst_ref, kw_ref, vwt_ref, g_ref, ovt_ref, fq_ref,
                o_ref, m_scr, acc_scr, imp_scr, sa_scr, sb_scr, *, tq, nsel):
    i = pl.program_id(2)
    r4 = NSA_GROUP
    rr = r4 * tq
    q4 = [q_ref[r * SLOT:(r + 1) * SLOT, :] for r in range(r4)]
    fq = fq_ref[0]

    kc = kc_ref[0]
    nc = kc.shape[0]
    qs = jnp.concatenate(q4, axis=1)
    s = _dot(kc, qs)
    t_glob = i * tq + (lax.broadcasted_iota(I32, (nc, rr), 1) & (tq - 1))
    c_idx = lax.broadcasted_iota(I32, (nc, rr), 0)
    cmask = (c_idx * CMP_STRIDE + (CMP_BLOCK - 1)) <= t_glob
    s = jnp.where(cmask, s, NEG)
    mx = jnp.max(s, axis=0, keepdims=True)
    e = jnp.where(cmask, jnp.exp(s - mx), 0.0)
    lsum = jnp.sum(e, axis=0, keepdims=True)
    p_cmp = (e / jnp.where(lsum > 0.0, lsum, 1.0)).astype(BF16)
    o_cmp = _dot(vct_ref[0], p_cmp)

    ovt = ovt_ref[...]
    imp = _dot(ovt, p_cmp[:, 0:tq])
    for r in range(1, r4):
        imp = imp + _dot(ovt, p_cmp[:, r * tq:(r + 1) * tq])
    nbp = imp.shape[0]
    jidx = lax.broadcasted_iota(I32, (nbp, tq), 0)
    cur = (i * tq + lax.broadcasted_iota(I32, (nbp, tq), 1)) // SLC_BLOCK
    forced = (jidx == 0) | (jidx == cur) | (jidx == cur - 1)
    imp = jnp.where(forced, 1e6, jnp.where(jidx > cur, -1e6, imp))
    imp_scr[...] = imp

    def rank_body(jp, rank):
        row = imp_scr[pl.ds(jp, 1), :]
        better = (row > imp) | ((row == imp) & (jp < jidx))
        return rank + better.astype(I32)

    n_live = (i + 1) * (tq // SLC_BLOCK)
    rank = lax.fori_loop(0, n_live, rank_body, jnp.zeros((nbp, tq), I32))
    sel = (rank < nsel) & (jidx <= cur)
    mask_t = jnp.where(sel, 0.0, NEG)
    feat = jnp.concatenate([jnp.zeros((SLOT - nbp, tq), F32), mask_t], axis=0)

    qxs = [q4[r] + (feat + fq[:, r:r + 1]).astype(BF16) for r in range(r4)]
    kget = lambda c, start, size: _rows(ks_ref, start, size) if c == 0 else None
    vget = lambda c, start, size: _cols(vst_ref, start, size) if c == 0 else None
    _flash_reset(m_scr, acc_scr)
    _flash_causal(kget, vget, qxs, i, sa_scr, sb_scr, m_scr, acc_scr)
    o_slc = acc_scr[...] / acc_scr[HEAD_DIM:HEAD_DIM + 1, :]

    qx = jnp.concatenate([q4[r] + fq[:, r4 + r:r4 + r + 1].astype(BF16) for r in range(r4)], axis=1)
    w0 = jnp.maximum(i - 2, 0) * tq
    s = _dot(_rows(kw_ref, w0, 3 * tq), qx)
    dist = (i * tq + (lax.broadcasted_iota(I32, (3 * tq, rr), 1) & (tq - 1))
            - (w0 + lax.broadcasted_iota(I32, (3 * tq, rr), 0)))
    s = jnp.where(dist >= 0, jnp.where(dist < WINDOW, s, NEG), NEG)
    e = jnp.exp(s - jnp.max(s, axis=0, keepdims=True)).astype(BF16)
    o_win = _dot(_cols(vwt_ref, w0, 3 * tq), e)
    o_win = o_win / o_win[HEAD_DIM:HEAD_DIM + 1, :]

    gate = jax.nn.sigmoid(g_ref[...])
    for r in range(r4):
        sl = slice(r * tq, (r + 1) * tq)
        o = (gate[3 * r:3 * r + 1, :] * o_cmp[:, sl] + gate[3 * r + 1:3 * r + 2, :] * o_slc[:, sl]
             + gate[3 * r + 2:3 * r + 3, :] * o_win[:, sl])
        o_ref[:, r * SLOT:(r + 1) * SLOT] = o.T.astype(o_ref.dtype)


def _nsa_attention(zat, zb, gt, kc, vct, ovt, fq, bsz, seq):
    tq = TQ
    ni = seq // tq
    nc = kc.shape[1]
    nsel = min(SLC_TOPK, seq // SLC_BLOCK)
    assert seq // SLC_BLOCK <= 64 and WINDOW == 2 * tq and ni % 2 == 0 and ni >= 3
    rr = NSA_GROUP * tq
    gw = NSA_GROUP * SLOT
    in_specs = [
        pl.BlockSpec((gw, tq), lambda b, g, i: (g, b * ni + i)),
        pl.BlockSpec((1, nc, SLOT), lambda b, g, i: (b, 0, g)),
        pl.BlockSpec((1, SLOT, nc), lambda b, g, i: (b, g, 0)),
        pl.BlockSpec((seq, SLOT), lambda b, g, i: (b, g)),
        pl.BlockSpec((SLOT, seq), lambda b, g, i: (_SLOT_SLC_V + g, b)),
        pl.BlockSpec((seq, SLOT), lambda b, g, i: (b, 2 + g)),
        pl.BlockSpec((SLOT, seq), lambda b, g, i: (_SLOT_WIN_V + g, b)),
        pl.BlockSpec((SLOT, tq), lambda b, g, i: (g, b * ni + i)),
        pl.BlockSpec(ovt.shape, lambda b, g, i: (0, 0)),
        pl.BlockSpec((1, SLOT, 8), lambda b, g, i: (g, 0, 0)),
    ]
    return pl.pallas_call(
        functools.partial(_nsa_kernel, tq=tq, nsel=nsel),
        out_shape=jax.ShapeDtypeStruct((bsz * seq, NSA_HEADS * SLOT), BF16),
        grid=(bsz, NSA_KV_HEADS, ni),
        in_specs=in_specs,
        out_specs=pl.BlockSpec((tq, gw), lambda b, g, i: (b * ni + i, g)),
        scratch_shapes=[pltpu.VMEM((1, rr), F32),
                        pltpu.VMEM((SLOT, rr), F32), pltpu.VMEM((64, tq), F32),
                        pltpu.VMEM((2 * tq, rr), F32), pltpu.VMEM((2 * tq, rr), F32)],
        compiler_params=_cparams(("parallel", "parallel", "arbitrary")),
        name="nsa_attention",
    )(zat, kc, vct, zb, zat, zb, zat, gt, ovt, fq)


def _moba_kernel(q_ref, k_ref, vt_ref, fq_ref, o_ref, m_scr, acc_scr, km_scr, gate_scr,
                 sa_scr, sb_scr, *, tq, nblk, ntop):
    i = pl.program_id(2)
    nh = MOBA_HPS

    @pl.when(i == 0)
    def _():
        for hh in range(nh):
            kh = k_ref[:, hh * SLOT:(hh + 1) * SLOT].astype(F32)
            km = jnp.mean(kh.reshape(nblk, tq, SLOT), axis=1)
            if nblk < 16:
                km = jnp.concatenate([km, jnp.zeros((16 - nblk, SLOT), F32)], axis=0)
            km_scr[hh] = km

    qs = [q_ref[hh * SLOT:(hh + 1) * SLOT, :] for hh in range(nh)]
    gates = [_dot(km_scr[hh].astype(BF16), qs[hh]) for hh in range(nh)]
    for hh in range(nh):
        gate_scr[hh] = gates[hh]
    jidx = lax.broadcasted_iota(I32, (16, tq), 0)

    def rank_body(jp, ranks):
        out = []
        for hh in range(nh):
            row = gate_scr[hh, pl.ds(jp, 1), :]
            better = (row > gates[hh]) | ((row == gates[hh]) & (jp < jidx))
            out.append(ranks[hh] + better.astype(I32))
        return tuple(out)

    ranks = lax.fori_loop(0, i, rank_body, tuple(jnp.zeros((16, tq), I32) for _ in range(nh)))
    qxs = []
    for hh in range(nh):
        keep = ((jidx < i) & (ranks[hh] < ntop)) | (jidx == i)
        mask_t = jnp.where(keep, 0.0, NEG)
        feat = jnp.concatenate([jnp.zeros((HEAD_DIM, tq), F32), mask_t,
                                jnp.zeros((SLOT - HEAD_DIM - 16, tq), F32)], axis=0)
        qxs.append(qs[hh] + (feat + fq_ref[hh][:, 0:1]).astype(BF16))
    kget = lambda c, start, size: _rows(k_ref, start, size, c * SLOT)
    vget = lambda c, start, size: _cols(vt_ref, start, size, c * SLOT)
    _flash_reset(m_scr, acc_scr)
    _flash_causal(kget, vget, qxs, i, sa_scr, sb_scr, m_scr, acc_scr)
    o = acc_scr[...] / acc_scr[HEAD_DIM:HEAD_DIM + 1, :]
    for hh in range(nh):
        o_ref[:, hh * SLOT:(hh + 1) * SLOT] = o[:, hh * tq:(hh + 1) * tq].T.astype(o_ref.dtype)


def _moba_attention(zat, zb, fq, bsz, seq):
    tq = TQ
    assert tq == MOBA_BLOCK and seq % tq == 0
    ni = seq // tq
    assert ni <= 16 and ni % 2 == 0
    ntop = min(MOBA_TOPK, ni)
    nh = MOBA_HPS
    hw = nh * SLOT
    q0 = _SLOT_MOBA_Q // nh
    v0 = _SLOT_MOBA_V // nh
    return pl.pallas_call(
        functools.partial(_moba_kernel, tq=tq, nblk=ni, ntop=ntop),
        out_shape=jax.ShapeDtypeStruct((bsz * seq, MOBA_HEADS * SLOT), BF16),
        grid=(bsz, MOBA_HEADS // nh, ni),
        in_specs=[pl.BlockSpec((hw, tq), lambda b, h, i: (q0 + h, b * ni + i)),
                  pl.BlockSpec((seq, hw), lambda b, h, i: (b, 1 + h)),
                  pl.BlockSpec((hw, seq), lambda b, h, i: (v0 + h, b)),
                  pl.BlockSpec((nh, SLOT, 8), lambda b, h, i: (h, 0, 0))],
        out_specs=pl.BlockSpec((tq, hw), lambda b, h, i: (b * ni + i, h)),
        scratch_shapes=[pltpu.VMEM((1, nh * tq), F32),
                        pltpu.VMEM((SLOT, nh * tq), F32),
                        pltpu.VMEM((nh, 16, SLOT), F32), pltpu.VMEM((nh, 16, tq), F32),
                        pltpu.VMEM((2 * tq, nh * tq), F32), pltpu.VMEM((2 * tq, nh * tq), F32)],
        compiler_params=_cparams(("parallel", "parallel", "arbitrary")),
        name="moba_attention",
    )(zat, zb, zat, fq)


def _merge_kernel(ya_ref, yb_ref, hc_ref, bc_ref, cc_ref, hcp_ref, ccp_ref, g0_ref, g1_ref, g2_ref,
                  h_ref, cw_ref, wa_ref, wb_ref, wc_ref, wo_ref, lg_ref, lb_ref, o_ref, *, tm, per, alpha):
    i = pl.program_id(0)
    u = cc_ref[...].astype(F32) * hc_ref[...].astype(F32)
    first = (i % per) == 0
    up = jnp.where(first, 0.0, ccp_ref[...].astype(F32) * hcp_ref[...].astype(F32))
    rowi = lax.broadcasted_iota(I32, u.shape, 0)
    u1 = jnp.where(rowi == 0, up[15:16, :], pltpu.roll(u, 1, 0))
    u2 = jnp.where(rowi == 0, up[14:15, :], jnp.where(rowi == 1, up[15:16, :], pltpu.roll(u, 2, 0)))
    cw = cw_ref[...]
    yc = bc_ref[...].astype(F32) * (cw[0:1, :] * u2 + cw[1:2, :] * u1 + cw[2:3, :] * u)
    merged = (jax.nn.sigmoid(g0_ref[...].astype(F32)) * _dot(ya_ref[...], wa_ref[...])
              + jax.nn.sigmoid(g1_ref[...].astype(F32)) * _dot(yb_ref[...], wb_ref[...])
              + jax.nn.sigmoid(g2_ref[...].astype(F32)) * _dot(yc.astype(BF16), wc_ref[...]))
    mix = _dot(merged.astype(BF16), wo_ref[...])
    o_ref[...] = _layer_norm(alpha * h_ref[...] + mix, lg_ref[...], lb_ref[...])


def _merge(ya, yb, zc, h, cw, wa, wb, wc, wo, lg, lb, seq, alpha):
    tm = 512
    t = h.shape[0]
    per = seq // tm
    d = D_MODEL
    row = lambda i: (i, 0)
    const = lambda i: (0, 0)
    prev = lambda c: (lambda i: (jnp.maximum(i * (tm // 16) - 1, 0), c))
    in_specs = [
        pl.BlockSpec((tm, d), row), pl.BlockSpec((tm, d), row),
        pl.BlockSpec((tm, CONV_CH), lambda i: (i, 1)), pl.BlockSpec((tm, CONV_CH), lambda i: (i, 2)),
        pl.BlockSpec((tm, CONV_CH), lambda i: (i, 3)),
        pl.BlockSpec((16, CONV_CH), prev(1)), pl.BlockSpec((16, CONV_CH), prev(3)),
        pl.BlockSpec((tm, d), lambda i: (i, 2)), pl.BlockSpec((tm, d), lambda i: (i, 3)),
        pl.BlockSpec((tm, d), lambda i: (i, 4)),
        pl.BlockSpec((tm, d), row),
        pl.BlockSpec((8, CONV_CH), const),
        pl.BlockSpec((d, d), const), pl.BlockSpec((d, d), const), pl.BlockSpec((CONV_CH, d), const),
        pl.BlockSpec((d, d), const), pl.BlockSpec((1, d), const), pl.BlockSpec((1, d), const),
    ]
    return pl.pallas_call(
        functools.partial(_merge_kernel, tm=tm, per=per, alpha=alpha),
        out_shape=jax.ShapeDtypeStruct((t, d), F32),
        grid=(t // tm,),
        in_specs=in_specs,
        out_specs=pl.BlockSpec((tm, d), row),
        compiler_params=_cparams(("parallel",)),
        name="mixer_merge",
    )(ya, yb, zc, zc, zc, zc, zc, zc, zc, zc, h, cw, wa, wb, wc, wo, lg, lb)


def _xattn_kernel(h_ref, kv_ref, wq_ref, wo_ref, lg_ref, lb_ref, o_ref, ob_ref, *, alpha):
    h = h_ref[...]
    q = _dot(h.astype(BF16), wq_ref[...]).astype(BF16)
    kv = kv_ref[...]
    nh = XATTN_HEADS
    hd = XATTN_HEAD_DIM
    outs = []
    for hh in range(nh):
        s = _dot_t(q[:, hh * hd:(hh + 1) * hd], kv[:, hh * hd:(hh + 1) * hd]) * (hd ** -0.5)
        s = s - jnp.max(s, axis=-1, keepdims=True)
        e = jnp.exp(s)
        p = e / jnp.sum(e, axis=-1, keepdims=True)
        outs.append(_dot(p.astype(BF16), kv[:, (nh + hh) * hd:(nh + hh + 1) * hd]))
    o = jnp.concatenate(outs, axis=-1).astype(BF16)
    y = _layer_norm(alpha * h + _dot(o, wo_ref[...]), lg_ref[...], lb_ref[...])
    o_ref[...] = y
    bits = lax.bitcast_convert_type(y.astype(BF16).astype(F32), I32)
    half = y.shape[1] // 2
    ob_ref[...] = (bits[:, half:] & -65536) | lax.shift_right_logical(bits[:, :half], 16)


def _unpack_pairs(words):
    lo = lax.bitcast_convert_type(lax.shift_left(words, 16), F32).astype(BF16)
    hi = lax.bitcast_convert_type(words & -65536, F32).astype(BF16)
    return lo, hi


def _xattn(h, kv, wq, wo, lg, lb, seq, mlen, alpha):
    tm = 512
    t = h.shape[0]
    per = seq // tm
    d = D_MODEL
    row = lambda i: (i, 0)
    const = lambda i: (0, 0)
    return pl.pallas_call(
        functools.partial(_xattn_kernel, alpha=alpha),
        out_shape=(jax.ShapeDtypeStruct((t, d), F32), jax.ShapeDtypeStruct((t, d // 2), I32)),
        grid=(t // tm,),
        in_specs=[pl.BlockSpec((tm, d), row), pl.BlockSpec((mlen, kv.shape[1]), lambda i: (i // per, 0)),
                  pl.BlockSpec(wq.shape, const), pl.BlockSpec(wo.shape, const),
                  pl.BlockSpec((1, d), const), pl.BlockSpec((1, d), const)],
        out_specs=(pl.BlockSpec((tm, d), row), pl.BlockSpec((tm, d // 2), row)),
        compiler_params=_cparams(("parallel",)),
        name="mem_xattn",
    )(h, kv, wq, wo, lg, lb)


def _expert_kernel(be_ref, nu_ref, x_ref, w13_ref, w2_ref, o_ref, w13b_scr, w2b_scr):
    i = pl.program_id(0)

    @pl.when((i == 0) | (be_ref[i] != be_ref[jnp.maximum(i - 1, 0)]))
    def _():
        w13b_scr[...] = w13_ref[0, 0].astype(BF16)
        w2b_scr[...] = w2_ref[0, 0].astype(BF16)

    @pl.when(i < nu_ref[0])
    def _():
        x_lo, x_hi = _unpack_pairs(x_ref[...])
        half = x_lo.shape[1]
        hmid = _dot(x_lo, w13b_scr[:half, :]) + _dot(x_hi, w13b_scr[half:, :])
        a = hmid[:, :EXPERT_DIM]
        g = hmid[:, EXPERT_DIM:]
        act = (a * jax.nn.sigmoid(a) * g).astype(BF16)
        o_ref[...] = _dot(act, w2b_scr[...]).astype(o_ref.dtype)


def _experts(xg, w13, w2, block_e, n_used, layer):
    n_pad = xg.shape[0]
    d = 2 * xg.shape[1]
    rows = EXP_ROWS
    nb = n_pad // rows
    blk = lambda i, be, nu: (jnp.minimum(i, nu[0] - 1), 0)
    grid_spec = pltpu.PrefetchScalarGridSpec(
        num_scalar_prefetch=2,
        grid=(nb,),
        in_specs=[pl.BlockSpec((rows, d // 2), blk),
                  pl.BlockSpec((1, 1, d, 2 * EXPERT_DIM), lambda i, be, nu: (layer, be[i], 0, 0)),
                  pl.BlockSpec((1, 1, EXPERT_DIM, d), lambda i, be, nu: (layer, be[i], 0, 0))],
        out_specs=pl.BlockSpec((rows, d), blk),
        scratch_shapes=[pltpu.VMEM((d, 2 * EXPERT_DIM), BF16), pltpu.VMEM((EXPERT_DIM, d), BF16)],
    )
    return pl.pallas_call(
        _expert_kernel,
        out_shape=jax.ShapeDtypeStruct((n_pad, d), BF16),
        grid_spec=grid_spec,
        compiler_params=_cparams(("arbitrary",)),
        name="moe_experts",
    )(block_e, n_used, xg, w13, w2)


def _moe_out_kernel(h_ref, y8_ref, wk_ref, w13_ref, w2_ref, lg_ref, lb_ref, o_ref, ob_ref, *, alpha):
    h = h_ref[...]
    hmid = _dot(h.astype(BF16), w13_ref[...])
    a = hmid[:, :EXPERT_DIM]
    g = hmid[:, EXPERT_DIM:]
    acc = alpha * h + _dot((a * jax.nn.sigmoid(a) * g).astype(BF16), w2_ref[...])
    wk = wk_ref[...]
    for k in range(TOP_K):
        acc = acc + wk[:, k:k + 1] * y8_ref[k].astype(F32)
    y = _layer_norm(acc, lg_ref[...], lb_ref[...])
    o_ref[...] = y
    ob_ref[...] = y.astype(BF16)


def _moe_out(h, y8, wk, w13, w2, lg, lb, alpha):
    tm = 256
    t, d = h.shape
    row = lambda i: (i, 0)
    const = lambda i: (0, 0)
    return pl.pallas_call(
        functools.partial(_moe_out_kernel, alpha=alpha),
        out_shape=(jax.ShapeDtypeStruct((t, d), F32), jax.ShapeDtypeStruct((t, d), BF16)),
        grid=(t // tm,),
        in_specs=[pl.BlockSpec((tm, d), row), pl.BlockSpec((TOP_K, tm, d), lambda i: (0, i, 0)),
                  pl.BlockSpec((tm, TOP_K), row),
                  pl.BlockSpec(w13.shape, const), pl.BlockSpec(w2.shape, const),
                  pl.BlockSpec((1, d), const), pl.BlockSpec((1, d), const)],
        out_specs=(pl.BlockSpec((tm, d), row), pl.BlockSpec((tm, d), row)),
        compiler_params=_cparams(("parallel",)),
        name="moe_shared_ln",
    )(h, y8, wk, w13, w2, lg, lb)


def _dest_kernel(idx_ref, rank_ref, ps_ref, o_ref):
    idx = idx_ref[...]
    ps = ps_ref[...]
    tm = idx.shape[1]
    eidx = lax.broadcasted_iota(I32, (N_EXPERTS, tm), 0)
    rows = [jnp.sum(jnp.where(eidx == idx[k:k + 1, :], ps, 0.0), axis=0, keepdims=True) for k in range(TOP_K)]
    o_ref[...] = jnp.concatenate(rows, axis=0).astype(I32) + rank_ref[...]


def _dest(idx, rank, pstarts):
    tm = 1024
    t = idx.shape[1]
    tm = min(tm, t)
    col = lambda i: (0, i)
    return pl.pallas_call(
        _dest_kernel,
        out_shape=jax.ShapeDtypeStruct((TOP_K, t), I32),
        grid=(t // tm,),
        in_specs=[pl.BlockSpec((TOP_K, tm), col), pl.BlockSpec((TOP_K, tm), col),
                  pl.BlockSpec((N_EXPERTS, 1), lambda i: (0, 0))],
        out_specs=pl.BlockSpec((TOP_K, tm), col),
        compiler_params=_cparams(("parallel",)),
        name="moe_dest",
    )(idx, rank, pstarts)


def _router_kernel(h_ref, rwt_ref, rb_ref, tri_ref, idx_ref, w_ref, rank_ref, cnt_ref, carry_scr, *, tm):
    i = pl.program_id(0)

    @pl.when(i == 0)
    def _():
        carry_scr[...] = jnp.zeros(carry_scr.shape, F32)

    ne = N_EXPERTS
    gsz = ne // N_GROUPS
    s = jax.nn.sigmoid(_dot_t(rwt_ref[...], h_ref[...].astype(BF16)))
    sb = s + rb_ref[...]
    sb3 = sb.reshape(N_GROUPS, gsz, tm)
    li = lax.broadcasted_iota(I32, (N_GROUPS, gsz, tm), 1)
    m1 = jnp.max(sb3, axis=1, keepdims=True)
    first = jnp.min(jnp.where(sb3 == m1, li, gsz), axis=1, keepdims=True)
    m2 = jnp.max(jnp.where(li == first, -jnp.inf, sb3), axis=1, keepdims=True)
    gs = (m1 + m2).reshape(N_GROUPS, tm)
    gi = lax.broadcasted_iota(I32, (N_GROUPS, tm), 0)
    grank = jnp.zeros((N_GROUPS, tm), I32)
    for gp in range(N_GROUPS):
        row = gs[gp:gp + 1, :]
        grank = grank + ((row > gs) | ((row == gs) & (gp < gi))).astype(I32)
    gkeep = (grank < TOPK_GROUPS).astype(F32)
    ekeep = jnp.broadcast_to(gkeep[:, None, :], (N_GROUPS, gsz, tm)).reshape(ne, tm)
    cand = jnp.where(ekeep > 0.0, sb, NEG)
    eidx = lax.broadcasted_iota(I32, (ne, tm), 0)
    sel = jnp.zeros((ne, tm), F32)
    idxs, wts = [], []
    for _ in range(TOP_K):
        mx = jnp.max(cand, axis=0, keepdims=True)
        ik = jnp.min(jnp.where(cand == mx, eidx, ne), axis=0, keepdims=True)
        hit = eidx == ik
        wts.append(jnp.sum(jnp.where(hit, s, 0.0), axis=0, keepdims=True))
        idxs.append(ik)
        sel = jnp.where(hit, 1.0, sel)
        cand = jnp.where(hit, -jnp.inf, cand)
    before = _dot(sel.astype(BF16), tri_ref[...]) + carry_scr[...]
    ranks = [jnp.sum(jnp.where(eidx == ik, before, 0.0), axis=0, keepdims=True) for ik in idxs]
    carry_scr[...] = carry_scr[...] + jnp.sum(sel, axis=1, keepdims=True)
    w = jnp.concatenate(wts, axis=0)
    idx_ref[...] = jnp.concatenate(idxs, axis=0)
    w_ref[...] = w / jnp.sum(w, axis=0, keepdims=True) * ROUTE_SCALE
    rank_ref[...] = jnp.concatenate(ranks, axis=0).astype(I32)
    cnt_ref[...] = jnp.broadcast_to(carry_scr[...], cnt_ref.shape)


def _router(h, rwt, rb):
    tm = 256
    t, d = h.shape
    tri = jnp.asarray(np.triu(np.ones((tm, tm), np.float32), 1)).astype(BF16)
    const = lambda i: (0, 0)
    col = lambda i: (0, i)
    return pl.pallas_call(
        functools.partial(_router_kernel, tm=tm),
        out_shape=(jax.ShapeDtypeStruct((TOP_K, t), I32), jax.ShapeDtypeStruct((TOP_K, t), F32),
                   jax.ShapeDtypeStruct((TOP_K, t), I32), jax.ShapeDtypeStruct((N_EXPERTS, SLOT), F32)),
        grid=(t // tm,),
        in_specs=[pl.BlockSpec((tm, d), lambda i: (i, 0)), pl.BlockSpec((N_EXPERTS, d), const),
                  pl.BlockSpec((N_EXPERTS, 1), const), pl.BlockSpec((tm, tm), const)],
        out_specs=(pl.BlockSpec((TOP_K, tm), col), pl.BlockSpec((TOP_K, tm), col),
                   pl.BlockSpec((TOP_K, tm), col), pl.BlockSpec((N_EXPERTS, SLOT), const)),
        scratch_shapes=[pltpu.VMEM((N_EXPERTS, 1), F32)],
        compiler_params=_cparams(("arbitrary",)),
        name="moe_router",
    )(h, rwt, rb, tri)


def _dispatch(hp, dest_flat, n_pad):
    t, w = hp.shape
    per_w = t // (SC_CORES * SC_SUBCORES)
    assert per_w * SC_CORES * SC_SUBCORES == t and per_w % SC_CHUNK == 0 and dest_flat.shape[0] == TOP_K * t
    mesh = plsc.VectorSubcoreMesh(core_axis_name="c", subcore_axis_name="s")

    @functools.partial(
        pl.kernel, out_type=jax.ShapeDtypeStruct((n_pad, w), I32), mesh=mesh,
        scratch_types=[pltpu.VMEM((SC_CHUNK,), I32)] * TOP_K
        + [pltpu.VMEM((SC_CHUNK, w), I32), pltpu.SemaphoreType.DMA],
        name="moe_dispatch")
    def scatter_rows(hp_hbm, dest_hbm, xg_hbm, *scratch):
        idx_vs, rows_v, sem = scratch[:TOP_K], scratch[TOP_K], scratch[TOP_K + 1]
        wid = lax.axis_index("s") * SC_CORES + lax.axis_index("c")
        base = wid * per_w

        @pl.loop(0, per_w // SC_CHUNK)
        def _(j):
            t0 = pl.multiple_of(base + j * SC_CHUNK, SC_CHUNK)
            pltpu.sync_copy(hp_hbm.at[pl.ds(t0, SC_CHUNK)], rows_v)
            for k in range(TOP_K):
                pltpu.sync_copy(dest_hbm.at[pl.ds(pl.multiple_of(k * t + t0, SC_CHUNK), SC_CHUNK)], idx_vs[k])
            copies = [pltpu.async_copy(rows_v, xg_hbm.at[idx_vs[k]], sem) for k in range(TOP_K)]
            for cp in copies:
                cp.wait()

    return scatter_rows(hp, dest_flat)


def _moe(h, hp, router_w, router_b, exp_w13, exp_w2, layer, shared_w13, shared_w2, lg, lb, alpha):
    t, d = h.shape
    idx, wts, rank, cnt = _router(h, router_w.T, router_b.astype(F32).reshape(N_EXPERTS, 1))
    rows = EXP_ROWS
    n_a = t * TOP_K
    counts = cnt[:, 0].astype(I32)
    pcounts = (counts + rows - 1) // rows * rows
    pends = jnp.cumsum(pcounts)
    pstarts = pends - pcounts
    dest = _dest(idx, rank, pstarts.astype(F32).reshape(N_EXPERTS, 1))
    n_blocks = -(-n_a // rows) + N_EXPERTS
    n_pad = n_blocks * rows
    n_used = (pends[-1:] // rows).astype(I32)
    first_row = jnp.minimum(jnp.arange(n_blocks, dtype=I32), n_used - 1) * rows
    block_e = jnp.sum((pends[None, :] <= first_row[:, None]).astype(I32), axis=1)
    xg = _dispatch(hp, dest.reshape(-1), n_pad)
    out = _experts(xg, exp_w13, exp_w2, block_e, n_used, layer)
    y8 = jnp.take(out, dest, axis=0, mode="clip")
    return _moe_out(h, y8, wts.T, shared_w13, shared_w2, lg, lb, alpha)


def _pad_slots(w, scale=1.0):
    dm = w.shape[0]
    nh = w.shape[1] // HEAD_DIM
    w = (w * scale).reshape(dm, nh, HEAD_DIM)
    return jnp.concatenate([w, jnp.zeros_like(w)], axis=-1).reshape(dm, nh * SLOT)


def _pad_rows(w):
    n = w.shape[1]
    nh = w.shape[0] // HEAD_DIM
    w = w.reshape(nh, HEAD_DIM, n)
    return jnp.concatenate([w, jnp.zeros_like(w)], axis=1).reshape(nh * SLOT, n)


def _alibi(n):
    return np.exp2(-8.0 * np.arange(1, n + 1, dtype=np.float64) / n).astype(np.float32)


def _key_features(seq):
    p = np.arange(seq)
    slc = np.zeros((seq, SLOT), np.float32)
    slc[:, HEAD_DIM] = p % SLC_BLOCK
    blk = p // SLC_BLOCK
    nz = blk > 0
    slc[p[nz], HEAD_DIM + blk[nz]] = 1.0
    win = np.zeros((seq, SLOT), np.float32)
    win[:, HEAD_DIM] = p // 64
    win[:, HEAD_DIM + 1] = p % 64
    mob = np.zeros((seq, SLOT), np.float32)
    mob[p, HEAD_DIM + p // MOBA_BLOCK] = 1.0
    mob[:, HEAD_DIM + 16] = p % MOBA_BLOCK
    return np.concatenate([slc] * 2 + [win] * 2 + [mob] * MOBA_HEADS, axis=1)


def _query_features():
    sl = _alibi(NSA_HEADS)
    nsa = np.zeros((NSA_KV_HEADS, SLOT, 8), np.float32)
    for g in range(NSA_KV_HEADS):
        for r in range(NSA_GROUP):
            s = sl[g * NSA_GROUP + r]
            nsa[g, HEAD_DIM, r] = s
            nsa[g, HEAD_DIM + 1:, r] = s * SLC_BLOCK * np.arange(1, 64)
            nsa[g, HEAD_DIM, NSA_GROUP + r] = s * 64
            nsa[g, HEAD_DIM + 1, NSA_GROUP + r] = s
    sm = _alibi(MOBA_HEADS)
    mob = np.zeros((MOBA_HEADS, SLOT, 8), np.float32)
    for h in range(MOBA_HEADS):
        mob[h, HEAD_DIM:HEAD_DIM + 16, 0] = sm[h] * MOBA_BLOCK * np.arange(16)
        mob[h, HEAD_DIM + 16, 0] = sm[h]
    return nsa, mob


def _overlap_t(seq):
    nc = seq // CMP_STRIDE
    c_start = np.arange(nc) * CMP_STRIDE
    b_start = np.arange(64) * SLC_BLOCK
    ov = ((c_start[None, :] < (b_start + SLC_BLOCK)[:, None])
          & ((c_start + CMP_BLOCK)[None, :] > b_start[:, None])
          & (np.arange(nc) < nc - 1)[None, :] & (b_start < seq)[:, None])
    return ov.astype(np.float32)


def _compress_weights(pe, w1, w2):
    ty = np.array([0, 0, 1, 1])
    eye = jnp.eye(4, dtype=F32)
    w1r = w1.reshape(2, CMP_BLOCK, HEAD_DIM, CMP_HIDDEN)[ty]
    top = jnp.einsum('spdj,sS->psdSj', w1r[:, :CMP_STRIDE], eye).reshape(CMP_STRIDE * 256, 4 * CMP_HIDDEN)
    bot = jnp.einsum('spdj,sS->psdSj', w1r[:, CMP_STRIDE:], eye).reshape(CMP_STRIDE * 256, 4 * CMP_HIDDEN)
    per = pe[ty]
    pet = jnp.transpose(per[:, :CMP_STRIDE], (1, 0, 2)).reshape(1, CMP_STRIDE * 256)
    peb = jnp.transpose(per[:, CMP_STRIDE:], (1, 0, 2)).reshape(1, CMP_STRIDE * 256)
    w2p = jnp.concatenate([w2[ty], jnp.zeros((4, CMP_HIDDEN, SLOT - HEAD_DIM), F32)], axis=-1)
    w2b = jnp.einsum('sjd,sS->sjSd', w2p, eye).reshape(4 * CMP_HIDDEN, 4 * SLOT)
    w2k = w2b[:, :2 * SLOT].astype(BF16)
    w2vt = w2b[:, 2 * SLOT:].T.astype(BF16)
    return pet, peb, top.astype(BF16), bot.astype(BF16), w2k, w2vt


def kernel(x, mem, w_in, cmp_pe, cmp_w1, cmp_w2, conv_w, w_branch, w_out, ln1_g, ln1_b,
           xattn_wq, xattn_wkv, xattn_wo, ln2_g, ln2_b, router_w, router_b, exp_w13, exp_w2,
           shared_w13, shared_w2, ln3_g, ln3_b):
    bsz, seq, d = x.shape
    mlen = mem.shape[1]
    depth = w_in.shape[0]
    alpha = (2.0 * depth) ** 0.25
    t = bsz * seq
    scale = HEAD_DIM ** -0.5

    kfeat = jnp.asarray(_key_features(seq))
    fq_nsa, fq_moba = (jnp.asarray(a) for a in _query_features())
    ovt = jnp.asarray(_overlap_t(seq)).astype(BF16)
    memf = mem.reshape(bsz * mlen, d)

    h = x.reshape(t, d)
    hb = h.astype(BF16)
    for l in range(depth):
        ht = hb.T
        wi = w_in[l]
        kv6 = wi[:, _OFF_NSA_KV:_OFF_NSA_G].reshape(d, 6, NSA_KV_HEADS * HEAD_DIM)
        mq, mk, mv = (wi[:, _OFF_MOBA + j * 512:_OFF_MOBA + (j + 1) * 512] for j in range(3))
        w_at = jnp.concatenate([wi[:, :512] * scale, kv6[:, 3], kv6[:, 5], mq * scale, mv],
                               axis=1).T.astype(BF16)
        w_b = jnp.concatenate([_pad_slots(kv6[:, 2]), _pad_slots(kv6[:, 4]), _pad_slots(mk)], axis=1).astype(BF16)
        wg = wi[:, _OFF_NSA_G:_OFF_MOBA].reshape(d, NSA_KV_HEADS, 12)
        wgt = jnp.concatenate([wg, jnp.zeros((d, NSA_KV_HEADS, SLOT - 12), F32)], axis=-1)
        wgt = wgt.reshape(d, NSA_KV_HEADS * SLOT).T.astype(BF16)
        w_c = jnp.concatenate([wi[:, _OFF_NSA_KV:_OFF_NSA_KV + 256], jnp.zeros((d, 256), F32),
                               wi[:, _OFF_CONV:]], axis=1).astype(BF16)

        value_slots = tuple(range(_SLOT_SLC_V, _SLOT_MOBA_Q)) + tuple(range(_SLOT_MOBA_V, _N_SLOTS_T))
        zat = _proj_t(w_at, ht, value_slots)
        gt = _mm(wgt, ht, F32, tm=256, tn=1024, name="proj_gate")
        zb = _mm(hb, w_b, BF16, tn=768, feats=kfeat, name="proj_b")
        zc = _mm(hb, w_c, BF16, tn=1024, name="proj_c")

        pet, peb, wt, wb, w2k, w2vt = _compress_weights(cmp_pe[l], cmp_w1[l], cmp_w2[l])
        sub = zc[:, :256].reshape(bsz, seq // CMP_STRIDE, CMP_STRIDE * 256)
        kc, vct = _compress(sub, pet, peb, wt, wb, w2k, w2vt)

        ya = _nsa_attention(zat, zb, gt, kc, vct, ovt, fq_nsa, bsz, seq)
        yb = _moba_attention(zat, zb, fq_moba, bsz, seq)

        cw = jnp.concatenate([conv_w[l], jnp.zeros((5, CONV_CH), F32)], axis=0)
        h = _merge(ya, yb, zc, h, cw, _pad_rows(w_branch[l, 0]).astype(BF16),
                   _pad_rows(w_branch[l, 1]).astype(BF16), w_branch[l, 2].astype(BF16),
                   w_out[l].astype(BF16), ln1_g[l][None], ln1_b[l][None], seq, alpha)

        kv = _mm(memf, xattn_wkv[l].astype(BF16), BF16, tm=512, name="xattn_kv")
        h, hp = _xattn(h, kv, xattn_wq[l].astype(BF16), xattn_wo[l].astype(BF16),
                       ln2_g[l][None], ln2_b[l][None], seq, mlen, alpha)

        h, hb = _moe(h, hp, router_w[l].astype(BF16), router_b[l], exp_w13, exp_w2, l,
                     shared_w13[l].astype(BF16), shared_w2[l].astype(BF16), ln3_g[l][None], ln3_b[l][None], alpha)
    return h.reshape(bsz, seq, d)
```

```python
import functools

import jax
import jax.numpy as jnp
import numpy as np
from jax import lax
from jax.experimental import pallas as pl
from jax.experimental.pallas import tpu as pltpu
from jax.experimental.pallas import tpu_sc as plsc

F32 = jnp.float32
BF16 = jnp.bfloat16
I32 = jnp.int32

D_MODEL = 1024
HEAD_DIM = 64
SLOT = 128
NEG = -1e30
LN_EPS = 1e-5

NSA_HEADS = 8
NSA_KV_HEADS = 2
NSA_GROUP = 4
CMP_BLOCK = 32
CMP_STRIDE = 16
CMP_HIDDEN = 256
SLC_BLOCK = 64
SLC_TOPK = 16
WINDOW = 512
MOBA_HEADS = 8
MOBA_BLOCK = 256
MOBA_TOPK = 3
CONV_CH = 512
XATTN_HEADS = 4
XATTN_HEAD_DIM = 128
N_EXPERTS = 256
TOP_K = 8
N_GROUPS = 8
TOPK_GROUPS = 4
EXPERT_DIM = 256
ROUTE_SCALE = 2.5

TQ = 256
MOBA_HPS = 4
EXP_ROWS = 512
VMEM_LIMIT = 48 * 1024 * 1024
SC_CORES = 2
SC_SUBCORES = 16
SC_CHUNK = 64

_OFF_NSA_Q = 0
_OFF_NSA_KV = 512
_OFF_NSA_G = 1280
_OFF_MOBA = 1304
_OFF_CONV = 2840
_OFF_MERGE = 4376

_SLOT_NSA_Q = 0
_SLOT_SLC_V = 8
_SLOT_WIN_V = 10
_SLOT_MOBA_Q = 12
_SLOT_MOBA_V = 20
_N_SLOTS_T = 28


def _cparams(sem):
    return pltpu.CompilerParams(dimension_semantics=sem, vmem_limit_bytes=VMEM_LIMIT)


def _dot(a, b):
    return jnp.dot(a, b, preferred_element_type=F32)


def _dot_t(a, b):
    return lax.dot_general(a, b, (((1,), (1,)), ((), ())), preferred_element_type=F32)


def _layer_norm(x, g, b):
    mu = jnp.mean(x, axis=-1, keepdims=True)
    xc = x - mu
    var = jnp.mean(xc * xc, axis=-1, keepdims=True)
    return xc * lax.rsqrt(var + LN_EPS) * g + b


def _mm_kernel(x_ref, w_ref, o_ref):
    o_ref[...] = _dot(x_ref[...].astype(BF16), w_ref[...]).astype(o_ref.dtype)


def _mm_feat_kernel(x_ref, w_ref, f_ref, o_ref):
    y = _dot(x_ref[...].astype(BF16), w_ref[...]) + f_ref[...]
    o_ref[...] = y.astype(o_ref.dtype)


def _mm(x, w, out_dtype, *, tm=1024, tn=512, feats=None, name):
    m, k = x.shape
    n = w.shape[1]
    tm = min(tm, m)
    tn = min(tn, n)
    assert m % tm == 0 and n % tn == 0, (m, n, tm, tn)
    in_specs = [pl.BlockSpec((tm, k), lambda i, j: (i, 0)),
                pl.BlockSpec((k, tn), lambda i, j: (0, j))]
    args = [x, w]
    if feats is None:
        body = _mm_kernel
    else:
        per = feats.shape[0] // tm
        assert feats.shape[0] % tm == 0
        in_specs.append(pl.BlockSpec((tm, tn), lambda i, j: (i % per, j)))
        args.append(feats)
        body = _mm_feat_kernel
    return pl.pallas_call(
        body,
        out_shape=jax.ShapeDtypeStruct((m, n), out_dtype),
        grid=(m // tm, n // tn),
        in_specs=in_specs,
        out_specs=pl.BlockSpec((tm, tn), lambda i, j: (i, j)),
        compiler_params=_cparams(("parallel", "parallel")),
        name=name,
    )(*args)


def _proj_t_kernel(w_ref, ht_ref, o_ref, *, value_slots):
    acc = _dot(w_ref[...], ht_ref[...])
    tn = acc.shape[1]
    zero = jnp.zeros((SLOT - HEAD_DIM, tn), o_ref.dtype)
    ones_row = (lax.broadcasted_iota(I32, (SLOT - HEAD_DIM, tn), 0) == 0).astype(o_ref.dtype)
    for s in range(w_ref.shape[0] // HEAD_DIM):
        o_ref[s * SLOT:s * SLOT + HEAD_DIM, :] = acc[s * HEAD_DIM:(s + 1) * HEAD_DIM, :].astype(o_ref.dtype)
        o_ref[s * SLOT + HEAD_DIM:(s + 1) * SLOT, :] = ones_row if s in value_slots else zero


def _proj_t(w, ht, value_slots):
    m, k = w.shape
    t = ht.shape[1]
    tn = min(512, t)
    n_out = m // HEAD_DIM * SLOT
    return pl.pallas_call(
        functools.partial(_proj_t_kernel, value_slots=value_slots),
        out_shape=jax.ShapeDtypeStruct((n_out, t), BF16),
        grid=(t // tn,),
        in_specs=[pl.BlockSpec((m, k), lambda j: (0, 0)), pl.BlockSpec((k, tn), lambda j: (0, j))],
        out_specs=pl.BlockSpec((n_out, tn), lambda j: (0, j)),
        compiler_params=_cparams(("parallel",)),
        name="proj_at",
    )(w, ht)


def _cmp_kernel(sub_ref, pet_ref, peb_ref, wt_ref, wb_ref, w2k_ref, w2vt_ref, kc_ref, vct_ref):
    sub = sub_ref[0].astype(F32)
    nc = sub.shape[0]
    a = _dot((sub + pet_ref[...]).astype(BF16), wt_ref[...])
    b = _dot((sub + peb_ref[...]).astype(BF16), wb_ref[...])
    hid = jax.nn.gelu(a + pltpu.roll(b, nc - 1, 0)).astype(BF16)
    kc_ref[0] = _dot(hid, w2k_ref[...]).astype(kc_ref.dtype)
    vct_ref[0] = _dot_t(w2vt_ref[...], hid).astype(vct_ref.dtype)


def _compress(sub, pet, peb, wt, wb, w2k, w2vt):
    bsz, nc, kk = sub.shape
    n_h = wt.shape[1]
    n_o = w2k.shape[1]
    const = lambda b: (0, 0)
    return pl.pallas_call(
        _cmp_kernel,
        out_shape=(jax.ShapeDtypeStruct((bsz, nc, n_o), BF16), jax.ShapeDtypeStruct((bsz, n_o, nc), BF16)),
        grid=(bsz,),
        in_specs=[pl.BlockSpec((1, nc, kk), lambda b: (b, 0, 0)),
                  pl.BlockSpec((1, kk), const), pl.BlockSpec((1, kk), const),
                  pl.BlockSpec((kk, n_h), const), pl.BlockSpec((kk, n_h), const),
                  pl.BlockSpec((n_h, n_o), const), pl.BlockSpec((n_o, n_h), const)],
        out_specs=(pl.BlockSpec((1, nc, n_o), lambda b: (b, 0, 0)), pl.BlockSpec((1, n_o, nc), lambda b: (b, 0, 0))),
        compiler_params=_cparams(("parallel",)),
        name="nsa_compress",
    )(sub, pet, peb, wt, wb, w2k, w2vt)


def _flash_reset(m_scr, acc_scr):
    m_scr[...] = jnp.full(m_scr.shape, NEG, F32)
    acc_scr[...] = jnp.zeros(acc_scr.shape, F32)


def _qk(kget, qxs, start, size):
    k0 = kget(0, start, size)
    if kget(1, start, size) is None:
        return _dot(k0, jnp.concatenate(qxs, axis=1))
    return jnp.concatenate([_dot(k0, qxs[0])] + [_dot(kget(c, start, size), qxs[c])
                                                 for c in range(1, len(qxs))], axis=1)


def _softmax_pv(s, vget, nq, start, size, m_scr, acc_scr, bias=None):
    if bias is not None:
        s = s + jnp.concatenate([bias] * nq, axis=1)
    m_prev = m_scr[...]
    m_new = jnp.maximum(m_prev, jnp.max(s, axis=0, keepdims=True))
    alpha = jnp.exp(m_prev - m_new)
    p = jnp.exp(s - m_new).astype(BF16)
    v0 = vget(0, start, size)
    if vget(1, start, size) is None:
        pv = _dot(v0, p)
    else:
        pv = jnp.concatenate([_dot(v0, p[:, :TQ])] + [_dot(vget(c, start, size), p[:, c * TQ:(c + 1) * TQ])
                                                      for c in range(1, nq)], axis=1)
    acc_scr[...] = alpha * acc_scr[...] + pv
    m_scr[...] = m_new


def _rows(ref, start, size, lane0=0):
    return ref[pl.ds(pl.multiple_of(start, TQ), size), lane0:lane0 + SLOT]


def _cols(ref, start, size, row0=0):
    return ref[row0:row0 + SLOT, pl.ds(pl.multiple_of(start, TQ), size)]


def _flash_causal(kget, vget, qxs, i, cb_ref, sa_scr, sb_scr, m_scr, acc_scr):
    tq = TQ
    tk = 2 * tq
    nq = len(qxs)
    n_steps = i // 2 + 1
    sa_scr[...] = _qk(kget, qxs, 0, tk)

    def body(jj, carry):
        j = 2 * jj
        s = sa_scr[...]
        sb_scr[...] = _qk(kget, qxs, (j + 1) * tk, tk)
        _softmax_pv(s, vget, nq, j * tk, tk, m_scr, acc_scr)
        s = sb_scr[...]
        sa_scr[...] = _qk(kget, qxs, (j + 2) * tk, tk)
        _softmax_pv(s, vget, nq, (j + 1) * tk, tk, m_scr, acc_scr)
        return carry

    lax.fori_loop(0, (n_steps - 1) // 2, body, 0)
    last = n_steps - 1
    bias = cb_ref[i % 2]

    @pl.when(last % 2 == 0)
    def _():
        _softmax_pv(sa_scr[...], vget, nq, last * tk, tk, m_scr, acc_scr, bias=bias)

    @pl.when(last % 2 == 1)
    def _():
        s = sa_scr[...]
        sb_scr[...] = _qk(kget, qxs, last * tk, tk)
        _softmax_pv(s, vget, nq, (last - 1) * tk, tk, m_scr, acc_scr)
        _softmax_pv(sb_scr[...], vget, nq, last * tk, tk, m_scr, acc_scr, bias=bias)


def _nsa_kernel(q_ref, kc_ref, vct_ref, ks_ref, vst_ref, kw_ref, vwt_ref, g_ref, ovt_ref, fq_ref, cb_ref, wb_ref,
                o_ref, m_scr, acc_scr, imp_scr, sa_scr, sb_scr, *, tq, nsel):
    i = pl.program_id(2)
    r4 = NSA_GROUP
    rr = r4 * tq
    q4 = [q_ref[r * SLOT:(r + 1) * SLOT, :] for r in range(r4)]
    fq = fq_ref[0]

    kc = kc_ref[0]
    nc = kc.shape[0]
    qs = jnp.concatenate(q4, axis=1)
    s = _dot(kc, qs)
    t_row = i * tq + (lax.broadcasted_iota(I32, (1, rr), 1) & (tq - 1))
    last_c = (t_row - (CMP_BLOCK - 1)) >> 4
    cmask = lax.broadcasted_iota(I32, (nc, rr), 0) <= last_c
    s = jnp.where(cmask, s, NEG)
    mx = jnp.max(s, axis=0, keepdims=True)
    e = jnp.where(cmask, jnp.exp(s - mx), 0.0)
    lsum = jnp.sum(e, axis=0, keepdims=True)
    p_cmp = (e * (1.0 / jnp.where(lsum > 0.0, lsum, 1.0))).astype(BF16)
    o_cmp = _dot(vct_ref[0], p_cmp)

    ovt = ovt_ref[...]
    imp = _dot(ovt, p_cmp[:, 0:tq])
    for r in range(1, r4):
        imp = imp + _dot(ovt, p_cmp[:, r * tq:(r + 1) * tq])
    nbp = imp.shape[0]
    jidx = lax.broadcasted_iota(I32, (nbp, tq), 0)
    cur = (i * tq + lax.broadcasted_iota(I32, (nbp, tq), 1)) // SLC_BLOCK
    forced = (jidx == 0) | (jidx == cur) | (jidx == cur - 1)
    imp = jnp.where(forced, 1e6, jnp.where(jidx > cur, -1e6, imp))
    imp_scr[...] = imp

    def rank_body(jp, rank):
        row = imp_scr[pl.ds(jp, 1), :]
        better = (row > imp) | ((row == imp) & (jp < jidx))
        return rank + better.astype(I32)

    n_live = (i + 1) * (tq // SLC_BLOCK)
    rank = lax.fori_loop(0, n_live, rank_body, jnp.zeros((nbp, tq), I32))
    sel = (rank < nsel) & (jidx <= cur)
    mask_t = jnp.where(sel, 0.0, NEG)
    feat = jnp.concatenate([jnp.zeros((SLOT - nbp, tq), F32), mask_t], axis=0)

    qxs = [q4[r] + (feat + fq[:, r:r + 1]).astype(BF16) for r in range(r4)]
    kget = lambda c, start, size: _rows(ks_ref, start, size) if c == 0 else None
    vget = lambda c, start, size: _cols(vst_ref, start, size) if c == 0 else None
    _flash_reset(m_scr, acc_scr)
    _flash_causal(kget, vget, qxs, i, cb_ref, sa_scr, sb_scr, m_scr, acc_scr)
    o_slc = acc_scr[...] * (1.0 / acc_scr[HEAD_DIM:HEAD_DIM + 1, :])

    qx = jnp.concatenate([q4[r] + fq[:, r4 + r:r4 + r + 1].astype(BF16) for r in range(r4)], axis=1)
    w0 = jnp.maximum(i - 2, 0) * tq
    s = _dot(_rows(kw_ref, w0, 3 * tq), qx)
    s = s + jnp.concatenate([wb_ref[jnp.minimum(i, 2)]] * r4, axis=1)
    e = jnp.exp(s - jnp.max(s, axis=0, keepdims=True)).astype(BF16)
    o_win = _dot(_cols(vwt_ref, w0, 3 * tq), e)
    o_win = o_win * (1.0 / o_win[HEAD_DIM:HEAD_DIM + 1, :])

    gate = jax.nn.sigmoid(g_ref[...])
    for r in range(r4):
        sl = slice(r * tq, (r + 1) * tq)
        o = (gate[3 * r:3 * r + 1, :] * o_cmp[:, sl] + gate[3 * r + 1:3 * r + 2, :] * o_slc[:, sl]
             + gate[3 * r + 2:3 * r + 3, :] * o_win[:, sl])
        o_ref[:, r * SLOT:(r + 1) * SLOT] = o.T.astype(o_ref.dtype)


def _nsa_attention(zat, zb, gt, kc, vct, ovt, fq, cb, wb, bsz, seq):
    tq = TQ
    ni = seq // tq
    nc = kc.shape[1]
    nsel = min(SLC_TOPK, seq // SLC_BLOCK)
    assert seq // SLC_BLOCK <= 64 and WINDOW == 2 * tq and ni % 2 == 0 and ni >= 3
    rr = NSA_GROUP * tq
    gw = NSA_GROUP * SLOT
    in_specs = [
        pl.BlockSpec((gw, tq), lambda b, g, i: (g, b * ni + i)),
        pl.BlockSpec((1, nc, SLOT), lambda b, g, i: (b, 0, g)),
        pl.BlockSpec((1, SLOT, nc), lambda b, g, i: (b, g, 0)),
        pl.BlockSpec((seq, SLOT), lambda b, g, i: (b, g)),
        pl.BlockSpec((SLOT, seq), lambda b, g, i: (_SLOT_SLC_V + g, b)),
        pl.BlockSpec((seq, SLOT), lambda b, g, i: (b, 2 + g)),
        pl.BlockSpec((SLOT, seq), lambda b, g, i: (_SLOT_WIN_V + g, b)),
        pl.BlockSpec((SLOT, tq), lambda b, g, i: (g, b * ni + i)),
        pl.BlockSpec(ovt.shape, lambda b, g, i: (0, 0)),
        pl.BlockSpec((1, SLOT, 8), lambda b, g, i: (g, 0, 0)),
        pl.BlockSpec(cb.shape, lambda b, g, i: (0, 0, 0)),
        pl.BlockSpec(wb.shape, lambda b, g, i: (0, 0, 0)),
    ]
    return pl.pallas_call(
        functools.partial(_nsa_kernel, tq=tq, nsel=nsel),
        out_shape=jax.ShapeDtypeStruct((bsz * seq, NSA_HEADS * SLOT), BF16),
        grid=(bsz, NSA_KV_HEADS, ni),
        in_specs=in_specs,
        out_specs=pl.BlockSpec((tq, gw), lambda b, g, i: (b * ni + i, g)),
        scratch_shapes=[pltpu.VMEM((1, rr), F32),
                        pltpu.VMEM((SLOT, rr), F32), pltpu.VMEM((64, tq), F32),
                        pltpu.VMEM((2 * tq, rr), F32), pltpu.VMEM((2 * tq, rr), F32)],
        compiler_params=_cparams(("parallel", "parallel", "arbitrary")),
        name="nsa_attention",
    )(zat, kc, vct, zb, zat, zb, zat, gt, ovt, fq, cb, wb)


def _moba_kernel(q_ref, k_ref, vt_ref, fq_ref, cb_ref, o_ref, m_scr, acc_scr, km_scr, gate_scr,
                 sa_scr, sb_scr, *, tq, nblk, ntop):
    i = pl.program_id(2)
    nh = MOBA_HPS

    @pl.when(i == 0)
    def _():
        for hh in range(nh):
            kh = k_ref[:, hh * SLOT:(hh + 1) * SLOT].astype(F32)
            km = jnp.mean(kh.reshape(nblk, tq, SLOT), axis=1)
            if nblk < 16:
                km = jnp.concatenate([km, jnp.zeros((16 - nblk, SLOT), F32)], axis=0)
            km_scr[hh] = km

    qs = [q_ref[hh * SLOT:(hh + 1) * SLOT, :] for hh in range(nh)]
    gates = [_dot(km_scr[hh].astype(BF16), qs[hh]) for hh in range(nh)]
    for hh in range(nh):
        gate_scr[hh] = gates[hh]
    jidx = lax.broadcasted_iota(I32, (16, tq), 0)

    def rank_body(jp, ranks):
        out = []
        for hh in range(nh):
            row = gate_scr[hh, pl.ds(jp, 1), :]
            better = (row > gates[hh]) | ((row == gates[hh]) & (jp < jidx))
            out.append(ranks[hh] + better.astype(I32))
        return tuple(out)

    ranks = lax.fori_loop(0, i, rank_body, tuple(jnp.zeros((16, tq), I32) for _ in range(nh)))
    qxs = []
    for hh in range(nh):
        keep = ((jidx < i) & (ranks[hh] < ntop)) | (jidx == i)
        mask_t = jnp.where(keep, 0.0, NEG)
        feat = jnp.concatenate([jnp.zeros((HEAD_DIM, tq), F32), mask_t,
                                jnp.zeros((SLOT - HEAD_DIM - 16, tq), F32)], axis=0)
        qxs.append(qs[hh] + (feat + fq_ref[hh][:, 0:1]).astype(BF16))
    kget = lambda c, start, size: _rows(k_ref, start, size, c * SLOT)
    vget = lambda c, start, size: _cols(vt_ref, start, size, c * SLOT)
    _flash_reset(m_scr, acc_scr)
    _flash_causal(kget, vget, qxs, i, cb_ref, sa_scr, sb_scr, m_scr, acc_scr)
    o = acc_scr[...] * (1.0 / acc_scr[HEAD_DIM:HEAD_DIM + 1, :])
    for hh in range(nh):
        o_ref[:, hh * SLOT:(hh + 1) * SLOT] = o[:, hh * tq:(hh + 1) * tq].T.astype(o_ref.dtype)


def _moba_attention(zat, zb, fq, cb, bsz, seq):
    tq = TQ
    assert tq == MOBA_BLOCK and seq % tq == 0
    ni = seq // tq
    assert ni <= 16 and ni % 2 == 0
    ntop = min(MOBA_TOPK, ni)
    nh = MOBA_HPS
    hw = nh * SLOT
    q0 = _SLOT_MOBA_Q // nh
    v0 = _SLOT_MOBA_V // nh
    return pl.pallas_call(
        functools.partial(_moba_kernel, tq=tq, nblk=ni, ntop=ntop),
        out_shape=jax.ShapeDtypeStruct((bsz * seq, MOBA_HEADS * SLOT), BF16),
        grid=(bsz, MOBA_HEADS // nh, ni),
        in_specs=[pl.BlockSpec((hw, tq), lambda b, h, i: (q0 + h, b * ni + i)),
                  pl.BlockSpec((seq, hw), lambda b, h, i: (b, 1 + h)),
                  pl.BlockSpec((hw, seq), lambda b, h, i: (v0 + h, b)),
                  pl.BlockSpec((nh, SLOT, 8), lambda b, h, i: (h, 0, 0)),
                  pl.BlockSpec(cb.shape, lambda b, h, i: (0, 0, 0))],
        out_specs=pl.BlockSpec((tq, hw), lambda b, h, i: (b * ni + i, h)),
        scratch_shapes=[pltpu.VMEM((1, nh * tq), F32),
                        pltpu.VMEM((SLOT, nh * tq), F32),
                        pltpu.VMEM((nh, 16, SLOT), F32), pltpu.VMEM((nh, 16, tq), F32),
                        pltpu.VMEM((2 * tq, nh * tq), F32), pltpu.VMEM((2 * tq, nh * tq), F32)],
        compiler_params=_cparams(("parallel", "parallel", "arbitrary")),
        name="moba_attention",
    )(zat, zb, zat, fq, cb)


def _merge_kernel(ya_ref, yb_ref, hc_ref, bc_ref, cc_ref, hcp_ref, ccp_ref, g0_ref, g1_ref, g2_ref,
                  h_ref, cw_ref, wa_ref, wb_ref, wc_ref, wo_ref, lg_ref, lb_ref, o_ref, *, tm, per, alpha):
    i = pl.program_id(0)
    u = cc_ref[...].astype(F32) * hc_ref[...].astype(F32)
    first = (i % per) == 0
    up = jnp.where(first, 0.0, ccp_ref[...].astype(F32) * hcp_ref[...].astype(F32))
    rowi = lax.broadcasted_iota(I32, u.shape, 0)
    u1 = jnp.where(rowi == 0, up[15:16, :], pltpu.roll(u, 1, 0))
    u2 = jnp.where(rowi == 0, up[14:15, :], jnp.where(rowi == 1, up[15:16, :], pltpu.roll(u, 2, 0)))
    cw = cw_ref[...]
    yc = bc_ref[...].astype(F32) * (cw[0:1, :] * u2 + cw[1:2, :] * u1 + cw[2:3, :] * u)
    merged = (jax.nn.sigmoid(g0_ref[...].astype(F32)) * _dot(ya_ref[...], wa_ref[...])
              + jax.nn.sigmoid(g1_ref[...].astype(F32)) * _dot(yb_ref[...], wb_ref[...])
              + jax.nn.sigmoid(g2_ref[...].astype(F32)) * _dot(yc.astype(BF16), wc_ref[...]))
    mix = _dot(merged.astype(BF16), wo_ref[...])
    o_ref[...] = _layer_norm(alpha * h_ref[...] + mix, lg_ref[...], lb_ref[...])


def _merge(ya, yb, zc, h, cw, wa, wb, wc, wo, lg, lb, seq, alpha):
    tm = 512
    t = h.shape[0]
    per = seq // tm
    d = D_MODEL
    row = lambda i: (i, 0)
    const = lambda i: (0, 0)
    prev = lambda c: (lambda i: (jnp.maximum(i * (tm // 16) - 1, 0), c))
    in_specs = [
        pl.BlockSpec((tm, d), row), pl.BlockSpec((tm, d), row),
        pl.BlockSpec((tm, CONV_CH), lambda i: (i, 1)), pl.BlockSpec((tm, CONV_CH), lambda i: (i, 2)),
        pl.BlockSpec((tm, CONV_CH), lambda i: (i, 3)),
        pl.BlockSpec((16, CONV_CH), prev(1)), pl.BlockSpec((16, CONV_CH), prev(3)),
        pl.BlockSpec((tm, d), lambda i: (i, 2)), pl.BlockSpec((tm, d), lambda i: (i, 3)),
        pl.BlockSpec((tm, d), lambda i: (i, 4)),
        pl.BlockSpec((tm, d), row),
        pl.BlockSpec((8, CONV_CH), const),
        pl.BlockSpec((d, d), const), pl.BlockSpec((d, d), const), pl.BlockSpec((CONV_CH, d), const),
        pl.BlockSpec((d, d), const), pl.BlockSpec((1, d), const), pl.BlockSpec((1, d), const),
    ]
    return pl.pallas_call(
        functools.partial(_merge_kernel, tm=tm, per=per, alpha=alpha),
        out_shape=jax.ShapeDtypeStruct((t, d), F32),
        grid=(t // tm,),
        in_specs=in_specs,
        out_specs=pl.BlockSpec((tm, d), row),
        compiler_params=_cparams(("parallel",)),
        name="mixer_merge",
    )(ya, yb, zc, zc, zc, zc, zc, zc, zc, zc, h, cw, wa, wb, wc, wo, lg, lb)


def _xattn_kernel(h_ref, kv_ref, wq_ref, wo_ref, lg_ref, lb_ref, o_ref, ob_ref, *, alpha):
    h = h_ref[...]
    q = _dot(h.astype(BF16), wq_ref[...]).astype(BF16)
    kv = kv_ref[...]
    nh = XATTN_HEADS
    hd = XATTN_HEAD_DIM
    outs = []
    for hh in range(nh):
        s = _dot_t(q[:, hh * hd:(hh + 1) * hd], kv[:, hh * hd:(hh + 1) * hd]) * (hd ** -0.5)
        s = s - jnp.max(s, axis=-1, keepdims=True)
        e = jnp.exp(s)
        p = e / jnp.sum(e, axis=-1, keepdims=True)
        outs.append(_dot(p.astype(BF16), kv[:, (nh + hh) * hd:(nh + hh + 1) * hd]))
    o = jnp.concatenate(outs, axis=-1).astype(BF16)
    y = _layer_norm(alpha * h + _dot(o, wo_ref[...]), lg_ref[...], lb_ref[...])
    o_ref[...] = y
    bits = lax.bitcast_convert_type(y.astype(BF16).astype(F32), I32)
    half = y.shape[1] // 2
    ob_ref[...] = (bits[:, half:] & -65536) | lax.shift_right_logical(bits[:, :half], 16)


def _unpack_pairs(words):
    lo = lax.bitcast_convert_type(lax.shift_left(words, 16), F32).astype(BF16)
    hi = lax.bitcast_convert_type(words & -65536, F32).astype(BF16)
    return lo, hi


def _xattn(h, kv, wq, wo, lg, lb, seq, mlen, alpha):
    tm = 512
    t = h.shape[0]
    per = seq // tm
    d = D_MODEL
    row = lambda i: (i, 0)
    const = lambda i: (0, 0)
    return pl.pallas_call(
        functools.partial(_xattn_kernel, alpha=alpha),
        out_shape=(jax.ShapeDtypeStruct((t, d), F32), jax.ShapeDtypeStruct((t, d // 2), I32)),
        grid=(t // tm,),
        in_specs=[pl.BlockSpec((tm, d), row), pl.BlockSpec((mlen, kv.shape[1]), lambda i: (i // per, 0)),
                  pl.BlockSpec(wq.shape, const), pl.BlockSpec(wo.shape, const),
                  pl.BlockSpec((1, d), const), pl.BlockSpec((1, d), const)],
        out_specs=(pl.BlockSpec((tm, d), row), pl.BlockSpec((tm, d // 2), row)),
        compiler_params=_cparams(("parallel",)),
        name="mem_xattn",
    )(h, kv, wq, wo, lg, lb)


def _expert_kernel(be_ref, nu_ref, x_ref, w13_ref, w2_ref, o_ref, w13b_scr, w2b_scr):
    i = pl.program_id(0)

    @pl.when((i == 0) | (be_ref[i] != be_ref[jnp.maximum(i - 1, 0)]))
    def _():
        w13b_scr[...] = w13_ref[0, 0].astype(BF16)
        w2b_scr[...] = w2_ref[0, 0].astype(BF16)

    @pl.when(i < nu_ref[0])
    def _():
        x_lo, x_hi = _unpack_pairs(x_ref[...])
        half = x_lo.shape[1]
        hmid = _dot(x_lo, w13b_scr[:half, :]) + _dot(x_hi, w13b_scr[half:, :])
        a = hmid[:, :EXPERT_DIM]
        g = hmid[:, EXPERT_DIM:]
        act = (a * jax.nn.sigmoid(a) * g).astype(BF16)
        o_ref[...] = _dot(act, w2b_scr[...]).astype(o_ref.dtype)


def _experts(xg, w13, w2, block_e, n_used, layer):
    n_pad = xg.shape[0]
    d = 2 * xg.shape[1]
    rows = EXP_ROWS
    nb = n_pad // rows
    blk = lambda i, be, nu: (jnp.minimum(i, nu[0] - 1), 0)
    grid_spec = pltpu.PrefetchScalarGridSpec(
        num_scalar_prefetch=2,
        grid=(nb,),
        in_specs=[pl.BlockSpec((rows, d // 2), blk),
                  pl.BlockSpec((1, 1, d, 2 * EXPERT_DIM), lambda i, be, nu: (layer, be[i], 0, 0)),
                  pl.BlockSpec((1, 1, EXPERT_DIM, d), lambda i, be, nu: (layer, be[i], 0, 0))],
        out_specs=pl.BlockSpec((rows, d), blk),
        scratch_shapes=[pltpu.VMEM((d, 2 * EXPERT_DIM), BF16), pltpu.VMEM((EXPERT_DIM, d), BF16)],
    )
    return pl.pallas_call(
        _expert_kernel,
        out_shape=jax.ShapeDtypeStruct((n_pad, d), BF16),
        grid_spec=grid_spec,
        compiler_params=_cparams(("arbitrary",)),
        name="moe_experts",
    )(block_e, n_used, xg, w13, w2)


def _moe_out_kernel(h_ref, y8_ref, wk_ref, w13_ref, w2_ref, lg_ref, lb_ref, o_ref, ob_ref, *, alpha):
    h = h_ref[...]
    hmid = _dot(h.astype(BF16), w13_ref[...])
    a = hmid[:, :EXPERT_DIM]
    g = hmid[:, EXPERT_DIM:]
    acc = alpha * h + _dot((a * jax.nn.sigmoid(a) * g).astype(BF16), w2_ref[...])
    wk = wk_ref[...]
    for k in range(TOP_K):
        acc = acc + wk[:, k:k + 1] * y8_ref[k].astype(F32)
    y = _layer_norm(acc, lg_ref[...], lb_ref[...])
    o_ref[...] = y
    ob_ref[...] = y.astype(BF16)


def _moe_out(h, y8, wk, w13, w2, lg, lb, alpha):
    tm = 256
    t, d = h.shape
    row = lambda i: (i, 0)
    const = lambda i: (0, 0)
    return pl.pallas_call(
        functools.partial(_moe_out_kernel, alpha=alpha),
        out_shape=(jax.ShapeDtypeStruct((t, d), F32), jax.ShapeDtypeStruct((t, d), BF16)),
        grid=(t // tm,),
        in_specs=[pl.BlockSpec((tm, d), row), pl.BlockSpec((TOP_K, tm, d), lambda i: (0, i, 0)),
                  pl.BlockSpec((tm, TOP_K), row),
                  pl.BlockSpec(w13.shape, const), pl.BlockSpec(w2.shape, const),
                  pl.BlockSpec((1, d), const), pl.BlockSpec((1, d), const)],
        out_specs=(pl.BlockSpec((tm, d), row), pl.BlockSpec((tm, d), row)),
        compiler_params=_cparams(("parallel",)),
        name="moe_shared_ln",
    )(h, y8, wk, w13, w2, lg, lb)


def _dest_kernel(idx_ref, rank_ref, ps_ref, o_ref):
    idx = idx_ref[...]
    ps = ps_ref[...]
    tm = idx.shape[1]
    eidx = lax.broadcasted_iota(I32, (N_EXPERTS, tm), 0)
    rows = [jnp.sum(jnp.where(eidx == idx[k:k + 1, :], ps, 0.0), axis=0, keepdims=True) for k in range(TOP_K)]
    o_ref[...] = jnp.concatenate(rows, axis=0).astype(I32) + rank_ref[...]


def _dest(idx, rank, pstarts):
    tm = 1024
    t = idx.shape[1]
    tm = min(tm, t)
    col = lambda i: (0, i)
    return pl.pallas_call(
        _dest_kernel,
        out_shape=jax.ShapeDtypeStruct((TOP_K, t), I32),
        grid=(t // tm,),
        in_specs=[pl.BlockSpec((TOP_K, tm), col), pl.BlockSpec((TOP_K, tm), col),
                  pl.BlockSpec((N_EXPERTS, 1), lambda i: (0, 0))],
        out_specs=pl.BlockSpec((TOP_K, tm), col),
        compiler_params=_cparams(("parallel",)),
        name="moe_dest",
    )(idx, rank, pstarts)


def _router_kernel(h_ref, rwt_ref, rb_ref, tri_ref, idx_ref, w_ref, rank_ref, cnt_ref, carry_scr, *, tm):
    i = pl.program_id(0)

    @pl.when(i == 0)
    def _():
        carry_scr[...] = jnp.zeros(carry_scr.shape, F32)

    ne = N_EXPERTS
    gsz = ne // N_GROUPS
    s = jax.nn.sigmoid(_dot_t(rwt_ref[...], h_ref[...].astype(BF16)))
    sb = s + rb_ref[...]
    sb3 = sb.reshape(N_GROUPS, gsz, tm)
    li = lax.broadcasted_iota(I32, (N_GROUPS, gsz, tm), 1)
    m1 = jnp.max(sb3, axis=1, keepdims=True)
    first = jnp.min(jnp.where(sb3 == m1, li, gsz), axis=1, keepdims=True)
    m2 = jnp.max(jnp.where(li == first, -jnp.inf, sb3), axis=1, keepdims=True)
    gs = (m1 + m2).reshape(N_GROUPS, tm)
    gi = lax.broadcasted_iota(I32, (N_GROUPS, tm), 0)
    grank = jnp.zeros((N_GROUPS, tm), I32)
    for gp in range(N_GROUPS):
        row = gs[gp:gp + 1, :]
        grank = grank + ((row > gs) | ((row == gs) & (gp < gi))).astype(I32)
    gkeep = (grank < TOPK_GROUPS).astype(F32)
    ekeep = jnp.broadcast_to(gkeep[:, None, :], (N_GROUPS, gsz, tm)).reshape(ne, tm)
    cand = jnp.where(ekeep > 0.0, sb, NEG)
    eidx = lax.broadcasted_iota(I32, (ne, tm), 0)
    sel = jnp.zeros((ne, tm), F32)
    idxs, wts = [], []
    for _ in range(TOP_K):
        mx = jnp.max(cand, axis=0, keepdims=True)
        ik = jnp.min(jnp.where(cand == mx, eidx, ne), axis=0, keepdims=True)
        hit = eidx == ik
        wts.append(jnp.sum(jnp.where(hit, s, 0.0), axis=0, keepdims=True))
        idxs.append(ik)
        sel = jnp.where(hit, 1.0, sel)
        cand = jnp.where(hit, -jnp.inf, cand)
    before = _dot(sel.astype(BF16), tri_ref[...]) + carry_scr[...]
    ranks = [jnp.sum(jnp.where(eidx == ik, before, 0.0), axis=0, keepdims=True) for ik in idxs]
    carry_scr[...] = carry_scr[...] + jnp.sum(sel, axis=1, keepdims=True)
    w = jnp.concatenate(wts, axis=0)
    idx_ref[...] = jnp.concatenate(idxs, axis=0)
    w_ref[...] = w / jnp.sum(w, axis=0, keepdims=True) * ROUTE_SCALE
    rank_ref[...] = jnp.concatenate(ranks, axis=0).astype(I32)
    cnt_ref[...] = jnp.broadcast_to(carry_scr[...], cnt_ref.shape)


def _router(h, rwt, rb):
    tm = 256
    t, d = h.shape
    tri = jnp.asarray(np.triu(np.ones((tm, tm), np.float32), 1)).astype(BF16)
    const = lambda i: (0, 0)
    col = lambda i: (0, i)
    return pl.pallas_call(
        functools.partial(_router_kernel, tm=tm),
        out_shape=(jax.ShapeDtypeStruct((TOP_K, t), I32), jax.ShapeDtypeStruct((TOP_K, t), F32),
                   jax.ShapeDtypeStruct((TOP_K, t), I32), jax.ShapeDtypeStruct((N_EXPERTS, SLOT), F32)),
        grid=(t // tm,),
        in_specs=[pl.BlockSpec((tm, d), lambda i: (i, 0)), pl.BlockSpec((N_EXPERTS, d), const),
                  pl.BlockSpec((N_EXPERTS, 1), const), pl.BlockSpec((tm, tm), const)],
        out_specs=(pl.BlockSpec((TOP_K, tm), col), pl.BlockSpec((TOP_K, tm), col),
                   pl.BlockSpec((TOP_K, tm), col), pl.BlockSpec((N_EXPERTS, SLOT), const)),
        scratch_shapes=[pltpu.VMEM((N_EXPERTS, 1), F32)],
        compiler_params=_cparams(("arbitrary",)),
        name="moe_router",
    )(h, rwt, rb, tri)


def _dispatch(hp, dest_flat, n_pad):
    t, w = hp.shape
    per_w = t // (SC_CORES * SC_SUBCORES)
    assert per_w * SC_CORES * SC_SUBCORES == t and per_w % SC_CHUNK == 0 and dest_flat.shape[0] == TOP_K * t
    mesh = plsc.VectorSubcoreMesh(core_axis_name="c", subcore_axis_name="s")

    @functools.partial(
        pl.kernel, out_type=jax.ShapeDtypeStruct((n_pad, w), I32), mesh=mesh,
        scratch_types=[pltpu.VMEM((SC_CHUNK,), I32)] * TOP_K
        + [pltpu.VMEM((SC_CHUNK, w), I32), pltpu.SemaphoreType.DMA],
        name="moe_dispatch")
    def scatter_rows(hp_hbm, dest_hbm, xg_hbm, *scratch):
        idx_vs, rows_v, sem = scratch[:TOP_K], scratch[TOP_K], scratch[TOP_K + 1]
        wid = lax.axis_index("s") * SC_CORES + lax.axis_index("c")
        base = wid * per_w

        @pl.loop(0, per_w // SC_CHUNK)
        def _(j):
            t0 = pl.multiple_of(base + j * SC_CHUNK, SC_CHUNK)
            pltpu.sync_copy(hp_hbm.at[pl.ds(t0, SC_CHUNK)], rows_v)
            for k in range(TOP_K):
                pltpu.sync_copy(dest_hbm.at[pl.ds(pl.multiple_of(k * t + t0, SC_CHUNK), SC_CHUNK)], idx_vs[k])
            copies = [pltpu.async_copy(rows_v, xg_hbm.at[idx_vs[k]], sem) for k in range(TOP_K)]
            for cp in copies:
                cp.wait()

    return scatter_rows(hp, dest_flat)


def _moe(h, hp, router_w, router_b, exp_w13, exp_w2, layer, shared_w13, shared_w2, lg, lb, alpha):
    t, d = h.shape
    idx, wts, rank, cnt = _router(h, router_w.T, router_b.astype(F32).reshape(N_EXPERTS, 1))
    rows = EXP_ROWS
    n_a = t * TOP_K
    counts = cnt[:, 0].astype(I32)
    pcounts = (counts + rows - 1) // rows * rows
    pends = jnp.cumsum(pcounts)
    pstarts = pends - pcounts
    dest = _dest(idx, rank, pstarts.astype(F32).reshape(N_EXPERTS, 1))
    n_blocks = -(-n_a // rows) + N_EXPERTS
    n_pad = n_blocks * rows
    n_used = (pends[-1:] // rows).astype(I32)
    first_row = jnp.minimum(jnp.arange(n_blocks, dtype=I32), n_used - 1) * rows
    block_e = jnp.sum((pends[None, :] <= first_row[:, None]).astype(I32), axis=1)
    xg = _dispatch(hp, dest.reshape(-1), n_pad)
    out = _experts(xg, exp_w13, exp_w2, block_e, n_used, layer)
    y8 = jnp.take(out, dest, axis=0, mode="clip")
    return _moe_out(h, y8, wts.T, shared_w13, shared_w2, lg, lb, alpha)


def _pad_slots(w, scale=1.0):
    dm = w.shape[0]
    nh = w.shape[1] // HEAD_DIM
    w = (w * scale).reshape(dm, nh, HEAD_DIM)
    return jnp.concatenate([w, jnp.zeros_like(w)], axis=-1).reshape(dm, nh * SLOT)


def _pad_rows(w):
    n = w.shape[1]
    nh = w.shape[0] // HEAD_DIM
    w = w.reshape(nh, HEAD_DIM, n)
    return jnp.concatenate([w, jnp.zeros_like(w)], axis=1).reshape(nh * SLOT, n)


def _alibi(n):
    return np.exp2(-8.0 * np.arange(1, n + 1, dtype=np.float64) / n).astype(np.float32)


def _key_features(seq):
    p = np.arange(seq)
    slc = np.zeros((seq, SLOT), np.float32)
    slc[:, HEAD_DIM] = p % SLC_BLOCK
    blk = p // SLC_BLOCK
    nz = blk > 0
    slc[p[nz], HEAD_DIM + blk[nz]] = 1.0
    win = np.zeros((seq, SLOT), np.float32)
    win[:, HEAD_DIM] = p // 64
    win[:, HEAD_DIM + 1] = p % 64
    mob = np.zeros((seq, SLOT), np.float32)
    mob[p, HEAD_DIM + p // MOBA_BLOCK] = 1.0
    mob[:, HEAD_DIM + 16] = p % MOBA_BLOCK
    return np.concatenate([slc] * 2 + [win] * 2 + [mob] * MOBA_HEADS, axis=1)


def _query_features():
    sl = _alibi(NSA_HEADS)
    nsa = np.zeros((NSA_KV_HEADS, SLOT, 8), np.float32)
    for g in range(NSA_KV_HEADS):
        for r in range(NSA_GROUP):
            s = sl[g * NSA_GROUP + r]
            nsa[g, HEAD_DIM, r] = s
            nsa[g, HEAD_DIM + 1:, r] = s * SLC_BLOCK * np.arange(1, 64)
            nsa[g, HEAD_DIM, NSA_GROUP + r] = s * 64
            nsa[g, HEAD_DIM + 1, NSA_GROUP + r] = s
    sm = _alibi(MOBA_HEADS)
    mob = np.zeros((MOBA_HEADS, SLOT, 8), np.float32)
    for h in range(MOBA_HEADS):
        mob[h, HEAD_DIM:HEAD_DIM + 16, 0] = sm[h] * MOBA_BLOCK * np.arange(16)
        mob[h, HEAD_DIM + 16, 0] = sm[h]
    return nsa, mob


def _mask_biases():
    t = np.arange(TQ)[None, :]
    p2 = np.arange(2 * TQ)[:, None]
    cb = np.stack([p2 <= t, p2 <= TQ + t])
    p3 = np.arange(3 * TQ)[:, None]
    dist = np.stack([c * TQ + t - p3 for c in range(3)])
    wb = (dist >= 0) & (dist < WINDOW)
    to_bias = lambda m: np.where(m, 0.0, NEG).astype(np.float32)
    return to_bias(cb), to_bias(wb)


def _overlap_t(seq):
    nc = seq // CMP_STRIDE
    c_start = np.arange(nc) * CMP_STRIDE
    b_start = np.arange(64) * SLC_BLOCK
    ov = ((c_start[None, :] < (b_start + SLC_BLOCK)[:, None])
          & ((c_start + CMP_BLOCK)[None, :] > b_start[:, None])
          & (np.arange(nc) < nc - 1)[None, :] & (b_start < seq)[:, None])
    return ov.astype(np.float32)


def _compress_weights(pe, w1, w2):
    ty = np.array([0, 0, 1, 1])
    eye = jnp.eye(4, dtype=F32)
    w1r = w1.reshape(2, CMP_BLOCK, HEAD_DIM, CMP_HIDDEN)[ty]
    top = jnp.einsum('spdj,sS->psdSj', w1r[:, :CMP_STRIDE], eye).reshape(CMP_STRIDE * 256, 4 * CMP_HIDDEN)
    bot = jnp.einsum('spdj,sS->psdSj', w1r[:, CMP_STRIDE:], eye).reshape(CMP_STRIDE * 256, 4 * CMP_HIDDEN)
    per = pe[ty]
    pet = jnp.transpose(per[:, :CMP_STRIDE], (1, 0, 2)).reshape(1, CMP_STRIDE * 256)
    peb = jnp.transpose(per[:, CMP_STRIDE:], (1, 0, 2)).reshape(1, CMP_STRIDE * 256)
    w2p = jnp.concatenate([w2[ty], jnp.zeros((4, CMP_HIDDEN, SLOT - HEAD_DIM), F32)], axis=-1)
    w2b = jnp.einsum('sjd,sS->sjSd', w2p, eye).reshape(4 * CMP_HIDDEN, 4 * SLOT)
    w2k = w2b[:, :2 * SLOT].astype(BF16)
    w2vt = w2b[:, 2 * SLOT:].T.astype(BF16)
    return pet, peb, top.astype(BF16), bot.astype(BF16), w2k, w2vt


def kernel(x, mem, w_in, cmp_pe, cmp_w1, cmp_w2, conv_w, w_branch, w_out, ln1_g, ln1_b,
           xattn_wq, xattn_wkv, xattn_wo, ln2_g, ln2_b, router_w, router_b, exp_w13, exp_w2,
           shared_w13, shared_w2, ln3_g, ln3_b):
    bsz, seq, d = x.shape
    mlen = mem.shape[1]
    depth = w_in.shape[0]
    alpha = (2.0 * depth) ** 0.25
    t = bsz * seq
    scale = HEAD_DIM ** -0.5

    kfeat = jnp.asarray(_key_features(seq))
    fq_nsa, fq_moba = (jnp.asarray(a) for a in _query_features())
    cbias, wbias = (jnp.asarray(a) for a in _mask_biases())
    ovt = jnp.asarray(_overlap_t(seq)).astype(BF16)
    memf = mem.reshape(bsz * mlen, d)

    h = x.reshape(t, d)
    hb = h.astype(BF16)
    for l in range(depth):
        ht = hb.T
        wi = w_in[l]
        kv6 = wi[:, _OFF_NSA_KV:_OFF_NSA_G].reshape(d, 6, NSA_KV_HEADS * HEAD_DIM)
        mq, mk, mv = (wi[:, _OFF_MOBA + j * 512:_OFF_MOBA + (j + 1) * 512] for j in range(3))
        w_at = jnp.concatenate([wi[:, :512] * scale, kv6[:, 3], kv6[:, 5], mq * scale, mv],
                               axis=1).T.astype(BF16)
        w_b = jnp.concatenate([_pad_slots(kv6[:, 2]), _pad_slots(kv6[:, 4]), _pad_slots(mk)], axis=1).astype(BF16)
        wg = wi[:, _OFF_NSA_G:_OFF_MOBA].reshape(d, NSA_KV_HEADS, 12)
        wgt = jnp.concatenate([wg, jnp.zeros((d, NSA_KV_HEADS, SLOT - 12), F32)], axis=-1)
        wgt = wgt.reshape(d, NSA_KV_HEADS * SLOT).T.astype(BF16)
        w_c = jnp.concatenate([wi[:, _OFF_NSA_KV:_OFF_NSA_KV + 256], jnp.zeros((d, 256), F32),
                               wi[:, _OFF_CONV:]], axis=1).astype(BF16)

        value_slots = tuple(range(_SLOT_SLC_V, _SLOT_MOBA_Q)) + tuple(range(_SLOT_MOBA_V, _N_SLOTS_T))
        zat = _proj_t(w_at, ht, value_slots)
        gt = _mm(wgt, ht, F32, tm=256, tn=1024, name="proj_gate")
        zb = _mm(hb, w_b, BF16, tn=768, feats=kfeat, name="proj_b")
        zc = _mm(hb, w_c, BF16, tn=1024, name="proj_c")

        pet, peb, wt, wb, w2k, w2vt = _compress_weights(cmp_pe[l], cmp_w1[l], cmp_w2[l])
        sub = zc[:, :256].reshape(bsz, seq // CMP_STRIDE, CMP_STRIDE * 256)
        kc, vct = _compress(sub, pet, peb, wt, wb, w2k, w2vt)

        ya = _nsa_attention(zat, zb, gt, kc, vct, ovt, fq_nsa, cbias, wbias, bsz, seq)
        yb = _moba_attention(zat, zb, fq_moba, cbias, bsz, seq)

        cw = jnp.concatenate([conv_w[l], jnp.zeros((5, CONV_CH), F32)], axis=0)
        h = _merge(ya, yb, zc, h, cw, _pad_rows(w_branch[l, 0]).astype(BF16),
                   _pad_rows(w_branch[l, 1]).astype(BF16), w_branch[l, 2].astype(BF16),
                   w_out[l].astype(BF16), ln1_g[l][None], ln1_b[l][None], seq, alpha)

        kv = _mm(memf, xattn_wkv[l].astype(BF16), BF16, tm=512, name="xattn_kv")
        h, hp = _xattn(h, kv, xattn_wq[l].astype(BF16), xattn_wo[l].astype(BF16),
                       ln2_g[l][None], ln2_b[l][None], seq, mlen, alpha)

        h, hb = _moe(h, hp, router_w[l].astype(BF16), router_b[l], exp_w13, exp_w2, l,
                     shared_w13[l].astype(BF16), shared_w2[l].astype(BF16), ln3_g[l][None], ln3_b[l][None], alpha)
    return h.reshape(bsz, seq, d)
```

```python
import functools

import jax
import jax.numpy as jnp
import numpy as np
from jax import lax
from jax.experimental import pallas as pl
from jax.experimental.pallas import tpu as pltpu
from jax.experimental.pallas import tpu_sc as plsc

F32 = jnp.float32
BF16 = jnp.bfloat16
I32 = jnp.int32

D_MODEL = 1024
HEAD_DIM = 64
SLOT = 128
NEG = -1e30
LN_EPS = 1e-5

NSA_HEADS = 8
NSA_KV_HEADS = 2
NSA_GROUP = 4
CMP_BLOCK = 32
CMP_STRIDE = 16
CMP_HIDDEN = 256
SLC_BLOCK = 64
SLC_TOPK = 16
WINDOW = 512
MOBA_HEADS = 8
MOBA_BLOCK = 256
MOBA_TOPK = 3
CONV_CH = 512
XATTN_HEADS = 4
XATTN_HEAD_DIM = 128
N_EXPERTS = 256
TOP_K = 8
N_GROUPS = 8
TOPK_GROUPS = 4
EXPERT_DIM = 256
ROUTE_SCALE = 2.5

TQ = 256
MOBA_HPS = 4
EXP_ROWS = 512
VMEM_LIMIT = 48 * 1024 * 1024
SC_CORES = 2
SC_SUBCORES = 16
SC_CHUNK = 64

_OFF_NSA_Q = 0
_OFF_NSA_KV = 512
_OFF_NSA_G = 1280
_OFF_MOBA = 1304
_OFF_CONV = 2840
_OFF_MERGE = 4376

_SLOT_NSA_Q = 0
_SLOT_SLC_V = 8
_SLOT_WIN_V = 10
_SLOT_MOBA_Q = 12
_SLOT_MOBA_V = 20
_N_SLOTS_T = 28


def _cparams(sem):
    return pltpu.CompilerParams(dimension_semantics=sem, vmem_limit_bytes=VMEM_LIMIT)


def _dot(a, b):
    return jnp.dot(a, b, preferred_element_type=F32)


def _dot_t(a, b):
    return lax.dot_general(a, b, (((1,), (1,)), ((), ())), preferred_element_type=F32)


def _layer_norm(x, g, b):
    mu = jnp.mean(x, axis=-1, keepdims=True)
    xc = x - mu
    var = jnp.mean(xc * xc, axis=-1, keepdims=True)
    return xc * lax.rsqrt(var + LN_EPS) * g + b


def _mm_kernel(x_ref, w_ref, o_ref):
    o_ref[...] = _dot(x_ref[...].astype(BF16), w_ref[...]).astype(o_ref.dtype)


def _mm_feat_kernel(x_ref, w_ref, f_ref, o_ref):
    y = _dot(x_ref[...].astype(BF16), w_ref[...]) + f_ref[...]
    o_ref[...] = y.astype(o_ref.dtype)


def _mm(x, w, out_dtype, *, tm=1024, tn=512, feats=None, name):
    m, k = x.shape
    n = w.shape[1]
    tm = min(tm, m)
    tn = min(tn, n)
    assert m % tm == 0 and n % tn == 0, (m, n, tm, tn)
    in_specs = [pl.BlockSpec((tm, k), lambda i, j: (i, 0)),
                pl.BlockSpec((k, tn), lambda i, j: (0, j))]
    args = [x, w]
    if feats is None:
        body = _mm_kernel
    else:
        per = feats.shape[0] // tm
        assert feats.shape[0] % tm == 0
        in_specs.append(pl.BlockSpec((tm, tn), lambda i, j: (i % per, j)))
        args.append(feats)
        body = _mm_feat_kernel
    return pl.pallas_call(
        body,
        out_shape=jax.ShapeDtypeStruct((m, n), out_dtype),
        grid=(m // tm, n // tn),
        in_specs=in_specs,
        out_specs=pl.BlockSpec((tm, tn), lambda i, j: (i, j)),
        compiler_params=_cparams(("parallel", "parallel")),
        name=name,
    )(*args)


def _proj_t_kernel(w_ref, ht_ref, o_ref, *, value_slots):
    acc = _dot(w_ref[...], ht_ref[...])
    tn = acc.shape[1]
    zero = jnp.zeros((SLOT - HEAD_DIM, tn), o_ref.dtype)
    ones_row = (lax.broadcasted_iota(I32, (SLOT - HEAD_DIM, tn), 0) == 0).astype(o_ref.dtype)
    for s in range(w_ref.shape[0] // HEAD_DIM):
        o_ref[s * SLOT:s * SLOT + HEAD_DIM, :] = acc[s * HEAD_DIM:(s + 1) * HEAD_DIM, :].astype(o_ref.dtype)
        o_ref[s * SLOT + HEAD_DIM:(s + 1) * SLOT, :] = ones_row if s in value_slots else zero


def _proj_t(w, ht, value_slots):
    m, k = w.shape
    t = ht.shape[1]
    tn = min(512, t)
    n_out = m // HEAD_DIM * SLOT
    return pl.pallas_call(
        functools.partial(_proj_t_kernel, value_slots=value_slots),
        out_shape=jax.ShapeDtypeStruct((n_out, t), BF16),
        grid=(t // tn,),
        in_specs=[pl.BlockSpec((m, k), lambda j: (0, 0)), pl.BlockSpec((k, tn), lambda j: (0, j))],
        out_specs=pl.BlockSpec((n_out, tn), lambda j: (0, j)),
        compiler_params=_cparams(("parallel",)),
        name="proj_at",
    )(w, ht)


def _cmp_kernel(sub_ref, pet_ref, peb_ref, wt_ref, wb_ref, w2k_ref, w2vt_ref, kc_ref, vct_ref):
    sub = sub_ref[0].astype(F32)
    nc = sub.shape[0]
    a = _dot((sub + pet_ref[...]).astype(BF16), wt_ref[...])
    b = _dot((sub + peb_ref[...]).astype(BF16), wb_ref[...])
    hid = jax.nn.gelu(a + pltpu.roll(b, nc - 1, 0)).astype(BF16)
    kc_ref[0] = _dot(hid, w2k_ref[...]).astype(kc_ref.dtype)
    vct_ref[0] = _dot_t(w2vt_ref[...], hid).astype(vct_ref.dtype)


def _compress(sub, pet, peb, wt, wb, w2k, w2vt):
    bsz, nc, kk = sub.shape
    n_h = wt.shape[1]
    n_o = w2k.shape[1]
    const = lambda b: (0, 0)
    return pl.pallas_call(
        _cmp_kernel,
        out_shape=(jax.ShapeDtypeStruct((bsz, nc, n_o), BF16), jax.ShapeDtypeStruct((bsz, n_o, nc), BF16)),
        grid=(bsz,),
        in_specs=[pl.BlockSpec((1, nc, kk), lambda b: (b, 0, 0)),
                  pl.BlockSpec((1, kk), const), pl.BlockSpec((1, kk), const),
                  pl.BlockSpec((kk, n_h), const), pl.BlockSpec((kk, n_h), const),
                  pl.BlockSpec((n_h, n_o), const), pl.BlockSpec((n_o, n_h), const)],
        out_specs=(pl.BlockSpec((1, nc, n_o), lambda b: (b, 0, 0)), pl.BlockSpec((1, n_o, nc), lambda b: (b, 0, 0))),
        compiler_params=_cparams(("parallel",)),
        name="nsa_compress",
    )(sub, pet, peb, wt, wb, w2k, w2vt)


def _flash_reset(m_scr, acc_scr):
    m_scr[...] = jnp.full(m_scr.shape, NEG, F32)
    acc_scr[...] = jnp.zeros(acc_scr.shape, F32)


def _qk(kget, qxs, start, size):
    k0 = kget(0, start, size)
    if kget(1, start, size) is None:
        return _dot(k0, jnp.concatenate(qxs, axis=1))
    return jnp.concatenate([_dot(k0, qxs[0])] + [_dot(kget(c, start, size), qxs[c])
                                                 for c in range(1, len(qxs))], axis=1)


def _softmax_pv(s, vget, nq, start, size, m_scr, acc_scr, bias=None):
    if bias is not None:
        s = s + jnp.concatenate([bias] * nq, axis=1)
    m_prev = m_scr[...]
    m_new = jnp.maximum(m_prev, jnp.max(s, axis=0, keepdims=True))
    alpha = jnp.exp(m_prev - m_new)
    p = jnp.exp(s - m_new).astype(BF16)
    v0 = vget(0, start, size)
    if vget(1, start, size) is None:
        pv = _dot(v0, p)
    else:
        pv = jnp.concatenate([_dot(v0, p[:, :TQ])] + [_dot(vget(c, start, size), p[:, c * TQ:(c + 1) * TQ])
                                                      for c in range(1, nq)], axis=1)
    acc_scr[...] = alpha * acc_scr[...] + pv
    m_scr[...] = m_new


def _rows(ref, start, size, lane0=0):
    return ref[pl.ds(pl.multiple_of(start, TQ), size), lane0:lane0 + SLOT]


def _cols(ref, start, size, row0=0):
    return ref[row0:row0 + SLOT, pl.ds(pl.multiple_of(start, TQ), size)]


def _flash_causal(kget, vget, qxs, i, cb_ref, sa_scr, sb_scr, m_scr, acc_scr):
    tq = TQ
    tk = 2 * tq
    nq = len(qxs)
    n_steps = i // 2 + 1
    sa_scr[...] = _qk(kget, qxs, 0, tk)

    def body(jj, carry):
        j = 2 * jj
        s = sa_scr[...]
        sb_scr[...] = _qk(kget, qxs, (j + 1) * tk, tk)
        _softmax_pv(s, vget, nq, j * tk, tk, m_scr, acc_scr)
        s = sb_scr[...]
        sa_scr[...] = _qk(kget, qxs, (j + 2) * tk, tk)
        _softmax_pv(s, vget, nq, (j + 1) * tk, tk, m_scr, acc_scr)
        return carry

    lax.fori_loop(0, (n_steps - 1) // 2, body, 0)
    last = n_steps - 1
    bias = cb_ref[i % 2]

    @pl.when(last % 2 == 0)
    def _():
        _softmax_pv(sa_scr[...], vget, nq, last * tk, tk, m_scr, acc_scr, bias=bias)

    @pl.when(last % 2 == 1)
    def _():
        s = sa_scr[...]
        sb_scr[...] = _qk(kget, qxs, last * tk, tk)
        _softmax_pv(s, vget, nq, (last - 1) * tk, tk, m_scr, acc_scr)
        _softmax_pv(sb_scr[...], vget, nq, last * tk, tk, m_scr, acc_scr, bias=bias)


def _nsa_kernel(q_ref, kc_ref, vct_ref, ks_ref, vst_ref, kw_ref, vwt_ref, g_ref, ovt_ref, fq_ref, cb_ref, wb_ref,
                o_ref, m_scr, acc_scr, imp_scr, sa_scr, sb_scr, *, tq, nsel):
    i = pl.program_id(2)
    r4 = NSA_GROUP
    rr = r4 * tq
    q4 = [q_ref[r * SLOT:(r + 1) * SLOT, :] for r in range(r4)]
    fq = fq_ref[0]

    kc = kc_ref[0]
    nc = kc.shape[0]
    qs = jnp.concatenate(q4, axis=1)
    s = _dot(kc, qs)
    t_row = i * tq + (lax.broadcasted_iota(I32, (1, rr), 1) & (tq - 1))
    last_c = (t_row - (CMP_BLOCK - 1)) >> 4
    cmask = lax.broadcasted_iota(I32, (nc, rr), 0) <= last_c
    s = jnp.where(cmask, s, NEG)
    mx = jnp.max(s, axis=0, keepdims=True)
    e = jnp.where(cmask, jnp.exp(s - mx), 0.0)
    lsum = jnp.sum(e, axis=0, keepdims=True)
    p_cmp = (e * (1.0 / jnp.where(lsum > 0.0, lsum, 1.0))).astype(BF16)
    o_cmp = _dot(vct_ref[0], p_cmp)

    ovt = ovt_ref[...]
    imp = _dot(ovt, p_cmp[:, 0:tq])
    for r in range(1, r4):
        imp = imp + _dot(ovt, p_cmp[:, r * tq:(r + 1) * tq])
    nbp = imp.shape[0]
    jidx = lax.broadcasted_iota(I32, (nbp, tq), 0)
    cur = (i * tq + lax.broadcasted_iota(I32, (nbp, tq), 1)) // SLC_BLOCK
    forced = (jidx == 0) | (jidx == cur) | (jidx == cur - 1)
    imp = jnp.where(forced, 1e6, jnp.where(jidx > cur, -1e6, imp))
    imp_scr[...] = imp

    def rank_body(jp, rank):
        row = imp_scr[pl.ds(jp, 1), :]
        better = (row > imp) | ((row == imp) & (jp < jidx))
        return rank + better.astype(I32)

    n_live = (i + 1) * (tq // SLC_BLOCK)
    rank = lax.fori_loop(0, n_live, rank_body, jnp.zeros((nbp, tq), I32))
    sel = (rank < nsel) & (jidx <= cur)
    mask_t = jnp.where(sel, 0.0, NEG)
    feat = jnp.concatenate([jnp.zeros((SLOT - nbp, tq), F32), mask_t], axis=0)

    qxs = [q4[r] + (feat + fq[:, r:r + 1]).astype(BF16) for r in range(r4)]
    kget = lambda c, start, size: _rows(ks_ref, start, size) if c == 0 else None
    vget = lambda c, start, size: _cols(vst_ref, start, size) if c == 0 else None
    _flash_reset(m_scr, acc_scr)
    _flash_causal(kget, vget, qxs, i, cb_ref, sa_scr, sb_scr, m_scr, acc_scr)
    o_slc = acc_scr[...] * (1.0 / acc_scr[HEAD_DIM:HEAD_DIM + 1, :])

    qx = jnp.concatenate([q4[r] + fq[:, r4 + r:r4 + r + 1].astype(BF16) for r in range(r4)], axis=1)
    w0 = jnp.maximum(i - 2, 0) * tq
    s = _dot(_rows(kw_ref, w0, 3 * tq), qx)
    s = s + jnp.concatenate([wb_ref[jnp.minimum(i, 2)]] * r4, axis=1)
    e = jnp.exp(s - jnp.max(s, axis=0, keepdims=True)).astype(BF16)
    o_win = _dot(_cols(vwt_ref, w0, 3 * tq), e)
    o_win = o_win * (1.0 / o_win[HEAD_DIM:HEAD_DIM + 1, :])

    gate = jax.nn.sigmoid(g_ref[...])
    for r in range(r4):
        sl = slice(r * tq, (r + 1) * tq)
        o = (gate[3 * r:3 * r + 1, :] * o_cmp[:, sl] + gate[3 * r + 1:3 * r + 2, :] * o_slc[:, sl]
             + gate[3 * r + 2:3 * r + 3, :] * o_win[:, sl])
        o_ref[:, r * SLOT:(r + 1) * SLOT] = o.T.astype(o_ref.dtype)


def _nsa_attention(zat, zb, gt, kc, vct, ovt, fq, cb, wb, bsz, seq):
    tq = TQ
    ni = seq // tq
    nc = kc.shape[1]
    nsel = min(SLC_TOPK, seq // SLC_BLOCK)
    assert seq // SLC_BLOCK <= 64 and WINDOW == 2 * tq and ni % 2 == 0 and ni >= 3
    rr = NSA_GROUP * tq
    gw = NSA_GROUP * SLOT
    in_specs = [
        pl.BlockSpec((gw, tq), lambda b, g, i: (g, b * ni + i)),
        pl.BlockSpec((1, nc, SLOT), lambda b, g, i: (b, 0, g)),
        pl.BlockSpec((1, SLOT, nc), lambda b, g, i: (b, g, 0)),
        pl.BlockSpec((seq, SLOT), lambda b, g, i: (b, g)),
        pl.BlockSpec((SLOT, seq), lambda b, g, i: (_SLOT_SLC_V + g, b)),
        pl.BlockSpec((seq, SLOT), lambda b, g, i: (b, 2 + g)),
        pl.BlockSpec((SLOT, seq), lambda b, g, i: (_SLOT_WIN_V + g, b)),
        pl.BlockSpec((SLOT, tq), lambda b, g, i: (g, b * ni + i)),
        pl.BlockSpec(ovt.shape, lambda b, g, i: (0, 0)),
        pl.BlockSpec((1, SLOT, 8), lambda b, g, i: (g, 0, 0)),
        pl.BlockSpec(cb.shape, lambda b, g, i: (0, 0, 0)),
        pl.BlockSpec(wb.shape, lambda b, g, i: (0, 0, 0)),
    ]
    return pl.pallas_call(
        functools.partial(_nsa_kernel, tq=tq, nsel=nsel),
        out_shape=jax.ShapeDtypeStruct((bsz * seq, NSA_HEADS * SLOT), BF16),
        grid=(bsz, NSA_KV_HEADS, ni),
        in_specs=in_specs,
        out_specs=pl.BlockSpec((tq, gw), lambda b, g, i: (b * ni + i, g)),
        scratch_shapes=[pltpu.VMEM((1, rr), F32),
                        pltpu.VMEM((SLOT, rr), F32), pltpu.VMEM((64, tq), F32),
                        pltpu.VMEM((2 * tq, rr), F32), pltpu.VMEM((2 * tq, rr), F32)],
        compiler_params=_cparams(("parallel", "parallel", "arbitrary")),
        name="nsa_attention",
    )(zat, kc, vct, zb, zat, zb, zat, gt, ovt, fq, cb, wb)


def _moba_kernel(q_ref, k_ref, vt_ref, fq_ref, cb_ref, o_ref, m_scr, acc_scr, km_scr, gate_scr,
                 sa_scr, sb_scr, *, tq, nblk, ntop):
    i = pl.program_id(2)
    nh = MOBA_HPS

    @pl.when(i == 0)
    def _():
        for hh in range(nh):
            kh = k_ref[:, hh * SLOT:(hh + 1) * SLOT].astype(F32)
            km = jnp.mean(kh.reshape(nblk, tq, SLOT), axis=1)
            if nblk < 16:
                km = jnp.concatenate([km, jnp.zeros((16 - nblk, SLOT), F32)], axis=0)
            km_scr[hh] = km

    qs = [q_ref[hh * SLOT:(hh + 1) * SLOT, :] for hh in range(nh)]
    gates = [_dot(km_scr[hh].astype(BF16), qs[hh]) for hh in range(nh)]
    for hh in range(nh):
        gate_scr[hh] = gates[hh]
    jidx = lax.broadcasted_iota(I32, (16, tq), 0)

    def rank_body(jp, ranks):
        out = []
        for hh in range(nh):
            row = gate_scr[hh, pl.ds(jp, 1), :]
            better = (row > gates[hh]) | ((row == gates[hh]) & (jp < jidx))
            out.append(ranks[hh] + better.astype(I32))
        return tuple(out)

    ranks = lax.fori_loop(0, i, rank_body, tuple(jnp.zeros((16, tq), I32) for _ in range(nh)))
    qxs = []
    for hh in range(nh):
        keep = ((jidx < i) & (ranks[hh] < ntop)) | (jidx == i)
        mask_t = jnp.where(keep, 0.0, NEG)
        feat = jnp.concatenate([jnp.zeros((HEAD_DIM, tq), F32), mask_t,
                                jnp.zeros((SLOT - HEAD_DIM - 16, tq), F32)], axis=0)
        qxs.append(qs[hh] + (feat + fq_ref[hh][:, 0:1]).astype(BF16))
    kget = lambda c, start, size: _rows(k_ref, start, size, c * SLOT)
    vget = lambda c, start, size: _cols(vt_ref, start, size, c * SLOT)
    _flash_reset(m_scr, acc_scr)
    _flash_causal(kget, vget, qxs, i, cb_ref, sa_scr, sb_scr, m_scr, acc_scr)
    o = acc_scr[...] * (1.0 / acc_scr[HEAD_DIM:HEAD_DIM + 1, :])
    for hh in range(nh):
        o_ref[:, hh * SLOT:(hh + 1) * SLOT] = o[:, hh * tq:(hh + 1) * tq].T.astype(o_ref.dtype)


def _moba_attention(zat, zb, fq, cb, bsz, seq):
    tq = TQ
    assert tq == MOBA_BLOCK and seq % tq == 0
    ni = seq // tq
    assert ni <= 16 and ni % 2 == 0
    ntop = min(MOBA_TOPK, ni)
    nh = MOBA_HPS
    hw = nh * SLOT
    q0 = _SLOT_MOBA_Q // nh
    v0 = _SLOT_MOBA_V // nh
    return pl.pallas_call(
        functools.partial(_moba_kernel, tq=tq, nblk=ni, ntop=ntop),
        out_shape=jax.ShapeDtypeStruct((bsz * seq, MOBA_HEADS * SLOT), BF16),
        grid=(bsz, MOBA_HEADS // nh, ni),
        in_specs=[pl.BlockSpec((hw, tq), lambda b, h, i: (q0 + h, b * ni + i)),
                  pl.BlockSpec((seq, hw), lambda b, h, i: (b, 1 + h)),
                  pl.BlockSpec((hw, seq), lambda b, h, i: (v0 + h, b)),
                  pl.BlockSpec((nh, SLOT, 8), lambda b, h, i: (h, 0, 0)),
                  pl.BlockSpec(cb.shape, lambda b, h, i: (0, 0, 0))],
        out_specs=pl.BlockSpec((tq, hw), lambda b, h, i: (b * ni + i, h)),
        scratch_shapes=[pltpu.VMEM((1, nh * tq), F32),
                        pltpu.VMEM((SLOT, nh * tq), F32),
                        pltpu.VMEM((nh, 16, SLOT), F32), pltpu.VMEM((nh, 16, tq), F32),
                        pltpu.VMEM((2 * tq, nh * tq), F32), pltpu.VMEM((2 * tq, nh * tq), F32)],
        compiler_params=_cparams(("parallel", "parallel", "arbitrary")),
        name="moba_attention",
    )(zat, zb, zat, fq, cb)


def _merge_kernel(ya_ref, yb_ref, hc_ref, bc_ref, cc_ref, hcp_ref, ccp_ref, g0_ref, g1_ref, g2_ref,
                  h_ref, cw_ref, wa_ref, wb_ref, wc_ref, wo_ref, lg_ref, lb_ref, o_ref, *, tm, per, alpha):
    i = pl.program_id(0)
    u = cc_ref[...].astype(F32) * hc_ref[...].astype(F32)
    first = (i % per) == 0
    up = jnp.where(first, 0.0, ccp_ref[...].astype(F32) * hcp_ref[...].astype(F32))
    rowi = lax.broadcasted_iota(I32, u.shape, 0)
    u1 = jnp.where(rowi == 0, up[15:16, :], pltpu.roll(u, 1, 0))
    u2 = jnp.where(rowi == 0, up[14:15, :], jnp.where(rowi == 1, up[15:16, :], pltpu.roll(u, 2, 0)))
    cw = cw_ref[...]
    yc = bc_ref[...].astype(F32) * (cw[0:1, :] * u2 + cw[1:2, :] * u1 + cw[2:3, :] * u)
    merged = (jax.nn.sigmoid(g0_ref[...].astype(F32)) * _dot(ya_ref[...], wa_ref[...])
              + jax.nn.sigmoid(g1_ref[...].astype(F32)) * _dot(yb_ref[...], wb_ref[...])
              + jax.nn.sigmoid(g2_ref[...].astype(F32)) * _dot(yc.astype(BF16), wc_ref[...]))
    mix = _dot(merged.astype(BF16), wo_ref[...])
    o_ref[...] = _layer_norm(alpha * h_ref[...] + mix, lg_ref[...], lb_ref[...])


def _merge(ya, yb, zc, h, cw, wa, wb, wc, wo, lg, lb, seq, alpha):
    tm = 512
    t = h.shape[0]
    per = seq // tm
    d = D_MODEL
    row = lambda i: (i, 0)
    const = lambda i: (0, 0)
    prev = lambda c: (lambda i: (jnp.maximum(i * (tm // 16) - 1, 0), c))
    in_specs = [
        pl.BlockSpec((tm, d), row), pl.BlockSpec((tm, d), row),
        pl.BlockSpec((tm, CONV_CH), lambda i: (i, 1)), pl.BlockSpec((tm, CONV_CH), lambda i: (i, 2)),
        pl.BlockSpec((tm, CONV_CH), lambda i: (i, 3)),
        pl.BlockSpec((16, CONV_CH), prev(1)), pl.BlockSpec((16, CONV_CH), prev(3)),
        pl.BlockSpec((tm, d), lambda i: (i, 2)), pl.BlockSpec((tm, d), lambda i: (i, 3)),
        pl.BlockSpec((tm, d), lambda i: (i, 4)),
        pl.BlockSpec((tm, d), row),
        pl.BlockSpec((8, CONV_CH), const),
        pl.BlockSpec((d, d), const), pl.BlockSpec((d, d), const), pl.BlockSpec((CONV_CH, d), const),
        pl.BlockSpec((d, d), const), pl.BlockSpec((1, d), const), pl.BlockSpec((1, d), const),
    ]
    return pl.pallas_call(
        functools.partial(_merge_kernel, tm=tm, per=per, alpha=alpha),
        out_shape=jax.ShapeDtypeStruct((t, d), F32),
        grid=(t // tm,),
        in_specs=in_specs,
        out_specs=pl.BlockSpec((tm, d), row),
        compiler_params=_cparams(("parallel",)),
        name="mixer_merge",
    )(ya, yb, zc, zc, zc, zc, zc, zc, zc, zc, h, cw, wa, wb, wc, wo, lg, lb)


def _xattn_kernel(h_ref, kv_ref, wq_ref, wo_ref, lg_ref, lb_ref, o_ref, ob_ref, *, alpha):
    h = h_ref[...]
    q = _dot(h.astype(BF16), wq_ref[...]).astype(BF16)
    kv = kv_ref[...]
    nh = XATTN_HEADS
    hd = XATTN_HEAD_DIM
    outs = []
    for hh in range(nh):
        s = _dot_t(q[:, hh * hd:(hh + 1) * hd], kv[:, hh * hd:(hh + 1) * hd]) * (hd ** -0.5)
        s = s - jnp.max(s, axis=-1, keepdims=True)
        e = jnp.exp(s)
        p = e / jnp.sum(e, axis=-1, keepdims=True)
        outs.append(_dot(p.astype(BF16), kv[:, (nh + hh) * hd:(nh + hh + 1) * hd]))
    o = jnp.concatenate(outs, axis=-1).astype(BF16)
    y = _layer_norm(alpha * h + _dot(o, wo_ref[...]), lg_ref[...], lb_ref[...])
    o_ref[...] = y
    bits = lax.bitcast_convert_type(y.astype(BF16).astype(F32), I32)
    half = y.shape[1] // 2
    ob_ref[...] = (bits[:, half:] & -65536) | lax.shift_right_logical(bits[:, :half], 16)


def _unpack_pairs(words):
    lo = lax.bitcast_convert_type(lax.shift_left(words, 16), F32).astype(BF16)
    hi = lax.bitcast_convert_type(words & -65536, F32).astype(BF16)
    return lo, hi


def _xattn(h, kv, wq, wo, lg, lb, seq, mlen, alpha):
    tm = 512
    t = h.shape[0]
    per = seq // tm
    d = D_MODEL
    row = lambda i: (i, 0)
    const = lambda i: (0, 0)
    return pl.pallas_call(
        functools.partial(_xattn_kernel, alpha=alpha),
        out_shape=(jax.ShapeDtypeStruct((t, d), F32), jax.ShapeDtypeStruct((t, d // 2), I32)),
        grid=(t // tm,),
        in_specs=[pl.BlockSpec((tm, d), row), pl.BlockSpec((mlen, kv.shape[1]), lambda i: (i // per, 0)),
                  pl.BlockSpec(wq.shape, const), pl.BlockSpec(wo.shape, const),
                  pl.BlockSpec((1, d), const), pl.BlockSpec((1, d), const)],
        out_specs=(pl.BlockSpec((tm, d), row), pl.BlockSpec((tm, d // 2), row)),
        compiler_params=_cparams(("parallel",)),
        name="mem_xattn",
    )(h, kv, wq, wo, lg, lb)


def _expert_kernel(be_ref, nu_ref, nxt_ref, slot_ref, x_ref, w13_hbm, w2_hbm, o_ref,
                   w13f_scr, w2f_scr, w13b_scr, w2b_scr, sem, *, layer):
    i = pl.program_id(0)
    e = be_ref[i]
    s = slot_ref[i]

    def weight_copies(expert, slot):
        return (pltpu.make_async_copy(w13_hbm.at[layer, expert], w13f_scr.at[slot], sem.at[0, slot]),
                pltpu.make_async_copy(w2_hbm.at[layer, expert], w2f_scr.at[slot], sem.at[1, slot]))

    @pl.when(i == 0)
    def _():
        for cp in weight_copies(e, s):
            cp.start()

    @pl.when((i == 0) | (e != be_ref[jnp.maximum(i - 1, 0)]))
    def _():
        for cp in weight_copies(e, s):
            cp.wait()
        nxt = nxt_ref[i]

        @pl.when(nxt >= 0)
        def _():
            for cp in weight_copies(nxt, 1 - s):
                cp.start()

        w13b_scr[...] = w13f_scr[s].astype(BF16)
        w2b_scr[...] = w2f_scr[s].astype(BF16)

    @pl.when(i < nu_ref[0])
    def _():
        x_lo, x_hi = _unpack_pairs(x_ref[...])
        half = x_lo.shape[1]
        hmid = _dot(x_lo, w13b_scr[:half, :]) + _dot(x_hi, w13b_scr[half:, :])
        a = hmid[:, :EXPERT_DIM]
        g = hmid[:, EXPERT_DIM:]
        act = (a * jax.nn.sigmoid(a) * g).astype(BF16)
        o_ref[...] = _dot(act, w2b_scr[...]).astype(o_ref.dtype)


def _experts(xg, w13, w2, block_e, n_used, next_e, slot, layer):
    n_pad = xg.shape[0]
    d = 2 * xg.shape[1]
    rows = EXP_ROWS
    nb = n_pad // rows
    blk = lambda i, be, nu, nx, sl: (jnp.minimum(i, nu[0] - 1), 0)
    grid_spec = pltpu.PrefetchScalarGridSpec(
        num_scalar_prefetch=4,
        grid=(nb,),
        in_specs=[pl.BlockSpec((rows, d // 2), blk),
                  pl.BlockSpec(memory_space=pl.ANY), pl.BlockSpec(memory_space=pl.ANY)],
        out_specs=pl.BlockSpec((rows, d), blk),
        scratch_shapes=[pltpu.VMEM((2, d, 2 * EXPERT_DIM), F32), pltpu.VMEM((2, EXPERT_DIM, d), F32),
                        pltpu.VMEM((d, 2 * EXPERT_DIM), BF16), pltpu.VMEM((EXPERT_DIM, d), BF16),
                        pltpu.SemaphoreType.DMA((2, 2))],
    )
    return pl.pallas_call(
        functools.partial(_expert_kernel, layer=layer),
        out_shape=jax.ShapeDtypeStruct((n_pad, d), BF16),
        grid_spec=grid_spec,
        compiler_params=_cparams(("arbitrary",)),
        name="moe_experts",
    )(block_e, n_used, next_e, slot, xg, w13, w2)


def _moe_out_kernel(h_ref, y8_ref, wk_ref, w13_ref, w2_ref, lg_ref, lb_ref, o_ref, ob_ref, *, alpha):
    h = h_ref[...]
    hmid = _dot(h.astype(BF16), w13_ref[...])
    a = hmid[:, :EXPERT_DIM]
    g = hmid[:, EXPERT_DIM:]
    acc = alpha * h + _dot((a * jax.nn.sigmoid(a) * g).astype(BF16), w2_ref[...])
    wk = wk_ref[...]
    for k in range(TOP_K):
        acc = acc + wk[:, k:k + 1] * y8_ref[k].astype(F32)
    y = _layer_norm(acc, lg_ref[...], lb_ref[...])
    o_ref[...] = y
    ob_ref[...] = y.astype(BF16)


def _moe_out(h, y8, wk, w13, w2, lg, lb, alpha):
    tm = 256
    t, d = h.shape
    row = lambda i: (i, 0)
    const = lambda i: (0, 0)
    return pl.pallas_call(
        functools.partial(_moe_out_kernel, alpha=alpha),
        out_shape=(jax.ShapeDtypeStruct((t, d), F32), jax.ShapeDtypeStruct((t, d), BF16)),
        grid=(t // tm,),
        in_specs=[pl.BlockSpec((tm, d), row), pl.BlockSpec((TOP_K, tm, d), lambda i: (0, i, 0)),
                  pl.BlockSpec((tm, TOP_K), row),
                  pl.BlockSpec(w13.shape, const), pl.BlockSpec(w2.shape, const),
                  pl.BlockSpec((1, d), const), pl.BlockSpec((1, d), const)],
        out_specs=(pl.BlockSpec((tm, d), row), pl.BlockSpec((tm, d), row)),
        compiler_params=_cparams(("parallel",)),
        name="moe_shared_ln",
    )(h, y8, wk, w13, w2, lg, lb)


def _dest_kernel(idx_ref, rank_ref, ps_ref, o_ref):
    idx = idx_ref[...]
    ps = ps_ref[...]
    tm = idx.shape[1]
    eidx = lax.broadcasted_iota(I32, (N_EXPERTS, tm), 0)
    rows = [jnp.sum(jnp.where(eidx == idx[k:k + 1, :], ps, 0.0), axis=0, keepdims=True) for k in range(TOP_K)]
    o_ref[...] = jnp.concatenate(rows, axis=0).astype(I32) + rank_ref[...]


def _dest(idx, rank, pstarts):
    tm = 1024
    t = idx.shape[1]
    tm = min(tm, t)
    col = lambda i: (0, i)
    return pl.pallas_call(
        _dest_kernel,
        out_shape=jax.ShapeDtypeStruct((TOP_K, t), I32),
        grid=(t // tm,),
        in_specs=[pl.BlockSpec((TOP_K, tm), col), pl.BlockSpec((TOP_K, tm), col),
                  pl.BlockSpec((N_EXPERTS, 1), lambda i: (0, 0))],
        out_specs=pl.BlockSpec((TOP_K, tm), col),
        compiler_params=_cparams(("parallel",)),
        name="moe_dest",
    )(idx, rank, pstarts)


def _router_kernel(h_ref, rwt_ref, rb_ref, tri_ref, idx_ref, w_ref, rank_ref, cnt_ref, carry_scr, *, tm):
    i = pl.program_id(0)

    @pl.when(i == 0)
    def _():
        carry_scr[...] = jnp.zeros(carry_scr.shape, F32)

    ne = N_EXPERTS
    gsz = ne // N_GROUPS
    s = jax.nn.sigmoid(_dot_t(rwt_ref[...], h_ref[...].astype(BF16)))
    sb = s + rb_ref[...]
    sb3 = sb.reshape(N_GROUPS, gsz, tm)
    li = lax.broadcasted_iota(I32, (N_GROUPS, gsz, tm), 1)
    m1 = jnp.max(sb3, axis=1, keepdims=True)
    first = jnp.min(jnp.where(sb3 == m1, li, gsz), axis=1, keepdims=True)
    m2 = jnp.max(jnp.where(li == first, -jnp.inf, sb3), axis=1, keepdims=True)
    gs = (m1 + m2).reshape(N_GROUPS, tm)
    gi = lax.broadcasted_iota(I32, (N_GROUPS, tm), 0)
    grank = jnp.zeros((N_GROUPS, tm), I32)
    for gp in range(N_GROUPS):
        row = gs[gp:gp + 1, :]
        grank = grank + ((row > gs) | ((row == gs) & (gp < gi))).astype(I32)
    gkeep = (grank < TOPK_GROUPS).astype(F32)
    ekeep = jnp.broadcast_to(gkeep[:, None, :], (N_GROUPS, gsz, tm)).reshape(ne, tm)
    cand = jnp.where(ekeep > 0.0, sb, NEG)
    eidx = lax.broadcasted_iota(I32, (ne, tm), 0)
    sel = jnp.zeros((ne, tm), F32)
    idxs, wts = [], []
    for _ in range(TOP_K):
        mx = jnp.max(cand, axis=0, keepdims=True)
        ik = jnp.min(jnp.where(cand == mx, eidx, ne), axis=0, keepdims=True)
        hit = eidx == ik
        wts.append(jnp.sum(jnp.where(hit, s, 0.0), axis=0, keepdims=True))
        idxs.append(ik)
        sel = jnp.where(hit, 1.0, sel)
        cand = jnp.where(hit, -jnp.inf, cand)
    before = _dot(sel.astype(BF16), tri_ref[...]) + carry_scr[...]
    ranks = [jnp.sum(jnp.where(eidx == ik, before, 0.0), axis=0, keepdims=True) for ik in idxs]
    carry_scr[...] = carry_scr[...] + jnp.sum(sel, axis=1, keepdims=True)
    w = jnp.concatenate(wts, axis=0)
    idx_ref[...] = jnp.concatenate(idxs, axis=0)
    w_ref[...] = w / jnp.sum(w, axis=0, keepdims=True) * ROUTE_SCALE
    rank_ref[...] = jnp.concatenate(ranks, axis=0).astype(I32)
    cnt_ref[...] = jnp.broadcast_to(carry_scr[...], cnt_ref.shape)


def _router(h, rwt, rb):
    tm = 256
    t, d = h.shape
    tri = jnp.asarray(np.triu(np.ones((tm, tm), np.float32), 1)).astype(BF16)
    const = lambda i: (0, 0)
    col = lambda i: (0, i)
    return pl.pallas_call(
        functools.partial(_router_kernel, tm=tm),
        out_shape=(jax.ShapeDtypeStruct((TOP_K, t), I32), jax.ShapeDtypeStruct((TOP_K, t), F32),
                   jax.ShapeDtypeStruct((TOP_K, t), I32), jax.ShapeDtypeStruct((N_EXPERTS, SLOT), F32)),
        grid=(t // tm,),
        in_specs=[pl.BlockSpec((tm, d), lambda i: (i, 0)), pl.BlockSpec((N_EXPERTS, d), const),
                  pl.BlockSpec((N_EXPERTS, 1), const), pl.BlockSpec((tm, tm), const)],
        out_specs=(pl.BlockSpec((TOP_K, tm), col), pl.BlockSpec((TOP_K, tm), col),
                   pl.BlockSpec((TOP_K, tm), col), pl.BlockSpec((N_EXPERTS, SLOT), const)),
        scratch_shapes=[pltpu.VMEM((N_EXPERTS, 1), F32)],
        compiler_params=_cparams(("arbitrary",)),
        name="moe_router",
    )(h, rwt, rb, tri)


def _dispatch(hp, dest_flat, n_pad):
    t, w = hp.shape
    per_w = t // (SC_CORES * SC_SUBCORES)
    assert per_w * SC_CORES * SC_SUBCORES == t and per_w % SC_CHUNK == 0 and dest_flat.shape[0] == TOP_K * t
    mesh = plsc.VectorSubcoreMesh(core_axis_name="c", subcore_axis_name="s")

    @functools.partial(
        pl.kernel, out_type=jax.ShapeDtypeStruct((n_pad, w), I32), mesh=mesh,
        scratch_types=[pltpu.VMEM((SC_CHUNK,), I32)] * TOP_K
        + [pltpu.VMEM((SC_CHUNK, w), I32), pltpu.SemaphoreType.DMA],
        name="moe_dispatch")
    def scatter_rows(hp_hbm, dest_hbm, xg_hbm, *scratch):
        idx_vs, rows_v, sem = scratch[:TOP_K], scratch[TOP_K], scratch[TOP_K + 1]
        wid = lax.axis_index("s") * SC_CORES + lax.axis_index("c")
        base = wid * per_w

        @pl.loop(0, per_w // SC_CHUNK)
        def _(j):
            t0 = pl.multiple_of(base + j * SC_CHUNK, SC_CHUNK)
            pltpu.sync_copy(hp_hbm.at[pl.ds(t0, SC_CHUNK)], rows_v)
            for k in range(TOP_K):
                pltpu.sync_copy(dest_hbm.at[pl.ds(pl.multiple_of(k * t + t0, SC_CHUNK), SC_CHUNK)], idx_vs[k])
            copies = [pltpu.async_copy(rows_v, xg_hbm.at[idx_vs[k]], sem) for k in range(TOP_K)]
            for cp in copies:
                cp.wait()

    return scatter_rows(hp, dest_flat)


def _moe(h, hp, router_w, router_b, exp_w13, exp_w2, layer, shared_w13, shared_w2, lg, lb, alpha):
    t, d = h.shape
    idx, wts, rank, cnt = _router(h, router_w.T, router_b.astype(F32).reshape(N_EXPERTS, 1))
    rows = EXP_ROWS
    n_a = t * TOP_K
    counts = cnt[:, 0].astype(I32)
    pcounts = (counts + rows - 1) // rows * rows
    pends = jnp.cumsum(pcounts)
    pstarts = pends - pcounts
    dest = _dest(idx, rank, pstarts.astype(F32).reshape(N_EXPERTS, 1))
    n_blocks = -(-n_a // rows) + N_EXPERTS
    n_pad = n_blocks * rows
    n_used = (pends[-1:] // rows).astype(I32)
    first_row = jnp.minimum(jnp.arange(n_blocks, dtype=I32), n_used - 1) * rows
    block_e = jnp.sum((pends[None, :] <= first_row[:, None]).astype(I32), axis=1)
    eidx = jnp.arange(N_EXPERTS, dtype=I32)
    has_rows = pcounts > 0
    at_or_after = lax.cummin(jnp.where(has_rows, eidx, N_EXPERTS), axis=0, reverse=True)
    after = jnp.concatenate([at_or_after[1:], jnp.full((1,), N_EXPERTS, I32)])
    next_e = jnp.take(jnp.where(after < N_EXPERTS, after, -1), block_e)
    slot = jnp.take((jnp.cumsum(has_rows.astype(I32)) - 1) % 2, block_e).astype(I32)
    xg = _dispatch(hp, dest.reshape(-1), n_pad)
    out = _experts(xg, exp_w13, exp_w2, block_e, n_used, next_e.astype(I32), slot, layer)
    y8 = jnp.take(out, dest, axis=0, mode="clip")
    return _moe_out(h, y8, wts.T, shared_w13, shared_w2, lg, lb, alpha)


def _pad_slots(w, scale=1.0):
    dm = w.shape[0]
    nh = w.shape[1] // HEAD_DIM
    w = (w * scale).reshape(dm, nh, HEAD_DIM)
    return jnp.concatenate([w, jnp.zeros_like(w)], axis=-1).reshape(dm, nh * SLOT)


def _pad_rows(w):
    n = w.shape[1]
    nh = w.shape[0] // HEAD_DIM
    w = w.reshape(nh, HEAD_DIM, n)
    return jnp.concatenate([w, jnp.zeros_like(w)], axis=1).reshape(nh * SLOT, n)


def _alibi(n):
    return np.exp2(-8.0 * np.arange(1, n + 1, dtype=np.float64) / n).astype(np.float32)


def _key_features(seq):
    p = np.arange(seq)
    slc = np.zeros((seq, SLOT), np.float32)
    slc[:, HEAD_DIM] = p % SLC_BLOCK
    blk = p // SLC_BLOCK
    nz = blk > 0
    slc[p[nz], HEAD_DIM + blk[nz]] = 1.0
    win = np.zeros((seq, SLOT), np.float32)
    win[:, HEAD_DIM] = p // 64
    win[:, HEAD_DIM + 1] = p % 64
    mob = np.zeros((seq, SLOT), np.float32)
    mob[p, HEAD_DIM + p // MOBA_BLOCK] = 1.0
    mob[:, HEAD_DIM + 16] = p % MOBA_BLOCK
    return np.concatenate([slc] * 2 + [win] * 2 + [mob] * MOBA_HEADS, axis=1)


def _query_features():
    sl = _alibi(NSA_HEADS)
    nsa = np.zeros((NSA_KV_HEADS, SLOT, 8), np.float32)
    for g in range(NSA_KV_HEADS):
        for r in range(NSA_GROUP):
            s = sl[g * NSA_GROUP + r]
            nsa[g, HEAD_DIM, r] = s
            nsa[g, HEAD_DIM + 1:, r] = s * SLC_BLOCK * np.arange(1, 64)
            nsa[g, HEAD_DIM, NSA_GROUP + r] = s * 64
            nsa[g, HEAD_DIM + 1, NSA_GROUP + r] = s
    sm = _alibi(MOBA_HEADS)
    mob = np.zeros((MOBA_HEADS, SLOT, 8), np.float32)
    for h in range(MOBA_HEADS):
        mob[h, HEAD_DIM:HEAD_DIM + 16, 0] = sm[h] * MOBA_BLOCK * np.arange(16)
        mob[h, HEAD_DIM + 16, 0] = sm[h]
    return nsa, mob


def _mask_biases():
    t = np.arange(TQ)[None, :]
    p2 = np.arange(2 * TQ)[:, None]
    cb = np.stack([p2 <= t, p2 <= TQ + t])
    p3 = np.arange(3 * TQ)[:, None]
    dist = np.stack([c * TQ + t - p3 for c in range(3)])
    wb = (dist >= 0) & (dist < WINDOW)
    to_bias = lambda m: np.where(m, 0.0, NEG).astype(np.float32)
    return to_bias(cb), to_bias(wb)


def _overlap_t(seq):
    nc = seq // CMP_STRIDE
    c_start = np.arange(nc) * CMP_STRIDE
    b_start = np.arange(64) * SLC_BLOCK
    ov = ((c_start[None, :] < (b_start + SLC_BLOCK)[:, None])
          & ((c_start + CMP_BLOCK)[None, :] > b_start[:, None])
          & (np.arange(nc) < nc - 1)[None, :] & (b_start < seq)[:, None])
    return ov.astype(np.float32)


def _compress_weights(pe, w1, w2):
    ty = np.array([0, 0, 1, 1])
    eye = jnp.eye(4, dtype=F32)
    w1r = w1.reshape(2, CMP_BLOCK, HEAD_DIM, CMP_HIDDEN)[ty]
    top = jnp.einsum('spdj,sS->psdSj', w1r[:, :CMP_STRIDE], eye).reshape(CMP_STRIDE * 256, 4 * CMP_HIDDEN)
    bot = jnp.einsum('spdj,sS->psdSj', w1r[:, CMP_STRIDE:], eye).reshape(CMP_STRIDE * 256, 4 * CMP_HIDDEN)
    per = pe[ty]
    pet = jnp.transpose(per[:, :CMP_STRIDE], (1, 0, 2)).reshape(1, CMP_STRIDE * 256)
    peb = jnp.transpose(per[:, CMP_STRIDE:], (1, 0, 2)).reshape(1, CMP_STRIDE * 256)
    w2p = jnp.concatenate([w2[ty], jnp.zeros((4, CMP_HIDDEN, SLOT - HEAD_DIM), F32)], axis=-1)
    w2b = jnp.einsum('sjd,sS->sjSd', w2p, eye).reshape(4 * CMP_HIDDEN, 4 * SLOT)
    w2k = w2b[:, :2 * SLOT].astype(BF16)
    w2vt = w2b[:, 2 * SLOT:].T.astype(BF16)
    return pet, peb, top.astype(BF16), bot.astype(BF16), w2k, w2vt


def kernel(x, mem, w_in, cmp_pe, cmp_w1, cmp_w2, conv_w, w_branch, w_out, ln1_g, ln1_b,
           xattn_wq, xattn_wkv, xattn_wo, ln2_g, ln2_b, router_w, router_b, exp_w13, exp_w2,
           shared_w13, shared_w2, ln3_g, ln3_b):
    bsz, seq, d = x.shape
    mlen = mem.shape[1]
    depth = w_in.shape[0]
    alpha = (2.0 * depth) ** 0.25
    t = bsz * seq
    scale = HEAD_DIM ** -0.5

    kfeat = jnp.asarray(_key_features(seq))
    fq_nsa, fq_moba = (jnp.asarray(a) for a in _query_features())
    cbias, wbias = (jnp.asarray(a) for a in _mask_biases())
    ovt = jnp.asarray(_overlap_t(seq)).astype(BF16)
    memf = mem.reshape(bsz * mlen, d)

    h = x.reshape(t, d)
    hb = h.astype(BF16)
    for l in range(depth):
        ht = hb.T
        wi = w_in[l]
        kv6 = wi[:, _OFF_NSA_KV:_OFF_NSA_G].reshape(d, 6, NSA_KV_HEADS * HEAD_DIM)
        mq, mk, mv = (wi[:, _OFF_MOBA + j * 512:_OFF_MOBA + (j + 1) * 512] for j in range(3))
        w_at = jnp.concatenate([wi[:, :512] * scale, kv6[:, 3], kv6[:, 5], mq * scale, mv],
                               axis=1).T.astype(BF16)
        w_b = jnp.concatenate([_pad_slots(kv6[:, 2]), _pad_slots(kv6[:, 4]), _pad_slots(mk)], axis=1).astype(BF16)
        wg = wi[:, _OFF_NSA_G:_OFF_MOBA].reshape(d, NSA_KV_HEADS, 12)
        wgt = jnp.concatenate([wg, jnp.zeros((d, NSA_KV_HEADS, SLOT - 12), F32)], axis=-1)
        wgt = wgt.reshape(d, NSA_KV_HEADS * SLOT).T.astype(BF16)
        w_c = jnp.concatenate([wi[:, _OFF_NSA_KV:_OFF_NSA_KV + 256], jnp.zeros((d, 256), F32),
                               wi[:, _OFF_CONV:]], axis=1).astype(BF16)

        value_slots = tuple(range(_SLOT_SLC_V, _SLOT_MOBA_Q)) + tuple(range(_SLOT_MOBA_V, _N_SLOTS_T))
        zat = _proj_t(w_at, ht, value_slots)
        gt = _mm(wgt, ht, F32, tm=256, tn=1024, name="proj_gate")
        zb = _mm(hb, w_b, BF16, tn=768, feats=kfeat, name="proj_b")
        zc = _mm(hb, w_c, BF16, tn=1024, name="proj_c")

        pet, peb, wt, wb, w2k, w2vt = _compress_weights(cmp_pe[l], cmp_w1[l], cmp_w2[l])
        sub = zc[:, :256].reshape(bsz, seq // CMP_STRIDE, CMP_STRIDE * 256)
        kc, vct = _compress(sub, pet, peb, wt, wb, w2k, w2vt)

        ya = _nsa_attention(zat, zb, gt, kc, vct, ovt, fq_nsa, cbias, wbias, bsz, seq)
        yb = _moba_attention(zat, zb, fq_moba, cbias, bsz, seq)

        cw = jnp.concatenate([conv_w[l], jnp.zeros((5, CONV_CH), F32)], axis=0)
        h = _merge(ya, yb, zc, h, cw, _pad_rows(w_branch[l, 0]).astype(BF16),
                   _pad_rows(w_branch[l, 1]).astype(BF16), w_branch[l, 2].astype(BF16),
                   w_out[l].astype(BF16), ln1_g[l][None], ln1_b[l][None], seq, alpha)

        kv = _mm(memf, xattn_wkv[l].astype(BF16), BF16, tm=512, name="xattn_kv")
        h, hp = _xattn(h, kv, xattn_wq[l].astype(BF16), xattn_wo[l].astype(BF16),
                       ln2_g[l][None], ln2_b[l][None], seq, mlen, alpha)

        h, hb = _moe(h, hp, router_w[l].astype(BF16), router_b[l], exp_w13, exp_w2, l,
                     shared_w13[l].astype(BF16), shared_w2[l].astype(BF16), ln3_g[l][None], ln3_b[l][None], alpha)
    return h.reshape(bsz, seq, d)
```

```python
import functools

import jax
import jax.numpy as jnp
import numpy as np
from jax import lax
from jax.experimental import pallas as pl
from jax.experimental.pallas import tpu as pltpu
from jax.experimental.pallas import tpu_sc as plsc

F32 = jnp.float32
BF16 = jnp.bfloat16
I32 = jnp.int32

D_MODEL = 1024
HEAD_DIM = 64
SLOT = 128
NEG = -1e30
LN_EPS = 1e-5

NSA_HEADS = 8
NSA_KV_HEADS = 2
NSA_GROUP = 4
CMP_BLOCK = 32
CMP_STRIDE = 16
CMP_HIDDEN = 256
SLC_BLOCK = 64
SLC_TOPK = 16
WINDOW = 512
MOBA_HEADS = 8
MOBA_BLOCK = 256
MOBA_TOPK = 3
CONV_CH = 512
XATTN_HEADS = 4
XATTN_HEAD_DIM = 128
N_EXPERTS = 256
TOP_K = 8
N_GROUPS = 8
TOPK_GROUPS = 4
EXPERT_DIM = 256
ROUTE_SCALE = 2.5

TQ = 256
MOBA_HPS = 4
PV_ROWS = 80
EXP_ROWS = 512
VMEM_LIMIT = 48 * 1024 * 1024
SC_CORES = 2
SC_SUBCORES = 16
SC_CHUNK = 64

_OFF_NSA_Q = 0
_OFF_NSA_KV = 512
_OFF_NSA_G = 1280
_OFF_MOBA = 1304
_OFF_CONV = 2840
_OFF_MERGE = 4376

_SLOT_NSA_Q = 0
_SLOT_SLC_V = 8
_SLOT_WIN_V = 10
_SLOT_MOBA_Q = 12
_SLOT_MOBA_V = 20
_N_SLOTS_T = 28


def _cparams(sem):
    return pltpu.CompilerParams(dimension_semantics=sem, vmem_limit_bytes=VMEM_LIMIT)


def _dot(a, b):
    return jnp.dot(a, b, preferred_element_type=F32)


def _dot_t(a, b):
    return lax.dot_general(a, b, (((1,), (1,)), ((), ())), preferred_element_type=F32)


def _layer_norm(x, g, b):
    mu = jnp.mean(x, axis=-1, keepdims=True)
    xc = x - mu
    var = jnp.mean(xc * xc, axis=-1, keepdims=True)
    return xc * lax.rsqrt(var + LN_EPS) * g + b


def _mm_kernel(x_ref, w_ref, o_ref):
    o_ref[...] = _dot(x_ref[...].astype(BF16), w_ref[...]).astype(o_ref.dtype)


def _mm_feat_kernel(x_ref, w_ref, f_ref, o_ref):
    y = _dot(x_ref[...].astype(BF16), w_ref[...]) + f_ref[...]
    o_ref[...] = y.astype(o_ref.dtype)


def _mm(x, w, out_dtype, *, tm=1024, tn=512, feats=None, name):
    m, k = x.shape
    n = w.shape[1]
    tm = min(tm, m)
    tn = min(tn, n)
    assert m % tm == 0 and n % tn == 0, (m, n, tm, tn)
    in_specs = [pl.BlockSpec((tm, k), lambda i, j: (i, 0)),
                pl.BlockSpec((k, tn), lambda i, j: (0, j))]
    args = [x, w]
    if feats is None:
        body = _mm_kernel
    else:
        per = feats.shape[0] // tm
        assert feats.shape[0] % tm == 0
        in_specs.append(pl.BlockSpec((tm, tn), lambda i, j: (i % per, j)))
        args.append(feats)
        body = _mm_feat_kernel
    return pl.pallas_call(
        body,
        out_shape=jax.ShapeDtypeStruct((m, n), out_dtype),
        grid=(m // tm, n // tn),
        in_specs=in_specs,
        out_specs=pl.BlockSpec((tm, tn), lambda i, j: (i, j)),
        compiler_params=_cparams(("parallel", "parallel")),
        name=name,
    )(*args)


def _proj_t_kernel(w_ref, ht_ref, o_ref, *, value_slots):
    acc = _dot(w_ref[...], ht_ref[...])
    tn = acc.shape[1]
    zero = jnp.zeros((SLOT - HEAD_DIM, tn), o_ref.dtype)
    ones_row = (lax.broadcasted_iota(I32, (SLOT - HEAD_DIM, tn), 0) == 0).astype(o_ref.dtype)
    for s in range(w_ref.shape[0] // HEAD_DIM):
        o_ref[s * SLOT:s * SLOT + HEAD_DIM, :] = acc[s * HEAD_DIM:(s + 1) * HEAD_DIM, :].astype(o_ref.dtype)
        o_ref[s * SLOT + HEAD_DIM:(s + 1) * SLOT, :] = ones_row if s in value_slots else zero


def _proj_t(w, ht, value_slots):
    m, k = w.shape
    t = ht.shape[1]
    tn = min(512, t)
    n_out = m // HEAD_DIM * SLOT
    return pl.pallas_call(
        functools.partial(_proj_t_kernel, value_slots=value_slots),
        out_shape=jax.ShapeDtypeStruct((n_out, t), BF16),
        grid=(t // tn,),
        in_specs=[pl.BlockSpec((m, k), lambda j: (0, 0)), pl.BlockSpec((k, tn), lambda j: (0, j))],
        out_specs=pl.BlockSpec((n_out, tn), lambda j: (0, j)),
        compiler_params=_cparams(("parallel",)),
        name="proj_at",
    )(w, ht)


def _cmp_kernel(sub_ref, pet_ref, peb_ref, wt_ref, wb_ref, w2k_ref, w2vt_ref, kc_ref, vct_ref):
    sub = sub_ref[0].astype(F32)
    nc = sub.shape[0]
    a = _dot((sub + pet_ref[...]).astype(BF16), wt_ref[...])
    b = _dot((sub + peb_ref[...]).astype(BF16), wb_ref[...])
    hid = jax.nn.gelu(a + pltpu.roll(b, nc - 1, 0)).astype(BF16)
    kc_ref[0] = _dot(hid, w2k_ref[...]).astype(kc_ref.dtype)
    vct_ref[0] = _dot_t(w2vt_ref[...], hid).astype(vct_ref.dtype)


def _compress(sub, pet, peb, wt, wb, w2k, w2vt):
    bsz, nc, kk = sub.shape
    n_h = wt.shape[1]
    n_o = w2k.shape[1]
    const = lambda b: (0, 0)
    return pl.pallas_call(
        _cmp_kernel,
        out_shape=(jax.ShapeDtypeStruct((bsz, nc, n_o), BF16), jax.ShapeDtypeStruct((bsz, n_o, nc), BF16)),
        grid=(bsz,),
        in_specs=[pl.BlockSpec((1, nc, kk), lambda b: (b, 0, 0)),
                  pl.BlockSpec((1, kk), const), pl.BlockSpec((1, kk), const),
                  pl.BlockSpec((kk, n_h), const), pl.BlockSpec((kk, n_h), const),
                  pl.BlockSpec((n_h, n_o), const), pl.BlockSpec((n_o, n_h), const)],
        out_specs=(pl.BlockSpec((1, nc, n_o), lambda b: (b, 0, 0)), pl.BlockSpec((1, n_o, nc), lambda b: (b, 0, 0))),
        compiler_params=_cparams(("parallel",)),
        name="nsa_compress",
    )(sub, pet, peb, wt, wb, w2k, w2vt)


def _flash_reset(m_scr, acc_scr):
    m_scr[...] = jnp.full(m_scr.shape, NEG, F32)
    acc_scr[...] = jnp.zeros(acc_scr.shape, F32)


def _qk(kget, qxs, start, size):
    k0 = kget(0, start, size)
    if kget(1, start, size) is None:
        return _dot(k0, jnp.concatenate(qxs, axis=1))
    return jnp.concatenate([_dot(k0, qxs[0])] + [_dot(kget(c, start, size), qxs[c])
                                                 for c in range(1, len(qxs))], axis=1)


def _softmax_pv(s, vget, nq, start, size, m_scr, acc_scr, bias=None):
    if bias is not None:
        s = s + jnp.concatenate([bias] * nq, axis=1)
    m_prev = m_scr[...]
    m_new = jnp.maximum(m_prev, jnp.max(s, axis=0, keepdims=True))
    alpha = jnp.exp(m_prev - m_new)
    p = jnp.exp(s - m_new).astype(BF16)
    v0 = vget(0, start, size)
    if vget(1, start, size) is None:
        pv = _dot(v0, p)
    else:
        pv = jnp.concatenate([_dot(v0, p[:, :TQ])] + [_dot(vget(c, start, size), p[:, c * TQ:(c + 1) * TQ])
                                                      for c in range(1, nq)], axis=1)
    acc_scr[...] = alpha * acc_scr[...] + pv
    m_scr[...] = m_new


def _rows(ref, start, size, lane0=0):
    return ref[pl.ds(pl.multiple_of(start, TQ), size), lane0:lane0 + SLOT]


def _cols(ref, start, size, row0=0):
    return ref[row0:row0 + PV_ROWS, pl.ds(pl.multiple_of(start, TQ), size)]


def _to_token_rows(o_t):
    pad = jnp.zeros((SLOT - PV_ROWS, o_t.shape[1]), o_t.dtype)
    return jnp.concatenate([o_t, pad], axis=0).T


def _flash_causal(kget, vget, qxs, i, cb_ref, sa_scr, sb_scr, m_scr, acc_scr):
    tq = TQ
    tk = 2 * tq
    nq = len(qxs)
    n_steps = i // 2 + 1
    sa_scr[...] = _qk(kget, qxs, 0, tk)

    def body(jj, carry):
        j = 2 * jj
        s = sa_scr[...]
        sb_scr[...] = _qk(kget, qxs, (j + 1) * tk, tk)
        _softmax_pv(s, vget, nq, j * tk, tk, m_scr, acc_scr)
        s = sb_scr[...]
        sa_scr[...] = _qk(kget, qxs, (j + 2) * tk, tk)
        _softmax_pv(s, vget, nq, (j + 1) * tk, tk, m_scr, acc_scr)
        return carry

    lax.fori_loop(0, (n_steps - 1) // 2, body, 0)
    last = n_steps - 1
    bias = cb_ref[i % 2]

    @pl.when(last % 2 == 0)
    def _():
        _softmax_pv(sa_scr[...], vget, nq, last * tk, tk, m_scr, acc_scr, bias=bias)

    @pl.when(last % 2 == 1)
    def _():
        s = sa_scr[...]
        sb_scr[...] = _qk(kget, qxs, last * tk, tk)
        _softmax_pv(s, vget, nq, (last - 1) * tk, tk, m_scr, acc_scr)
        _softmax_pv(sb_scr[...], vget, nq, last * tk, tk, m_scr, acc_scr, bias=bias)


def _nsa_kernel(q_ref, kc_ref, vct_ref, ks_ref, vst_ref, kw_ref, vwt_ref, g_ref, ovt_ref, fq_ref, cb_ref, wb_ref,
                o_ref, m_scr, acc_scr, imp_scr, sa_scr, sb_scr, *, tq, nsel):
    i = pl.program_id(2)
    r4 = NSA_GROUP
    rr = r4 * tq
    q4 = [q_ref[r * SLOT:(r + 1) * SLOT, :] for r in range(r4)]
    fq = fq_ref[0]

    kc = kc_ref[0]
    nc = kc.shape[0]
    qs = jnp.concatenate(q4, axis=1)
    s = _dot(kc, qs)
    t_row = i * tq + (lax.broadcasted_iota(I32, (1, rr), 1) & (tq - 1))
    last_c = (t_row - (CMP_BLOCK - 1)) >> 4
    cmask = lax.broadcasted_iota(I32, (nc, rr), 0) <= last_c
    s = jnp.where(cmask, s, NEG)
    mx = jnp.max(s, axis=0, keepdims=True)
    e = jnp.where(cmask, jnp.exp(s - mx), 0.0)
    lsum = jnp.sum(e, axis=0, keepdims=True)
    p_cmp = (e * (1.0 / jnp.where(lsum > 0.0, lsum, 1.0))).astype(BF16)
    o_cmp = _dot(vct_ref[0], p_cmp)

    ovt = ovt_ref[...]
    imp = _dot(ovt, p_cmp[:, 0:tq])
    for r in range(1, r4):
        imp = imp + _dot(ovt, p_cmp[:, r * tq:(r + 1) * tq])
    nbp = imp.shape[0]
    jidx = lax.broadcasted_iota(I32, (nbp, tq), 0)
    cur = (i * tq + lax.broadcasted_iota(I32, (nbp, tq), 1)) // SLC_BLOCK
    forced = (jidx == 0) | (jidx == cur) | (jidx == cur - 1)
    imp = jnp.where(forced, 1e6, jnp.where(jidx > cur, -1e6, imp))
    imp_scr[...] = imp

    def rank_body(jp, rank):
        row = imp_scr[pl.ds(jp, 1), :]
        better = (row > imp) | ((row == imp) & (jp < jidx))
        return rank + better.astype(I32)

    n_live = (i + 1) * (tq // SLC_BLOCK)
    rank = lax.fori_loop(0, n_live, rank_body, jnp.zeros((nbp, tq), I32))
    sel = (rank < nsel) & (jidx <= cur)
    mask_t = jnp.where(sel, 0.0, NEG)
    feat = jnp.concatenate([jnp.zeros((SLOT - nbp, tq), F32), mask_t], axis=0)

    qxs = [q4[r] + (feat + fq[:, r:r + 1]).astype(BF16) for r in range(r4)]
    kget = lambda c, start, size: _rows(ks_ref, start, size) if c == 0 else None
    vget = lambda c, start, size: _cols(vst_ref, start, size) if c == 0 else None
    _flash_reset(m_scr, acc_scr)
    _flash_causal(kget, vget, qxs, i, cb_ref, sa_scr, sb_scr, m_scr, acc_scr)
    o_slc = acc_scr[...] * (1.0 / acc_scr[HEAD_DIM:HEAD_DIM + 1, :])

    qx = jnp.concatenate([q4[r] + fq[:, r4 + r:r4 + r + 1].astype(BF16) for r in range(r4)], axis=1)
    w0 = jnp.maximum(i - 2, 0) * tq
    s = _dot(_rows(kw_ref, w0, 3 * tq), qx)
    s = s + jnp.concatenate([wb_ref[jnp.minimum(i, 2)]] * r4, axis=1)
    e = jnp.exp(s - jnp.max(s, axis=0, keepdims=True)).astype(BF16)
    o_win = _dot(_cols(vwt_ref, w0, 3 * tq), e)
    o_win = o_win * (1.0 / o_win[HEAD_DIM:HEAD_DIM + 1, :])

    gate = jax.nn.sigmoid(g_ref[...])
    for r in range(r4):
        sl = slice(r * tq, (r + 1) * tq)
        o = (gate[3 * r:3 * r + 1, :] * o_cmp[:PV_ROWS, sl] + gate[3 * r + 1:3 * r + 2, :] * o_slc[:, sl]
             + gate[3 * r + 2:3 * r + 3, :] * o_win[:, sl])
        o_ref[:, r * SLOT:(r + 1) * SLOT] = _to_token_rows(o).astype(o_ref.dtype)


def _nsa_attention(zat, zb, gt, kc, vct, ovt, fq, cb, wb, bsz, seq):
    tq = TQ
    ni = seq // tq
    nc = kc.shape[1]
    nsel = min(SLC_TOPK, seq // SLC_BLOCK)
    assert seq // SLC_BLOCK <= 64 and WINDOW == 2 * tq and ni % 2 == 0 and ni >= 3
    rr = NSA_GROUP * tq
    gw = NSA_GROUP * SLOT
    in_specs = [
        pl.BlockSpec((gw, tq), lambda b, g, i: (g, b * ni + i)),
        pl.BlockSpec((1, nc, SLOT), lambda b, g, i: (b, 0, g)),
        pl.BlockSpec((1, SLOT, nc), lambda b, g, i: (b, g, 0)),
        pl.BlockSpec((seq, SLOT), lambda b, g, i: (b, g)),
        pl.BlockSpec((SLOT, seq), lambda b, g, i: (_SLOT_SLC_V + g, b)),
        pl.BlockSpec((seq, SLOT), lambda b, g, i: (b, 2 + g)),
        pl.BlockSpec((SLOT, seq), lambda b, g, i: (_SLOT_WIN_V + g, b)),
        pl.BlockSpec((SLOT, tq), lambda b, g, i: (g, b * ni + i)),
        pl.BlockSpec(ovt.shape, lambda b, g, i: (0, 0)),
        pl.BlockSpec((1, SLOT, 8), lambda b, g, i: (g, 0, 0)),
        pl.BlockSpec(cb.shape, lambda b, g, i: (0, 0, 0)),
        pl.BlockSpec(wb.shape, lambda b, g, i: (0, 0, 0)),
    ]
    return pl.pallas_call(
        functools.partial(_nsa_kernel, tq=tq, nsel=nsel),
        out_shape=jax.ShapeDtypeStruct((bsz * seq, NSA_HEADS * SLOT), BF16),
        grid=(bsz, NSA_KV_HEADS, ni),
        in_specs=in_specs,
        out_specs=pl.BlockSpec((tq, gw), lambda b, g, i: (b * ni + i, g)),
        scratch_shapes=[pltpu.VMEM((1, rr), F32),
                        pltpu.VMEM((PV_ROWS, rr), F32), pltpu.VMEM((64, tq), F32),
                        pltpu.VMEM((2 * tq, rr), F32), pltpu.VMEM((2 * tq, rr), F32)],
        compiler_params=_cparams(("parallel", "parallel", "arbitrary")),
        name="nsa_attention",
    )(zat, kc, vct, zb, zat, zb, zat, gt, ovt, fq, cb, wb)


def _moba_kernel(q_ref, k_ref, vt_ref, fq_ref, cb_ref, o_ref, m_scr, acc_scr, km_scr, gate_scr,
                 sa_scr, sb_scr, *, tq, nblk, ntop):
    i = pl.program_id(2)
    nh = MOBA_HPS

    @pl.when(i == 0)
    def _():
        for hh in range(nh):
            kh = k_ref[:, hh * SLOT:(hh + 1) * SLOT].astype(F32)
            km = jnp.mean(kh.reshape(nblk, tq, SLOT), axis=1)
            if nblk < 16:
                km = jnp.concatenate([km, jnp.zeros((16 - nblk, SLOT), F32)], axis=0)
            km_scr[hh] = km

    qs = [q_ref[hh * SLOT:(hh + 1) * SLOT, :] for hh in range(nh)]
    gates = [_dot(km_scr[hh].astype(BF16), qs[hh]) for hh in range(nh)]
    for hh in range(nh):
        gate_scr[hh] = gates[hh]
    jidx = lax.broadcasted_iota(I32, (16, tq), 0)

    def rank_body(jp, ranks):
        out = []
        for hh in range(nh):
            row = gate_scr[hh, pl.ds(jp, 1), :]
            better = (row > gates[hh]) | ((row == gates[hh]) & (jp < jidx))
            out.append(ranks[hh] + better.astype(I32))
        return tuple(out)

    ranks = lax.fori_loop(0, i, rank_body, tuple(jnp.zeros((16, tq), I32) for _ in range(nh)))
    qxs = []
    for hh in range(nh):
        keep = ((jidx < i) & (ranks[hh] < ntop)) | (jidx == i)
        mask_t = jnp.where(keep, 0.0, NEG)
        feat = jnp.concatenate([jnp.zeros((HEAD_DIM, tq), F32), mask_t,
                                jnp.zeros((SLOT - HEAD_DIM - 16, tq), F32)], axis=0)
        qxs.append(qs[hh] + (feat + fq_ref[hh][:, 0:1]).astype(BF16))
    kget = lambda c, start, size: _rows(k_ref, start, size, c * SLOT)
    vget = lambda c, start, size: _cols(vt_ref, start, size, c * SLOT)
    _flash_reset(m_scr, acc_scr)
    _flash_causal(kget, vget, qxs, i, cb_ref, sa_scr, sb_scr, m_scr, acc_scr)
    o = acc_scr[...] * (1.0 / acc_scr[HEAD_DIM:HEAD_DIM + 1, :])
    for hh in range(nh):
        o_ref[:, hh * SLOT:(hh + 1) * SLOT] = _to_token_rows(o[:, hh * tq:(hh + 1) * tq]).astype(o_ref.dtype)


def _moba_attention(zat, zb, fq, cb, bsz, seq):
    tq = TQ
    assert tq == MOBA_BLOCK and seq % tq == 0
    ni = seq // tq
    assert ni <= 16 and ni % 2 == 0
    ntop = min(MOBA_TOPK, ni)
    nh = MOBA_HPS
    hw = nh * SLOT
    q0 = _SLOT_MOBA_Q // nh
    v0 = _SLOT_MOBA_V // nh
    return pl.pallas_call(
        functools.partial(_moba_kernel, tq=tq, nblk=ni, ntop=ntop),
        out_shape=jax.ShapeDtypeStruct((bsz * seq, MOBA_HEADS * SLOT), BF16),
        grid=(bsz, MOBA_HEADS // nh, ni),
        in_specs=[pl.BlockSpec((hw, tq), lambda b, h, i: (q0 + h, b * ni + i)),
                  pl.BlockSpec((seq, hw), lambda b, h, i: (b, 1 + h)),
                  pl.BlockSpec((hw, seq), lambda b, h, i: (v0 + h, b)),
                  pl.BlockSpec((nh, SLOT, 8), lambda b, h, i: (h, 0, 0)),
                  pl.BlockSpec(cb.shape, lambda b, h, i: (0, 0, 0))],
        out_specs=pl.BlockSpec((tq, hw), lambda b, h, i: (b * ni + i, h)),
        scratch_shapes=[pltpu.VMEM((1, nh * tq), F32),
                        pltpu.VMEM((PV_ROWS, nh * tq), F32),
                        pltpu.VMEM((nh, 16, SLOT), F32), pltpu.VMEM((nh, 16, tq), F32),
                        pltpu.VMEM((2 * tq, nh * tq), F32), pltpu.VMEM((2 * tq, nh * tq), F32)],
        compiler_params=_cparams(("parallel", "parallel", "arbitrary")),
        name="moba_attention",
    )(zat, zb, zat, fq, cb)


def _merge_kernel(ya_ref, yb_ref, hc_ref, bc_ref, cc_ref, hcp_ref, ccp_ref, g0_ref, g1_ref, g2_ref,
                  h_ref, cw_ref, wa_ref, wb_ref, wc_ref, wo_ref, lg_ref, lb_ref, o_ref, *, tm, per, alpha):
    i = pl.program_id(0)
    u = cc_ref[...].astype(F32) * hc_ref[...].astype(F32)
    first = (i % per) == 0
    up = jnp.where(first, 0.0, ccp_ref[...].astype(F32) * hcp_ref[...].astype(F32))
    rowi = lax.broadcasted_iota(I32, u.shape, 0)
    u1 = jnp.where(rowi == 0, up[15:16, :], pltpu.roll(u, 1, 0))
    u2 = jnp.where(rowi == 0, up[14:15, :], jnp.where(rowi == 1, up[15:16, :], pltpu.roll(u, 2, 0)))
    cw = cw_ref[...]
    yc = bc_ref[...].astype(F32) * (cw[0:1, :] * u2 + cw[1:2, :] * u1 + cw[2:3, :] * u)
    merged = (jax.nn.sigmoid(g0_ref[...].astype(F32)) * _dot(ya_ref[...], wa_ref[...])
              + jax.nn.sigmoid(g1_ref[...].astype(F32)) * _dot(yb_ref[...], wb_ref[...])
              + jax.nn.sigmoid(g2_ref[...].astype(F32)) * _dot(yc.astype(BF16), wc_ref[...]))
    mix = _dot(merged.astype(BF16), wo_ref[...])
    o_ref[...] = _layer_norm(alpha * h_ref[...] + mix, lg_ref[...], lb_ref[...])


def _merge(ya, yb, zc, h, cw, wa, wb, wc, wo, lg, lb, seq, alpha):
    tm = 512
    t = h.shape[0]
    per = seq // tm
    d = D_MODEL
    row = lambda i: (i, 0)
    const = lambda i: (0, 0)
    prev = lambda c: (lambda i: (jnp.maximum(i * (tm // 16) - 1, 0), c))
    in_specs = [
        pl.BlockSpec((tm, d), row), pl.BlockSpec((tm, d), row),
        pl.BlockSpec((tm, CONV_CH), lambda i: (i, 1)), pl.BlockSpec((tm, CONV_CH), lambda i: (i, 2)),
        pl.BlockSpec((tm, CONV_CH), lambda i: (i, 3)),
        pl.BlockSpec((16, CONV_CH), prev(1)), pl.BlockSpec((16, CONV_CH), prev(3)),
        pl.BlockSpec((tm, d), lambda i: (i, 2)), pl.BlockSpec((tm, d), lambda i: (i, 3)),
        pl.BlockSpec((tm, d), lambda i: (i, 4)),
        pl.BlockSpec((tm, d), row),
        pl.BlockSpec((8, CONV_CH), const),
        pl.BlockSpec((d, d), const), pl.BlockSpec((d, d), const), pl.BlockSpec((CONV_CH, d), const),
        pl.BlockSpec((d, d), const), pl.BlockSpec((1, d), const), pl.BlockSpec((1, d), const),
    ]
    return pl.pallas_call(
        functools.partial(_merge_kernel, tm=tm, per=per, alpha=alpha),
        out_shape=jax.ShapeDtypeStruct((t, d), F32),
        grid=(t // tm,),
        in_specs=in_specs,
        out_specs=pl.BlockSpec((tm, d), row),
        compiler_params=_cparams(("parallel",)),
        name="mixer_merge",
    )(ya, yb, zc, zc, zc, zc, zc, zc, zc, zc, h, cw, wa, wb, wc, wo, lg, lb)


def _xattn_kernel(h_ref, kv_ref, wq_ref, wo_ref, lg_ref, lb_ref, o_ref, ob_ref, *, alpha):
    h = h_ref[...]
    q = _dot(h.astype(BF16), wq_ref[...]).astype(BF16)
    kv = kv_ref[...]
    nh = XATTN_HEADS
    hd = XATTN_HEAD_DIM
    outs = []
    for hh in range(nh):
        s = _dot_t(q[:, hh * hd:(hh + 1) * hd], kv[:, hh * hd:(hh + 1) * hd]) * (hd ** -0.5)
        s = s - jnp.max(s, axis=-1, keepdims=True)
        e = jnp.exp(s)
        p = e / jnp.sum(e, axis=-1, keepdims=True)
        outs.append(_dot(p.astype(BF16), kv[:, (nh + hh) * hd:(nh + hh + 1) * hd]))
    o = jnp.concatenate(outs, axis=-1).astype(BF16)
    y = _layer_norm(alpha * h + _dot(o, wo_ref[...]), lg_ref[...], lb_ref[...])
    o_ref[...] = y
    bits = lax.bitcast_convert_type(y.astype(BF16).astype(F32), I32)
    half = y.shape[1] // 2
    ob_ref[...] = (bits[:, half:] & -65536) | lax.shift_right_logical(bits[:, :half], 16)


def _unpack_pairs(words):
    lo = lax.bitcast_convert_type(lax.shift_left(words, 16), F32).astype(BF16)
    hi = lax.bitcast_convert_type(words & -65536, F32).astype(BF16)
    return lo, hi


def _xattn(h, kv, wq, wo, lg, lb, seq, mlen, alpha):
    tm = 512
    t = h.shape[0]
    per = seq // tm
    d = D_MODEL
    row = lambda i: (i, 0)
    const = lambda i: (0, 0)
    return pl.pallas_call(
        functools.partial(_xattn_kernel, alpha=alpha),
        out_shape=(jax.ShapeDtypeStruct((t, d), F32), jax.ShapeDtypeStruct((t, d // 2), I32)),
        grid=(t // tm,),
        in_specs=[pl.BlockSpec((tm, d), row), pl.BlockSpec((mlen, kv.shape[1]), lambda i: (i // per, 0)),
                  pl.BlockSpec(wq.shape, const), pl.BlockSpec(wo.shape, const),
                  pl.BlockSpec((1, d), const), pl.BlockSpec((1, d), const)],
        out_specs=(pl.BlockSpec((tm, d), row), pl.BlockSpec((tm, d // 2), row)),
        compiler_params=_cparams(("parallel",)),
        name="mem_xattn",
    )(h, kv, wq, wo, lg, lb)


def _expert_kernel(be_ref, nu_ref, nxt_ref, slot_ref, x_ref, w13_hbm, w2_hbm, o_ref,
                   w13f_scr, w2f_scr, w13b_scr, w2b_scr, sem, *, layer):
    i = pl.program_id(0)
    e = be_ref[i]
    s = slot_ref[i]

    def weight_copies(expert, slot):
        return (pltpu.make_async_copy(w13_hbm.at[layer, expert], w13f_scr.at[slot], sem.at[0, slot]),
                pltpu.make_async_copy(w2_hbm.at[layer, expert], w2f_scr.at[slot], sem.at[1, slot]))

    @pl.when(i == 0)
    def _():
        for cp in weight_copies(e, s):
            cp.start()

    @pl.when((i == 0) | (e != be_ref[jnp.maximum(i - 1, 0)]))
    def _():
        for cp in weight_copies(e, s):
            cp.wait()
        nxt = nxt_ref[i]

        @pl.when(nxt >= 0)
        def _():
            for cp in weight_copies(nxt, 1 - s):
                cp.start()

        w13b_scr[...] = w13f_scr[s].astype(BF16)
        w2b_scr[...] = w2f_scr[s].astype(BF16)

    @pl.when(i < nu_ref[0])
    def _():
        x_lo, x_hi = _unpack_pairs(x_ref[...])
        half = x_lo.shape[1]
        hmid = _dot(x_lo, w13b_scr[:half, :]) + _dot(x_hi, w13b_scr[half:, :])
        a = hmid[:, :EXPERT_DIM]
        g = hmid[:, EXPERT_DIM:]
        act = (a * jax.nn.sigmoid(a) * g).astype(BF16)
        o_ref[...] = _dot(act, w2b_scr[...]).astype(o_ref.dtype)


def _experts(xg, w13, w2, block_e, n_used, next_e, slot, layer):
    n_pad = xg.shape[0]
    d = 2 * xg.shape[1]
    rows = EXP_ROWS
    nb = n_pad // rows
    blk = lambda i, be, nu, nx, sl: (jnp.minimum(i, nu[0] - 1), 0)
    grid_spec = pltpu.PrefetchScalarGridSpec(
        num_scalar_prefetch=4,
        grid=(nb,),
        in_specs=[pl.BlockSpec((rows, d // 2), blk),
                  pl.BlockSpec(memory_space=pl.ANY), pl.BlockSpec(memory_space=pl.ANY)],
        out_specs=pl.BlockSpec((rows, d), blk),
        scratch_shapes=[pltpu.VMEM((2, d, 2 * EXPERT_DIM), F32), pltpu.VMEM((2, EXPERT_DIM, d), F32),
                        pltpu.VMEM((d, 2 * EXPERT_DIM), BF16), pltpu.VMEM((EXPERT_DIM, d), BF16),
                        pltpu.SemaphoreType.DMA((2, 2))],
    )
    return pl.pallas_call(
        functools.partial(_expert_kernel, layer=layer),
        out_shape=jax.ShapeDtypeStruct((n_pad, d), BF16),
        grid_spec=grid_spec,
        compiler_params=_cparams(("arbitrary",)),
        name="moe_experts",
    )(block_e, n_used, next_e, slot, xg, w13, w2)


def _moe_out_kernel(h_ref, y8_ref, wk_ref, w13_ref, w2_ref, lg_ref, lb_ref, o_ref, ob_ref, *, alpha):
    h = h_ref[...]
    hmid = _dot(h.astype(BF16), w13_ref[...])
    a = hmid[:, :EXPERT_DIM]
    g = hmid[:, EXPERT_DIM:]
    acc = alpha * h + _dot((a * jax.nn.sigmoid(a) * g).astype(BF16), w2_ref[...])
    wk = wk_ref[...]
    for k in range(TOP_K):
        acc = acc + wk[:, k:k + 1] * y8_ref[k].astype(F32)
    y = _layer_norm(acc, lg_ref[...], lb_ref[...])
    o_ref[...] = y
    ob_ref[...] = y.astype(BF16)


def _moe_out(h, y8, wk, w13, w2, lg, lb, alpha):
    tm = 256
    t, d = h.shape
    row = lambda i: (i, 0)
    const = lambda i: (0, 0)
    return pl.pallas_call(
        functools.partial(_moe_out_kernel, alpha=alpha),
        out_shape=(jax.ShapeDtypeStruct((t, d), F32), jax.ShapeDtypeStruct((t, d), BF16)),
        grid=(t // tm,),
        in_specs=[pl.BlockSpec((tm, d), row), pl.BlockSpec((TOP_K, tm, d), lambda i: (0, i, 0)),
                  pl.BlockSpec((tm, TOP_K), row),
                  pl.BlockSpec(w13.shape, const), pl.BlockSpec(w2.shape, const),
                  pl.BlockSpec((1, d), const), pl.BlockSpec((1, d), const)],
        out_specs=(pl.BlockSpec((tm, d), row), pl.BlockSpec((tm, d), row)),
        compiler_params=_cparams(("parallel",)),
        name="moe_shared_ln",
    )(h, y8, wk, w13, w2, lg, lb)


def _dest_kernel(idx_ref, rank_ref, ps_ref, o_ref):
    idx = idx_ref[...]
    ps = ps_ref[...]
    tm = idx.shape[1]
    eidx = lax.broadcasted_iota(I32, (N_EXPERTS, tm), 0)
    rows = [jnp.sum(jnp.where(eidx == idx[k:k + 1, :], ps, 0.0), axis=0, keepdims=True) for k in range(TOP_K)]
    o_ref[...] = jnp.concatenate(rows, axis=0).astype(I32) + rank_ref[...]


def _dest(idx, rank, pstarts):
    tm = 1024
    t = idx.shape[1]
    tm = min(tm, t)
    col = lambda i: (0, i)
    return pl.pallas_call(
        _dest_kernel,
        out_shape=jax.ShapeDtypeStruct((TOP_K, t), I32),
        grid=(t // tm,),
        in_specs=[pl.BlockSpec((TOP_K, tm), col), pl.BlockSpec((TOP_K, tm), col),
                  pl.BlockSpec((N_EXPERTS, 1), lambda i: (0, 0))],
        out_specs=pl.BlockSpec((TOP_K, tm), col),
        compiler_params=_cparams(("parallel",)),
        name="moe_dest",
    )(idx, rank, pstarts)


def _router_kernel(h_ref, rwt_ref, rb_ref, tri_ref, idx_ref, w_ref, rank_ref, cnt_ref, carry_scr, *, tm):
    i = pl.program_id(0)

    @pl.when(i == 0)
    def _():
        carry_scr[...] = jnp.zeros(carry_scr.shape, F32)

    ne = N_EXPERTS
    gsz = ne // N_GROUPS
    s = jax.nn.sigmoid(_dot_t(rwt_ref[...], h_ref[...].astype(BF16)))
    sb = s + rb_ref[...]
    sb3 = sb.reshape(N_GROUPS, gsz, tm)
    li = lax.broadcasted_iota(I32, (N_GROUPS, gsz, tm), 1)
    m1 = jnp.max(sb3, axis=1, keepdims=True)
    first = jnp.min(jnp.where(sb3 == m1, li, gsz), axis=1, keepdims=True)
    m2 = jnp.max(jnp.where(li == first, -jnp.inf, sb3), axis=1, keepdims=True)
    gs = (m1 + m2).reshape(N_GROUPS, tm)
    gi = lax.broadcasted_iota(I32, (N_GROUPS, tm), 0)
    grank = jnp.zeros((N_GROUPS, tm), I32)
    for gp in range(N_GROUPS):
        row = gs[gp:gp + 1, :]
        grank = grank + ((row > gs) | ((row == gs) & (gp < gi))).astype(I32)
    gkeep = (grank < TOPK_GROUPS).astype(F32)
    ekeep = jnp.broadcast_to(gkeep[:, None, :], (N_GROUPS, gsz, tm)).reshape(ne, tm)
    cand = jnp.where(ekeep > 0.0, sb, NEG)
    eidx = lax.broadcasted_iota(I32, (ne, tm), 0)
    sel = jnp.zeros((ne, tm), F32)
    idxs, wts = [], []
    for _ in range(TOP_K):
        mx = jnp.max(cand, axis=0, keepdims=True)
        ik = jnp.min(jnp.where(cand == mx, eidx, ne), axis=0, keepdims=True)
        hit = eidx == ik
        wts.append(jnp.sum(jnp.where(hit, s, 0.0), axis=0, keepdims=True))
        idxs.append(ik)
        sel = jnp.where(hit, 1.0, sel)
        cand = jnp.where(hit, -jnp.inf, cand)
    before = _dot(sel.astype(BF16), tri_ref[...]) + carry_scr[...]
    ranks = [jnp.sum(jnp.where(eidx == ik, before, 0.0), axis=0, keepdims=True) for ik in idxs]
    carry_scr[...] = carry_scr[...] + jnp.sum(sel, axis=1, keepdims=True)
    w = jnp.concatenate(wts, axis=0)
    idx_ref[...] = jnp.concatenate(idxs, axis=0)
    w_ref[...] = w / jnp.sum(w, axis=0, keepdims=True) * ROUTE_SCALE
    rank_ref[...] = jnp.concatenate(ranks, axis=0).astype(I32)
    cnt_ref[...] = jnp.broadcast_to(carry_scr[...], cnt_ref.shape)


def _router(h, rwt, rb):
    tm = 256
    t, d = h.shape
    tri = jnp.asarray(np.triu(np.ones((tm, tm), np.float32), 1)).astype(BF16)
    const = lambda i: (0, 0)
    col = lambda i: (0, i)
    return pl.pallas_call(
        functools.partial(_router_kernel, tm=tm),
        out_shape=(jax.ShapeDtypeStruct((TOP_K, t), I32), jax.ShapeDtypeStruct((TOP_K, t), F32),
                   jax.ShapeDtypeStruct((TOP_K, t), I32), jax.ShapeDtypeStruct((N_EXPERTS, SLOT), F32)),
        grid=(t // tm,),
        in_specs=[pl.BlockSpec((tm, d), lambda i: (i, 0)), pl.BlockSpec((N_EXPERTS, d), const),
                  pl.BlockSpec((N_EXPERTS, 1), const), pl.BlockSpec((tm, tm), const)],
        out_specs=(pl.BlockSpec((TOP_K, tm), col), pl.BlockSpec((TOP_K, tm), col),
                   pl.BlockSpec((TOP_K, tm), col), pl.BlockSpec((N_EXPERTS, SLOT), const)),
        scratch_shapes=[pltpu.VMEM((N_EXPERTS, 1), F32)],
        compiler_params=_cparams(("arbitrary",)),
        name="moe_router",
    )(h, rwt, rb, tri)


def _dispatch(hp, dest_flat, n_pad):
    t, w = hp.shape
    per_w = t // (SC_CORES * SC_SUBCORES)
    assert per_w * SC_CORES * SC_SUBCORES == t and per_w % SC_CHUNK == 0 and dest_flat.shape[0] == TOP_K * t
    mesh = plsc.VectorSubcoreMesh(core_axis_name="c", subcore_axis_name="s")

    @functools.partial(
        pl.kernel, out_type=jax.ShapeDtypeStruct((n_pad, w), I32), mesh=mesh,
        scratch_types=[pltpu.VMEM((SC_CHUNK,), I32)] * TOP_K
        + [pltpu.VMEM((SC_CHUNK, w), I32), pltpu.SemaphoreType.DMA],
        name="moe_dispatch")
    def scatter_rows(hp_hbm, dest_hbm, xg_hbm, *scratch):
        idx_vs, rows_v, sem = scratch[:TOP_K], scratch[TOP_K], scratch[TOP_K + 1]
        wid = lax.axis_index("s") * SC_CORES + lax.axis_index("c")
        base = wid * per_w

        @pl.loop(0, per_w // SC_CHUNK)
        def _(j):
            t0 = pl.multiple_of(base + j * SC_CHUNK, SC_CHUNK)
            pltpu.sync_copy(hp_hbm.at[pl.ds(t0, SC_CHUNK)], rows_v)
            for k in range(TOP_K):
                pltpu.sync_copy(dest_hbm.at[pl.ds(pl.multiple_of(k * t + t0, SC_CHUNK), SC_CHUNK)], idx_vs[k])
            copies = [pltpu.async_copy(rows_v, xg_hbm.at[idx_vs[k]], sem) for k in range(TOP_K)]
            for cp in copies:
                cp.wait()

    return scatter_rows(hp, dest_flat)


def _moe(h, hp, router_w, router_b, exp_w13, exp_w2, layer, shared_w13, shared_w2, lg, lb, alpha):
    t, d = h.shape
    idx, wts, rank, cnt = _router(h, router_w.T, router_b.astype(F32).reshape(N_EXPERTS, 1))
    rows = EXP_ROWS
    n_a = t * TOP_K
    counts = cnt[:, 0].astype(I32)
    pcounts = (counts + rows - 1) // rows * rows
    pends = jnp.cumsum(pcounts)
    pstarts = pends - pcounts
    dest = _dest(idx, rank, pstarts.astype(F32).reshape(N_EXPERTS, 1))
    n_blocks = -(-n_a // rows) + N_EXPERTS
    n_pad = n_blocks * rows
    n_used = (pends[-1:] // rows).astype(I32)
    first_row = jnp.minimum(jnp.arange(n_blocks, dtype=I32), n_used - 1) * rows
    block_e = jnp.sum((pends[None, :] <= first_row[:, None]).astype(I32), axis=1)
    eidx = jnp.arange(N_EXPERTS, dtype=I32)
    has_rows = pcounts > 0
    at_or_after = lax.cummin(jnp.where(has_rows, eidx, N_EXPERTS), axis=0, reverse=True)
    after = jnp.concatenate([at_or_after[1:], jnp.full((1,), N_EXPERTS, I32)])
    next_e = jnp.take(jnp.where(after < N_EXPERTS, after, -1), block_e)
    slot = jnp.take((jnp.cumsum(has_rows.astype(I32)) - 1) % 2, block_e).astype(I32)
    xg = _dispatch(hp, dest.reshape(-1), n_pad)
    out = _experts(xg, exp_w13, exp_w2, block_e, n_used, next_e.astype(I32), slot, layer)
    y8 = jnp.take(out, dest, axis=0, mode="clip")
    return _moe_out(h, y8, wts.T, shared_w13, shared_w2, lg, lb, alpha)


def _pad_slots(w, scale=1.0):
    dm = w.shape[0]
    nh = w.shape[1] // HEAD_DIM
    w = (w * scale).reshape(dm, nh, HEAD_DIM)
    return jnp.concatenate([w, jnp.zeros_like(w)], axis=-1).reshape(dm, nh * SLOT)


def _pad_rows(w):
    n = w.shape[1]
    nh = w.shape[0] // HEAD_DIM
    w = w.reshape(nh, HEAD_DIM, n)
    return jnp.concatenate([w, jnp.zeros_like(w)], axis=1).reshape(nh * SLOT, n)


def _alibi(n):
    return np.exp2(-8.0 * np.arange(1, n + 1, dtype=np.float64) / n).astype(np.float32)


def _key_features(seq):
    p = np.arange(seq)
    slc = np.zeros((seq, SLOT), np.float32)
    slc[:, HEAD_DIM] = p % SLC_BLOCK
    blk = p // SLC_BLOCK
    nz = blk > 0
    slc[p[nz], HEAD_DIM + blk[nz]] = 1.0
    win = np.zeros((seq, SLOT), np.float32)
    win[:, HEAD_DIM] = p // 64
    win[:, HEAD_DIM + 1] = p % 64
    mob = np.zeros((seq, SLOT), np.float32)
    mob[p, HEAD_DIM + p // MOBA_BLOCK] = 1.0
    mob[:, HEAD_DIM + 16] = p % MOBA_BLOCK
    return np.concatenate([slc] * 2 + [win] * 2 + [mob] * MOBA_HEADS, axis=1)


def _query_features():
    sl = _alibi(NSA_HEADS)
    nsa = np.zeros((NSA_KV_HEADS, SLOT, 8), np.float32)
    for g in range(NSA_KV_HEADS):
        for r in range(NSA_GROUP):
            s = sl[g * NSA_GROUP + r]
            nsa[g, HEAD_DIM, r] = s
            nsa[g, HEAD_DIM + 1:, r] = s * SLC_BLOCK * np.arange(1, 64)
            nsa[g, HEAD_DIM, NSA_GROUP + r] = s * 64
            nsa[g, HEAD_DIM + 1, NSA_GROUP + r] = s
    sm = _alibi(MOBA_HEADS)
    mob = np.zeros((MOBA_HEADS, SLOT, 8), np.float32)
    for h in range(MOBA_HEADS):
        mob[h, HEAD_DIM:HEAD_DIM + 16, 0] = sm[h] * MOBA_BLOCK * np.arange(16)
        mob[h, HEAD_DIM + 16, 0] = sm[h]
    return nsa, mob


def _mask_biases():
    t = np.arange(TQ)[None, :]
    p2 = np.arange(2 * TQ)[:, None]
    cb = np.stack([p2 <= t, p2 <= TQ + t])
    p3 = np.arange(3 * TQ)[:, None]
    dist = np.stack([c * TQ + t - p3 for c in range(3)])
    wb = (dist >= 0) & (dist < WINDOW)
    to_bias = lambda m: np.where(m, 0.0, NEG).astype(np.float32)
    return to_bias(cb), to_bias(wb)


def _overlap_t(seq):
    nc = seq // CMP_STRIDE
    c_start = np.arange(nc) * CMP_STRIDE
    b_start = np.arange(64) * SLC_BLOCK
    ov = ((c_start[None, :] < (b_start + SLC_BLOCK)[:, None])
          & ((c_start + CMP_BLOCK)[None, :] > b_start[:, None])
          & (np.arange(nc) < nc - 1)[None, :] & (b_start < seq)[:, None])
    return ov.astype(np.float32)


def _compress_weights(pe, w1, w2):
    ty = np.array([0, 0, 1, 1])
    eye = jnp.eye(4, dtype=F32)
    w1r = w1.reshape(2, CMP_BLOCK, HEAD_DIM, CMP_HIDDEN)[ty]
    top = jnp.einsum('spdj,sS->psdSj', w1r[:, :CMP_STRIDE], eye).reshape(CMP_STRIDE * 256, 4 * CMP_HIDDEN)
    bot = jnp.einsum('spdj,sS->psdSj', w1r[:, CMP_STRIDE:], eye).reshape(CMP_STRIDE * 256, 4 * CMP_HIDDEN)
    per = pe[ty]
    pet = jnp.transpose(per[:, :CMP_STRIDE], (1, 0, 2)).reshape(1, CMP_STRIDE * 256)
    peb = jnp.transpose(per[:, CMP_STRIDE:], (1, 0, 2)).reshape(1, CMP_STRIDE * 256)
    w2p = jnp.concatenate([w2[ty], jnp.zeros((4, CMP_HIDDEN, SLOT - HEAD_DIM), F32)], axis=-1)
    w2b = jnp.einsum('sjd,sS->sjSd', w2p, eye).reshape(4 * CMP_HIDDEN, 4 * SLOT)
    w2k = w2b[:, :2 * SLOT].astype(BF16)
    w2vt = w2b[:, 2 * SLOT:].T.astype(BF16)
    return pet, peb, top.astype(BF16), bot.astype(BF16), w2k, w2vt


def kernel(x, mem, w_in, cmp_pe, cmp_w1, cmp_w2, conv_w, w_branch, w_out, ln1_g, ln1_b,
           xattn_wq, xattn_wkv, xattn_wo, ln2_g, ln2_b, router_w, router_b, exp_w13, exp_w2,
           shared_w13, shared_w2, ln3_g, ln3_b):
    bsz, seq, d = x.shape
    mlen = mem.shape[1]
    depth = w_in.shape[0]
    alpha = (2.0 * depth) ** 0.25
    t = bsz * seq
    scale = HEAD_DIM ** -0.5

    kfeat = jnp.asarray(_key_features(seq))
    fq_nsa, fq_moba = (jnp.asarray(a) for a in _query_features())
    cbias, wbias = (jnp.asarray(a) for a in _mask_biases())
    ovt = jnp.asarray(_overlap_t(seq)).astype(BF16)
    memf = mem.reshape(bsz * mlen, d)

    h = x.reshape(t, d)
    hb = h.astype(BF16)
    for l in range(depth):
        ht = hb.T
        wi = w_in[l]
        kv6 = wi[:, _OFF_NSA_KV:_OFF_NSA_G].reshape(d, 6, NSA_KV_HEADS * HEAD_DIM)
        mq, mk, mv = (wi[:, _OFF_MOBA + j * 512:_OFF_MOBA + (j + 1) * 512] for j in range(3))
        w_at = jnp.concatenate([wi[:, :512] * scale, kv6[:, 3], kv6[:, 5], mq * scale, mv],
                               axis=1).T.astype(BF16)
        w_b = jnp.concatenate([_pad_slots(kv6[:, 2]), _pad_slots(kv6[:, 4]), _pad_slots(mk)], axis=1).astype(BF16)
        wg = wi[:, _OFF_NSA_G:_OFF_MOBA].reshape(d, NSA_KV_HEADS, 12)
        wgt = jnp.concatenate([wg, jnp.zeros((d, NSA_KV_HEADS, SLOT - 12), F32)], axis=-1)
        wgt = wgt.reshape(d, NSA_KV_HEADS * SLOT).T.astype(BF16)
        w_c = jnp.concatenate([wi[:, _OFF_NSA_KV:_OFF_NSA_KV + 256], jnp.zeros((d, 256), F32),
                               wi[:, _OFF_CONV:]], axis=1).astype(BF16)

        value_slots = tuple(range(_SLOT_SLC_V, _SLOT_MOBA_Q)) + tuple(range(_SLOT_MOBA_V, _N_SLOTS_T))
        zat = _proj_t(w_at, ht, value_slots)
        gt = _mm(wgt, ht, F32, tm=256, tn=1024, name="proj_gate")
        zb = _mm(hb, w_b, BF16, tn=768, feats=kfeat, name="proj_b")
        zc = _mm(hb, w_c, BF16, tn=1024, name="proj_c")

        pet, peb, wt, wb, w2k, w2vt = _compress_weights(cmp_pe[l], cmp_w1[l], cmp_w2[l])
        sub = zc[:, :256].reshape(bsz, seq // CMP_STRIDE, CMP_STRIDE * 256)
        kc, vct = _compress(sub, pet, peb, wt, wb, w2k, w2vt)

        ya = _nsa_attention(zat, zb, gt, kc, vct, ovt, fq_nsa, cbias, wbias, bsz, seq)
        yb = _moba_attention(zat, zb, fq_moba, cbias, bsz, seq)

        cw = jnp.concatenate([conv_w[l], jnp.zeros((5, CONV_CH), F32)], axis=0)
        h = _merge(ya, yb, zc, h, cw, _pad_rows(w_branch[l, 0]).astype(BF16),
                   _pad_rows(w_branch[l, 1]).astype(BF16), w_branch[l, 2].astype(BF16),
                   w_out[l].astype(BF16), ln1_g[l][None], ln1_b[l][None], seq, alpha)

        kv = _mm(memf, xattn_wkv[l].astype(BF16), BF16, tm=512, name="xattn_kv")
        h, hp = _xattn(h, kv, xattn_wq[l].astype(BF16), xattn_wo[l].astype(BF16),
                       ln2_g[l][None], ln2_b[l][None], seq, mlen, alpha)

        h, hb = _moe(h, hp, router_w[l].astype(BF16), router_b[l], exp_w13, exp_w2, l,
                     shared_w13[l].astype(BF16), shared_w2[l].astype(BF16), ln3_g[l][None], ln3_b[l][None], alpha)
    return h.reshape(bsz, seq, d)
```

```python
import functools

import jax
import jax.numpy as jnp
import numpy as np
from jax import lax
from jax.experimental import pallas as pl
from jax.experimental.pallas import tpu as pltpu
from jax.experimental.pallas import tpu_sc as plsc

F32 = jnp.float32
BF16 = jnp.bfloat16
I32 = jnp.int32

D_MODEL = 1024
HEAD_DIM = 64
SLOT = 128
NEG = -1e30
LN_EPS = 1e-5

NSA_HEADS = 8
NSA_KV_HEADS = 2
NSA_GROUP = 4
CMP_BLOCK = 32
CMP_STRIDE = 16
CMP_HIDDEN = 256
SLC_BLOCK = 64
SLC_TOPK = 16
WINDOW = 512
MOBA_HEADS = 8
MOBA_BLOCK = 256
MOBA_TOPK = 3
CONV_CH = 512
XATTN_HEADS = 4
XATTN_HEAD_DIM = 128
N_EXPERTS = 256
TOP_K = 8
N_GROUPS = 8
TOPK_GROUPS = 4
EXPERT_DIM = 256
ROUTE_SCALE = 2.5

TQ = 256
MOBA_HPS = 4
PV_ROWS = 80
EXP_ROWS = 512
VMEM_LIMIT = 48 * 1024 * 1024
SC_CORES = 2
SC_SUBCORES = 16
SC_CHUNK = 64

_OFF_NSA_Q = 0
_OFF_NSA_KV = 512
_OFF_NSA_G = 1280
_OFF_MOBA = 1304
_OFF_CONV = 2840
_OFF_MERGE = 4376

_SLOT_NSA_Q = 0
_SLOT_SLC_V = 8
_SLOT_WIN_V = 10
_SLOT_MOBA_Q = 12
_SLOT_MOBA_V = 20
_N_SLOTS_T = 28


def _cparams(sem):
    return pltpu.CompilerParams(dimension_semantics=sem, vmem_limit_bytes=VMEM_LIMIT)


def _dot(a, b):
    return jnp.dot(a, b, preferred_element_type=F32)


def _dot_t(a, b):
    return lax.dot_general(a, b, (((1,), (1,)), ((), ())), preferred_element_type=F32)


def _layer_norm(x, g, b):
    mu = jnp.mean(x, axis=-1, keepdims=True)
    xc = x - mu
    var = jnp.mean(xc * xc, axis=-1, keepdims=True)
    return xc * lax.rsqrt(var + LN_EPS) * g + b


def _mm_kernel(x_ref, w_ref, o_ref):
    o_ref[...] = _dot(x_ref[...].astype(BF16), w_ref[...]).astype(o_ref.dtype)


def _mm_feat_kernel(x_ref, w_ref, f_ref, o_ref):
    y = _dot(x_ref[...].astype(BF16), w_ref[...]) + f_ref[...]
    o_ref[...] = y.astype(o_ref.dtype)


def _mm(x, w, out_dtype, *, tm=1024, tn=512, feats=None, name):
    m, k = x.shape
    n = w.shape[1]
    tm = min(tm, m)
    tn = min(tn, n)
    assert m % tm == 0 and n % tn == 0, (m, n, tm, tn)
    in_specs = [pl.BlockSpec((tm, k), lambda i, j: (i, 0)),
                pl.BlockSpec((k, tn), lambda i, j: (0, j))]
    args = [x, w]
    if feats is None:
        body = _mm_kernel
    else:
        per = feats.shape[0] // tm
        assert feats.shape[0] % tm == 0
        in_specs.append(pl.BlockSpec((tm, tn), lambda i, j: (i % per, j)))
        args.append(feats)
        body = _mm_feat_kernel
    return pl.pallas_call(
        body,
        out_shape=jax.ShapeDtypeStruct((m, n), out_dtype),
        grid=(m // tm, n // tn),
        in_specs=in_specs,
        out_specs=pl.BlockSpec((tm, tn), lambda i, j: (i, j)),
        compiler_params=_cparams(("parallel", "parallel")),
        name=name,
    )(*args)


def _proj_t_kernel(w_ref, ht_ref, o_ref, *, value_slots):
    acc = _dot(w_ref[...], ht_ref[...])
    tn = acc.shape[1]
    zero = jnp.zeros((SLOT - HEAD_DIM, tn), o_ref.dtype)
    ones_row = (lax.broadcasted_iota(I32, (SLOT - HEAD_DIM, tn), 0) == 0).astype(o_ref.dtype)
    for s in range(w_ref.shape[0] // HEAD_DIM):
        o_ref[s * SLOT:s * SLOT + HEAD_DIM, :] = acc[s * HEAD_DIM:(s + 1) * HEAD_DIM, :].astype(o_ref.dtype)
        o_ref[s * SLOT + HEAD_DIM:(s + 1) * SLOT, :] = ones_row if s in value_slots else zero


def _proj_t(w, ht, value_slots):
    m, k = w.shape
    t = ht.shape[1]
    tn = min(512, t)
    n_out = m // HEAD_DIM * SLOT
    return pl.pallas_call(
        functools.partial(_proj_t_kernel, value_slots=value_slots),
        out_shape=jax.ShapeDtypeStruct((n_out, t), BF16),
        grid=(t // tn,),
        in_specs=[pl.BlockSpec((m, k), lambda j: (0, 0)), pl.BlockSpec((k, tn), lambda j: (0, j))],
        out_specs=pl.BlockSpec((n_out, tn), lambda j: (0, j)),
        compiler_params=_cparams(("parallel",)),
        name="proj_at",
    )(w, ht)


def _cmp_kernel(sub_ref, pet_ref, peb_ref, wt_ref, wb_ref, w2k_ref, w2vt_ref, kc_ref, vct_ref):
    sub = sub_ref[0].astype(F32)
    nc = sub.shape[0]
    a = _dot((sub + pet_ref[...]).astype(BF16), wt_ref[...])
    b = _dot((sub + peb_ref[...]).astype(BF16), wb_ref[...])
    hid = jax.nn.gelu(a + pltpu.roll(b, nc - 1, 0)).astype(BF16)
    kc_ref[0] = _dot(hid, w2k_ref[...]).astype(kc_ref.dtype)
    vct_ref[0] = _dot_t(w2vt_ref[...], hid).astype(vct_ref.dtype)


def _compress(sub, pet, peb, wt, wb, w2k, w2vt):
    bsz, nc, kk = sub.shape
    n_h = wt.shape[1]
    n_o = w2k.shape[1]
    const = lambda b: (0, 0)
    return pl.pallas_call(
        _cmp_kernel,
        out_shape=(jax.ShapeDtypeStruct((bsz, nc, n_o), BF16), jax.ShapeDtypeStruct((bsz, n_o, nc), BF16)),
        grid=(bsz,),
        in_specs=[pl.BlockSpec((1, nc, kk), lambda b: (b, 0, 0)),
                  pl.BlockSpec((1, kk), const), pl.BlockSpec((1, kk), const),
                  pl.BlockSpec((kk, n_h), const), pl.BlockSpec((kk, n_h), const),
                  pl.BlockSpec((n_h, n_o), const), pl.BlockSpec((n_o, n_h), const)],
        out_specs=(pl.BlockSpec((1, nc, n_o), lambda b: (b, 0, 0)), pl.BlockSpec((1, n_o, nc), lambda b: (b, 0, 0))),
        compiler_params=_cparams(("parallel",)),
        name="nsa_compress",
    )(sub, pet, peb, wt, wb, w2k, w2vt)


def _flash_reset(m_scr, acc_scr):
    m_scr[...] = jnp.full(m_scr.shape, NEG, F32)
    acc_scr[...] = jnp.zeros(acc_scr.shape, F32)


def _qk(kget, qxs, start, size):
    k0 = kget(0, start, size)
    if kget(1, start, size) is None:
        return _dot(k0, jnp.concatenate(qxs, axis=1))
    return jnp.concatenate([_dot(k0, qxs[0])] + [_dot(kget(c, start, size), qxs[c])
                                                 for c in range(1, len(qxs))], axis=1)


def _softmax_pv(s, vget, nq, start, size, m_scr, acc_scr, bias=None):
    if bias is not None:
        s = s + jnp.concatenate([bias] * nq, axis=1)
    m_prev = m_scr[...]
    m_new = jnp.maximum(m_prev, jnp.max(s, axis=0, keepdims=True))
    alpha = jnp.exp(m_prev - m_new)
    p = jnp.exp(s - m_new).astype(BF16)
    v0 = vget(0, start, size)
    if vget(1, start, size) is None:
        pv = _dot(v0, p)
    else:
        pv = jnp.concatenate([_dot(v0, p[:, :TQ])] + [_dot(vget(c, start, size), p[:, c * TQ:(c + 1) * TQ])
                                                      for c in range(1, nq)], axis=1)
    acc_scr[...] = alpha * acc_scr[...] + pv
    m_scr[...] = m_new


def _rows(ref, start, size, lane0=0):
    return ref[pl.ds(pl.multiple_of(start, TQ), size), lane0:lane0 + SLOT]


def _cols(ref, start, size, row0=0):
    return ref[row0:row0 + PV_ROWS, pl.ds(pl.multiple_of(start, TQ), size)]


def _to_token_rows(o_t):
    pad = jnp.zeros((SLOT - PV_ROWS, o_t.shape[1]), o_t.dtype)
    return jnp.concatenate([o_t, pad], axis=0).T


def _flash_causal(kget, vget, qxs, i, cb_ref, sa_scr, sb_scr, m_scr, acc_scr):
    tq = TQ
    tk = 2 * tq
    nq = len(qxs)
    n_steps = i // 2 + 1
    sa_scr[...] = _qk(kget, qxs, 0, tk)

    def body(jj, carry):
        j = 2 * jj
        s = sa_scr[...]
        sb_scr[...] = _qk(kget, qxs, (j + 1) * tk, tk)
        _softmax_pv(s, vget, nq, j * tk, tk, m_scr, acc_scr)
        s = sb_scr[...]
        sa_scr[...] = _qk(kget, qxs, (j + 2) * tk, tk)
        _softmax_pv(s, vget, nq, (j + 1) * tk, tk, m_scr, acc_scr)
        return carry

    lax.fori_loop(0, (n_steps - 1) // 2, body, 0)
    last = n_steps - 1
    bias = cb_ref[i % 2]

    @pl.when(last % 2 == 0)
    def _():
        _softmax_pv(sa_scr[...], vget, nq, last * tk, tk, m_scr, acc_scr, bias=bias)

    @pl.when(last % 2 == 1)
    def _():
        s = sa_scr[...]
        sb_scr[...] = _qk(kget, qxs, last * tk, tk)
        _softmax_pv(s, vget, nq, (last - 1) * tk, tk, m_scr, acc_scr)
        _softmax_pv(sb_scr[...], vget, nq, last * tk, tk, m_scr, acc_scr, bias=bias)


def _nsa_kernel(q_ref, kc_ref, vct_ref, ks_ref, vst_ref, kw_ref, vwt_ref, g_ref, ovt_ref, fq_ref, cb_ref, wb_ref,
                o_ref, m_scr, acc_scr, imp_scr, sa_scr, sb_scr, *, tq, nsel):
    i = pl.program_id(2)
    r4 = NSA_GROUP
    rr = r4 * tq
    q4 = [q_ref[r * SLOT:(r + 1) * SLOT, :] for r in range(r4)]
    fq = fq_ref[0]

    kc = kc_ref[0]
    nc = kc.shape[0]
    qs = jnp.concatenate(q4, axis=1)
    s = _dot(kc, qs)
    t_row = i * tq + (lax.broadcasted_iota(I32, (1, rr), 1) & (tq - 1))
    last_c = (t_row - (CMP_BLOCK - 1)) >> 4
    cmask = lax.broadcasted_iota(I32, (nc, rr), 0) <= last_c
    s = jnp.where(cmask, s, NEG)
    mx = jnp.max(s, axis=0, keepdims=True)
    e = jnp.where(cmask, jnp.exp(s - mx), 0.0)
    lsum = jnp.sum(e, axis=0, keepdims=True)
    p_cmp = (e * (1.0 / jnp.where(lsum > 0.0, lsum, 1.0))).astype(BF16)
    o_cmp = _dot(vct_ref[0], p_cmp)

    ovt = ovt_ref[...]
    imp4 = _dot(ovt, p_cmp)
    imp = imp4[:, 0:tq]
    for r in range(1, r4):
        imp = imp + imp4[:, r * tq:(r + 1) * tq]
    nbp = imp.shape[0]
    jidx = lax.broadcasted_iota(I32, (nbp, tq), 0)
    cur = (i * tq + lax.broadcasted_iota(I32, (nbp, tq), 1)) // SLC_BLOCK
    forced = (jidx == 0) | (jidx == cur) | (jidx == cur - 1)
    key = jnp.where(forced, jnp.int32(0x49742400),
                    jnp.where(jidx > cur, -1, lax.bitcast_convert_type(imp, I32)))
    imp_scr[...] = key

    def rank_body(jj, rank):
        for jp in (2 * jj, 2 * jj + 1):
            row = imp_scr[pl.ds(jp, 1), :]
            rank = rank + (row > key - (jidx > jp).astype(I32)).astype(I32)
        return rank

    n_live = (i + 1) * (tq // SLC_BLOCK)
    rank = lax.fori_loop(0, n_live // 2, rank_body, jnp.zeros((nbp, tq), I32))
    sel = (rank < nsel) & (jidx <= cur)
    mask_t = jnp.where(sel, 0.0, NEG)
    feat = jnp.concatenate([jnp.zeros((SLOT - nbp, tq), F32), mask_t], axis=0)

    qxs = [q4[r] + (feat + fq[:, r:r + 1]).astype(BF16) for r in range(r4)]
    kget = lambda c, start, size: _rows(ks_ref, start, size) if c == 0 else None
    vget = lambda c, start, size: _cols(vst_ref, start, size) if c == 0 else None
    _flash_reset(m_scr, acc_scr)
    _flash_causal(kget, vget, qxs, i, cb_ref, sa_scr, sb_scr, m_scr, acc_scr)
    o_slc = acc_scr[...] * (1.0 / acc_scr[HEAD_DIM:HEAD_DIM + 1, :])

    qx = jnp.concatenate([q4[r] + fq[:, r4 + r:r4 + r + 1].astype(BF16) for r in range(r4)], axis=1)
    w0 = jnp.maximum(i - 2, 0) * tq
    s = _dot(_rows(kw_ref, w0, 3 * tq), qx)
    s = s + jnp.concatenate([wb_ref[jnp.minimum(i, 2)]] * r4, axis=1)
    e = jnp.exp(s - jnp.max(s, axis=0, keepdims=True)).astype(BF16)
    o_win = _dot(_cols(vwt_ref, w0, 3 * tq), e)
    o_win = o_win * (1.0 / o_win[HEAD_DIM:HEAD_DIM + 1, :])

    gate = jax.nn.sigmoid(g_ref[...])
    for r in range(r4):
        sl = slice(r * tq, (r + 1) * tq)
        o = (gate[3 * r:3 * r + 1, :] * o_cmp[:PV_ROWS, sl] + gate[3 * r + 1:3 * r + 2, :] * o_slc[:, sl]
             + gate[3 * r + 2:3 * r + 3, :] * o_win[:, sl])
        o_ref[:, r * SLOT:(r + 1) * SLOT] = _to_token_rows(o).astype(o_ref.dtype)


def _nsa_attention(zat, zb, gt, kc, vct, ovt, fq, cb, wb, bsz, seq):
    tq = TQ
    ni = seq // tq
    nc = kc.shape[1]
    nsel = min(SLC_TOPK, seq // SLC_BLOCK)
    assert seq // SLC_BLOCK <= 64 and WINDOW == 2 * tq and ni % 2 == 0 and ni >= 3
    rr = NSA_GROUP * tq
    gw = NSA_GROUP * SLOT
    in_specs = [
        pl.BlockSpec((gw, tq), lambda b, g, i: (g, b * ni + i)),
        pl.BlockSpec((1, nc, SLOT), lambda b, g, i: (b, 0, g)),
        pl.BlockSpec((1, SLOT, nc), lambda b, g, i: (b, g, 0)),
        pl.BlockSpec((seq, SLOT), lambda b, g, i: (b, g)),
        pl.BlockSpec((SLOT, seq), lambda b, g, i: (_SLOT_SLC_V + g, b)),
        pl.BlockSpec((seq, SLOT), lambda b, g, i: (b, 2 + g)),
        pl.BlockSpec((SLOT, seq), lambda b, g, i: (_SLOT_WIN_V + g, b)),
        pl.BlockSpec((SLOT, tq), lambda b, g, i: (g, b * ni + i)),
        pl.BlockSpec(ovt.shape, lambda b, g, i: (0, 0)),
        pl.BlockSpec((1, SLOT, 8), lambda b, g, i: (g, 0, 0)),
        pl.BlockSpec(cb.shape, lambda b, g, i: (0, 0, 0)),
        pl.BlockSpec(wb.shape, lambda b, g, i: (0, 0, 0)),
    ]
    return pl.pallas_call(
        functools.partial(_nsa_kernel, tq=tq, nsel=nsel),
        out_shape=jax.ShapeDtypeStruct((bsz * seq, NSA_HEADS * SLOT), BF16),
        grid=(bsz, NSA_KV_HEADS, ni),
        in_specs=in_specs,
        out_specs=pl.BlockSpec((tq, gw), lambda b, g, i: (b * ni + i, g)),
        scratch_shapes=[pltpu.VMEM((1, rr), F32),
                        pltpu.VMEM((PV_ROWS, rr), F32), pltpu.VMEM((64, tq), I32),
                        pltpu.VMEM((2 * tq, rr), F32), pltpu.VMEM((2 * tq, rr), F32)],
        compiler_params=_cparams(("parallel", "parallel", "arbitrary")),
        name="nsa_attention",
    )(zat, kc, vct, zb, zat, zb, zat, gt, ovt, fq, cb, wb)


def _moba_kernel(q_ref, k_ref, vt_ref, fq_ref, cb_ref, o_ref, m_scr, acc_scr, km_scr, gate_scr,
                 sa_scr, sb_scr, *, tq, nblk, ntop):
    i = pl.program_id(2)
    nh = MOBA_HPS

    @pl.when(i == 0)
    def _():
        for hh in range(nh):
            kh = k_ref[:, hh * SLOT:(hh + 1) * SLOT].astype(F32)
            km = jnp.mean(kh.reshape(nblk, tq, SLOT), axis=1)
            if nblk < 16:
                km = jnp.concatenate([km, jnp.zeros((16 - nblk, SLOT), F32)], axis=0)
            km_scr[hh] = km

    qs = [q_ref[hh * SLOT:(hh + 1) * SLOT, :] for hh in range(nh)]
    gates = [_dot(km_scr[hh].astype(BF16), qs[hh]) for hh in range(nh)]
    for hh in range(nh):
        gate_scr[hh] = gates[hh]
    jidx = lax.broadcasted_iota(I32, (16, tq), 0)

    def rank_body(jp, ranks):
        out = []
        for hh in range(nh):
            row = gate_scr[hh, pl.ds(jp, 1), :]
            better = (row > gates[hh]) | ((row == gates[hh]) & (jp < jidx))
            out.append(ranks[hh] + better.astype(I32))
        return tuple(out)

    ranks = lax.fori_loop(0, i, rank_body, tuple(jnp.zeros((16, tq), I32) for _ in range(nh)))
    qxs = []
    for hh in range(nh):
        keep = ((jidx < i) & (ranks[hh] < ntop)) | (jidx == i)
        mask_t = jnp.where(keep, 0.0, NEG)
        feat = jnp.concatenate([jnp.zeros((HEAD_DIM, tq), F32), mask_t,
                                jnp.zeros((SLOT - HEAD_DIM - 16, tq), F32)], axis=0)
        qxs.append(qs[hh] + (feat + fq_ref[hh][:, 0:1]).astype(BF16))
    kget = lambda c, start, size: _rows(k_ref, start, size, c * SLOT)
    vget = lambda c, start, size: _cols(vt_ref, start, size, c * SLOT)
    _flash_reset(m_scr, acc_scr)
    _flash_causal(kget, vget, qxs, i, cb_ref, sa_scr, sb_scr, m_scr, acc_scr)
    o = acc_scr[...] * (1.0 / acc_scr[HEAD_DIM:HEAD_DIM + 1, :])
    for hh in range(nh):
        o_ref[:, hh * SLOT:(hh + 1) * SLOT] = _to_token_rows(o[:, hh * tq:(hh + 1) * tq]).astype(o_ref.dtype)


def _moba_attention(zat, zb, fq, cb, bsz, seq):
    tq = TQ
    assert tq == MOBA_BLOCK and seq % tq == 0
    ni = seq // tq
    assert ni <= 16 and ni % 2 == 0
    ntop = min(MOBA_TOPK, ni)
    nh = MOBA_HPS
    hw = nh * SLOT
    q0 = _SLOT_MOBA_Q // nh
    v0 = _SLOT_MOBA_V // nh
    return pl.pallas_call(
        functools.partial(_moba_kernel, tq=tq, nblk=ni, ntop=ntop),
        out_shape=jax.ShapeDtypeStruct((bsz * seq, MOBA_HEADS * SLOT), BF16),
        grid=(bsz, MOBA_HEADS // nh, ni),
        in_specs=[pl.BlockSpec((hw, tq), lambda b, h, i: (q0 + h, b * ni + i)),
                  pl.BlockSpec((seq, hw), lambda b, h, i: (b, 1 + h)),
                  pl.BlockSpec((hw, seq), lambda b, h, i: (v0 + h, b)),
                  pl.BlockSpec((nh, SLOT, 8), lambda b, h, i: (h, 0, 0)),
                  pl.BlockSpec(cb.shape, lambda b, h, i: (0, 0, 0))],
        out_specs=pl.BlockSpec((tq, hw), lambda b, h, i: (b * ni + i, h)),
        scratch_shapes=[pltpu.VMEM((1, nh * tq), F32),
                        pltpu.VMEM((PV_ROWS, nh * tq), F32),
                        pltpu.VMEM((nh, 16, SLOT), F32), pltpu.VMEM((nh, 16, tq), F32),
                        pltpu.VMEM((2 * tq, nh * tq), F32), pltpu.VMEM((2 * tq, nh * tq), F32)],
        compiler_params=_cparams(("parallel", "parallel", "arbitrary")),
        name="moba_attention",
    )(zat, zb, zat, fq, cb)


def _merge_kernel(ya_ref, yb_ref, hc_ref, bc_ref, cc_ref, hcp_ref, ccp_ref, g0_ref, g1_ref, g2_ref,
                  h_ref, cw_ref, wa_ref, wb_ref, wc_ref, wo_ref, lg_ref, lb_ref, o_ref, *, tm, per, alpha):
    i = pl.program_id(0)
    u = cc_ref[...].astype(F32) * hc_ref[...].astype(F32)
    first = (i % per) == 0
    up = jnp.where(first, 0.0, ccp_ref[...].astype(F32) * hcp_ref[...].astype(F32))
    rowi = lax.broadcasted_iota(I32, u.shape, 0)
    u1 = jnp.where(rowi == 0, up[15:16, :], pltpu.roll(u, 1, 0))
    u2 = jnp.where(rowi == 0, up[14:15, :], jnp.where(rowi == 1, up[15:16, :], pltpu.roll(u, 2, 0)))
    cw = cw_ref[...]
    yc = bc_ref[...].astype(F32) * (cw[0:1, :] * u2 + cw[1:2, :] * u1 + cw[2:3, :] * u)
    merged = (jax.nn.sigmoid(g0_ref[...].astype(F32)) * _dot(ya_ref[...], wa_ref[...])
              + jax.nn.sigmoid(g1_ref[...].astype(F32)) * _dot(yb_ref[...], wb_ref[...])
              + jax.nn.sigmoid(g2_ref[...].astype(F32)) * _dot(yc.astype(BF16), wc_ref[...]))
    mix = _dot(merged.astype(BF16), wo_ref[...])
    o_ref[...] = _layer_norm(alpha * h_ref[...] + mix, lg_ref[...], lb_ref[...])


def _merge(ya, yb, zc, h, cw, wa, wb, wc, wo, lg, lb, seq, alpha):
    tm = 512
    t = h.shape[0]
    per = seq // tm
    d = D_MODEL
    row = lambda i: (i, 0)
    const = lambda i: (0, 0)
    prev = lambda c: (lambda i: (jnp.maximum(i * (tm // 16) - 1, 0), c))
    in_specs = [
        pl.BlockSpec((tm, d), row), pl.BlockSpec((tm, d), row),
        pl.BlockSpec((tm, CONV_CH), lambda i: (i, 1)), pl.BlockSpec((tm, CONV_CH), lambda i: (i, 2)),
        pl.BlockSpec((tm, CONV_CH), lambda i: (i, 3)),
        pl.BlockSpec((16, CONV_CH), prev(1)), pl.BlockSpec((16, CONV_CH), prev(3)),
        pl.BlockSpec((tm, d), lambda i: (i, 2)), pl.BlockSpec((tm, d), lambda i: (i, 3)),
        pl.BlockSpec((tm, d), lambda i: (i, 4)),
        pl.BlockSpec((tm, d), row),
        pl.BlockSpec((8, CONV_CH), const),
        pl.BlockSpec((d, d), const), pl.BlockSpec((d, d), const), pl.BlockSpec((CONV_CH, d), const),
        pl.BlockSpec((d, d), const), pl.BlockSpec((1, d), const), pl.BlockSpec((1, d), const),
    ]
    return pl.pallas_call(
        functools.partial(_merge_kernel, tm=tm, per=per, alpha=alpha),
        out_shape=jax.ShapeDtypeStruct((t, d), F32),
        grid=(t // tm,),
        in_specs=in_specs,
        out_specs=pl.BlockSpec((tm, d), row),
        compiler_params=_cparams(("parallel",)),
        name="mixer_merge",
    )(ya, yb, zc, zc, zc, zc, zc, zc, zc, zc, h, cw, wa, wb, wc, wo, lg, lb)


def _xattn_kernel(h_ref, kv_ref, wq_ref, wo_ref, lg_ref, lb_ref, o_ref, ob_ref, *, alpha):
    h = h_ref[...]
    q = _dot(h.astype(BF16), wq_ref[...]).astype(BF16)
    kv = kv_ref[...]
    nh = XATTN_HEADS
    hd = XATTN_HEAD_DIM
    outs = []
    for hh in range(nh):
        s = _dot_t(q[:, hh * hd:(hh + 1) * hd], kv[:, hh * hd:(hh + 1) * hd]) * (hd ** -0.5)
        s = s - jnp.max(s, axis=-1, keepdims=True)
        e = jnp.exp(s)
        p = e / jnp.sum(e, axis=-1, keepdims=True)
        outs.append(_dot(p.astype(BF16), kv[:, (nh + hh) * hd:(nh + hh + 1) * hd]))
    o = jnp.concatenate(outs, axis=-1).astype(BF16)
    y = _layer_norm(alpha * h + _dot(o, wo_ref[...]), lg_ref[...], lb_ref[...])
    o_ref[...] = y
    bits = lax.bitcast_convert_type(y.astype(BF16).astype(F32), I32)
    half = y.shape[1] // 2
    ob_ref[...] = (bits[:, half:] & -65536) | lax.shift_right_logical(bits[:, :half], 16)


def _unpack_pairs(words):
    lo = lax.bitcast_convert_type(lax.shift_left(words, 16), F32).astype(BF16)
    hi = lax.bitcast_convert_type(words & -65536, F32).astype(BF16)
    return lo, hi


def _xattn(h, kv, wq, wo, lg, lb, seq, mlen, alpha):
    tm = 512
    t = h.shape[0]
    per = seq // tm
    d = D_MODEL
    row = lambda i: (i, 0)
    const = lambda i: (0, 0)
    return pl.pallas_call(
        functools.partial(_xattn_kernel, alpha=alpha),
        out_shape=(jax.ShapeDtypeStruct((t, d), F32), jax.ShapeDtypeStruct((t, d // 2), I32)),
        grid=(t // tm,),
        in_specs=[pl.BlockSpec((tm, d), row), pl.BlockSpec((mlen, kv.shape[1]), lambda i: (i // per, 0)),
                  pl.BlockSpec(wq.shape, const), pl.BlockSpec(wo.shape, const),
                  pl.BlockSpec((1, d), const), pl.BlockSpec((1, d), const)],
        out_specs=(pl.BlockSpec((tm, d), row), pl.BlockSpec((tm, d // 2), row)),
        compiler_params=_cparams(("parallel",)),
        name="mem_xattn",
    )(h, kv, wq, wo, lg, lb)


def _expert_kernel(be_ref, nu_ref, nxt_ref, slot_ref, x_ref, w13_hbm, w2_hbm, o_ref,
                   w13f_scr, w2f_scr, w13b_scr, w2b_scr, sem, *, layer):
    i = pl.program_id(0)
    e = be_ref[i]
    s = slot_ref[i]

    def weight_copies(expert, slot):
        return (pltpu.make_async_copy(w13_hbm.at[layer, expert], w13f_scr.at[slot], sem.at[0, slot]),
                pltpu.make_async_copy(w2_hbm.at[layer, expert], w2f_scr.at[slot], sem.at[1, slot]))

    @pl.when(i == 0)
    def _():
        for cp in weight_copies(e, s):
            cp.start()

    @pl.when((i == 0) | (e != be_ref[jnp.maximum(i - 1, 0)]))
    def _():
        for cp in weight_copies(e, s):
            cp.wait()
        nxt = nxt_ref[i]

        @pl.when(nxt >= 0)
        def _():
            for cp in weight_copies(nxt, 1 - s):
                cp.start()

        w13b_scr[...] = w13f_scr[s].astype(BF16)
        w2b_scr[...] = w2f_scr[s].astype(BF16)

    @pl.when(i < nu_ref[0])
    def _():
        x_lo, x_hi = _unpack_pairs(x_ref[...])
        half = x_lo.shape[1]
        hmid = _dot(x_lo, w13b_scr[:half, :]) + _dot(x_hi, w13b_scr[half:, :])
        a = hmid[:, :EXPERT_DIM]
        g = hmid[:, EXPERT_DIM:]
        act = (a * jax.nn.sigmoid(a) * g).astype(BF16)
        o_ref[...] = _dot(act, w2b_scr[...]).astype(o_ref.dtype)


def _experts(xg, w13, w2, block_e, n_used, next_e, slot, layer):
    n_pad = xg.shape[0]
    d = 2 * xg.shape[1]
    rows = EXP_ROWS
    nb = n_pad // rows
    blk = lambda i, be, nu, nx, sl: (jnp.minimum(i, nu[0] - 1), 0)
    grid_spec = pltpu.PrefetchScalarGridSpec(
        num_scalar_prefetch=4,
        grid=(nb,),
        in_specs=[pl.BlockSpec((rows, d // 2), blk),
                  pl.BlockSpec(memory_space=pl.ANY), pl.BlockSpec(memory_space=pl.ANY)],
        out_specs=pl.BlockSpec((rows, d), blk),
        scratch_shapes=[pltpu.VMEM((2, d, 2 * EXPERT_DIM), F32), pltpu.VMEM((2, EXPERT_DIM, d), F32),
                        pltpu.VMEM((d, 2 * EXPERT_DIM), BF16), pltpu.VMEM((EXPERT_DIM, d), BF16),
                        pltpu.SemaphoreType.DMA((2, 2))],
    )
    return pl.pallas_call(
        functools.partial(_expert_kernel, layer=layer),
        out_shape=jax.ShapeDtypeStruct((n_pad, d), BF16),
        grid_spec=grid_spec,
        compiler_params=_cparams(("arbitrary",)),
        name="moe_experts",
    )(block_e, n_used, next_e, slot, xg, w13, w2)


def _moe_out_kernel(h_ref, y8_ref, wk_ref, w13_ref, w2_ref, lg_ref, lb_ref, o_ref, ob_ref, *, alpha):
    h = h_ref[...]
    hmid = _dot(h.astype(BF16), w13_ref[...])
    a = hmid[:, :EXPERT_DIM]
    g = hmid[:, EXPERT_DIM:]
    acc = alpha * h + _dot((a * jax.nn.sigmoid(a) * g).astype(BF16), w2_ref[...])
    wk = wk_ref[...]
    for k in range(TOP_K):
        acc = acc + wk[:, k:k + 1] * y8_ref[k].astype(F32)
    y = _layer_norm(acc, lg_ref[...], lb_ref[...])
    o_ref[...] = y
    ob_ref[...] = y.astype(BF16)


def _moe_out(h, y8, wk, w13, w2, lg, lb, alpha):
    tm = 256
    t, d = h.shape
    row = lambda i: (i, 0)
    const = lambda i: (0, 0)
    return pl.pallas_call(
        functools.partial(_moe_out_kernel, alpha=alpha),
        out_shape=(jax.ShapeDtypeStruct((t, d), F32), jax.ShapeDtypeStruct((t, d), BF16)),
        grid=(t // tm,),
        in_specs=[pl.BlockSpec((tm, d), row), pl.BlockSpec((TOP_K, tm, d), lambda i: (0, i, 0)),
                  pl.BlockSpec((tm, TOP_K), row),
                  pl.BlockSpec(w13.shape, const), pl.BlockSpec(w2.shape, const),
                  pl.BlockSpec((1, d), const), pl.BlockSpec((1, d), const)],
        out_specs=(pl.BlockSpec((tm, d), row), pl.BlockSpec((tm, d), row)),
        compiler_params=_cparams(("parallel",)),
        name="moe_shared_ln",
    )(h, y8, wk, w13, w2, lg, lb)


def _dest_kernel(idx_ref, rank_ref, ps_ref, o_ref):
    idx = idx_ref[...]
    ps = ps_ref[...]
    tm = idx.shape[1]
    eidx = lax.broadcasted_iota(I32, (N_EXPERTS, tm), 0)
    rows = [jnp.sum(jnp.where(eidx == idx[k:k + 1, :], ps, 0.0), axis=0, keepdims=True) for k in range(TOP_K)]
    o_ref[...] = jnp.concatenate(rows, axis=0).astype(I32) + rank_ref[...]


def _dest(idx, rank, pstarts):
    tm = 1024
    t = idx.shape[1]
    tm = min(tm, t)
    col = lambda i: (0, i)
    return pl.pallas_call(
        _dest_kernel,
        out_shape=jax.ShapeDtypeStruct((TOP_K, t), I32),
        grid=(t // tm,),
        in_specs=[pl.BlockSpec((TOP_K, tm), col), pl.BlockSpec((TOP_K, tm), col),
                  pl.BlockSpec((N_EXPERTS, 1), lambda i: (0, 0))],
        out_specs=pl.BlockSpec((TOP_K, tm), col),
        compiler_params=_cparams(("parallel",)),
        name="moe_dest",
    )(idx, rank, pstarts)


def _router_kernel(h_ref, rwt_ref, rb_ref, tri_ref, idx_ref, w_ref, rank_ref, cnt_ref, carry_scr, *, tm):
    i = pl.program_id(0)

    @pl.when(i == 0)
    def _():
        carry_scr[...] = jnp.zeros(carry_scr.shape, F32)

    ne = N_EXPERTS
    gsz = ne // N_GROUPS
    s = jax.nn.sigmoid(_dot_t(rwt_ref[...], h_ref[...].astype(BF16)))
    sb = s + rb_ref[...]
    sb3 = sb.reshape(N_GROUPS, gsz, tm)
    li = lax.broadcasted_iota(I32, (N_GROUPS, gsz, tm), 1)
    m1 = jnp.max(sb3, axis=1, keepdims=True)
    first = jnp.min(jnp.where(sb3 == m1, li, gsz), axis=1, keepdims=True)
    m2 = jnp.max(jnp.where(li == first, -jnp.inf, sb3), axis=1, keepdims=True)
    gs = (m1 + m2).reshape(N_GROUPS, tm)
    gi = lax.broadcasted_iota(I32, (N_GROUPS, tm), 0)
    grank = jnp.zeros((N_GROUPS, tm), I32)
    for gp in range(N_GROUPS):
        row = gs[gp:gp + 1, :]
        grank = grank + ((row > gs) | ((row == gs) & (gp < gi))).astype(I32)
    gkeep = (grank < TOPK_GROUPS).astype(F32)
    ekeep = jnp.broadcast_to(gkeep[:, None, :], (N_GROUPS, gsz, tm)).reshape(ne, tm)
    cand = jnp.where(ekeep > 0.0, sb, NEG)
    eidx = lax.broadcasted_iota(I32, (ne, tm), 0)
    sel = jnp.zeros((ne, tm), F32)
    idxs, wts = [], []
    for _ in range(TOP_K):
        mx = jnp.max(cand, axis=0, keepdims=True)
        ik = jnp.min(jnp.where(cand == mx, eidx, ne), axis=0, keepdims=True)
        hit = eidx == ik
        wts.append(jnp.sum(jnp.where(hit, s, 0.0), axis=0, keepdims=True))
        idxs.append(ik)
        sel = jnp.where(hit, 1.0, sel)
        cand = jnp.where(hit, -jnp.inf, cand)
    before = _dot(sel.astype(BF16), tri_ref[...]) + carry_scr[...]
    ranks = [jnp.sum(jnp.where(eidx == ik, before, 0.0), axis=0, keepdims=True) for ik in idxs]
    carry_scr[...] = carry_scr[...] + jnp.sum(sel, axis=1, keepdims=True)
    w = jnp.concatenate(wts, axis=0)
    idx_ref[...] = jnp.concatenate(idxs, axis=0)
    w_ref[...] = w / jnp.sum(w, axis=0, keepdims=True) * ROUTE_SCALE
    rank_ref[...] = jnp.concatenate(ranks, axis=0).astype(I32)
    cnt_ref[...] = jnp.broadcast_to(carry_scr[...], cnt_ref.shape)


def _router(h, rwt, rb):
    tm = 256
    t, d = h.shape
    tri = jnp.asarray(np.triu(np.ones((tm, tm), np.float32), 1)).astype(BF16)
    const = lambda i: (0, 0)
    col = lambda i: (0, i)
    return pl.pallas_call(
        functools.partial(_router_kernel, tm=tm),
        out_shape=(jax.ShapeDtypeStruct((TOP_K, t), I32), jax.ShapeDtypeStruct((TOP_K, t), F32),
                   jax.ShapeDtypeStruct((TOP_K, t), I32), jax.ShapeDtypeStruct((N_EXPERTS, SLOT), F32)),
        grid=(t // tm,),
        in_specs=[pl.BlockSpec((tm, d), lambda i: (i, 0)), pl.BlockSpec((N_EXPERTS, d), const),
                  pl.BlockSpec((N_EXPERTS, 1), const), pl.BlockSpec((tm, tm), const)],
        out_specs=(pl.BlockSpec((TOP_K, tm), col), pl.BlockSpec((TOP_K, tm), col),
                   pl.BlockSpec((TOP_K, tm), col), pl.BlockSpec((N_EXPERTS, SLOT), const)),
        scratch_shapes=[pltpu.VMEM((N_EXPERTS, 1), F32)],
        compiler_params=_cparams(("arbitrary",)),
        name="moe_router",
    )(h, rwt, rb, tri)


def _dispatch(hp, dest_flat, n_pad):
    t, w = hp.shape
    per_w = t // (SC_CORES * SC_SUBCORES)
    assert per_w * SC_CORES * SC_SUBCORES == t and per_w % SC_CHUNK == 0 and dest_flat.shape[0] == TOP_K * t
    mesh = plsc.VectorSubcoreMesh(core_axis_name="c", subcore_axis_name="s")

    @functools.partial(
        pl.kernel, out_type=jax.ShapeDtypeStruct((n_pad, w), I32), mesh=mesh,
        scratch_types=[pltpu.VMEM((SC_CHUNK,), I32)] * TOP_K
        + [pltpu.VMEM((SC_CHUNK, w), I32), pltpu.SemaphoreType.DMA],
        name="moe_dispatch")
    def scatter_rows(hp_hbm, dest_hbm, xg_hbm, *scratch):
        idx_vs, rows_v, sem = scratch[:TOP_K], scratch[TOP_K], scratch[TOP_K + 1]
        wid = lax.axis_index("s") * SC_CORES + lax.axis_index("c")
        base = wid * per_w

        @pl.loop(0, per_w // SC_CHUNK)
        def _(j):
            t0 = pl.multiple_of(base + j * SC_CHUNK, SC_CHUNK)
            pltpu.sync_copy(hp_hbm.at[pl.ds(t0, SC_CHUNK)], rows_v)
            for k in range(TOP_K):
                pltpu.sync_copy(dest_hbm.at[pl.ds(pl.multiple_of(k * t + t0, SC_CHUNK), SC_CHUNK)], idx_vs[k])
            copies = [pltpu.async_copy(rows_v, xg_hbm.at[idx_vs[k]], sem) for k in range(TOP_K)]
            for cp in copies:
                cp.wait()

    return scatter_rows(hp, dest_flat)


def _moe(h, hp, router_w, router_b, exp_w13, exp_w2, layer, shared_w13, shared_w2, lg, lb, alpha):
    t, d = h.shape
    idx, wts, rank, cnt = _router(h, router_w.T, router_b.astype(F32).reshape(N_EXPERTS, 1))
    rows = EXP_ROWS
    n_a = t * TOP_K
    counts = cnt[:, 0].astype(I32)
    pcounts = (counts + rows - 1) // rows * rows
    pends = jnp.cumsum(pcounts)
    pstarts = pends - pcounts
    dest = _dest(idx, rank, pstarts.astype(F32).reshape(N_EXPERTS, 1))
    n_blocks = -(-n_a // rows) + N_EXPERTS
    n_pad = n_blocks * rows
    n_used = (pends[-1:] // rows).astype(I32)
    first_row = jnp.minimum(jnp.arange(n_blocks, dtype=I32), n_used - 1) * rows
    block_e = jnp.sum((pends[None, :] <= first_row[:, None]).astype(I32), axis=1)
    eidx = jnp.arange(N_EXPERTS, dtype=I32)
    has_rows = pcounts > 0
    at_or_after = lax.cummin(jnp.where(has_rows, eidx, N_EXPERTS), axis=0, reverse=True)
    after = jnp.concatenate([at_or_after[1:], jnp.full((1,), N_EXPERTS, I32)])
    next_e = jnp.take(jnp.where(after < N_EXPERTS, after, -1), block_e)
    slot = jnp.take((jnp.cumsum(has_rows.astype(I32)) - 1) % 2, block_e).astype(I32)
    xg = _dispatch(hp, dest.reshape(-1), n_pad)
    out = _experts(xg, exp_w13, exp_w2, block_e, n_used, next_e.astype(I32), slot, layer)
    y8 = jnp.take(out, dest, axis=0, mode="clip")
    return _moe_out(h, y8, wts.T, shared_w13, shared_w2, lg, lb, alpha)


def _pad_slots(w, scale=1.0):
    dm = w.shape[0]
    nh = w.shape[1] // HEAD_DIM
    w = (w * scale).reshape(dm, nh, HEAD_DIM)
    return jnp.concatenate([w, jnp.zeros_like(w)], axis=-1).reshape(dm, nh * SLOT)


def _pad_rows(w):
    n = w.shape[1]
    nh = w.shape[0] // HEAD_DIM
    w = w.reshape(nh, HEAD_DIM, n)
    return jnp.concatenate([w, jnp.zeros_like(w)], axis=1).reshape(nh * SLOT, n)


def _alibi(n):
    return np.exp2(-8.0 * np.arange(1, n + 1, dtype=np.float64) / n).astype(np.float32)


def _key_features(seq):
    p = np.arange(seq)
    slc = np.zeros((seq, SLOT), np.float32)
    slc[:, HEAD_DIM] = p % SLC_BLOCK
    blk = p // SLC_BLOCK
    nz = blk > 0
    slc[p[nz], HEAD_DIM + blk[nz]] = 1.0
    win = np.zeros((seq, SLOT), np.float32)
    win[:, HEAD_DIM] = p // 64
    win[:, HEAD_DIM + 1] = p % 64
    mob = np.zeros((seq, SLOT), np.float32)
    mob[p, HEAD_DIM + p // MOBA_BLOCK] = 1.0
    mob[:, HEAD_DIM + 16] = p % MOBA_BLOCK
    return np.concatenate([slc] * 2 + [win] * 2 + [mob] * MOBA_HEADS, axis=1)


def _query_features():
    sl = _alibi(NSA_HEADS)
    nsa = np.zeros((NSA_KV_HEADS, SLOT, 8), np.float32)
    for g in range(NSA_KV_HEADS):
        for r in range(NSA_GROUP):
            s = sl[g * NSA_GROUP + r]
            nsa[g, HEAD_DIM, r] = s
            nsa[g, HEAD_DIM + 1:, r] = s * SLC_BLOCK * np.arange(1, 64)
            nsa[g, HEAD_DIM, NSA_GROUP + r] = s * 64
            nsa[g, HEAD_DIM + 1, NSA_GROUP + r] = s
    sm = _alibi(MOBA_HEADS)
    mob = np.zeros((MOBA_HEADS, SLOT, 8), np.float32)
    for h in range(MOBA_HEADS):
        mob[h, HEAD_DIM:HEAD_DIM + 16, 0] = sm[h] * MOBA_BLOCK * np.arange(16)
        mob[h, HEAD_DIM + 16, 0] = sm[h]
    return nsa, mob


def _mask_biases():
    t = np.arange(TQ)[None, :]
    p2 = np.arange(2 * TQ)[:, None]
    cb = np.stack([p2 <= t, p2 <= TQ + t])
    p3 = np.arange(3 * TQ)[:, None]
    dist = np.stack([c * TQ + t - p3 for c in range(3)])
    wb = (dist >= 0) & (dist < WINDOW)
    to_bias = lambda m: np.where(m, 0.0, NEG).astype(np.float32)
    return to_bias(cb), to_bias(wb)


def _overlap_t(seq):
    nc = seq // CMP_STRIDE
    c_start = np.arange(nc) * CMP_STRIDE
    b_start = np.arange(64) * SLC_BLOCK
    ov = ((c_start[None, :] < (b_start + SLC_BLOCK)[:, None])
          & ((c_start + CMP_BLOCK)[None, :] > b_start[:, None])
          & (np.arange(nc) < nc - 1)[None, :] & (b_start < seq)[:, None])
    return ov.astype(np.float32)


def _compress_weights(pe, w1, w2):
    ty = np.array([0, 0, 1, 1])
    eye = jnp.eye(4, dtype=F32)
    w1r = w1.reshape(2, CMP_BLOCK, HEAD_DIM, CMP_HIDDEN)[ty]
    top = jnp.einsum('spdj,sS->psdSj', w1r[:, :CMP_STRIDE], eye).reshape(CMP_STRIDE * 256, 4 * CMP_HIDDEN)
    bot = jnp.einsum('spdj,sS->psdSj', w1r[:, CMP_STRIDE:], eye).reshape(CMP_STRIDE * 256, 4 * CMP_HIDDEN)
    per = pe[ty]
    pet = jnp.transpose(per[:, :CMP_STRIDE], (1, 0, 2)).reshape(1, CMP_STRIDE * 256)
    peb = jnp.transpose(per[:, CMP_STRIDE:], (1, 0, 2)).reshape(1, CMP_STRIDE * 256)
    w2p = jnp.concatenate([w2[ty], jnp.zeros((4, CMP_HIDDEN, SLOT - HEAD_DIM), F32)], axis=-1)
    w2b = jnp.einsum('sjd,sS->sjSd', w2p, eye).reshape(4 * CMP_HIDDEN, 4 * SLOT)
    w2k = w2b[:, :2 * SLOT].astype(BF16)
    w2vt = w2b[:, 2 * SLOT:].T.astype(BF16)
    return pet, peb, top.astype(BF16), bot.astype(BF16), w2k, w2vt


def kernel(x, mem, w_in, cmp_pe, cmp_w1, cmp_w2, conv_w, w_branch, w_out, ln1_g, ln1_b,
           xattn_wq, xattn_wkv, xattn_wo, ln2_g, ln2_b, router_w, router_b, exp_w13, exp_w2,
           shared_w13, shared_w2, ln3_g, ln3_b):
    bsz, seq, d = x.shape
    mlen = mem.shape[1]
    depth = w_in.shape[0]
    alpha = (2.0 * depth) ** 0.25
    t = bsz * seq
    scale = HEAD_DIM ** -0.5

    kfeat = jnp.asarray(_key_features(seq))
    fq_nsa, fq_moba = (jnp.asarray(a) for a in _query_features())
    cbias, wbias = (jnp.asarray(a) for a in _mask_biases())
    ovt = jnp.asarray(_overlap_t(seq)).astype(BF16)
    memf = mem.reshape(bsz * mlen, d)

    h = x.reshape(t, d)
    hb = h.astype(BF16)
    for l in range(depth):
        ht = hb.T
        wi = w_in[l]
        kv6 = wi[:, _OFF_NSA_KV:_OFF_NSA_G].reshape(d, 6, NSA_KV_HEADS * HEAD_DIM)
        mq, mk, mv = (wi[:, _OFF_MOBA + j * 512:_OFF_MOBA + (j + 1) * 512] for j in range(3))
        w_at = jnp.concatenate([wi[:, :512] * scale, kv6[:, 3], kv6[:, 5], mq * scale, mv],
                               axis=1).T.astype(BF16)
        w_b = jnp.concatenate([_pad_slots(kv6[:, 2]), _pad_slots(kv6[:, 4]), _pad_slots(mk)], axis=1).astype(BF16)
        wg = wi[:, _OFF_NSA_G:_OFF_MOBA].reshape(d, NSA_KV_HEADS, 12)
        wgt = jnp.concatenate([wg, jnp.zeros((d, NSA_KV_HEADS, SLOT - 12), F32)], axis=-1)
        wgt = wgt.reshape(d, NSA_KV_HEADS * SLOT).T.astype(BF16)
        w_c = jnp.concatenate([wi[:, _OFF_NSA_KV:_OFF_NSA_KV + 256], jnp.zeros((d, 256), F32),
                               wi[:, _OFF_CONV:]], axis=1).astype(BF16)

        value_slots = tuple(range(_SLOT_SLC_V, _SLOT_MOBA_Q)) + tuple(range(_SLOT_MOBA_V, _N_SLOTS_T))
        zat = _proj_t(w_at, ht, value_slots)
        gt = _mm(wgt, ht, F32, tm=256, tn=1024, name="proj_gate")
        zb = _mm(hb, w_b, BF16, tn=768, feats=kfeat, name="proj_b")
        zc = _mm(hb, w_c, BF16, tn=1024, name="proj_c")

        pet, peb, wt, wb, w2k, w2vt = _compress_weights(cmp_pe[l], cmp_w1[l], cmp_w2[l])
        sub = zc[:, :256].reshape(bsz, seq // CMP_STRIDE, CMP_STRIDE * 256)
        kc, vct = _compress(sub, pet, peb, wt, wb, w2k, w2vt)

        ya = _nsa_attention(zat, zb, gt, kc, vct, ovt, fq_nsa, cbias, wbias, bsz, seq)
        yb = _moba_attention(zat, zb, fq_moba, cbias, bsz, seq)

        cw = jnp.concatenate([conv_w[l], jnp.zeros((5, CONV_CH), F32)], axis=0)
        h = _merge(ya, yb, zc, h, cw, _pad_rows(w_branch[l, 0]).astype(BF16),
                   _pad_rows(w_branch[l, 1]).astype(BF16), w_branch[l, 2].astype(BF16),
                   w_out[l].astype(BF16), ln1_g[l][None], ln1_b[l][None], seq, alpha)

        kv = _mm(memf, xattn_wkv[l].astype(BF16), BF16, tm=512, name="xattn_kv")
        h, hp = _xattn(h, kv, xattn_wq[l].astype(BF16), xattn_wo[l].astype(BF16),
                       ln2_g[l][None], ln2_b[l][None], seq, mlen, alpha)

        h, hb = _moe(h, hp, router_w[l].astype(BF16), router_b[l], exp_w13, exp_w2, l,
                     shared_w13[l].astype(BF16), shared_w2[l].astype(BF16), ln3_g[l][None], ln3_b[l][None], alpha)
    return h.reshape(bsz, seq, d)
```

```python
import functools

import jax
import jax.numpy as jnp
import numpy as np
from jax import lax
from jax.experimental import pallas as pl
from jax.experimental.pallas import tpu as pltpu
from jax.experimental.pallas import tpu_sc as plsc

F32 = jnp.float32
BF16 = jnp.bfloat16
I32 = jnp.int32

D_MODEL = 1024
HEAD_DIM = 64
SLOT = 128
NEG = -1e30
LN_EPS = 1e-5

NSA_HEADS = 8
NSA_KV_HEADS = 2
NSA_GROUP = 4
CMP_BLOCK = 32
CMP_STRIDE = 16
CMP_HIDDEN = 256
SLC_BLOCK = 64
SLC_TOPK = 16
WINDOW = 512
MOBA_HEADS = 8
MOBA_BLOCK = 256
MOBA_TOPK = 3
CONV_CH = 512
XATTN_HEADS = 4
XATTN_HEAD_DIM = 128
N_EXPERTS = 256
TOP_K = 8
N_GROUPS = 8
TOPK_GROUPS = 4
EXPERT_DIM = 256
ROUTE_SCALE = 2.5

TQ = 256
MOBA_HPS = 4
PV_ROWS = 80
FORCED_KEY = int(np.float32(1e6).view(np.int32))
EXP_ROWS = 512
VMEM_LIMIT = 48 * 1024 * 1024
SC_CORES = 2
SC_SUBCORES = 16
SC_CHUNK = 64

_OFF_NSA_Q = 0
_OFF_NSA_KV = 512
_OFF_NSA_G = 1280
_OFF_MOBA = 1304
_OFF_CONV = 2840
_OFF_MERGE = 4376

_SLOT_NSA_Q = 0
_SLOT_SLC_V = 8
_SLOT_WIN_V = 10
_SLOT_MOBA_Q = 12
_SLOT_MOBA_V = 20
_N_SLOTS_T = 28


def _cparams(sem):
    return pltpu.CompilerParams(dimension_semantics=sem, vmem_limit_bytes=VMEM_LIMIT)


def _dot(a, b):
    return jnp.dot(a, b, preferred_element_type=F32)


def _dot_t(a, b):
    return lax.dot_general(a, b, (((1,), (1,)), ((), ())), preferred_element_type=F32)


def _layer_norm(x, g, b):
    mu = jnp.mean(x, axis=-1, keepdims=True)
    xc = x - mu
    var = jnp.mean(xc * xc, axis=-1, keepdims=True)
    return xc * lax.rsqrt(var + LN_EPS) * g + b


def _mm_kernel(x_ref, w_ref, o_ref):
    o_ref[...] = _dot(x_ref[...].astype(BF16), w_ref[...]).astype(o_ref.dtype)


def _mm_feat_kernel(x_ref, w_ref, f_ref, o_ref):
    y = _dot(x_ref[...].astype(BF16), w_ref[...]) + f_ref[...]
    o_ref[...] = y.astype(o_ref.dtype)


def _mm(x, w, out_dtype, *, tm=1024, tn=512, feats=None, name):
    m, k = x.shape
    n = w.shape[1]
    tm = min(tm, m)
    tn = min(tn, n)
    assert m % tm == 0 and n % tn == 0, (m, n, tm, tn)
    in_specs = [pl.BlockSpec((tm, k), lambda i, j: (i, 0)),
                pl.BlockSpec((k, tn), lambda i, j: (0, j))]
    args = [x, w]
    if feats is None:
        body = _mm_kernel
    else:
        per = feats.shape[0] // tm
        assert feats.shape[0] % tm == 0
        in_specs.append(pl.BlockSpec((tm, tn), lambda i, j: (i % per, j)))
        args.append(feats)
        body = _mm_feat_kernel
    return pl.pallas_call(
        body,
        out_shape=jax.ShapeDtypeStruct((m, n), out_dtype),
        grid=(m // tm, n // tn),
        in_specs=in_specs,
        out_specs=pl.BlockSpec((tm, tn), lambda i, j: (i, j)),
        compiler_params=_cparams(("parallel", "parallel")),
        name=name,
    )(*args)


def _proj_t_kernel(w_ref, ht_ref, o_ref, *, value_slots):
    acc = _dot(w_ref[...], ht_ref[...])
    tn = acc.shape[1]
    zero = jnp.zeros((SLOT - HEAD_DIM, tn), o_ref.dtype)
    ones_row = (lax.broadcasted_iota(I32, (SLOT - HEAD_DIM, tn), 0) == 0).astype(o_ref.dtype)
    for s in range(w_ref.shape[0] // HEAD_DIM):
        o_ref[s * SLOT:s * SLOT + HEAD_DIM, :] = acc[s * HEAD_DIM:(s + 1) * HEAD_DIM, :].astype(o_ref.dtype)
        o_ref[s * SLOT + HEAD_DIM:(s + 1) * SLOT, :] = ones_row if s in value_slots else zero


def _proj_t(w, ht, value_slots):
    m, k = w.shape
    t = ht.shape[1]
    tn = min(512, t)
    n_out = m // HEAD_DIM * SLOT
    return pl.pallas_call(
        functools.partial(_proj_t_kernel, value_slots=value_slots),
        out_shape=jax.ShapeDtypeStruct((n_out, t), BF16),
        grid=(t // tn,),
        in_specs=[pl.BlockSpec((m, k), lambda j: (0, 0)), pl.BlockSpec((k, tn), lambda j: (0, j))],
        out_specs=pl.BlockSpec((n_out, tn), lambda j: (0, j)),
        compiler_params=_cparams(("parallel",)),
        name="proj_at",
    )(w, ht)


def _cmp_kernel(sub_ref, pet_ref, peb_ref, wt_ref, wb_ref, w2k_ref, w2vt_ref, kc_ref, vct_ref):
    sub = sub_ref[0].astype(F32)
    nc = sub.shape[0]
    a = _dot((sub + pet_ref[...]).astype(BF16), wt_ref[...])
    b = _dot((sub + peb_ref[...]).astype(BF16), wb_ref[...])
    hid = jax.nn.gelu(a + pltpu.roll(b, nc - 1, 0)).astype(BF16)
    kc_ref[0] = _dot(hid, w2k_ref[...]).astype(kc_ref.dtype)
    vct_ref[0] = _dot_t(w2vt_ref[...], hid).astype(vct_ref.dtype)


def _compress(sub, pet, peb, wt, wb, w2k, w2vt):
    bsz, nc, kk = sub.shape
    n_h = wt.shape[1]
    n_o = w2k.shape[1]
    const = lambda b: (0, 0)
    return pl.pallas_call(
        _cmp_kernel,
        out_shape=(jax.ShapeDtypeStruct((bsz, nc, n_o), BF16), jax.ShapeDtypeStruct((bsz, n_o, nc), BF16)),
        grid=(bsz,),
        in_specs=[pl.BlockSpec((1, nc, kk), lambda b: (b, 0, 0)),
                  pl.BlockSpec((1, kk), const), pl.BlockSpec((1, kk), const),
                  pl.BlockSpec((kk, n_h), const), pl.BlockSpec((kk, n_h), const),
                  pl.BlockSpec((n_h, n_o), const), pl.BlockSpec((n_o, n_h), const)],
        out_specs=(pl.BlockSpec((1, nc, n_o), lambda b: (b, 0, 0)), pl.BlockSpec((1, n_o, nc), lambda b: (b, 0, 0))),
        compiler_params=_cparams(("parallel",)),
        name="nsa_compress",
    )(sub, pet, peb, wt, wb, w2k, w2vt)


def _flash_reset(m_scr, acc_scr):
    m_scr[...] = jnp.full(m_scr.shape, NEG, F32)
    acc_scr[...] = jnp.zeros(acc_scr.shape, F32)


def _qk(kget, qxs, start, size):
    k0 = kget(0, start, size)
    if kget(1, start, size) is None:
        return _dot(k0, jnp.concatenate(qxs, axis=1))
    return jnp.concatenate([_dot(k0, qxs[0])] + [_dot(kget(c, start, size), qxs[c])
                                                 for c in range(1, len(qxs))], axis=1)


def _softmax_pv(s, vget, nq, start, size, m_scr, acc_scr, bias=None):
    if bias is not None:
        s = s + jnp.concatenate([bias] * nq, axis=1)
    m_prev = m_scr[...]
    m_new = jnp.maximum(m_prev, jnp.max(s, axis=0, keepdims=True))
    alpha = jnp.exp(m_prev - m_new)
    p = jnp.exp(s - m_new).astype(BF16)
    v0 = vget(0, start, size)
    if vget(1, start, size) is None:
        pv = _dot(v0, p)
    else:
        pv = jnp.concatenate([_dot(v0, p[:, :TQ])] + [_dot(vget(c, start, size), p[:, c * TQ:(c + 1) * TQ])
                                                      for c in range(1, nq)], axis=1)
    acc_scr[...] = alpha * acc_scr[...] + pv
    m_scr[...] = m_new


def _rows(ref, start, size, lane0=0):
    return ref[pl.ds(pl.multiple_of(start, TQ), size), lane0:lane0 + SLOT]


def _cols(ref, start, size, row0=0):
    return ref[row0:row0 + PV_ROWS, pl.ds(pl.multiple_of(start, TQ), size)]


def _to_token_rows(o_t):
    pad = jnp.zeros((SLOT - PV_ROWS, o_t.shape[1]), o_t.dtype)
    return jnp.concatenate([o_t, pad], axis=0).T


def _flash_causal(kget, vget, qxs, i, cb_ref, sa_scr, sb_scr, m_scr, acc_scr):
    tq = TQ
    tk = 2 * tq
    nq = len(qxs)
    n_steps = i // 2 + 1
    sa_scr[...] = _qk(kget, qxs, 0, tk)

    def body(jj, carry):
        j = 2 * jj
        s = sa_scr[...]
        sb_scr[...] = _qk(kget, qxs, (j + 1) * tk, tk)
        _softmax_pv(s, vget, nq, j * tk, tk, m_scr, acc_scr)
        s = sb_scr[...]
        sa_scr[...] = _qk(kget, qxs, (j + 2) * tk, tk)
        _softmax_pv(s, vget, nq, (j + 1) * tk, tk, m_scr, acc_scr)
        return carry

    lax.fori_loop(0, (n_steps - 1) // 2, body, 0)
    last = n_steps - 1
    bias = cb_ref[i % 2]

    @pl.when(last % 2 == 0)
    def _():
        _softmax_pv(sa_scr[...], vget, nq, last * tk, tk, m_scr, acc_scr, bias=bias)

    @pl.when(last % 2 == 1)
    def _():
        s = sa_scr[...]
        sb_scr[...] = _qk(kget, qxs, last * tk, tk)
        _softmax_pv(s, vget, nq, (last - 1) * tk, tk, m_scr, acc_scr)
        _softmax_pv(sb_scr[...], vget, nq, last * tk, tk, m_scr, acc_scr, bias=bias)


def _nsa_kernel(q_ref, kc_ref, vct_ref, ks_ref, vst_ref, kw_ref, vwt_ref, g_ref, ovt_ref, fq_ref, cb_ref, wb_ref,
                o_ref, m_scr, acc_scr, imp_scr, sa_scr, sb_scr, *, tq, nsel):
    i = pl.program_id(2)
    r4 = NSA_GROUP
    rr = r4 * tq
    q4 = [q_ref[r * SLOT:(r + 1) * SLOT, :] for r in range(r4)]
    fq = fq_ref[0]

    kc = kc_ref[0]
    nc = kc.shape[0]
    qs = jnp.concatenate(q4, axis=1)
    s = _dot(kc, qs)
    t_row = i * tq + (lax.broadcasted_iota(I32, (1, rr), 1) & (tq - 1))
    last_c = (t_row - (CMP_BLOCK - 1)) // CMP_STRIDE
    cmask = lax.broadcasted_iota(I32, (nc, rr), 0) <= last_c
    s = jnp.where(cmask, s, NEG)
    mx = jnp.max(s, axis=0, keepdims=True)
    e = jnp.where(cmask, jnp.exp(s - mx), 0.0)
    lsum = jnp.sum(e, axis=0, keepdims=True)
    p_cmp = (e * (1.0 / jnp.where(lsum > 0.0, lsum, 1.0))).astype(BF16)
    o_cmp = _dot(vct_ref[0], p_cmp)

    ovt = ovt_ref[...]
    imp4 = _dot(ovt, p_cmp)
    imp = imp4[:, 0:tq]
    for r in range(1, r4):
        imp = imp + imp4[:, r * tq:(r + 1) * tq]
    nbp = imp.shape[0]
    jidx = lax.broadcasted_iota(I32, (nbp, tq), 0)
    cur = (i * tq + lax.broadcasted_iota(I32, (nbp, tq), 1)) // SLC_BLOCK
    forced = (jidx == 0) | (jidx == cur) | (jidx == cur - 1)
    key = jnp.where(forced, FORCED_KEY, jnp.where(jidx > cur, -1, lax.bitcast_convert_type(imp, I32)))
    imp_scr[...] = key

    def rank_body(jj, rank):
        for jp in (2 * jj, 2 * jj + 1):
            row = imp_scr[pl.ds(jp, 1), :]
            rank = rank + (row > key - (jidx > jp).astype(I32)).astype(I32)
        return rank

    n_live = (i + 1) * (tq // SLC_BLOCK)
    rank = lax.fori_loop(0, n_live // 2, rank_body, jnp.zeros((nbp, tq), I32))
    sel = (rank < nsel) & (jidx <= cur)
    mask_t = jnp.where(sel, 0.0, NEG)
    feat = jnp.concatenate([jnp.zeros((SLOT - nbp, tq), F32), mask_t], axis=0)

    qxs = [q4[r] + (feat + fq[:, r:r + 1]).astype(BF16) for r in range(r4)]
    kget = lambda c, start, size: _rows(ks_ref, start, size) if c == 0 else None
    vget = lambda c, start, size: _cols(vst_ref, start, size) if c == 0 else None
    _flash_reset(m_scr, acc_scr)
    _flash_causal(kget, vget, qxs, i, cb_ref, sa_scr, sb_scr, m_scr, acc_scr)
    o_slc = acc_scr[...] * (1.0 / acc_scr[HEAD_DIM:HEAD_DIM + 1, :])

    qx = jnp.concatenate([q4[r] + fq[:, r4 + r:r4 + r + 1].astype(BF16) for r in range(r4)], axis=1)
    w0 = jnp.maximum(i - 2, 0) * tq
    s = _dot(_rows(kw_ref, w0, 3 * tq), qx)
    s = s + jnp.concatenate([wb_ref[jnp.minimum(i, 2)]] * r4, axis=1)
    e = jnp.exp(s - jnp.max(s, axis=0, keepdims=True)).astype(BF16)
    o_win = _dot(_cols(vwt_ref, w0, 3 * tq), e)
    o_win = o_win * (1.0 / o_win[HEAD_DIM:HEAD_DIM + 1, :])

    gate = jax.nn.sigmoid(g_ref[...])
    for r in range(r4):
        sl = slice(r * tq, (r + 1) * tq)
        o = (gate[3 * r:3 * r + 1, :] * o_cmp[:PV_ROWS, sl] + gate[3 * r + 1:3 * r + 2, :] * o_slc[:, sl]
             + gate[3 * r + 2:3 * r + 3, :] * o_win[:, sl])
        o_ref[:, r * SLOT:(r + 1) * SLOT] = _to_token_rows(o).astype(o_ref.dtype)


def _nsa_attention(zat, zb, gt, kc, vct, ovt, fq, cb, wb, bsz, seq):
    tq = TQ
    ni = seq // tq
    nc = kc.shape[1]
    nsel = min(SLC_TOPK, seq // SLC_BLOCK)
    assert seq // SLC_BLOCK <= 64 and WINDOW == 2 * tq and ni % 2 == 0 and ni >= 3
    rr = NSA_GROUP * tq
    gw = NSA_GROUP * SLOT
    in_specs = [
        pl.BlockSpec((gw, tq), lambda b, g, i: (g, b * ni + i)),
        pl.BlockSpec((1, nc, SLOT), lambda b, g, i: (b, 0, g)),
        pl.BlockSpec((1, SLOT, nc), lambda b, g, i: (b, g, 0)),
        pl.BlockSpec((seq, SLOT), lambda b, g, i: (b, g)),
        pl.BlockSpec((SLOT, seq), lambda b, g, i: (_SLOT_SLC_V + g, b)),
        pl.BlockSpec((seq, SLOT), lambda b, g, i: (b, 2 + g)),
        pl.BlockSpec((SLOT, seq), lambda b, g, i: (_SLOT_WIN_V + g, b)),
        pl.BlockSpec((SLOT, tq), lambda b, g, i: (g, b * ni + i)),
        pl.BlockSpec(ovt.shape, lambda b, g, i: (0, 0)),
        pl.BlockSpec((1, SLOT, 8), lambda b, g, i: (g, 0, 0)),
        pl.BlockSpec(cb.shape, lambda b, g, i: (0, 0, 0)),
        pl.BlockSpec(wb.shape, lambda b, g, i: (0, 0, 0)),
    ]
    return pl.pallas_call(
        functools.partial(_nsa_kernel, tq=tq, nsel=nsel),
        out_shape=jax.ShapeDtypeStruct((bsz * seq, NSA_HEADS * SLOT), BF16),
        grid=(bsz, NSA_KV_HEADS, ni),
        in_specs=in_specs,
        out_specs=pl.BlockSpec((tq, gw), lambda b, g, i: (b * ni + i, g)),
        scratch_shapes=[pltpu.VMEM((1, rr), F32),
                        pltpu.VMEM((PV_ROWS, rr), F32), pltpu.VMEM((64, tq), I32),
                        pltpu.VMEM((2 * tq, rr), F32), pltpu.VMEM((2 * tq, rr), F32)],
        compiler_params=_cparams(("parallel", "parallel", "arbitrary")),
        name="nsa_attention",
    )(zat, kc, vct, zb, zat, zb, zat, gt, ovt, fq, cb, wb)


def _moba_kernel(q_ref, k_ref, vt_ref, fq_ref, cb_ref, o_ref, m_scr, acc_scr, km_scr, gate_scr,
                 sa_scr, sb_scr, *, tq, nblk, ntop):
    i = pl.program_id(2)
    nh = MOBA_HPS

    @pl.when(i == 0)
    def _():
        for hh in range(nh):
            kh = k_ref[:, hh * SLOT:(hh + 1) * SLOT].astype(F32)
            km = jnp.mean(kh.reshape(nblk, tq, SLOT), axis=1)
            if nblk < 16:
                km = jnp.concatenate([km, jnp.zeros((16 - nblk, SLOT), F32)], axis=0)
            km_scr[hh] = km

    qs = [q_ref[hh * SLOT:(hh + 1) * SLOT, :] for hh in range(nh)]
    gates = [_dot(km_scr[hh].astype(BF16), qs[hh]) for hh in range(nh)]
    for hh in range(nh):
        gate_scr[hh] = gates[hh]
    jidx = lax.broadcasted_iota(I32, (16, tq), 0)

    def rank_body(jp, ranks):
        out = []
        for hh in range(nh):
            row = gate_scr[hh, pl.ds(jp, 1), :]
            better = (row > gates[hh]) | ((row == gates[hh]) & (jp < jidx))
            out.append(ranks[hh] + better.astype(I32))
        return tuple(out)

    ranks = lax.fori_loop(0, i, rank_body, tuple(jnp.zeros((16, tq), I32) for _ in range(nh)))
    qxs = []
    for hh in range(nh):
        keep = ((jidx < i) & (ranks[hh] < ntop)) | (jidx == i)
        mask_t = jnp.where(keep, 0.0, NEG)
        feat = jnp.concatenate([jnp.zeros((HEAD_DIM, tq), F32), mask_t,
                                jnp.zeros((SLOT - HEAD_DIM - 16, tq), F32)], axis=0)
        qxs.append(qs[hh] + (feat + fq_ref[hh][:, 0:1]).astype(BF16))
    kget = lambda c, start, size: _rows(k_ref, start, size, c * SLOT)
    vget = lambda c, start, size: _cols(vt_ref, start, size, c * SLOT)
    _flash_reset(m_scr, acc_scr)
    _flash_causal(kget, vget, qxs, i, cb_ref, sa_scr, sb_scr, m_scr, acc_scr)
    o = acc_scr[...] * (1.0 / acc_scr[HEAD_DIM:HEAD_DIM + 1, :])
    for hh in range(nh):
        o_ref[:, hh * SLOT:(hh + 1) * SLOT] = _to_token_rows(o[:, hh * tq:(hh + 1) * tq]).astype(o_ref.dtype)


def _moba_attention(zat, zb, fq, cb, bsz, seq):
    tq = TQ
    assert tq == MOBA_BLOCK and seq % tq == 0
    ni = seq // tq
    assert ni <= 16 and ni % 2 == 0
    ntop = min(MOBA_TOPK, ni)
    nh = MOBA_HPS
    hw = nh * SLOT
    q0 = _SLOT_MOBA_Q // nh
    v0 = _SLOT_MOBA_V // nh
    return pl.pallas_call(
        functools.partial(_moba_kernel, tq=tq, nblk=ni, ntop=ntop),
        out_shape=jax.ShapeDtypeStruct((bsz * seq, MOBA_HEADS * SLOT), BF16),
        grid=(bsz, MOBA_HEADS // nh, ni),
        in_specs=[pl.BlockSpec((hw, tq), lambda b, h, i: (q0 + h, b * ni + i)),
                  pl.BlockSpec((seq, hw), lambda b, h, i: (b, 1 + h)),
                  pl.BlockSpec((hw, seq), lambda b, h, i: (v0 + h, b)),
                  pl.BlockSpec((nh, SLOT, 8), lambda b, h, i: (h, 0, 0)),
                  pl.BlockSpec(cb.shape, lambda b, h, i: (0, 0, 0))],
        out_specs=pl.BlockSpec((tq, hw), lambda b, h, i: (b * ni + i, h)),
        scratch_shapes=[pltpu.VMEM((1, nh * tq), F32),
                        pltpu.VMEM((PV_ROWS, nh * tq), F32),
                        pltpu.VMEM((nh, 16, SLOT), F32), pltpu.VMEM((nh, 16, tq), F32),
                        pltpu.VMEM((2 * tq, nh * tq), F32), pltpu.VMEM((2 * tq, nh * tq), F32)],
        compiler_params=_cparams(("parallel", "parallel", "arbitrary")),
        name="moba_attention",
    )(zat, zb, zat, fq, cb)


def _merge_kernel(ya_ref, yb_ref, hc_ref, bc_ref, cc_ref, hcp_ref, ccp_ref, g0_ref, g1_ref, g2_ref,
                  h_ref, cw_ref, wa_ref, wb_ref, wc_ref, wo_ref, lg_ref, lb_ref, o_ref, *, tm, per, alpha):
    i = pl.program_id(0)
    u = cc_ref[...].astype(F32) * hc_ref[...].astype(F32)
    first = (i % per) == 0
    up = jnp.where(first, 0.0, ccp_ref[...].astype(F32) * hcp_ref[...].astype(F32))
    rowi = lax.broadcasted_iota(I32, u.shape, 0)
    u1 = jnp.where(rowi == 0, up[15:16, :], pltpu.roll(u, 1, 0))
    u2 = jnp.where(rowi == 0, up[14:15, :], jnp.where(rowi == 1, up[15:16, :], pltpu.roll(u, 2, 0)))
    cw = cw_ref[...]
    yc = bc_ref[...].astype(F32) * (cw[0:1, :] * u2 + cw[1:2, :] * u1 + cw[2:3, :] * u)
    merged = (jax.nn.sigmoid(g0_ref[...].astype(F32)) * _dot(ya_ref[...], wa_ref[...])
              + jax.nn.sigmoid(g1_ref[...].astype(F32)) * _dot(yb_ref[...], wb_ref[...])
              + jax.nn.sigmoid(g2_ref[...].astype(F32)) * _dot(yc.astype(BF16), wc_ref[...]))
    mix = _dot(merged.astype(BF16), wo_ref[...])
    o_ref[...] = _layer_norm(alpha * h_ref[...] + mix, lg_ref[...], lb_ref[...])


def _merge(ya, yb, zc, h, cw, wa, wb, wc, wo, lg, lb, seq, alpha):
    tm = 512
    t = h.shape[0]
    per = seq // tm
    d = D_MODEL
    row = lambda i: (i, 0)
    const = lambda i: (0, 0)
    prev = lambda c: (lambda i: (jnp.maximum(i * (tm // 16) - 1, 0), c))
    in_specs = [
        pl.BlockSpec((tm, d), row), pl.BlockSpec((tm, d), row),
        pl.BlockSpec((tm, CONV_CH), lambda i: (i, 1)), pl.BlockSpec((tm, CONV_CH), lambda i: (i, 2)),
        pl.BlockSpec((tm, CONV_CH), lambda i: (i, 3)),
        pl.BlockSpec((16, CONV_CH), prev(1)), pl.BlockSpec((16, CONV_CH), prev(3)),
        pl.BlockSpec((tm, d), lambda i: (i, 2)), pl.BlockSpec((tm, d), lambda i: (i, 3)),
        pl.BlockSpec((tm, d), lambda i: (i, 4)),
        pl.BlockSpec((tm, d), row),
        pl.BlockSpec((8, CONV_CH), const),
        pl.BlockSpec((d, d), const), pl.BlockSpec((d, d), const), pl.BlockSpec((CONV_CH, d), const),
        pl.BlockSpec((d, d), const), pl.BlockSpec((1, d), const), pl.BlockSpec((1, d), const),
    ]
    return pl.pallas_call(
        functools.partial(_merge_kernel, tm=tm, per=per, alpha=alpha),
        out_shape=jax.ShapeDtypeStruct((t, d), F32),
        grid=(t // tm,),
        in_specs=in_specs,
        out_specs=pl.BlockSpec((tm, d), row),
        compiler_params=_cparams(("parallel",)),
        name="mixer_merge",
    )(ya, yb, zc, zc, zc, zc, zc, zc, zc, zc, h, cw, wa, wb, wc, wo, lg, lb)


def _xattn_kernel(h_ref, kv_ref, wq_ref, wo_ref, lg_ref, lb_ref, o_ref, ob_ref, *, alpha):
    h = h_ref[...]
    q = _dot(h.astype(BF16), wq_ref[...]).astype(BF16)
    kv = kv_ref[...]
    nh = XATTN_HEADS
    hd = XATTN_HEAD_DIM
    outs = []
    for hh in range(nh):
        s = _dot_t(q[:, hh * hd:(hh + 1) * hd], kv[:, hh * hd:(hh + 1) * hd]) * (hd ** -0.5)
        s = s - jnp.max(s, axis=-1, keepdims=True)
        e = jnp.exp(s)
        p = e / jnp.sum(e, axis=-1, keepdims=True)
        outs.append(_dot(p.astype(BF16), kv[:, (nh + hh) * hd:(nh + hh + 1) * hd]))
    o = jnp.concatenate(outs, axis=-1).astype(BF16)
    y = _layer_norm(alpha * h + _dot(o, wo_ref[...]), lg_ref[...], lb_ref[...])
    o_ref[...] = y
    bits = lax.bitcast_convert_type(y.astype(BF16).astype(F32), I32)
    half = y.shape[1] // 2
    ob_ref[...] = (bits[:, half:] & -65536) | lax.shift_right_logical(bits[:, :half], 16)


def _unpack_pairs(words):
    lo = lax.bitcast_convert_type(lax.shift_left(words, 16), F32).astype(BF16)
    hi = lax.bitcast_convert_type(words & -65536, F32).astype(BF16)
    return lo, hi


def _xattn(h, kv, wq, wo, lg, lb, seq, mlen, alpha):
    tm = 512
    t = h.shape[0]
    per = seq // tm
    d = D_MODEL
    row = lambda i: (i, 0)
    const = lambda i: (0, 0)
    return pl.pallas_call(
        functools.partial(_xattn_kernel, alpha=alpha),
        out_shape=(jax.ShapeDtypeStruct((t, d), F32), jax.ShapeDtypeStruct((t, d // 2), I32)),
        grid=(t // tm,),
        in_specs=[pl.BlockSpec((tm, d), row), pl.BlockSpec((mlen, kv.shape[1]), lambda i: (i // per, 0)),
                  pl.BlockSpec(wq.shape, const), pl.BlockSpec(wo.shape, const),
                  pl.BlockSpec((1, d), const), pl.BlockSpec((1, d), const)],
        out_specs=(pl.BlockSpec((tm, d), row), pl.BlockSpec((tm, d // 2), row)),
        compiler_params=_cparams(("parallel",)),
        name="mem_xattn",
    )(h, kv, wq, wo, lg, lb)


def _expert_kernel(be_ref, nu_ref, nxt_ref, slot_ref, x_ref, w13_hbm, w2_hbm, o_ref,
                   w13f_scr, w2f_scr, w13b_scr, w2b_scr, sem, *, layer):
    i = pl.program_id(0)
    e = be_ref[i]
    s = slot_ref[i]

    def weight_copies(expert, slot):
        return (pltpu.make_async_copy(w13_hbm.at[layer, expert], w13f_scr.at[slot], sem.at[0, slot]),
                pltpu.make_async_copy(w2_hbm.at[layer, expert], w2f_scr.at[slot], sem.at[1, slot]))

    @pl.when(i == 0)
    def _():
        for cp in weight_copies(e, s):
            cp.start()

    @pl.when((i == 0) | (e != be_ref[jnp.maximum(i - 1, 0)]))
    def _():
        for cp in weight_copies(e, s):
            cp.wait()
        nxt = nxt_ref[i]

        @pl.when(nxt >= 0)
        def _():
            for cp in weight_copies(nxt, 1 - s):
                cp.start()

        w13b_scr[...] = w13f_scr[s].astype(BF16)
        w2b_scr[...] = w2f_scr[s].astype(BF16)

    @pl.when(i < nu_ref[0])
    def _():
        x_lo, x_hi = _unpack_pairs(x_ref[...])
        half = x_lo.shape[1]
        hmid = _dot(x_lo, w13b_scr[:half, :]) + _dot(x_hi, w13b_scr[half:, :])
        a = hmid[:, :EXPERT_DIM]
        g = hmid[:, EXPERT_DIM:]
        act = (a * jax.nn.sigmoid(a) * g).astype(BF16)
        o_ref[...] = _dot(act, w2b_scr[...]).astype(o_ref.dtype)


def _experts(xg, w13, w2, block_e, n_used, next_e, slot, layer):
    n_pad = xg.shape[0]
    d = 2 * xg.shape[1]
    rows = EXP_ROWS
    nb = n_pad // rows
    blk = lambda i, be, nu, nx, sl: (jnp.minimum(i, nu[0] - 1), 0)
    grid_spec = pltpu.PrefetchScalarGridSpec(
        num_scalar_prefetch=4,
        grid=(nb,),
        in_specs=[pl.BlockSpec((rows, d // 2), blk),
                  pl.BlockSpec(memory_space=pl.ANY), pl.BlockSpec(memory_space=pl.ANY)],
        out_specs=pl.BlockSpec((rows, d), blk),
        scratch_shapes=[pltpu.VMEM((2, d, 2 * EXPERT_DIM), F32), pltpu.VMEM((2, EXPERT_DIM, d), F32),
                        pltpu.VMEM((d, 2 * EXPERT_DIM), BF16), pltpu.VMEM((EXPERT_DIM, d), BF16),
                        pltpu.SemaphoreType.DMA((2, 2))],
    )
    return pl.pallas_call(
        functools.partial(_expert_kernel, layer=layer),
        out_shape=jax.ShapeDtypeStruct((n_pad, d), BF16),
        grid_spec=grid_spec,
        compiler_params=_cparams(("arbitrary",)),
        name="moe_experts",
    )(block_e, n_used, next_e, slot, xg, w13, w2)


def _moe_out_kernel(h_ref, y8_ref, wk_ref, w13_ref, w2_ref, lg_ref, lb_ref, o_ref, ob_ref, *, alpha):
    h = h_ref[...]
    hmid = _dot(h.astype(BF16), w13_ref[...])
    a = hmid[:, :EXPERT_DIM]
    g = hmid[:, EXPERT_DIM:]
    acc = alpha * h + _dot((a * jax.nn.sigmoid(a) * g).astype(BF16), w2_ref[...])
    wk = wk_ref[...]
    for k in range(TOP_K):
        acc = acc + wk[:, k:k + 1] * y8_ref[k].astype(F32)
    y = _layer_norm(acc, lg_ref[...], lb_ref[...])
    o_ref[...] = y
    ob_ref[...] = y.astype(BF16)


def _moe_out(h, y8, wk, w13, w2, lg, lb, alpha):
    tm = 256
    t, d = h.shape
    row = lambda i: (i, 0)
    const = lambda i: (0, 0)
    return pl.pallas_call(
        functools.partial(_moe_out_kernel, alpha=alpha),
        out_shape=(jax.ShapeDtypeStruct((t, d), F32), jax.ShapeDtypeStruct((t, d), BF16)),
        grid=(t // tm,),
        in_specs=[pl.BlockSpec((tm, d), row), pl.BlockSpec((TOP_K, tm, d), lambda i: (0, i, 0)),
                  pl.BlockSpec((tm, TOP_K), row),
                  pl.BlockSpec(w13.shape, const), pl.BlockSpec(w2.shape, const),
                  pl.BlockSpec((1, d), const), pl.BlockSpec((1, d), const)],
        out_specs=(pl.BlockSpec((tm, d), row), pl.BlockSpec((tm, d), row)),
        compiler_params=_cparams(("parallel",)),
        name="moe_shared_ln",
    )(h, y8, wk, w13, w2, lg, lb)


def _dest_kernel(idx_ref, rank_ref, ps_ref, o_ref):
    idx = idx_ref[...]
    ps = ps_ref[...]
    tm = idx.shape[1]
    eidx = lax.broadcasted_iota(I32, (N_EXPERTS, tm), 0)
    rows = [jnp.sum(jnp.where(eidx == idx[k:k + 1, :], ps, 0.0), axis=0, keepdims=True) for k in range(TOP_K)]
    o_ref[...] = jnp.concatenate(rows, axis=0).astype(I32) + rank_ref[...]


def _dest(idx, rank, pstarts):
    tm = 1024
    t = idx.shape[1]
    tm = min(tm, t)
    col = lambda i: (0, i)
    return pl.pallas_call(
        _dest_kernel,
        out_shape=jax.ShapeDtypeStruct((TOP_K, t), I32),
        grid=(t // tm,),
        in_specs=[pl.BlockSpec((TOP_K, tm), col), pl.BlockSpec((TOP_K, tm), col),
                  pl.BlockSpec((N_EXPERTS, 1), lambda i: (0, 0))],
        out_specs=pl.BlockSpec((TOP_K, tm), col),
        compiler_params=_cparams(("parallel",)),
        name="moe_dest",
    )(idx, rank, pstarts)


def _router_kernel(h_ref, rwt_ref, rb_ref, tri_ref, idx_ref, w_ref, rank_ref, cnt_ref, carry_scr, *, tm):
    i = pl.program_id(0)

    @pl.when(i == 0)
    def _():
        carry_scr[...] = jnp.zeros(carry_scr.shape, F32)

    ne = N_EXPERTS
    gsz = ne // N_GROUPS
    s = jax.nn.sigmoid(_dot_t(rwt_ref[...], h_ref[...].astype(BF16)))
    sb = s + rb_ref[...]
    sb3 = sb.reshape(N_GROUPS, gsz, tm)
    li = lax.broadcasted_iota(I32, (N_GROUPS, gsz, tm), 1)
    m1 = jnp.max(sb3, axis=1, keepdims=True)
    first = jnp.min(jnp.where(sb3 == m1, li, gsz), axis=1, keepdims=True)
    m2 = jnp.max(jnp.where(li == first, -jnp.inf, sb3), axis=1, keepdims=True)
    gs = (m1 + m2).reshape(N_GROUPS, tm)
    gi = lax.broadcasted_iota(I32, (N_GROUPS, tm), 0)
    grank = jnp.zeros((N_GROUPS, tm), I32)
    for gp in range(N_GROUPS):
        row = gs[gp:gp + 1, :]
        grank = grank + ((row > gs) | ((row == gs) & (gp < gi))).astype(I32)
    gkeep = (grank < TOPK_GROUPS).astype(F32)
    ekeep = jnp.broadcast_to(gkeep[:, None, :], (N_GROUPS, gsz, tm)).reshape(ne, tm)
    cand = jnp.where(ekeep > 0.0, sb, NEG)
    eidx = lax.broadcasted_iota(I32, (ne, tm), 0)
    sel = jnp.zeros((ne, tm), F32)
    idxs, wts = [], []
    for _ in range(TOP_K):
        mx = jnp.max(cand, axis=0, keepdims=True)
        ik = jnp.min(jnp.where(cand == mx, eidx, ne), axis=0, keepdims=True)
        hit = eidx == ik
        wts.append(jnp.sum(jnp.where(hit, s, 0.0), axis=0, keepdims=True))
        idxs.append(ik)
        sel = jnp.where(hit, 1.0, sel)
        cand = jnp.where(hit, -jnp.inf, cand)
    before = _dot(sel.astype(BF16), tri_ref[...]) + carry_scr[...]
    ranks = [jnp.sum(jnp.where(eidx == ik, before, 0.0), axis=0, keepdims=True) for ik in idxs]
    carry_scr[...] = carry_scr[...] + jnp.sum(sel, axis=1, keepdims=True)
    w = jnp.concatenate(wts, axis=0)
    idx_ref[...] = jnp.concatenate(idxs, axis=0)
    w_ref[...] = w / jnp.sum(w, axis=0, keepdims=True) * ROUTE_SCALE
    rank_ref[...] = jnp.concatenate(ranks, axis=0).astype(I32)
    cnt_ref[...] = jnp.broadcast_to(carry_scr[...], cnt_ref.shape)


def _router(h, rwt, rb):
    tm = 512
    t, d = h.shape
    tri = jnp.asarray(np.triu(np.ones((tm, tm), np.float32), 1)).astype(BF16)
    const = lambda i: (0, 0)
    col = lambda i: (0, i)
    return pl.pallas_call(
        functools.partial(_router_kernel, tm=tm),
        out_shape=(jax.ShapeDtypeStruct((TOP_K, t), I32), jax.ShapeDtypeStruct((TOP_K, t), F32),
                   jax.ShapeDtypeStruct((TOP_K, t), I32), jax.ShapeDtypeStruct((N_EXPERTS, SLOT), F32)),
        grid=(t // tm,),
        in_specs=[pl.BlockSpec((tm, d), lambda i: (i, 0)), pl.BlockSpec((N_EXPERTS, d), const),
                  pl.BlockSpec((N_EXPERTS, 1), const), pl.BlockSpec((tm, tm), const)],
        out_specs=(pl.BlockSpec((TOP_K, tm), col), pl.BlockSpec((TOP_K, tm), col),
                   pl.BlockSpec((TOP_K, tm), col), pl.BlockSpec((N_EXPERTS, SLOT), const)),
        scratch_shapes=[pltpu.VMEM((N_EXPERTS, 1), F32)],
        compiler_params=_cparams(("arbitrary",)),
        name="moe_router",
    )(h, rwt, rb, tri)


def _dispatch(hp, dest_flat, n_pad):
    t, w = hp.shape
    per_w = t // (SC_CORES * SC_SUBCORES)
    assert per_w * SC_CORES * SC_SUBCORES == t and per_w % SC_CHUNK == 0 and dest_flat.shape[0] == TOP_K * t
    mesh = plsc.VectorSubcoreMesh(core_axis_name="c", subcore_axis_name="s")

    @functools.partial(
        pl.kernel, out_type=jax.ShapeDtypeStruct((n_pad, w), I32), mesh=mesh,
        scratch_types=[pltpu.VMEM((SC_CHUNK,), I32)] * TOP_K
        + [pltpu.VMEM((SC_CHUNK, w), I32), pltpu.SemaphoreType.DMA],
        name="moe_dispatch")
    def scatter_rows(hp_hbm, dest_hbm, xg_hbm, *scratch):
        idx_vs, rows_v, sem = scratch[:TOP_K], scratch[TOP_K], scratch[TOP_K + 1]
        wid = lax.axis_index("s") * SC_CORES + lax.axis_index("c")
        base = wid * per_w

        @pl.loop(0, per_w // SC_CHUNK)
        def _(j):
            t0 = pl.multiple_of(base + j * SC_CHUNK, SC_CHUNK)
            pltpu.sync_copy(hp_hbm.at[pl.ds(t0, SC_CHUNK)], rows_v)
            for k in range(TOP_K):
                pltpu.sync_copy(dest_hbm.at[pl.ds(pl.multiple_of(k * t + t0, SC_CHUNK), SC_CHUNK)], idx_vs[k])
            copies = [pltpu.async_copy(rows_v, xg_hbm.at[idx_vs[k]], sem) for k in range(TOP_K)]
            for cp in copies:
                cp.wait()

    return scatter_rows(hp, dest_flat)


def _moe(h, hp, router_w, router_b, exp_w13, exp_w2, layer, shared_w13, shared_w2, lg, lb, alpha):
    t, d = h.shape
    idx, wts, rank, cnt = _router(h, router_w.T, router_b.astype(F32).reshape(N_EXPERTS, 1))
    rows = EXP_ROWS
    n_a = t * TOP_K
    counts = cnt[:, 0].astype(I32)
    pcounts = (counts + rows - 1) // rows * rows
    pends = jnp.cumsum(pcounts)
    pstarts = pends - pcounts
    dest = _dest(idx, rank, pstarts.astype(F32).reshape(N_EXPERTS, 1))
    n_blocks = -(-n_a // rows) + N_EXPERTS
    n_pad = n_blocks * rows
    n_used = (pends[-1:] // rows).astype(I32)
    first_row = jnp.minimum(jnp.arange(n_blocks, dtype=I32), n_used - 1) * rows
    block_e = jnp.sum((pends[None, :] <= first_row[:, None]).astype(I32), axis=1)
    eidx = jnp.arange(N_EXPERTS, dtype=I32)
    has_rows = pcounts > 0
    at_or_after = lax.cummin(jnp.where(has_rows, eidx, N_EXPERTS), axis=0, reverse=True)
    after = jnp.concatenate([at_or_after[1:], jnp.full((1,), N_EXPERTS, I32)])
    next_e = jnp.take(jnp.where(after < N_EXPERTS, after, -1), block_e)
    slot = jnp.take((jnp.cumsum(has_rows.astype(I32)) - 1) % 2, block_e).astype(I32)
    xg = _dispatch(hp, dest.reshape(-1), n_pad)
    out = _experts(xg, exp_w13, exp_w2, block_e, n_used, next_e.astype(I32), slot, layer)
    y8 = jnp.take(out, dest, axis=0, mode="clip")
    return _moe_out(h, y8, wts.T, shared_w13, shared_w2, lg, lb, alpha)


def _pad_slots(w, scale=1.0):
    dm = w.shape[0]
    nh = w.shape[1] // HEAD_DIM
    w = (w * scale).reshape(dm, nh, HEAD_DIM)
    return jnp.concatenate([w, jnp.zeros_like(w)], axis=-1).reshape(dm, nh * SLOT)


def _pad_rows(w):
    n = w.shape[1]
    nh = w.shape[0] // HEAD_DIM
    w = w.reshape(nh, HEAD_DIM, n)
    return jnp.concatenate([w, jnp.zeros_like(w)], axis=1).reshape(nh * SLOT, n)


def _alibi(n):
    return np.exp2(-8.0 * np.arange(1, n + 1, dtype=np.float64) / n).astype(np.float32)


def _key_features(seq):
    p = np.arange(seq)
    slc = np.zeros((seq, SLOT), np.float32)
    slc[:, HEAD_DIM] = p % SLC_BLOCK
    blk = p // SLC_BLOCK
    nz = blk > 0
    slc[p[nz], HEAD_DIM + blk[nz]] = 1.0
    win = np.zeros((seq, SLOT), np.float32)
    win[:, HEAD_DIM] = p // 64
    win[:, HEAD_DIM + 1] = p % 64
    mob = np.zeros((seq, SLOT), np.float32)
    mob[p, HEAD_DIM + p // MOBA_BLOCK] = 1.0
    mob[:, HEAD_DIM + 16] = p % MOBA_BLOCK
    return np.concatenate([slc] * 2 + [win] * 2 + [mob] * MOBA_HEADS, axis=1)


def _query_features():
    sl = _alibi(NSA_HEADS)
    nsa = np.zeros((NSA_KV_HEADS, SLOT, 8), np.float32)
    for g in range(NSA_KV_HEADS):
        for r in range(NSA_GROUP):
            s = sl[g * NSA_GROUP + r]
            nsa[g, HEAD_DIM, r] = s
            nsa[g, HEAD_DIM + 1:, r] = s * SLC_BLOCK * np.arange(1, 64)
            nsa[g, HEAD_DIM, NSA_GROUP + r] = s * 64
            nsa[g, HEAD_DIM + 1, NSA_GROUP + r] = s
    sm = _alibi(MOBA_HEADS)
    mob = np.zeros((MOBA_HEADS, SLOT, 8), np.float32)
    for h in range(MOBA_HEADS):
        mob[h, HEAD_DIM:HEAD_DIM + 16, 0] = sm[h] * MOBA_BLOCK * np.arange(16)
        mob[h, HEAD_DIM + 16, 0] = sm[h]
    return nsa, mob


def _mask_biases():
    t = np.arange(TQ)[None, :]
    p2 = np.arange(2 * TQ)[:, None]
    cb = np.stack([p2 <= t, p2 <= TQ + t])
    p3 = np.arange(3 * TQ)[:, None]
    dist = np.stack([c * TQ + t - p3 for c in range(3)])
    wb = (dist >= 0) & (dist < WINDOW)
    to_bias = lambda m: np.where(m, 0.0, NEG).astype(np.float32)
    return to_bias(cb), to_bias(wb)


def _overlap_t(seq):
    nc = seq // CMP_STRIDE
    c_start = np.arange(nc) * CMP_STRIDE
    b_start = np.arange(64) * SLC_BLOCK
    ov = ((c_start[None, :] < (b_start + SLC_BLOCK)[:, None])
          & ((c_start + CMP_BLOCK)[None, :] > b_start[:, None])
          & (np.arange(nc) < nc - 1)[None, :] & (b_start < seq)[:, None])
    return ov.astype(np.float32)


def _compress_weights(pe, w1, w2):
    ty = np.array([0, 0, 1, 1])
    eye = jnp.eye(4, dtype=F32)
    w1r = w1.reshape(2, CMP_BLOCK, HEAD_DIM, CMP_HIDDEN)[ty]
    top = jnp.einsum('spdj,sS->psdSj', w1r[:, :CMP_STRIDE], eye).reshape(CMP_STRIDE * 256, 4 * CMP_HIDDEN)
    bot = jnp.einsum('spdj,sS->psdSj', w1r[:, CMP_STRIDE:], eye).reshape(CMP_STRIDE * 256, 4 * CMP_HIDDEN)
    per = pe[ty]
    pet = jnp.transpose(per[:, :CMP_STRIDE], (1, 0, 2)).reshape(1, CMP_STRIDE * 256)
    peb = jnp.transpose(per[:, CMP_STRIDE:], (1, 0, 2)).reshape(1, CMP_STRIDE * 256)
    w2p = jnp.concatenate([w2[ty], jnp.zeros((4, CMP_HIDDEN, SLOT - HEAD_DIM), F32)], axis=-1)
    w2b = jnp.einsum('sjd,sS->sjSd', w2p, eye).reshape(4 * CMP_HIDDEN, 4 * SLOT)
    w2k = w2b[:, :2 * SLOT].astype(BF16)
    w2vt = w2b[:, 2 * SLOT:].T.astype(BF16)
    return pet, peb, top.astype(BF16), bot.astype(BF16), w2k, w2vt


def kernel(x, mem, w_in, cmp_pe, cmp_w1, cmp_w2, conv_w, w_branch, w_out, ln1_g, ln1_b,
           xattn_wq, xattn_wkv, xattn_wo, ln2_g, ln2_b, router_w, router_b, exp_w13, exp_w2,
           shared_w13, shared_w2, ln3_g, ln3_b):
    bsz, seq, d = x.shape
    mlen = mem.shape[1]
    depth = w_in.shape[0]
    alpha = (2.0 * depth) ** 0.25
    t = bsz * seq
    scale = HEAD_DIM ** -0.5

    kfeat = jnp.asarray(_key_features(seq))
    fq_nsa, fq_moba = (jnp.asarray(a) for a in _query_features())
    cbias, wbias = (jnp.asarray(a) for a in _mask_biases())
    ovt = jnp.asarray(_overlap_t(seq)).astype(BF16)
    memf = mem.reshape(bsz * mlen, d)

    h = x.reshape(t, d)
    hb = h.astype(BF16)
    for l in range(depth):
        ht = hb.T
        wi = w_in[l]
        kv6 = wi[:, _OFF_NSA_KV:_OFF_NSA_G].reshape(d, 6, NSA_KV_HEADS * HEAD_DIM)
        mq, mk, mv = (wi[:, _OFF_MOBA + j * 512:_OFF_MOBA + (j + 1) * 512] for j in range(3))
        w_at = jnp.concatenate([wi[:, :512] * scale, kv6[:, 3], kv6[:, 5], mq * scale, mv],
                               axis=1).T.astype(BF16)
        w_b = jnp.concatenate([_pad_slots(kv6[:, 2]), _pad_slots(kv6[:, 4]), _pad_slots(mk)], axis=1).astype(BF16)
        wg = wi[:, _OFF_NSA_G:_OFF_MOBA].reshape(d, NSA_KV_HEADS, 12)
        wgt = jnp.concatenate([wg, jnp.zeros((d, NSA_KV_HEADS, SLOT - 12), F32)], axis=-1)
        wgt = wgt.reshape(d, NSA_KV_HEADS * SLOT).T.astype(BF16)
        w_c = jnp.concatenate([wi[:, _OFF_NSA_KV:_OFF_NSA_KV + 256], jnp.zeros((d, 256), F32),
                               wi[:, _OFF_CONV:]], axis=1).astype(BF16)

        value_slots = tuple(range(_SLOT_SLC_V, _SLOT_MOBA_Q)) + tuple(range(_SLOT_MOBA_V, _N_SLOTS_T))
        zat = _proj_t(w_at, ht, value_slots)
        gt = _mm(wgt, ht, F32, tm=256, tn=1024, name="proj_gate")
        zb = _mm(hb, w_b, BF16, tn=768, feats=kfeat, name="proj_b")
        zc = _mm(hb, w_c, BF16, tn=1024, name="proj_c")

        pet, peb, wt, wb, w2k, w2vt = _compress_weights(cmp_pe[l], cmp_w1[l], cmp_w2[l])
        sub = zc[:, :256].reshape(bsz, seq // CMP_STRIDE, CMP_STRIDE * 256)
        kc, vct = _compress(sub, pet, peb, wt, wb, w2k, w2vt)

        ya = _nsa_attention(zat, zb, gt, kc, vct, ovt, fq_nsa, cbias, wbias, bsz, seq)
        yb = _moba_attention(zat, zb, fq_moba, cbias, bsz, seq)

        cw = jnp.concatenate([conv_w[l], jnp.zeros((5, CONV_CH), F32)], axis=0)
        h = _merge(ya, yb, zc, h, cw, _pad_rows(w_branch[l, 0]).astype(BF16),
                   _pad_rows(w_branch[l, 1]).astype(BF16), w_branch[l, 2].astype(BF16),
                   w_out[l].astype(BF16), ln1_g[l][None], ln1_b[l][None], seq, alpha)

        kv = _mm(memf, xattn_wkv[l].astype(BF16), BF16, tm=512, name="xattn_kv")
        h, hp = _xattn(h, kv, xattn_wq[l].astype(BF16), xattn_wo[l].astype(BF16),
                       ln2_g[l][None], ln2_b[l][None], seq, mlen, alpha)

        h, hb = _moe(h, hp, router_w[l].astype(BF16), router_b[l], exp_w13, exp_w2, l,
                     shared_w13[l].astype(BF16), shared_w2[l].astype(BF16), ln3_g[l][None], ln3_b[l][None], alpha)
    return h.reshape(bsz, seq, d)
```

```python
import functools

import jax
import jax.numpy as jnp
import numpy as np
from jax import lax
from jax.experimental import pallas as pl
from jax.experimental.pallas import tpu as pltpu
from jax.experimental.pallas import tpu_sc as plsc

F32 = jnp.float32
BF16 = jnp.bfloat16
I32 = jnp.int32

D_MODEL = 1024
HEAD_DIM = 64
SLOT = 128
NEG = -1e30
LN_EPS = 1e-5

NSA_HEADS = 8
NSA_KV_HEADS = 2
NSA_GROUP = 4
CMP_BLOCK = 32
CMP_STRIDE = 16
CMP_HIDDEN = 256
SLC_BLOCK = 64
SLC_TOPK = 16
WINDOW = 512
MOBA_HEADS = 8
MOBA_BLOCK = 256
MOBA_TOPK = 3
CONV_CH = 512
XATTN_HEADS = 4
XATTN_HEAD_DIM = 128
N_EXPERTS = 256
TOP_K = 8
N_GROUPS = 8
TOPK_GROUPS = 4
EXPERT_DIM = 256
ROUTE_SCALE = 2.5

TQ = 256
MOBA_HPS = 4
PV_ROWS = 80
FORCED_KEY = int(np.float32(1e6).view(np.int32))
EXP_ROWS = 512
VMEM_LIMIT = 48 * 1024 * 1024
SC_CORES = 2
SC_SUBCORES = 16
SC_CHUNK = 64

_OFF_NSA_Q = 0
_OFF_NSA_KV = 512
_OFF_NSA_G = 1280
_OFF_MOBA = 1304
_OFF_CONV = 2840
_OFF_MERGE = 4376

_SLOT_NSA_Q = 0
_SLOT_SLC_V = 8
_SLOT_WIN_V = 10
_SLOT_MOBA_Q = 12
_SLOT_MOBA_V = 20
_N_SLOTS_T = 28


def _cparams(sem):
    return pltpu.CompilerParams(dimension_semantics=sem, vmem_limit_bytes=VMEM_LIMIT)


def _dot(a, b):
    return jnp.dot(a, b, preferred_element_type=F32)


def _dot_t(a, b):
    return lax.dot_general(a, b, (((1,), (1,)), ((), ())), preferred_element_type=F32)


def _layer_norm(x, g, b):
    mu = jnp.mean(x, axis=-1, keepdims=True)
    xc = x - mu
    var = jnp.mean(xc * xc, axis=-1, keepdims=True)
    return xc * lax.rsqrt(var + LN_EPS) * g + b


def _mm_kernel(x_ref, w_ref, o_ref):
    o_ref[...] = _dot(x_ref[...].astype(BF16), w_ref[...]).astype(o_ref.dtype)


def _mm_feat_kernel(x_ref, w_ref, f_ref, o_ref):
    y = _dot(x_ref[...].astype(BF16), w_ref[...]) + f_ref[...]
    o_ref[...] = y.astype(o_ref.dtype)


def _mm(x, w, out_dtype, *, tm=1024, tn=512, feats=None, name):
    m, k = x.shape
    n = w.shape[1]
    tm = min(tm, m)
    tn = min(tn, n)
    assert m % tm == 0 and n % tn == 0, (m, n, tm, tn)
    in_specs = [pl.BlockSpec((tm, k), lambda i, j: (i, 0)),
                pl.BlockSpec((k, tn), lambda i, j: (0, j))]
    args = [x, w]
    if feats is None:
        body = _mm_kernel
    else:
        per = feats.shape[0] // tm
        assert feats.shape[0] % tm == 0
        in_specs.append(pl.BlockSpec((tm, tn), lambda i, j: (i % per, j)))
        args.append(feats)
        body = _mm_feat_kernel
    return pl.pallas_call(
        body,
        out_shape=jax.ShapeDtypeStruct((m, n), out_dtype),
        grid=(m // tm, n // tn),
        in_specs=in_specs,
        out_specs=pl.BlockSpec((tm, tn), lambda i, j: (i, j)),
        compiler_params=_cparams(("parallel", "parallel")),
        name=name,
    )(*args)


def _proj_t_kernel(w_ref, ht_ref, o_ref, *, value_slots):
    acc = _dot(w_ref[...], ht_ref[...])
    tn = acc.shape[1]
    zero = jnp.zeros((SLOT - HEAD_DIM, tn), o_ref.dtype)
    ones_row = (lax.broadcasted_iota(I32, (SLOT - HEAD_DIM, tn), 0) == 0).astype(o_ref.dtype)
    for s in range(w_ref.shape[0] // HEAD_DIM):
        o_ref[s * SLOT:s * SLOT + HEAD_DIM, :] = acc[s * HEAD_DIM:(s + 1) * HEAD_DIM, :].astype(o_ref.dtype)
        o_ref[s * SLOT + HEAD_DIM:(s + 1) * SLOT, :] = ones_row if s in value_slots else zero


def _proj_t(w, ht, value_slots):
    m, k = w.shape
    t = ht.shape[1]
    tn = min(512, t)
    n_out = m // HEAD_DIM * SLOT
    return pl.pallas_call(
        functools.partial(_proj_t_kernel, value_slots=value_slots),
        out_shape=jax.ShapeDtypeStruct((n_out, t), BF16),
        grid=(t // tn,),
        in_specs=[pl.BlockSpec((m, k), lambda j: (0, 0)), pl.BlockSpec((k, tn), lambda j: (0, j))],
        out_specs=pl.BlockSpec((n_out, tn), lambda j: (0, j)),
        compiler_params=_cparams(("parallel",)),
        name="proj_at",
    )(w, ht)


def _cmp_kernel(sub_ref, pet_ref, peb_ref, wt_ref, wb_ref, w2k_ref, w2vt_ref, kc_ref, vct_ref):
    sub = sub_ref[0].astype(F32)
    nc = sub.shape[0]
    a = _dot((sub + pet_ref[...]).astype(BF16), wt_ref[...])
    b = _dot((sub + peb_ref[...]).astype(BF16), wb_ref[...])
    hid = jax.nn.gelu(a + pltpu.roll(b, nc - 1, 0)).astype(BF16)
    kc_ref[0] = _dot(hid, w2k_ref[...]).astype(kc_ref.dtype)
    vct_ref[0] = _dot_t(w2vt_ref[...], hid).astype(vct_ref.dtype)


def _compress(sub, pet, peb, wt, wb, w2k, w2vt):
    bsz, nc, kk = sub.shape
    n_h = wt.shape[1]
    n_o = w2k.shape[1]
    const = lambda b: (0, 0)
    return pl.pallas_call(
        _cmp_kernel,
        out_shape=(jax.ShapeDtypeStruct((bsz, nc, n_o), BF16), jax.ShapeDtypeStruct((bsz, n_o, nc), BF16)),
        grid=(bsz,),
        in_specs=[pl.BlockSpec((1, nc, kk), lambda b: (b, 0, 0)),
                  pl.BlockSpec((1, kk), const), pl.BlockSpec((1, kk), const),
                  pl.BlockSpec((kk, n_h), const), pl.BlockSpec((kk, n_h), const),
                  pl.BlockSpec((n_h, n_o), const), pl.BlockSpec((n_o, n_h), const)],
        out_specs=(pl.BlockSpec((1, nc, n_o), lambda b: (b, 0, 0)), pl.BlockSpec((1, n_o, nc), lambda b: (b, 0, 0))),
        compiler_params=_cparams(("parallel",)),
        name="nsa_compress",
    )(sub, pet, peb, wt, wb, w2k, w2vt)


def _flash_reset(m_scr, acc_scr):
    m_scr[...] = jnp.full(m_scr.shape, NEG, F32)
    acc_scr[...] = jnp.zeros(acc_scr.shape, F32)


def _qk(kget, qxs, start, size):
    k0 = kget(0, start, size)
    if kget(1, start, size) is None:
        return _dot(k0, jnp.concatenate(qxs, axis=1))
    return jnp.concatenate([_dot(k0, qxs[0])] + [_dot(kget(c, start, size), qxs[c])
                                                 for c in range(1, len(qxs))], axis=1)


def _softmax_pv(s, vget, nq, start, size, m_scr, acc_scr, bias=None):
    if bias is not None:
        s = s + jnp.concatenate([bias] * nq, axis=1)
    m_prev = m_scr[...]
    m_new = jnp.maximum(m_prev, jnp.max(s, axis=0, keepdims=True))
    alpha = jnp.exp(m_prev - m_new)
    p = jnp.exp(s - m_new).astype(BF16)
    v0 = vget(0, start, size)
    if vget(1, start, size) is None:
        pv = _dot(v0, p)
    else:
        pv = jnp.concatenate([_dot(v0, p[:, :TQ])] + [_dot(vget(c, start, size), p[:, c * TQ:(c + 1) * TQ])
                                                      for c in range(1, nq)], axis=1)
    acc_scr[...] = alpha * acc_scr[...] + pv
    m_scr[...] = m_new


def _rows(ref, start, size, lane0=0):
    return ref[pl.ds(pl.multiple_of(start, TQ), size), lane0:lane0 + SLOT]


def _cols(ref, start, size, row0=0):
    return ref[row0:row0 + PV_ROWS, pl.ds(pl.multiple_of(start, TQ), size)]


def _store_token_rows(o_ref, heads_t):
    for pr in range(len(heads_t) // 2):
        pair = jnp.concatenate([heads_t[2 * pr][:HEAD_DIM], heads_t[2 * pr + 1][:HEAD_DIM]], axis=0)
        o_ref[:, pr * SLOT:(pr + 1) * SLOT] = pair.T.astype(o_ref.dtype)


def _flash_causal(kget, vget, qxs, i, cb_ref, sa_scr, sb_scr, m_scr, acc_scr):
    tq = TQ
    tk = 2 * tq
    nq = len(qxs)
    n_steps = i // 2 + 1
    sa_scr[...] = _qk(kget, qxs, 0, tk)

    def body(jj, carry):
        j = 2 * jj
        s = sa_scr[...]
        sb_scr[...] = _qk(kget, qxs, (j + 1) * tk, tk)
        _softmax_pv(s, vget, nq, j * tk, tk, m_scr, acc_scr)
        s = sb_scr[...]
        sa_scr[...] = _qk(kget, qxs, (j + 2) * tk, tk)
        _softmax_pv(s, vget, nq, (j + 1) * tk, tk, m_scr, acc_scr)
        return carry

    lax.fori_loop(0, (n_steps - 1) // 2, body, 0)
    last = n_steps - 1
    bias = cb_ref[i % 2]

    @pl.when(last % 2 == 0)
    def _():
        _softmax_pv(sa_scr[...], vget, nq, last * tk, tk, m_scr, acc_scr, bias=bias)

    @pl.when(last % 2 == 1)
    def _():
        s = sa_scr[...]
        sb_scr[...] = _qk(kget, qxs, last * tk, tk)
        _softmax_pv(s, vget, nq, (last - 1) * tk, tk, m_scr, acc_scr)
        _softmax_pv(sb_scr[...], vget, nq, last * tk, tk, m_scr, acc_scr, bias=bias)


def _nsa_kernel(q_ref, kc_ref, vct_ref, ks_ref, vst_ref, kw_ref, vwt_ref, g_ref, ovt_ref, fq_ref, cb_ref, wb_ref,
                o_ref, m_scr, acc_scr, imp_scr, sa_scr, sb_scr, *, tq, nsel):
    i = pl.program_id(2)
    r4 = NSA_GROUP
    rr = r4 * tq
    q4 = [q_ref[r * SLOT:(r + 1) * SLOT, :] for r in range(r4)]
    fq = fq_ref[0]

    kc = kc_ref[0]
    nc = kc.shape[0]
    qs = jnp.concatenate(q4, axis=1)
    s = _dot(kc, qs)
    t_row = i * tq + (lax.broadcasted_iota(I32, (1, rr), 1) & (tq - 1))
    last_c = (t_row - (CMP_BLOCK - 1)) // CMP_STRIDE
    cmask = lax.broadcasted_iota(I32, (nc, rr), 0) <= last_c
    s = jnp.where(cmask, s, NEG)
    mx = jnp.max(s, axis=0, keepdims=True)
    e = jnp.where(cmask, jnp.exp(s - mx), 0.0)
    lsum = jnp.sum(e, axis=0, keepdims=True)
    p_cmp = (e * (1.0 / jnp.where(lsum > 0.0, lsum, 1.0))).astype(BF16)
    o_cmp = _dot(vct_ref[0], p_cmp)

    ovt = ovt_ref[...]
    imp4 = _dot(ovt, p_cmp)
    imp = imp4[:, 0:tq]
    for r in range(1, r4):
        imp = imp + imp4[:, r * tq:(r + 1) * tq]
    nbp = imp.shape[0]
    jidx = lax.broadcasted_iota(I32, (nbp, tq), 0)
    cur = (i * tq + lax.broadcasted_iota(I32, (nbp, tq), 1)) // SLC_BLOCK
    forced = (jidx == 0) | (jidx == cur) | (jidx == cur - 1)
    key = jnp.where(forced, FORCED_KEY, jnp.where(jidx > cur, -1, lax.bitcast_convert_type(imp, I32)))
    imp_scr[...] = key

    def rank_body(jj, rank):
        for jp in (2 * jj, 2 * jj + 1):
            row = imp_scr[pl.ds(jp, 1), :]
            rank = rank + (row > key - (jidx > jp).astype(I32)).astype(I32)
        return rank

    n_live = (i + 1) * (tq // SLC_BLOCK)
    rank = lax.fori_loop(0, n_live // 2, rank_body, jnp.zeros((nbp, tq), I32))
    sel = (rank < nsel) & (jidx <= cur)
    mask_t = jnp.where(sel, 0.0, NEG)
    feat = jnp.concatenate([jnp.zeros((SLOT - nbp, tq), F32), mask_t], axis=0)

    qxs = [q4[r] + (feat + fq[:, r:r + 1]).astype(BF16) for r in range(r4)]
    kget = lambda c, start, size: _rows(ks_ref, start, size) if c == 0 else None
    vget = lambda c, start, size: _cols(vst_ref, start, size) if c == 0 else None
    _flash_reset(m_scr, acc_scr)
    _flash_causal(kget, vget, qxs, i, cb_ref, sa_scr, sb_scr, m_scr, acc_scr)
    o_slc = acc_scr[...] * (1.0 / acc_scr[HEAD_DIM:HEAD_DIM + 1, :])

    qx = jnp.concatenate([q4[r] + fq[:, r4 + r:r4 + r + 1].astype(BF16) for r in range(r4)], axis=1)
    w0 = jnp.maximum(i - 2, 0) * tq
    s = _dot(_rows(kw_ref, w0, 3 * tq), qx)
    s = s + jnp.concatenate([wb_ref[jnp.minimum(i, 2)]] * r4, axis=1)
    e = jnp.exp(s - jnp.max(s, axis=0, keepdims=True)).astype(BF16)
    o_win = _dot(_cols(vwt_ref, w0, 3 * tq), e)
    o_win = o_win * (1.0 / o_win[HEAD_DIM:HEAD_DIM + 1, :])

    gate = jax.nn.sigmoid(g_ref[...])
    outs = []
    for r in range(r4):
        sl = slice(r * tq, (r + 1) * tq)
        outs.append(gate[3 * r:3 * r + 1, :] * o_cmp[:PV_ROWS, sl] + gate[3 * r + 1:3 * r + 2, :] * o_slc[:, sl]
                    + gate[3 * r + 2:3 * r + 3, :] * o_win[:, sl])
    _store_token_rows(o_ref, outs)


def _nsa_attention(zat, zb, gt, kc, vct, ovt, fq, cb, wb, bsz, seq):
    tq = TQ
    ni = seq // tq
    nc = kc.shape[1]
    nsel = min(SLC_TOPK, seq // SLC_BLOCK)
    assert seq // SLC_BLOCK <= 64 and WINDOW == 2 * tq and ni % 2 == 0 and ni >= 3
    rr = NSA_GROUP * tq
    gw = NSA_GROUP * SLOT
    in_specs = [
        pl.BlockSpec((gw, tq), lambda b, g, i: (g, b * ni + i)),
        pl.BlockSpec((1, nc, SLOT), lambda b, g, i: (b, 0, g)),
        pl.BlockSpec((1, SLOT, nc), lambda b, g, i: (b, g, 0)),
        pl.BlockSpec((seq, SLOT), lambda b, g, i: (b, g)),
        pl.BlockSpec((SLOT, seq), lambda b, g, i: (_SLOT_SLC_V + g, b)),
        pl.BlockSpec((seq, SLOT), lambda b, g, i: (b, 2 + g)),
        pl.BlockSpec((SLOT, seq), lambda b, g, i: (_SLOT_WIN_V + g, b)),
        pl.BlockSpec((SLOT, tq), lambda b, g, i: (g, b * ni + i)),
        pl.BlockSpec(ovt.shape, lambda b, g, i: (0, 0)),
        pl.BlockSpec((1, SLOT, 8), lambda b, g, i: (g, 0, 0)),
        pl.BlockSpec(cb.shape, lambda b, g, i: (0, 0, 0)),
        pl.BlockSpec(wb.shape, lambda b, g, i: (0, 0, 0)),
    ]
    return pl.pallas_call(
        functools.partial(_nsa_kernel, tq=tq, nsel=nsel),
        out_shape=jax.ShapeDtypeStruct((bsz * seq, NSA_HEADS * HEAD_DIM), BF16),
        grid=(bsz, NSA_KV_HEADS, ni),
        in_specs=in_specs,
        out_specs=pl.BlockSpec((tq, NSA_GROUP * HEAD_DIM), lambda b, g, i: (b * ni + i, g)),
        scratch_shapes=[pltpu.VMEM((1, rr), F32),
                        pltpu.VMEM((PV_ROWS, rr), F32), pltpu.VMEM((64, tq), I32),
                        pltpu.VMEM((2 * tq, rr), F32), pltpu.VMEM((2 * tq, rr), F32)],
        compiler_params=_cparams(("parallel", "parallel", "arbitrary")),
        name="nsa_attention",
    )(zat, kc, vct, zb, zat, zb, zat, gt, ovt, fq, cb, wb)


def _moba_kernel(q_ref, k_ref, vt_ref, fq_ref, cb_ref, o_ref, m_scr, acc_scr, km_scr, gate_scr,
                 sa_scr, sb_scr, *, tq, nblk, ntop):
    i = pl.program_id(2)
    nh = MOBA_HPS

    @pl.when(i == 0)
    def _():
        for hh in range(nh):
            kh = k_ref[:, hh * SLOT:(hh + 1) * SLOT].astype(F32)
            km = jnp.mean(kh.reshape(nblk, tq, SLOT), axis=1)
            if nblk < 16:
                km = jnp.concatenate([km, jnp.zeros((16 - nblk, SLOT), F32)], axis=0)
            km_scr[hh] = km

    qs = [q_ref[hh * SLOT:(hh + 1) * SLOT, :] for hh in range(nh)]
    gates = [_dot(km_scr[hh].astype(BF16), qs[hh]) for hh in range(nh)]
    for hh in range(nh):
        gate_scr[hh] = gates[hh]
    jidx = lax.broadcasted_iota(I32, (16, tq), 0)

    def rank_body(jp, ranks):
        out = []
        for hh in range(nh):
            row = gate_scr[hh, pl.ds(jp, 1), :]
            better = (row > gates[hh]) | ((row == gates[hh]) & (jp < jidx))
            out.append(ranks[hh] + better.astype(I32))
        return tuple(out)

    ranks = lax.fori_loop(0, i, rank_body, tuple(jnp.zeros((16, tq), I32) for _ in range(nh)))
    qxs = []
    for hh in range(nh):
        keep = ((jidx < i) & (ranks[hh] < ntop)) | (jidx == i)
        mask_t = jnp.where(keep, 0.0, NEG)
        feat = jnp.concatenate([jnp.zeros((HEAD_DIM, tq), F32), mask_t,
                                jnp.zeros((SLOT - HEAD_DIM - 16, tq), F32)], axis=0)
        qxs.append(qs[hh] + (feat + fq_ref[hh][:, 0:1]).astype(BF16))
    kget = lambda c, start, size: _rows(k_ref, start, size, c * SLOT)
    vget = lambda c, start, size: _cols(vt_ref, start, size, c * SLOT)
    _flash_reset(m_scr, acc_scr)
    _flash_causal(kget, vget, qxs, i, cb_ref, sa_scr, sb_scr, m_scr, acc_scr)
    o = acc_scr[...] * (1.0 / acc_scr[HEAD_DIM:HEAD_DIM + 1, :])
    _store_token_rows(o_ref, [o[:, hh * tq:(hh + 1) * tq] for hh in range(nh)])


def _moba_attention(zat, zb, fq, cb, bsz, seq):
    tq = TQ
    assert tq == MOBA_BLOCK and seq % tq == 0
    ni = seq // tq
    assert ni <= 16 and ni % 2 == 0
    ntop = min(MOBA_TOPK, ni)
    nh = MOBA_HPS
    hw = nh * SLOT
    q0 = _SLOT_MOBA_Q // nh
    v0 = _SLOT_MOBA_V // nh
    return pl.pallas_call(
        functools.partial(_moba_kernel, tq=tq, nblk=ni, ntop=ntop),
        out_shape=jax.ShapeDtypeStruct((bsz * seq, MOBA_HEADS * HEAD_DIM), BF16),
        grid=(bsz, MOBA_HEADS // nh, ni),
        in_specs=[pl.BlockSpec((hw, tq), lambda b, h, i: (q0 + h, b * ni + i)),
                  pl.BlockSpec((seq, hw), lambda b, h, i: (b, 1 + h)),
                  pl.BlockSpec((hw, seq), lambda b, h, i: (v0 + h, b)),
                  pl.BlockSpec((nh, SLOT, 8), lambda b, h, i: (h, 0, 0)),
                  pl.BlockSpec(cb.shape, lambda b, h, i: (0, 0, 0))],
        out_specs=pl.BlockSpec((tq, nh * HEAD_DIM), lambda b, h, i: (b * ni + i, h)),
        scratch_shapes=[pltpu.VMEM((1, nh * tq), F32),
                        pltpu.VMEM((PV_ROWS, nh * tq), F32),
                        pltpu.VMEM((nh, 16, SLOT), F32), pltpu.VMEM((nh, 16, tq), F32),
                        pltpu.VMEM((2 * tq, nh * tq), F32), pltpu.VMEM((2 * tq, nh * tq), F32)],
        compiler_params=_cparams(("parallel", "parallel", "arbitrary")),
        name="moba_attention",
    )(zat, zb, zat, fq, cb)


def _merge_kernel(ya_ref, yb_ref, hc_ref, bc_ref, cc_ref, hcp_ref, ccp_ref, g0_ref, g1_ref, g2_ref,
                  h_ref, cw_ref, wa_ref, wb_ref, wc_ref, wo_ref, lg_ref, lb_ref, o_ref, *, tm, per, alpha):
    i = pl.program_id(0)
    u = cc_ref[...].astype(F32) * hc_ref[...].astype(F32)
    first = (i % per) == 0
    up = jnp.where(first, 0.0, ccp_ref[...].astype(F32) * hcp_ref[...].astype(F32))
    rowi = lax.broadcasted_iota(I32, u.shape, 0)
    u1 = jnp.where(rowi == 0, up[15:16, :], pltpu.roll(u, 1, 0))
    u2 = jnp.where(rowi == 0, up[14:15, :], jnp.where(rowi == 1, up[15:16, :], pltpu.roll(u, 2, 0)))
    cw = cw_ref[...]
    yc = bc_ref[...].astype(F32) * (cw[0:1, :] * u2 + cw[1:2, :] * u1 + cw[2:3, :] * u)
    merged = (jax.nn.sigmoid(g0_ref[...].astype(F32)) * _dot(ya_ref[...], wa_ref[...])
              + jax.nn.sigmoid(g1_ref[...].astype(F32)) * _dot(yb_ref[...], wb_ref[...])
              + jax.nn.sigmoid(g2_ref[...].astype(F32)) * _dot(yc.astype(BF16), wc_ref[...]))
    mix = _dot(merged.astype(BF16), wo_ref[...])
    o_ref[...] = _layer_norm(alpha * h_ref[...] + mix, lg_ref[...], lb_ref[...])


def _merge(ya, yb, zc, h, cw, wa, wb, wc, wo, lg, lb, seq, alpha):
    tm = 512
    t = h.shape[0]
    per = seq // tm
    d = D_MODEL
    row = lambda i: (i, 0)
    const = lambda i: (0, 0)
    prev = lambda c: (lambda i: (jnp.maximum(i * (tm // 16) - 1, 0), c))
    in_specs = [
        pl.BlockSpec((tm, ya.shape[1]), row), pl.BlockSpec((tm, yb.shape[1]), row),
        pl.BlockSpec((tm, CONV_CH), lambda i: (i, 1)), pl.BlockSpec((tm, CONV_CH), lambda i: (i, 2)),
        pl.BlockSpec((tm, CONV_CH), lambda i: (i, 3)),
        pl.BlockSpec((16, CONV_CH), prev(1)), pl.BlockSpec((16, CONV_CH), prev(3)),
        pl.BlockSpec((tm, d), lambda i: (i, 2)), pl.BlockSpec((tm, d), lambda i: (i, 3)),
        pl.BlockSpec((tm, d), lambda i: (i, 4)),
        pl.BlockSpec((tm, d), row),
        pl.BlockSpec((8, CONV_CH), const),
        pl.BlockSpec(wa.shape, const), pl.BlockSpec(wb.shape, const), pl.BlockSpec((CONV_CH, d), const),
        pl.BlockSpec((d, d), const), pl.BlockSpec((1, d), const), pl.BlockSpec((1, d), const),
    ]
    return pl.pallas_call(
        functools.partial(_merge_kernel, tm=tm, per=per, alpha=alpha),
        out_shape=jax.ShapeDtypeStruct((t, d), F32),
        grid=(t // tm,),
        in_specs=in_specs,
        out_specs=pl.BlockSpec((tm, d), row),
        compiler_params=_cparams(("parallel",)),
        name="mixer_merge",
    )(ya, yb, zc, zc, zc, zc, zc, zc, zc, zc, h, cw, wa, wb, wc, wo, lg, lb)


def _xattn_kernel(h_ref, kv_ref, wq_ref, wo_ref, lg_ref, lb_ref, o_ref, ob_ref, *, alpha):
    h = h_ref[...]
    q = _dot(h.astype(BF16), wq_ref[...]).astype(BF16)
    kv = kv_ref[...]
    nh = XATTN_HEADS
    hd = XATTN_HEAD_DIM
    outs = []
    for hh in range(nh):
        s = _dot_t(q[:, hh * hd:(hh + 1) * hd], kv[:, hh * hd:(hh + 1) * hd]) * (hd ** -0.5)
        s = s - jnp.max(s, axis=-1, keepdims=True)
        e = jnp.exp(s)
        p = e / jnp.sum(e, axis=-1, keepdims=True)
        outs.append(_dot(p.astype(BF16), kv[:, (nh + hh) * hd:(nh + hh + 1) * hd]))
    o = jnp.concatenate(outs, axis=-1).astype(BF16)
    y = _layer_norm(alpha * h + _dot(o, wo_ref[...]), lg_ref[...], lb_ref[...])
    o_ref[...] = y
    bits = lax.bitcast_convert_type(y.astype(BF16).astype(F32), I32)
    half = y.shape[1] // 2
    ob_ref[...] = (bits[:, half:] & -65536) | lax.shift_right_logical(bits[:, :half], 16)


def _unpack_pairs(words):
    lo = lax.bitcast_convert_type(lax.shift_left(words, 16), F32).astype(BF16)
    hi = lax.bitcast_convert_type(words & -65536, F32).astype(BF16)
    return lo, hi


def _xattn(h, kv, wq, wo, lg, lb, seq, mlen, alpha):
    tm = 512
    t = h.shape[0]
    per = seq // tm
    d = D_MODEL
    row = lambda i: (i, 0)
    const = lambda i: (0, 0)
    return pl.pallas_call(
        functools.partial(_xattn_kernel, alpha=alpha),
        out_shape=(jax.ShapeDtypeStruct((t, d), F32), jax.ShapeDtypeStruct((t, d // 2), I32)),
        grid=(t // tm,),
        in_specs=[pl.BlockSpec((tm, d), row), pl.BlockSpec((mlen, kv.shape[1]), lambda i: (i // per, 0)),
                  pl.BlockSpec(wq.shape, const), pl.BlockSpec(wo.shape, const),
                  pl.BlockSpec((1, d), const), pl.BlockSpec((1, d), const)],
        out_specs=(pl.BlockSpec((tm, d), row), pl.BlockSpec((tm, d // 2), row)),
        compiler_params=_cparams(("parallel",)),
        name="mem_xattn",
    )(h, kv, wq, wo, lg, lb)


def _expert_kernel(be_ref, nu_ref, nxt_ref, slot_ref, x_ref, w13_hbm, w2_hbm, o_ref,
                   w13f_scr, w2f_scr, w13b_scr, w2b_scr, sem, *, layer):
    i = pl.program_id(0)
    e = be_ref[i]
    s = slot_ref[i]

    def weight_copies(expert, slot):
        return (pltpu.make_async_copy(w13_hbm.at[layer, expert], w13f_scr.at[slot], sem.at[0, slot]),
                pltpu.make_async_copy(w2_hbm.at[layer, expert], w2f_scr.at[slot], sem.at[1, slot]))

    @pl.when(i == 0)
    def _():
        for cp in weight_copies(e, s):
            cp.start()

    @pl.when((i == 0) | (e != be_ref[jnp.maximum(i - 1, 0)]))
    def _():
        for cp in weight_copies(e, s):
            cp.wait()
        nxt = nxt_ref[i]

        @pl.when(nxt >= 0)
        def _():
            for cp in weight_copies(nxt, 1 - s):
                cp.start()

        w13b_scr[...] = w13f_scr[s].astype(BF16)
        w2b_scr[...] = w2f_scr[s].astype(BF16)

    @pl.when(i < nu_ref[0])
    def _():
        x_lo, x_hi = _unpack_pairs(x_ref[...])
        half = x_lo.shape[1]
        hmid = _dot(x_lo, w13b_scr[:half, :]) + _dot(x_hi, w13b_scr[half:, :])
        a = hmid[:, :EXPERT_DIM]
        g = hmid[:, EXPERT_DIM:]
        act = (a * jax.nn.sigmoid(a) * g).astype(BF16)
        o_ref[...] = _dot(act, w2b_scr[...]).astype(o_ref.dtype)


def _experts(xg, w13, w2, block_e, n_used, next_e, slot, layer):
    n_pad = xg.shape[0]
    d = 2 * xg.shape[1]
    rows = EXP_ROWS
    nb = n_pad // rows
    blk = lambda i, be, nu, nx, sl: (jnp.minimum(i, nu[0] - 1), 0)
    grid_spec = pltpu.PrefetchScalarGridSpec(
        num_scalar_prefetch=4,
        grid=(nb,),
        in_specs=[pl.BlockSpec((rows, d // 2), blk),
                  pl.BlockSpec(memory_space=pl.ANY), pl.BlockSpec(memory_space=pl.ANY)],
        out_specs=pl.BlockSpec((rows, d), blk),
        scratch_shapes=[pltpu.VMEM((2, d, 2 * EXPERT_DIM), F32), pltpu.VMEM((2, EXPERT_DIM, d), F32),
                        pltpu.VMEM((d, 2 * EXPERT_DIM), BF16), pltpu.VMEM((EXPERT_DIM, d), BF16),
                        pltpu.SemaphoreType.DMA((2, 2))],
    )
    return pl.pallas_call(
        functools.partial(_expert_kernel, layer=layer),
        out_shape=jax.ShapeDtypeStruct((n_pad, d), BF16),
        grid_spec=grid_spec,
        compiler_params=_cparams(("arbitrary",)),
        name="moe_experts",
    )(block_e, n_used, next_e, slot, xg, w13, w2)


def _moe_out_kernel(h_ref, y8_ref, wk_ref, w13_ref, w2_ref, lg_ref, lb_ref, o_ref, ob_ref, *, alpha):
    h = h_ref[...]
    hmid = _dot(h.astype(BF16), w13_ref[...])
    a = hmid[:, :EXPERT_DIM]
    g = hmid[:, EXPERT_DIM:]
    acc = alpha * h + _dot((a * jax.nn.sigmoid(a) * g).astype(BF16), w2_ref[...])
    wk = wk_ref[...]
    for k in range(TOP_K):
        acc = acc + wk[:, k:k + 1] * y8_ref[k].astype(F32)
    y = _layer_norm(acc, lg_ref[...], lb_ref[...])
    o_ref[...] = y
    ob_ref[...] = y.astype(BF16)


def _moe_out(h, y8, wk, w13, w2, lg, lb, alpha):
    tm = 256
    t, d = h.shape
    row = lambda i: (i, 0)
    const = lambda i: (0, 0)
    return pl.pallas_call(
        functools.partial(_moe_out_kernel, alpha=alpha),
        out_shape=(jax.ShapeDtypeStruct((t, d), F32), jax.ShapeDtypeStruct((t, d), BF16)),
        grid=(t // tm,),
        in_specs=[pl.BlockSpec((tm, d), row), pl.BlockSpec((TOP_K, tm, d), lambda i: (0, i, 0)),
                  pl.BlockSpec((tm, TOP_K), row),
                  pl.BlockSpec(w13.shape, const), pl.BlockSpec(w2.shape, const),
                  pl.BlockSpec((1, d), const), pl.BlockSpec((1, d), const)],
        out_specs=(pl.BlockSpec((tm, d), row), pl.BlockSpec((tm, d), row)),
        compiler_params=_cparams(("parallel",)),
        name="moe_shared_ln",
    )(h, y8, wk, w13, w2, lg, lb)


def _dest_kernel(idx_ref, rank_ref, ps_ref, o_ref):
    idx = idx_ref[...]
    ps = ps_ref[...]
    tm = idx.shape[1]
    eidx = lax.broadcasted_iota(I32, (N_EXPERTS, tm), 0)
    rows = [jnp.sum(jnp.where(eidx == idx[k:k + 1, :], ps, 0.0), axis=0, keepdims=True) for k in range(TOP_K)]
    o_ref[...] = jnp.concatenate(rows, axis=0).astype(I32) + rank_ref[...]


def _dest(idx, rank, pstarts):
    tm = 1024
    t = idx.shape[1]
    tm = min(tm, t)
    col = lambda i: (0, i)
    return pl.pallas_call(
        _dest_kernel,
        out_shape=jax.ShapeDtypeStruct((TOP_K, t), I32),
        grid=(t // tm,),
        in_specs=[pl.BlockSpec((TOP_K, tm), col), pl.BlockSpec((TOP_K, tm), col),
                  pl.BlockSpec((N_EXPERTS, 1), lambda i: (0, 0))],
        out_specs=pl.BlockSpec((TOP_K, tm), col),
        compiler_params=_cparams(("parallel",)),
        name="moe_dest",
    )(idx, rank, pstarts)


def _router_kernel(h_ref, rwt_ref, rb_ref, tri_ref, idx_ref, w_ref, rank_ref, cnt_ref, carry_scr, *, tm):
    i = pl.program_id(0)

    @pl.when(i == 0)
    def _():
        carry_scr[...] = jnp.zeros(carry_scr.shape, F32)

    ne = N_EXPERTS
    gsz = ne // N_GROUPS
    s = jax.nn.sigmoid(_dot_t(rwt_ref[...], h_ref[...].astype(BF16)))
    sb = s + rb_ref[...]
    sb3 = sb.reshape(N_GROUPS, gsz, tm)
    li = lax.broadcasted_iota(I32, (N_GROUPS, gsz, tm), 1)
    m1 = jnp.max(sb3, axis=1, keepdims=True)
    first = jnp.min(jnp.where(sb3 == m1, li, gsz), axis=1, keepdims=True)
    m2 = jnp.max(jnp.where(li == first, -jnp.inf, sb3), axis=1, keepdims=True)
    gs = (m1 + m2).reshape(N_GROUPS, tm)
    gi = lax.broadcasted_iota(I32, (N_GROUPS, tm), 0)
    grank = jnp.zeros((N_GROUPS, tm), I32)
    for gp in range(N_GROUPS):
        row = gs[gp:gp + 1, :]
        grank = grank + ((row > gs) | ((row == gs) & (gp < gi))).astype(I32)
    gkeep = (grank < TOPK_GROUPS).astype(F32)
    ekeep = jnp.broadcast_to(gkeep[:, None, :], (N_GROUPS, gsz, tm)).reshape(ne, tm)
    cand = jnp.where(ekeep > 0.0, sb, NEG)
    eidx = lax.broadcasted_iota(I32, (ne, tm), 0)
    sel = jnp.zeros((ne, tm), F32)
    idxs, wts = [], []
    for _ in range(TOP_K):
        mx = jnp.max(cand, axis=0, keepdims=True)
        ik = jnp.min(jnp.where(cand == mx, eidx, ne), axis=0, keepdims=True)
        hit = eidx == ik
        wts.append(jnp.sum(jnp.where(hit, s, 0.0), axis=0, keepdims=True))
        idxs.append(ik)
        sel = jnp.where(hit, 1.0, sel)
        cand = jnp.where(hit, -jnp.inf, cand)
    before = _dot(sel.astype(BF16), tri_ref[...]) + carry_scr[...]
    ranks = [jnp.sum(jnp.where(eidx == ik, before, 0.0), axis=0, keepdims=True) for ik in idxs]
    carry_scr[...] = carry_scr[...] + jnp.sum(sel, axis=1, keepdims=True)
    w = jnp.concatenate(wts, axis=0)
    idx_ref[...] = jnp.concatenate(idxs, axis=0)
    w_ref[...] = w / jnp.sum(w, axis=0, keepdims=True) * ROUTE_SCALE
    rank_ref[...] = jnp.concatenate(ranks, axis=0).astype(I32)
    cnt_ref[...] = jnp.broadcast_to(carry_scr[...], cnt_ref.shape)


def _router(h, rwt, rb):
    tm = 512
    t, d = h.shape
    tri = jnp.asarray(np.triu(np.ones((tm, tm), np.float32), 1)).astype(BF16)
    const = lambda i: (0, 0)
    col = lambda i: (0, i)
    return pl.pallas_call(
        functools.partial(_router_kernel, tm=tm),
        out_shape=(jax.ShapeDtypeStruct((TOP_K, t), I32), jax.ShapeDtypeStruct((TOP_K, t), F32),
                   jax.ShapeDtypeStruct((TOP_K, t), I32), jax.ShapeDtypeStruct((N_EXPERTS, SLOT), F32)),
        grid=(t // tm,),
        in_specs=[pl.BlockSpec((tm, d), lambda i: (i, 0)), pl.BlockSpec((N_EXPERTS, d), const),
                  pl.BlockSpec((N_EXPERTS, 1), const), pl.BlockSpec((tm, tm), const)],
        out_specs=(pl.BlockSpec((TOP_K, tm), col), pl.BlockSpec((TOP_K, tm), col),
                   pl.BlockSpec((TOP_K, tm), col), pl.BlockSpec((N_EXPERTS, SLOT), const)),
        scratch_shapes=[pltpu.VMEM((N_EXPERTS, 1), F32)],
        compiler_params=_cparams(("arbitrary",)),
        name="moe_router",
    )(h, rwt, rb, tri)


def _dispatch(hp, dest_flat, n_pad):
    t, w = hp.shape
    per_w = t // (SC_CORES * SC_SUBCORES)
    assert per_w * SC_CORES * SC_SUBCORES == t and per_w % SC_CHUNK == 0 and dest_flat.shape[0] == TOP_K * t
    mesh = plsc.VectorSubcoreMesh(core_axis_name="c", subcore_axis_name="s")

    @functools.partial(
        pl.kernel, out_type=jax.ShapeDtypeStruct((n_pad, w), I32), mesh=mesh,
        scratch_types=[pltpu.VMEM((SC_CHUNK,), I32)] * TOP_K
        + [pltpu.VMEM((SC_CHUNK, w), I32), pltpu.SemaphoreType.DMA],
        name="moe_dispatch")
    def scatter_rows(hp_hbm, dest_hbm, xg_hbm, *scratch):
        idx_vs, rows_v, sem = scratch[:TOP_K], scratch[TOP_K], scratch[TOP_K + 1]
        wid = lax.axis_index("s") * SC_CORES + lax.axis_index("c")
        base = wid * per_w

        @pl.loop(0, per_w // SC_CHUNK)
        def _(j):
            t0 = pl.multiple_of(base + j * SC_CHUNK, SC_CHUNK)
            pltpu.sync_copy(hp_hbm.at[pl.ds(t0, SC_CHUNK)], rows_v)
            for k in range(TOP_K):
                pltpu.sync_copy(dest_hbm.at[pl.ds(pl.multiple_of(k * t + t0, SC_CHUNK), SC_CHUNK)], idx_vs[k])
            copies = [pltpu.async_copy(rows_v, xg_hbm.at[idx_vs[k]], sem) for k in range(TOP_K)]
            for cp in copies:
                cp.wait()

    return scatter_rows(hp, dest_flat)


def _moe(h, hp, router_w, router_b, exp_w13, exp_w2, layer, shared_w13, shared_w2, lg, lb, alpha):
    t, d = h.shape
    idx, wts, rank, cnt = _router(h, router_w.T, router_b.astype(F32).reshape(N_EXPERTS, 1))
    rows = EXP_ROWS
    n_a = t * TOP_K
    counts = cnt[:, 0].astype(I32)
    pcounts = (counts + rows - 1) // rows * rows
    pends = jnp.cumsum(pcounts)
    pstarts = pends - pcounts
    dest = _dest(idx, rank, pstarts.astype(F32).reshape(N_EXPERTS, 1))
    n_blocks = -(-n_a // rows) + N_EXPERTS
    n_pad = n_blocks * rows
    n_used = (pends[-1:] // rows).astype(I32)
    first_row = jnp.minimum(jnp.arange(n_blocks, dtype=I32), n_used - 1) * rows
    block_e = jnp.sum((pends[None, :] <= first_row[:, None]).astype(I32), axis=1)
    eidx = jnp.arange(N_EXPERTS, dtype=I32)
    has_rows = pcounts > 0
    at_or_after = lax.cummin(jnp.where(has_rows, eidx, N_EXPERTS), axis=0, reverse=True)
    after = jnp.concatenate([at_or_after[1:], jnp.full((1,), N_EXPERTS, I32)])
    next_e = jnp.take(jnp.where(after < N_EXPERTS, after, -1), block_e)
    slot = jnp.take((jnp.cumsum(has_rows.astype(I32)) - 1) % 2, block_e).astype(I32)
    xg = _dispatch(hp, dest.reshape(-1), n_pad)
    out = _experts(xg, exp_w13, exp_w2, block_e, n_used, next_e.astype(I32), slot, layer)
    y8 = jnp.take(out, dest, axis=0, mode="clip")
    return _moe_out(h, y8, wts.T, shared_w13, shared_w2, lg, lb, alpha)


def _pad_slots(w, scale=1.0):
    dm = w.shape[0]
    nh = w.shape[1] // HEAD_DIM
    w = (w * scale).reshape(dm, nh, HEAD_DIM)
    return jnp.concatenate([w, jnp.zeros_like(w)], axis=-1).reshape(dm, nh * SLOT)


def _alibi(n):
    return np.exp2(-8.0 * np.arange(1, n + 1, dtype=np.float64) / n).astype(np.float32)


def _key_features(seq):
    p = np.arange(seq)
    slc = np.zeros((seq, SLOT), np.float32)
    slc[:, HEAD_DIM] = p % SLC_BLOCK
    blk = p // SLC_BLOCK
    nz = blk > 0
    slc[p[nz], HEAD_DIM + blk[nz]] = 1.0
    win = np.zeros((seq, SLOT), np.float32)
    win[:, HEAD_DIM] = p // 64
    win[:, HEAD_DIM + 1] = p % 64
    mob = np.zeros((seq, SLOT), np.float32)
    mob[p, HEAD_DIM + p // MOBA_BLOCK] = 1.0
    mob[:, HEAD_DIM + 16] = p % MOBA_BLOCK
    return np.concatenate([slc] * 2 + [win] * 2 + [mob] * MOBA_HEADS, axis=1)


def _query_features():
    sl = _alibi(NSA_HEADS)
    nsa = np.zeros((NSA_KV_HEADS, SLOT, 8), np.float32)
    for g in range(NSA_KV_HEADS):
        for r in range(NSA_GROUP):
            s = sl[g * NSA_GROUP + r]
            nsa[g, HEAD_DIM, r] = s
            nsa[g, HEAD_DIM + 1:, r] = s * SLC_BLOCK * np.arange(1, 64)
            nsa[g, HEAD_DIM, NSA_GROUP + r] = s * 64
            nsa[g, HEAD_DIM + 1, NSA_GROUP + r] = s
    sm = _alibi(MOBA_HEADS)
    mob = np.zeros((MOBA_HEADS, SLOT, 8), np.float32)
    for h in range(MOBA_HEADS):
        mob[h, HEAD_DIM:HEAD_DIM + 16, 0] = sm[h] * MOBA_BLOCK * np.arange(16)
        mob[h, HEAD_DIM + 16, 0] = sm[h]
    return nsa, mob


def _mask_biases():
    t = np.arange(TQ)[None, :]
    p2 = np.arange(2 * TQ)[:, None]
    cb = np.stack([p2 <= t, p2 <= TQ + t])
    p3 = np.arange(3 * TQ)[:, None]
    dist = np.stack([c * TQ + t - p3 for c in range(3)])
    wb = (dist >= 0) & (dist < WINDOW)
    to_bias = lambda m: np.where(m, 0.0, NEG).astype(np.float32)
    return to_bias(cb), to_bias(wb)


def _overlap_t(seq):
    nc = seq // CMP_STRIDE
    c_start = np.arange(nc) * CMP_STRIDE
    b_start = np.arange(64) * SLC_BLOCK
    ov = ((c_start[None, :] < (b_start + SLC_BLOCK)[:, None])
          & ((c_start + CMP_BLOCK)[None, :] > b_start[:, None])
          & (np.arange(nc) < nc - 1)[None, :] & (b_start < seq)[:, None])
    return ov.astype(np.float32)


def _compress_weights(pe, w1, w2):
    ty = np.array([0, 0, 1, 1])
    eye = jnp.eye(4, dtype=F32)
    w1r = w1.reshape(2, CMP_BLOCK, HEAD_DIM, CMP_HIDDEN)[ty]
    top = jnp.einsum('spdj,sS->psdSj', w1r[:, :CMP_STRIDE], eye).reshape(CMP_STRIDE * 256, 4 * CMP_HIDDEN)
    bot = jnp.einsum('spdj,sS->psdSj', w1r[:, CMP_STRIDE:], eye).reshape(CMP_STRIDE * 256, 4 * CMP_HIDDEN)
    per = pe[ty]
    pet = jnp.transpose(per[:, :CMP_STRIDE], (1, 0, 2)).reshape(1, CMP_STRIDE * 256)
    peb = jnp.transpose(per[:, CMP_STRIDE:], (1, 0, 2)).reshape(1, CMP_STRIDE * 256)
    w2p = jnp.concatenate([w2[ty], jnp.zeros((4, CMP_HIDDEN, SLOT - HEAD_DIM), F32)], axis=-1)
    w2b = jnp.einsum('sjd,sS->sjSd', w2p, eye).reshape(4 * CMP_HIDDEN, 4 * SLOT)
    w2k = w2b[:, :2 * SLOT].astype(BF16)
    w2vt = w2b[:, 2 * SLOT:].T.astype(BF16)
    return pet, peb, top.astype(BF16), bot.astype(BF16), w2k, w2vt


def kernel(x, mem, w_in, cmp_pe, cmp_w1, cmp_w2, conv_w, w_branch, w_out, ln1_g, ln1_b,
           xattn_wq, xattn_wkv, xattn_wo, ln2_g, ln2_b, router_w, router_b, exp_w13, exp_w2,
           shared_w13, shared_w2, ln3_g, ln3_b):
    bsz, seq, d = x.shape
    mlen = mem.shape[1]
    depth = w_in.shape[0]
    alpha = (2.0 * depth) ** 0.25
    t = bsz * seq
    scale = HEAD_DIM ** -0.5

    kfeat = jnp.asarray(_key_features(seq))
    fq_nsa, fq_moba = (jnp.asarray(a) for a in _query_features())
    cbias, wbias = (jnp.asarray(a) for a in _mask_biases())
    ovt = jnp.asarray(_overlap_t(seq)).astype(BF16)
    memf = mem.reshape(bsz * mlen, d)

    h = x.reshape(t, d)
    hb = h.astype(BF16)
    for l in range(depth):
        ht = hb.T
        wi = w_in[l]
        kv6 = wi[:, _OFF_NSA_KV:_OFF_NSA_G].reshape(d, 6, NSA_KV_HEADS * HEAD_DIM)
        mq, mk, mv = (wi[:, _OFF_MOBA + j * 512:_OFF_MOBA + (j + 1) * 512] for j in range(3))
        w_at = jnp.concatenate([wi[:, :512] * scale, kv6[:, 3], kv6[:, 5], mq * scale, mv],
                               axis=1).T.astype(BF16)
        w_b = jnp.concatenate([_pad_slots(kv6[:, 2]), _pad_slots(kv6[:, 4]), _pad_slots(mk)], axis=1).astype(BF16)
        wg = wi[:, _OFF_NSA_G:_OFF_MOBA].reshape(d, NSA_KV_HEADS, 12)
        wgt = jnp.concatenate([wg, jnp.zeros((d, NSA_KV_HEADS, SLOT - 12), F32)], axis=-1)
        wgt = wgt.reshape(d, NSA_KV_HEADS * SLOT).T.astype(BF16)
        w_c = jnp.concatenate([wi[:, _OFF_NSA_KV:_OFF_NSA_KV + 256], jnp.zeros((d, 256), F32),
                               wi[:, _OFF_CONV:]], axis=1).astype(BF16)

        value_slots = tuple(range(_SLOT_SLC_V, _SLOT_MOBA_Q)) + tuple(range(_SLOT_MOBA_V, _N_SLOTS_T))
        zat = _proj_t(w_at, ht, value_slots)
        gt = _mm(wgt, ht, F32, tm=256, tn=1024, name="proj_gate")
        zb = _mm(hb, w_b, BF16, tn=768, feats=kfeat, name="proj_b")
        zc = _mm(hb, w_c, BF16, tn=1024, name="proj_c")

        pet, peb, wt, wb, w2k, w2vt = _compress_weights(cmp_pe[l], cmp_w1[l], cmp_w2[l])
        sub = zc[:, :256].reshape(bsz, seq // CMP_STRIDE, CMP_STRIDE * 256)
        kc, vct = _compress(sub, pet, peb, wt, wb, w2k, w2vt)

        ya = _nsa_attention(zat, zb, gt, kc, vct, ovt, fq_nsa, cbias, wbias, bsz, seq)
        yb = _moba_attention(zat, zb, fq_moba, cbias, bsz, seq)

        cw = jnp.concatenate([conv_w[l], jnp.zeros((5, CONV_CH), F32)], axis=0)
        h = _merge(ya, yb, zc, h, cw, w_branch[l, 0].astype(BF16),
                   w_branch[l, 1].astype(BF16), w_branch[l, 2].astype(BF16),
                   w_out[l].astype(BF16), ln1_g[l][None], ln1_b[l][None], seq, alpha)

        kv = _mm(memf, xattn_wkv[l].astype(BF16), BF16, tm=512, name="xattn_kv")
        h, hp = _xattn(h, kv, xattn_wq[l].astype(BF16), xattn_wo[l].astype(BF16),
                       ln2_g[l][None], ln2_b[l][None], seq, mlen, alpha)

        h, hb = _moe(h, hp, router_w[l].astype(BF16), router_b[l], exp_w13, exp_w2, l,
                     shared_w13[l].astype(BF16), shared_w2[l].astype(BF16), ln3_g[l][None], ln3_b[l][None], alpha)
    return h.reshape(bsz, seq, d)
```

```python
import functools

import jax
import jax.numpy as jnp
import numpy as np
from jax import lax
from jax.experimental import pallas as pl
from jax.experimental.pallas import tpu as pltpu
from jax.experimental.pallas import tpu_sc as plsc

F32 = jnp.float32
BF16 = jnp.bfloat16
I32 = jnp.int32

D_MODEL = 1024
HEAD_DIM = 64
SLOT = 128
NEG = -1e30
LN_EPS = 1e-5

NSA_HEADS = 8
NSA_KV_HEADS = 2
NSA_GROUP = 4
CMP_BLOCK = 32
CMP_STRIDE = 16
CMP_HIDDEN = 256
SLC_BLOCK = 64
SLC_TOPK = 16
WINDOW = 512
MOBA_HEADS = 8
MOBA_BLOCK = 256
MOBA_TOPK = 3
CONV_CH = 512
XATTN_HEADS = 4
XATTN_HEAD_DIM = 128
N_EXPERTS = 256
TOP_K = 8
N_GROUPS = 8
TOPK_GROUPS = 4
EXPERT_DIM = 256
ROUTE_SCALE = 2.5

TQ = 256
MOBA_HPS = 4
PV_ROWS = 80
FORCED_KEY = int(np.float32(1e6).view(np.int32))
EXP_ROWS = 512
MOE_PARTS = 2
VMEM_LIMIT = 48 * 1024 * 1024
SC_CORES = 2
SC_SUBCORES = 16
SC_CHUNK = 64

_OFF_NSA_Q = 0
_OFF_NSA_KV = 512
_OFF_NSA_G = 1280
_OFF_MOBA = 1304
_OFF_CONV = 2840
_OFF_MERGE = 4376

_SLOT_NSA_Q = 0
_SLOT_SLC_V = 8
_SLOT_WIN_V = 10
_SLOT_MOBA_Q = 12
_SLOT_MOBA_V = 20
_N_SLOTS_T = 28


def _cparams(sem):
    return pltpu.CompilerParams(dimension_semantics=sem, vmem_limit_bytes=VMEM_LIMIT)


def _dot(a, b):
    return jnp.dot(a, b, preferred_element_type=F32)


def _dot_t(a, b):
    return lax.dot_general(a, b, (((1,), (1,)), ((), ())), preferred_element_type=F32)


def _layer_norm(x, g, b):
    mu = jnp.mean(x, axis=-1, keepdims=True)
    xc = x - mu
    var = jnp.mean(xc * xc, axis=-1, keepdims=True)
    return xc * lax.rsqrt(var + LN_EPS) * g + b


def _mm_kernel(x_ref, w_ref, o_ref):
    o_ref[...] = _dot(x_ref[...].astype(BF16), w_ref[...]).astype(o_ref.dtype)


def _mm_feat_kernel(x_ref, w_ref, f_ref, o_ref):
    y = _dot(x_ref[...].astype(BF16), w_ref[...]) + f_ref[...]
    o_ref[...] = y.astype(o_ref.dtype)


def _mm(x, w, out_dtype, *, tm=1024, tn=512, feats=None, name):
    m, k = x.shape
    n = w.shape[1]
    tm = min(tm, m)
    tn = min(tn, n)
    assert m % tm == 0 and n % tn == 0, (m, n, tm, tn)
    in_specs = [pl.BlockSpec((tm, k), lambda i, j: (i, 0)),
                pl.BlockSpec((k, tn), lambda i, j: (0, j))]
    args = [x, w]
    if feats is None:
        body = _mm_kernel
    else:
        per = feats.shape[0] // tm
        assert feats.shape[0] % tm == 0
        in_specs.append(pl.BlockSpec((tm, tn), lambda i, j: (i % per, j)))
        args.append(feats)
        body = _mm_feat_kernel
    return pl.pallas_call(
        body,
        out_shape=jax.ShapeDtypeStruct((m, n), out_dtype),
        grid=(m // tm, n // tn),
        in_specs=in_specs,
        out_specs=pl.BlockSpec((tm, tn), lambda i, j: (i, j)),
        compiler_params=_cparams(("parallel", "parallel")),
        name=name,
    )(*args)


def _proj_t_kernel(w_ref, ht_ref, o_ref, *, value_slots):
    acc = _dot(w_ref[...], ht_ref[...])
    tn = acc.shape[1]
    zero = jnp.zeros((SLOT - HEAD_DIM, tn), o_ref.dtype)
    ones_row = (lax.broadcasted_iota(I32, (SLOT - HEAD_DIM, tn), 0) == 0).astype(o_ref.dtype)
    for s in range(w_ref.shape[0] // HEAD_DIM):
        o_ref[s * SLOT:s * SLOT + HEAD_DIM, :] = acc[s * HEAD_DIM:(s + 1) * HEAD_DIM, :].astype(o_ref.dtype)
        o_ref[s * SLOT + HEAD_DIM:(s + 1) * SLOT, :] = ones_row if s in value_slots else zero


def _proj_t(w, ht, value_slots):
    m, k = w.shape
    t = ht.shape[1]
    tn = min(512, t)
    n_out = m // HEAD_DIM * SLOT
    return pl.pallas_call(
        functools.partial(_proj_t_kernel, value_slots=value_slots),
        out_shape=jax.ShapeDtypeStruct((n_out, t), BF16),
        grid=(t // tn,),
        in_specs=[pl.BlockSpec((m, k), lambda j: (0, 0)), pl.BlockSpec((k, tn), lambda j: (0, j))],
        out_specs=pl.BlockSpec((n_out, tn), lambda j: (0, j)),
        compiler_params=_cparams(("parallel",)),
        name="proj_at",
    )(w, ht)


def _cmp_kernel(sub_ref, pet_ref, peb_ref, wt_ref, wb_ref, w2k_ref, w2vt_ref, kc_ref, vct_ref):
    sub = sub_ref[0].astype(F32)
    nc = sub.shape[0]
    a = _dot((sub + pet_ref[...]).astype(BF16), wt_ref[...])
    b = _dot((sub + peb_ref[...]).astype(BF16), wb_ref[...])
    hid = jax.nn.gelu(a + pltpu.roll(b, nc - 1, 0)).astype(BF16)
    kc_ref[0] = _dot(hid, w2k_ref[...]).astype(kc_ref.dtype)
    vct_ref[0] = _dot_t(w2vt_ref[...], hid).astype(vct_ref.dtype)


def _compress(sub, pet, peb, wt, wb, w2k, w2vt):
    bsz, nc, kk = sub.shape
    n_h = wt.shape[1]
    n_o = w2k.shape[1]
    const = lambda b: (0, 0)
    return pl.pallas_call(
        _cmp_kernel,
        out_shape=(jax.ShapeDtypeStruct((bsz, nc, n_o), BF16), jax.ShapeDtypeStruct((bsz, n_o, nc), BF16)),
        grid=(bsz,),
        in_specs=[pl.BlockSpec((1, nc, kk), lambda b: (b, 0, 0)),
                  pl.BlockSpec((1, kk), const), pl.BlockSpec((1, kk), const),
                  pl.BlockSpec((kk, n_h), const), pl.BlockSpec((kk, n_h), const),
                  pl.BlockSpec((n_h, n_o), const), pl.BlockSpec((n_o, n_h), const)],
        out_specs=(pl.BlockSpec((1, nc, n_o), lambda b: (b, 0, 0)), pl.BlockSpec((1, n_o, nc), lambda b: (b, 0, 0))),
        compiler_params=_cparams(("parallel",)),
        name="nsa_compress",
    )(sub, pet, peb, wt, wb, w2k, w2vt)


def _flash_reset(m_scr, acc_scr):
    m_scr[...] = jnp.full(m_scr.shape, NEG, F32)
    acc_scr[...] = jnp.zeros(acc_scr.shape, F32)


def _qk(kget, qxs, start, size):
    k0 = kget(0, start, size)
    if kget(1, start, size) is None:
        return _dot(k0, jnp.concatenate(qxs, axis=1))
    return jnp.concatenate([_dot(k0, qxs[0])] + [_dot(kget(c, start, size), qxs[c])
                                                 for c in range(1, len(qxs))], axis=1)


def _softmax_pv(s, vget, nq, start, size, m_scr, acc_scr, bias=None):
    if bias is not None:
        s = s + jnp.concatenate([bias] * nq, axis=1)
    m_prev = m_scr[...]
    m_new = jnp.maximum(m_prev, jnp.max(s, axis=0, keepdims=True))
    alpha = jnp.exp(m_prev - m_new)
    p = jnp.exp(s - m_new).astype(BF16)
    v0 = vget(0, start, size)
    if vget(1, start, size) is None:
        pv = _dot(v0, p)
    else:
        pv = jnp.concatenate([_dot(v0, p[:, :TQ])] + [_dot(vget(c, start, size), p[:, c * TQ:(c + 1) * TQ])
                                                      for c in range(1, nq)], axis=1)
    acc_scr[...] = alpha * acc_scr[...] + pv
    m_scr[...] = m_new


def _rows(ref, start, size, lane0=0):
    return ref[pl.ds(pl.multiple_of(start, TQ), size), lane0:lane0 + SLOT]


def _cols(ref, start, size, row0=0):
    return ref[row0:row0 + PV_ROWS, pl.ds(pl.multiple_of(start, TQ), size)]


def _store_token_rows(o_ref, heads_t):
    for pr in range(len(heads_t) // 2):
        pair = jnp.concatenate([heads_t[2 * pr][:HEAD_DIM], heads_t[2 * pr + 1][:HEAD_DIM]], axis=0)
        o_ref[:, pr * SLOT:(pr + 1) * SLOT] = pair.T.astype(o_ref.dtype)


def _flash_causal(kget, vget, qxs, i, cb_ref, sa_scr, sb_scr, m_scr, acc_scr):
    tq = TQ
    tk = 2 * tq
    nq = len(qxs)
    n_steps = i // 2 + 1
    sa_scr[...] = _qk(kget, qxs, 0, tk)

    def body(jj, carry):
        j = 2 * jj
        s = sa_scr[...]
        sb_scr[...] = _qk(kget, qxs, (j + 1) * tk, tk)
        _softmax_pv(s, vget, nq, j * tk, tk, m_scr, acc_scr)
        s = sb_scr[...]
        sa_scr[...] = _qk(kget, qxs, (j + 2) * tk, tk)
        _softmax_pv(s, vget, nq, (j + 1) * tk, tk, m_scr, acc_scr)
        return carry

    lax.fori_loop(0, (n_steps - 1) // 2, body, 0)
    last = n_steps - 1
    bias = cb_ref[i % 2]

    @pl.when(last % 2 == 0)
    def _():
        _softmax_pv(sa_scr[...], vget, nq, last * tk, tk, m_scr, acc_scr, bias=bias)

    @pl.when(last % 2 == 1)
    def _():
        s = sa_scr[...]
        sb_scr[...] = _qk(kget, qxs, last * tk, tk)
        _softmax_pv(s, vget, nq, (last - 1) * tk, tk, m_scr, acc_scr)
        _softmax_pv(sb_scr[...], vget, nq, last * tk, tk, m_scr, acc_scr, bias=bias)


def _nsa_kernel(q_ref, kc_ref, vct_ref, ks_ref, vst_ref, kw_ref, vwt_ref, g_ref, ovt_ref, fq_ref, cb_ref, wb_ref,
                o_ref, m_scr, acc_scr, imp_scr, sa_scr, sb_scr, *, tq, nsel):
    i = pl.program_id(2)
    r4 = NSA_GROUP
    rr = r4 * tq
    q4 = [q_ref[r * SLOT:(r + 1) * SLOT, :] for r in range(r4)]
    fq = fq_ref[0]

    kc = kc_ref[0]
    nc = kc.shape[0]
    qs = jnp.concatenate(q4, axis=1)
    s = _dot(kc, qs)
    t_row = i * tq + (lax.broadcasted_iota(I32, (1, rr), 1) & (tq - 1))
    last_c = (t_row - (CMP_BLOCK - 1)) // CMP_STRIDE
    cmask = lax.broadcasted_iota(I32, (nc, rr), 0) <= last_c
    s = jnp.where(cmask, s, NEG)
    mx = jnp.max(s, axis=0, keepdims=True)
    e = jnp.where(cmask, jnp.exp(s - mx), 0.0)
    lsum = jnp.sum(e, axis=0, keepdims=True)
    p_cmp = (e * (1.0 / jnp.where(lsum > 0.0, lsum, 1.0))).astype(BF16)
    o_cmp = _dot(vct_ref[0], p_cmp)

    ovt = ovt_ref[...]
    imp4 = _dot(ovt, p_cmp)
    imp = imp4[:, 0:tq]
    for r in range(1, r4):
        imp = imp + imp4[:, r * tq:(r + 1) * tq]
    nbp = imp.shape[0]
    jidx = lax.broadcasted_iota(I32, (nbp, tq), 0)
    cur = (i * tq + lax.broadcasted_iota(I32, (nbp, tq), 1)) // SLC_BLOCK
    forced = (jidx == 0) | (jidx == cur) | (jidx == cur - 1)
    key = jnp.where(forced, FORCED_KEY, jnp.where(jidx > cur, -1, lax.bitcast_convert_type(imp, I32)))
    imp_scr[...] = key

    def rank_body(jj, rank):
        for jp in (2 * jj, 2 * jj + 1):
            row = imp_scr[pl.ds(jp, 1), :]
            rank = rank + (row > key - (jidx > jp).astype(I32)).astype(I32)
        return rank

    n_live = (i + 1) * (tq // SLC_BLOCK)
    rank = lax.fori_loop(0, n_live // 2, rank_body, jnp.zeros((nbp, tq), I32))
    sel = (rank < nsel) & (jidx <= cur)
    mask_t = jnp.where(sel, 0.0, NEG)
    feat = jnp.concatenate([jnp.zeros((SLOT - nbp, tq), F32), mask_t], axis=0)

    qxs = [q4[r] + (feat + fq[:, r:r + 1]).astype(BF16) for r in range(r4)]
    kget = lambda c, start, size: _rows(ks_ref, start, size) if c == 0 else None
    vget = lambda c, start, size: _cols(vst_ref, start, size) if c == 0 else None
    _flash_reset(m_scr, acc_scr)
    _flash_causal(kget, vget, qxs, i, cb_ref, sa_scr, sb_scr, m_scr, acc_scr)
    o_slc = acc_scr[...] * (1.0 / acc_scr[HEAD_DIM:HEAD_DIM + 1, :])

    qx = jnp.concatenate([q4[r] + fq[:, r4 + r:r4 + r + 1].astype(BF16) for r in range(r4)], axis=1)
    w0 = jnp.maximum(i - 2, 0) * tq
    s = _dot(_rows(kw_ref, w0, 3 * tq), qx)
    s = s + jnp.concatenate([wb_ref[jnp.minimum(i, 2)]] * r4, axis=1)
    e = jnp.exp(s - jnp.max(s, axis=0, keepdims=True)).astype(BF16)
    o_win = _dot(_cols(vwt_ref, w0, 3 * tq), e)
    o_win = o_win * (1.0 / o_win[HEAD_DIM:HEAD_DIM + 1, :])

    gate = jax.nn.sigmoid(g_ref[...])
    outs = []
    for r in range(r4):
        sl = slice(r * tq, (r + 1) * tq)
        outs.append(gate[3 * r:3 * r + 1, :] * o_cmp[:PV_ROWS, sl] + gate[3 * r + 1:3 * r + 2, :] * o_slc[:, sl]
                    + gate[3 * r + 2:3 * r + 3, :] * o_win[:, sl])
    _store_token_rows(o_ref, outs)


def _nsa_attention(zat, zb, gt, kc, vct, ovt, fq, cb, wb, bsz, seq):
    tq = TQ
    ni = seq // tq
    nc = kc.shape[1]
    nsel = min(SLC_TOPK, seq // SLC_BLOCK)
    assert seq // SLC_BLOCK <= 64 and WINDOW == 2 * tq and ni % 2 == 0 and ni >= 3
    rr = NSA_GROUP * tq
    gw = NSA_GROUP * SLOT
    in_specs = [
        pl.BlockSpec((gw, tq), lambda b, g, i: (g, b * ni + i)),
        pl.BlockSpec((1, nc, SLOT), lambda b, g, i: (b, 0, g)),
        pl.BlockSpec((1, SLOT, nc), lambda b, g, i: (b, g, 0)),
        pl.BlockSpec((seq, SLOT), lambda b, g, i: (b, g)),
        pl.BlockSpec((SLOT, seq), lambda b, g, i: (_SLOT_SLC_V + g, b)),
        pl.BlockSpec((seq, SLOT), lambda b, g, i: (b, 2 + g)),
        pl.BlockSpec((SLOT, seq), lambda b, g, i: (_SLOT_WIN_V + g, b)),
        pl.BlockSpec((SLOT, tq), lambda b, g, i: (g, b * ni + i)),
        pl.BlockSpec(ovt.shape, lambda b, g, i: (0, 0)),
        pl.BlockSpec((1, SLOT, 8), lambda b, g, i: (g, 0, 0)),
        pl.BlockSpec(cb.shape, lambda b, g, i: (0, 0, 0)),
        pl.BlockSpec(wb.shape, lambda b, g, i: (0, 0, 0)),
    ]
    return pl.pallas_call(
        functools.partial(_nsa_kernel, tq=tq, nsel=nsel),
        out_shape=jax.ShapeDtypeStruct((bsz * seq, NSA_HEADS * HEAD_DIM), BF16),
        grid=(bsz, NSA_KV_HEADS, ni),
        in_specs=in_specs,
        out_specs=pl.BlockSpec((tq, NSA_GROUP * HEAD_DIM), lambda b, g, i: (b * ni + i, g)),
        scratch_shapes=[pltpu.VMEM((1, rr), F32),
                        pltpu.VMEM((PV_ROWS, rr), F32), pltpu.VMEM((64, tq), I32),
                        pltpu.VMEM((2 * tq, rr), F32), pltpu.VMEM((2 * tq, rr), F32)],
        compiler_params=_cparams(("parallel", "parallel", "arbitrary")),
        name="nsa_attention",
    )(zat, kc, vct, zb, zat, zb, zat, gt, ovt, fq, cb, wb)


def _moba_kernel(q_ref, k_ref, vt_ref, fq_ref, cb_ref, o_ref, m_scr, acc_scr, km_scr, gate_scr,
                 sa_scr, sb_scr, *, tq, nblk, ntop):
    i = pl.program_id(2)
    nh = MOBA_HPS

    @pl.when(i == 0)
    def _():
        for hh in range(nh):
            kh = k_ref[:, hh * SLOT:(hh + 1) * SLOT].astype(F32)
            km = jnp.mean(kh.reshape(nblk, tq, SLOT), axis=1)
            if nblk < 16:
                km = jnp.concatenate([km, jnp.zeros((16 - nblk, SLOT), F32)], axis=0)
            km_scr[hh] = km

    qs = [q_ref[hh * SLOT:(hh + 1) * SLOT, :] for hh in range(nh)]
    gates = [_dot(km_scr[hh].astype(BF16), qs[hh]) for hh in range(nh)]
    for hh in range(nh):
        gate_scr[hh] = gates[hh]
    jidx = lax.broadcasted_iota(I32, (16, tq), 0)

    def rank_body(jp, ranks):
        out = []
        for hh in range(nh):
            row = gate_scr[hh, pl.ds(jp, 1), :]
            better = (row > gates[hh]) | ((row == gates[hh]) & (jp < jidx))
            out.append(ranks[hh] + better.astype(I32))
        return tuple(out)

    ranks = lax.fori_loop(0, i, rank_body, tuple(jnp.zeros((16, tq), I32) for _ in range(nh)))
    qxs = []
    for hh in range(nh):
        keep = ((jidx < i) & (ranks[hh] < ntop)) | (jidx == i)
        mask_t = jnp.where(keep, 0.0, NEG)
        feat = jnp.concatenate([jnp.zeros((HEAD_DIM, tq), F32), mask_t,
                                jnp.zeros((SLOT - HEAD_DIM - 16, tq), F32)], axis=0)
        qxs.append(qs[hh] + (feat + fq_ref[hh][:, 0:1]).astype(BF16))
    kget = lambda c, start, size: _rows(k_ref, start, size, c * SLOT)
    vget = lambda c, start, size: _cols(vt_ref, start, size, c * SLOT)
    _flash_reset(m_scr, acc_scr)
    _flash_causal(kget, vget, qxs, i, cb_ref, sa_scr, sb_scr, m_scr, acc_scr)
    o = acc_scr[...] * (1.0 / acc_scr[HEAD_DIM:HEAD_DIM + 1, :])
    _store_token_rows(o_ref, [o[:, hh * tq:(hh + 1) * tq] for hh in range(nh)])


def _moba_attention(zat, zb, fq, cb, bsz, seq):
    tq = TQ
    assert tq == MOBA_BLOCK and seq % tq == 0
    ni = seq // tq
    assert ni <= 16 and ni % 2 == 0
    ntop = min(MOBA_TOPK, ni)
    nh = MOBA_HPS
    hw = nh * SLOT
    q0 = _SLOT_MOBA_Q // nh
    v0 = _SLOT_MOBA_V // nh
    return pl.pallas_call(
        functools.partial(_moba_kernel, tq=tq, nblk=ni, ntop=ntop),
        out_shape=jax.ShapeDtypeStruct((bsz * seq, MOBA_HEADS * HEAD_DIM), BF16),
        grid=(bsz, MOBA_HEADS // nh, ni),
        in_specs=[pl.BlockSpec((hw, tq), lambda b, h, i: (q0 + h, b * ni + i)),
                  pl.BlockSpec((seq, hw), lambda b, h, i: (b, 1 + h)),
                  pl.BlockSpec((hw, seq), lambda b, h, i: (v0 + h, b)),
                  pl.BlockSpec((nh, SLOT, 8), lambda b, h, i: (h, 0, 0)),
                  pl.BlockSpec(cb.shape, lambda b, h, i: (0, 0, 0))],
        out_specs=pl.BlockSpec((tq, nh * HEAD_DIM), lambda b, h, i: (b * ni + i, h)),
        scratch_shapes=[pltpu.VMEM((1, nh * tq), F32),
                        pltpu.VMEM((PV_ROWS, nh * tq), F32),
                        pltpu.VMEM((nh, 16, SLOT), F32), pltpu.VMEM((nh, 16, tq), F32),
                        pltpu.VMEM((2 * tq, nh * tq), F32), pltpu.VMEM((2 * tq, nh * tq), F32)],
        compiler_params=_cparams(("parallel", "parallel", "arbitrary")),
        name="moba_attention",
    )(zat, zb, zat, fq, cb)


def _merge_kernel(ya_ref, yb_ref, hc_ref, bc_ref, cc_ref, hcp_ref, ccp_ref, g0_ref, g1_ref, g2_ref,
                  h_ref, cw_ref, wa_ref, wb_ref, wc_ref, wo_ref, lg_ref, lb_ref, o_ref, *, tm, per, alpha):
    i = pl.program_id(0)
    u = cc_ref[...].astype(F32) * hc_ref[...].astype(F32)
    first = (i % per) == 0
    up = jnp.where(first, 0.0, ccp_ref[...].astype(F32) * hcp_ref[...].astype(F32))
    rowi = lax.broadcasted_iota(I32, u.shape, 0)
    u1 = jnp.where(rowi == 0, up[15:16, :], pltpu.roll(u, 1, 0))
    u2 = jnp.where(rowi == 0, up[14:15, :], jnp.where(rowi == 1, up[15:16, :], pltpu.roll(u, 2, 0)))
    cw = cw_ref[...]
    yc = bc_ref[...].astype(F32) * (cw[0:1, :] * u2 + cw[1:2, :] * u1 + cw[2:3, :] * u)
    merged = (jax.nn.sigmoid(g0_ref[...].astype(F32)) * _dot(ya_ref[...], wa_ref[...])
              + jax.nn.sigmoid(g1_ref[...].astype(F32)) * _dot(yb_ref[...], wb_ref[...])
              + jax.nn.sigmoid(g2_ref[...].astype(F32)) * _dot(yc.astype(BF16), wc_ref[...]))
    mix = _dot(merged.astype(BF16), wo_ref[...])
    o_ref[...] = _layer_norm(alpha * h_ref[...] + mix, lg_ref[...], lb_ref[...])


def _merge(ya, yb, zc, h, cw, wa, wb, wc, wo, lg, lb, seq, alpha):
    tm = 512
    t = h.shape[0]
    per = seq // tm
    d = D_MODEL
    row = lambda i: (i, 0)
    const = lambda i: (0, 0)
    prev = lambda c: (lambda i: (jnp.maximum(i * (tm // 16) - 1, 0), c))
    in_specs = [
        pl.BlockSpec((tm, ya.shape[1]), row), pl.BlockSpec((tm, yb.shape[1]), row),
        pl.BlockSpec((tm, CONV_CH), lambda i: (i, 1)), pl.BlockSpec((tm, CONV_CH), lambda i: (i, 2)),
        pl.BlockSpec((tm, CONV_CH), lambda i: (i, 3)),
        pl.BlockSpec((16, CONV_CH), prev(1)), pl.BlockSpec((16, CONV_CH), prev(3)),
        pl.BlockSpec((tm, d), lambda i: (i, 2)), pl.BlockSpec((tm, d), lambda i: (i, 3)),
        pl.BlockSpec((tm, d), lambda i: (i, 4)),
        pl.BlockSpec((tm, d), row),
        pl.BlockSpec((8, CONV_CH), const),
        pl.BlockSpec(wa.shape, const), pl.BlockSpec(wb.shape, const), pl.BlockSpec((CONV_CH, d), const),
        pl.BlockSpec((d, d), const), pl.BlockSpec((1, d), const), pl.BlockSpec((1, d), const),
    ]
    return pl.pallas_call(
        functools.partial(_merge_kernel, tm=tm, per=per, alpha=alpha),
        out_shape=jax.ShapeDtypeStruct((t, d), F32),
        grid=(t // tm,),
        in_specs=in_specs,
        out_specs=pl.BlockSpec((tm, d), row),
        compiler_params=_cparams(("parallel",)),
        name="mixer_merge",
    )(ya, yb, zc, zc, zc, zc, zc, zc, zc, zc, h, cw, wa, wb, wc, wo, lg, lb)


def _xattn_kernel(h_ref, kv_ref, wq_ref, wo_ref, lg_ref, lb_ref, o_ref, ob_ref, *, alpha):
    h = h_ref[...]
    q = _dot(h.astype(BF16), wq_ref[...]).astype(BF16)
    kv = kv_ref[...]
    nh = XATTN_HEADS
    hd = XATTN_HEAD_DIM
    outs = []
    for hh in range(nh):
        s = _dot_t(q[:, hh * hd:(hh + 1) * hd], kv[:, hh * hd:(hh + 1) * hd]) * (hd ** -0.5)
        s = s - jnp.max(s, axis=-1, keepdims=True)
        e = jnp.exp(s)
        p = e / jnp.sum(e, axis=-1, keepdims=True)
        outs.append(_dot(p.astype(BF16), kv[:, (nh + hh) * hd:(nh + hh + 1) * hd]))
    o = jnp.concatenate(outs, axis=-1).astype(BF16)
    y = _layer_norm(alpha * h + _dot(o, wo_ref[...]), lg_ref[...], lb_ref[...])
    o_ref[...] = y
    bits = lax.bitcast_convert_type(y.astype(BF16).astype(F32), I32)
    half = y.shape[1] // 2
    ob_ref[...] = (bits[:, half:] & -65536) | lax.shift_right_logical(bits[:, :half], 16)


def _unpack_pairs(words):
    lo = lax.bitcast_convert_type(lax.shift_left(words, 16), F32).astype(BF16)
    hi = lax.bitcast_convert_type(words & -65536, F32).astype(BF16)
    return lo, hi


def _xattn(h, kv, wq, wo, lg, lb, seq, mlen, alpha):
    tm = 512
    t = h.shape[0]
    per = seq // tm
    d = D_MODEL
    row = lambda i: (i, 0)
    const = lambda i: (0, 0)
    return pl.pallas_call(
        functools.partial(_xattn_kernel, alpha=alpha),
        out_shape=(jax.ShapeDtypeStruct((t, d), F32), jax.ShapeDtypeStruct((t, d // 2), I32)),
        grid=(t // tm,),
        in_specs=[pl.BlockSpec((tm, d), row), pl.BlockSpec((mlen, kv.shape[1]), lambda i: (i // per, 0)),
                  pl.BlockSpec(wq.shape, const), pl.BlockSpec(wo.shape, const),
                  pl.BlockSpec((1, d), const), pl.BlockSpec((1, d), const)],
        out_specs=(pl.BlockSpec((tm, d), row), pl.BlockSpec((tm, d // 2), row)),
        compiler_params=_cparams(("parallel",)),
        name="mem_xattn",
    )(h, kv, wq, wo, lg, lb)


def _expert_kernel(be_ref, nu_ref, nxt_ref, slot_ref, x_ref, w13_hbm, w2_hbm, o_ref,
                   w13f_scr, w2f_scr, w13b_scr, w2b_scr, sem, *, layer):
    i = pl.program_id(0)
    e = be_ref[i]
    s = slot_ref[i]

    def weight_copies(expert, slot):
        return (pltpu.make_async_copy(w13_hbm.at[layer, expert], w13f_scr.at[slot], sem.at[0, slot]),
                pltpu.make_async_copy(w2_hbm.at[layer, expert], w2f_scr.at[slot], sem.at[1, slot]))

    @pl.when(i == 0)
    def _():
        for cp in weight_copies(e, s):
            cp.start()

    @pl.when((i == 0) | (e != be_ref[jnp.maximum(i - 1, 0)]))
    def _():
        for cp in weight_copies(e, s):
            cp.wait()
        nxt = nxt_ref[i]

        @pl.when(nxt >= 0)
        def _():
            for cp in weight_copies(nxt, 1 - s):
                cp.start()

        w13b_scr[...] = w13f_scr[s].astype(BF16)
        w2b_scr[...] = w2f_scr[s].astype(BF16)

    @pl.when(i < nu_ref[0])
    def _():
        x_lo, x_hi = _unpack_pairs(x_ref[...])
        half = x_lo.shape[1]
        hmid = _dot(x_lo, w13b_scr[:half, :]) + _dot(x_hi, w13b_scr[half:, :])
        a = hmid[:, :EXPERT_DIM]
        g = hmid[:, EXPERT_DIM:]
        act = (a * jax.nn.sigmoid(a) * g).astype(BF16)
        o_ref[...] = _dot(act, w2b_scr[...]).astype(o_ref.dtype)


def _experts(xg, w13, w2, block_e, n_used, next_e, slot, layer):
    n_pad = xg.shape[0]
    d = 2 * xg.shape[1]
    rows = EXP_ROWS
    nb = n_pad // rows
    blk = lambda i, be, nu, nx, sl: (jnp.minimum(i, nu[0] - 1), 0)
    grid_spec = pltpu.PrefetchScalarGridSpec(
        num_scalar_prefetch=4,
        grid=(nb,),
        in_specs=[pl.BlockSpec((rows, d // 2), blk),
                  pl.BlockSpec(memory_space=pl.ANY), pl.BlockSpec(memory_space=pl.ANY)],
        out_specs=pl.BlockSpec((rows, d), blk),
        scratch_shapes=[pltpu.VMEM((2, d, 2 * EXPERT_DIM), F32), pltpu.VMEM((2, EXPERT_DIM, d), F32),
                        pltpu.VMEM((d, 2 * EXPERT_DIM), BF16), pltpu.VMEM((EXPERT_DIM, d), BF16),
                        pltpu.SemaphoreType.DMA((2, 2))],
    )
    return pl.pallas_call(
        functools.partial(_expert_kernel, layer=layer),
        out_shape=jax.ShapeDtypeStruct((n_pad, d), BF16),
        grid_spec=grid_spec,
        compiler_params=_cparams(("arbitrary",)),
        name="moe_experts",
    )(block_e, n_used, next_e, slot, xg, w13, w2)


def _moe_out_kernel(h_ref, y8_ref, wk_ref, w13_ref, w2_ref, lg_ref, lb_ref, o_ref, ob_ref, *, alpha):
    h = h_ref[...]
    hmid = _dot(h.astype(BF16), w13_ref[...])
    a = hmid[:, :EXPERT_DIM]
    g = hmid[:, EXPERT_DIM:]
    acc = alpha * h + _dot((a * jax.nn.sigmoid(a) * g).astype(BF16), w2_ref[...])
    wk = wk_ref[...]
    for k in range(TOP_K):
        acc = acc + wk[:, k:k + 1] * y8_ref[k].astype(F32)
    y = _layer_norm(acc, lg_ref[...], lb_ref[...])
    o_ref[...] = y
    ob_ref[...] = y.astype(BF16)


def _moe_out_part_kernel(h_ref, y8_ref, wk_ref, w13_ref, w2_ref, lg_ref, lb_ref, prev_o, prev_ob, o_ref, ob_ref,
                         *, alpha):
    del prev_o, prev_ob
    _moe_out_kernel(h_ref, y8_ref, wk_ref, w13_ref, w2_ref, lg_ref, lb_ref, o_ref, ob_ref, alpha=alpha)


def _moe_out(h, y8, wk, w13, w2, lg, lb, alpha, part, n_parts, prev):
    tm = 256
    t, d = h.shape
    off = part * (t // n_parts // tm)
    row = lambda i: (i + off, 0)
    const = lambda i: (0, 0)
    in_specs = [pl.BlockSpec((tm, d), row), pl.BlockSpec((TOP_K, tm, d), lambda i: (0, i, 0)),
                pl.BlockSpec((tm, TOP_K), lambda i: (i, 0)),
                pl.BlockSpec(w13.shape, const), pl.BlockSpec(w2.shape, const),
                pl.BlockSpec((1, d), const), pl.BlockSpec((1, d), const)]
    args = [h, y8, wk, w13, w2, lg, lb]
    if prev is None:
        body, aliases = _moe_out_kernel, {}
    else:
        body, aliases = _moe_out_part_kernel, {7: 0, 8: 1}
        in_specs += [pl.BlockSpec(memory_space=pl.ANY), pl.BlockSpec(memory_space=pl.ANY)]
        args += list(prev)
    return pl.pallas_call(
        functools.partial(body, alpha=alpha),
        out_shape=(jax.ShapeDtypeStruct((t, d), F32), jax.ShapeDtypeStruct((t, d), BF16)),
        grid=(t // n_parts // tm,),
        in_specs=in_specs,
        out_specs=(pl.BlockSpec((tm, d), row), pl.BlockSpec((tm, d), row)),
        input_output_aliases=aliases,
        compiler_params=_cparams(("parallel",)),
        name="moe_shared_ln",
    )(*args)


def _dest_kernel(idx_ref, rank_ref, ps_ref, o_ref):
    idx = idx_ref[...]
    ps = ps_ref[...]
    tm = idx.shape[1]
    eidx = lax.broadcasted_iota(I32, (N_EXPERTS, tm), 0)
    rows = [jnp.sum(jnp.where(eidx == idx[k:k + 1, :], ps, 0.0), axis=0, keepdims=True) for k in range(TOP_K)]
    o_ref[...] = jnp.concatenate(rows, axis=0).astype(I32) + rank_ref[...]


def _dest(idx, rank, pstarts):
    tm = 1024
    t = idx.shape[1]
    tm = min(tm, t)
    col = lambda i: (0, i)
    return pl.pallas_call(
        _dest_kernel,
        out_shape=jax.ShapeDtypeStruct((TOP_K, t), I32),
        grid=(t // tm,),
        in_specs=[pl.BlockSpec((TOP_K, tm), col), pl.BlockSpec((TOP_K, tm), col),
                  pl.BlockSpec((N_EXPERTS, 1), lambda i: (0, 0))],
        out_specs=pl.BlockSpec((TOP_K, tm), col),
        compiler_params=_cparams(("parallel",)),
        name="moe_dest",
    )(idx, rank, pstarts)


def _router_kernel(h_ref, rwt_ref, rb_ref, tri_ref, idx_ref, w_ref, rank_ref, cnt_ref, carry_scr, *, tm):
    i = pl.program_id(0)

    @pl.when(i == 0)
    def _():
        carry_scr[...] = jnp.zeros(carry_scr.shape, F32)

    ne = N_EXPERTS
    gsz = ne // N_GROUPS
    s = jax.nn.sigmoid(_dot_t(rwt_ref[...], h_ref[...].astype(BF16)))
    sb = s + rb_ref[...]
    sb3 = sb.reshape(N_GROUPS, gsz, tm)
    li = lax.broadcasted_iota(I32, (N_GROUPS, gsz, tm), 1)
    m1 = jnp.max(sb3, axis=1, keepdims=True)
    first = jnp.min(jnp.where(sb3 == m1, li, gsz), axis=1, keepdims=True)
    m2 = jnp.max(jnp.where(li == first, -jnp.inf, sb3), axis=1, keepdims=True)
    gs = (m1 + m2).reshape(N_GROUPS, tm)
    gi = lax.broadcasted_iota(I32, (N_GROUPS, tm), 0)
    grank = jnp.zeros((N_GROUPS, tm), I32)
    for gp in range(N_GROUPS):
        row = gs[gp:gp + 1, :]
        grank = grank + ((row > gs) | ((row == gs) & (gp < gi))).astype(I32)
    gkeep = (grank < TOPK_GROUPS).astype(F32)
    ekeep = jnp.broadcast_to(gkeep[:, None, :], (N_GROUPS, gsz, tm)).reshape(ne, tm)
    cand = jnp.where(ekeep > 0.0, sb, NEG)
    eidx = lax.broadcasted_iota(I32, (ne, tm), 0)
    sel = jnp.zeros((ne, tm), F32)
    idxs, wts = [], []
    for _ in range(TOP_K):
        mx = jnp.max(cand, axis=0, keepdims=True)
        ik = jnp.min(jnp.where(cand == mx, eidx, ne), axis=0, keepdims=True)
        hit = eidx == ik
        wts.append(jnp.sum(jnp.where(hit, s, 0.0), axis=0, keepdims=True))
        idxs.append(ik)
        sel = jnp.where(hit, 1.0, sel)
        cand = jnp.where(hit, -jnp.inf, cand)
    before = _dot(sel.astype(BF16), tri_ref[...]) + carry_scr[...]
    ranks = [jnp.sum(jnp.where(eidx == ik, before, 0.0), axis=0, keepdims=True) for ik in idxs]
    carry_scr[...] = carry_scr[...] + jnp.sum(sel, axis=1, keepdims=True)
    w = jnp.concatenate(wts, axis=0)
    idx_ref[...] = jnp.concatenate(idxs, axis=0)
    w_ref[...] = w / jnp.sum(w, axis=0, keepdims=True) * ROUTE_SCALE
    rank_ref[...] = jnp.concatenate(ranks, axis=0).astype(I32)
    cnt_ref[...] = jnp.broadcast_to(carry_scr[...], cnt_ref.shape)


def _router(h, rwt, rb):
    tm = 512
    t, d = h.shape
    tri = jnp.asarray(np.triu(np.ones((tm, tm), np.float32), 1)).astype(BF16)
    const = lambda i: (0, 0)
    col = lambda i: (0, i)
    return pl.pallas_call(
        functools.partial(_router_kernel, tm=tm),
        out_shape=(jax.ShapeDtypeStruct((TOP_K, t), I32), jax.ShapeDtypeStruct((TOP_K, t), F32),
                   jax.ShapeDtypeStruct((TOP_K, t), I32), jax.ShapeDtypeStruct((N_EXPERTS, SLOT), F32)),
        grid=(t // tm,),
        in_specs=[pl.BlockSpec((tm, d), lambda i: (i, 0)), pl.BlockSpec((N_EXPERTS, d), const),
                  pl.BlockSpec((N_EXPERTS, 1), const), pl.BlockSpec((tm, tm), const)],
        out_specs=(pl.BlockSpec((TOP_K, tm), col), pl.BlockSpec((TOP_K, tm), col),
                   pl.BlockSpec((TOP_K, tm), col), pl.BlockSpec((N_EXPERTS, SLOT), const)),
        scratch_shapes=[pltpu.VMEM((N_EXPERTS, 1), F32)],
        compiler_params=_cparams(("arbitrary",)),
        name="moe_router",
    )(h, rwt, rb, tri)


def _dispatch(hp, dest_flat, n_pad):
    t, w = hp.shape
    per_w = t // (SC_CORES * SC_SUBCORES)
    assert per_w * SC_CORES * SC_SUBCORES == t and per_w % SC_CHUNK == 0 and dest_flat.shape[0] == TOP_K * t
    mesh = plsc.VectorSubcoreMesh(core_axis_name="c", subcore_axis_name="s")

    @functools.partial(
        pl.kernel, out_type=jax.ShapeDtypeStruct((n_pad, w), I32), mesh=mesh,
        scratch_types=[pltpu.VMEM((SC_CHUNK,), I32)] * TOP_K
        + [pltpu.VMEM((SC_CHUNK, w), I32), pltpu.SemaphoreType.DMA],
        name="moe_dispatch")
    def scatter_rows(hp_hbm, dest_hbm, xg_hbm, *scratch):
        idx_vs, rows_v, sem = scratch[:TOP_K], scratch[TOP_K], scratch[TOP_K + 1]
        wid = lax.axis_index("s") * SC_CORES + lax.axis_index("c")
        base = wid * per_w

        @pl.loop(0, per_w // SC_CHUNK)
        def _(j):
            t0 = pl.multiple_of(base + j * SC_CHUNK, SC_CHUNK)
            pltpu.sync_copy(hp_hbm.at[pl.ds(t0, SC_CHUNK)], rows_v)
            for k in range(TOP_K):
                pltpu.sync_copy(dest_hbm.at[pl.ds(pl.multiple_of(k * t + t0, SC_CHUNK), SC_CHUNK)], idx_vs[k])
            copies = [pltpu.async_copy(rows_v, xg_hbm.at[idx_vs[k]], sem) for k in range(TOP_K)]
            for cp in copies:
                cp.wait()

    return scatter_rows(hp, dest_flat)


def _moe(h, hp, router_w, router_b, exp_w13, exp_w2, layer, shared_w13, shared_w2, lg, lb, alpha):
    t, d = h.shape
    idx, wts, rank, cnt = _router(h, router_w.T, router_b.astype(F32).reshape(N_EXPERTS, 1))
    rows = EXP_ROWS
    n_a = t * TOP_K
    counts = cnt[:, 0].astype(I32)
    pcounts = (counts + rows - 1) // rows * rows
    pends = jnp.cumsum(pcounts)
    pstarts = pends - pcounts
    dest = _dest(idx, rank, pstarts.astype(F32).reshape(N_EXPERTS, 1))
    n_blocks = -(-n_a // rows) + N_EXPERTS
    n_pad = n_blocks * rows
    n_used = (pends[-1:] // rows).astype(I32)
    first_row = jnp.minimum(jnp.arange(n_blocks, dtype=I32), n_used - 1) * rows
    block_e = jnp.sum((pends[None, :] <= first_row[:, None]).astype(I32), axis=1)
    eidx = jnp.arange(N_EXPERTS, dtype=I32)
    has_rows = pcounts > 0
    at_or_after = lax.cummin(jnp.where(has_rows, eidx, N_EXPERTS), axis=0, reverse=True)
    after = jnp.concatenate([at_or_after[1:], jnp.full((1,), N_EXPERTS, I32)])
    next_e = jnp.take(jnp.where(after < N_EXPERTS, after, -1), block_e)
    slot = jnp.take((jnp.cumsum(has_rows.astype(I32)) - 1) % 2, block_e).astype(I32)
    xg = _dispatch(hp, dest.reshape(-1), n_pad)
    out = _experts(xg, exp_w13, exp_w2, block_e, n_used, next_e.astype(I32), slot, layer)
    wk = wts.T
    step = t // MOE_PARTS
    res = None
    for part in range(MOE_PARTS):
        sl = slice(part * step, (part + 1) * step)
        y8 = jnp.take(out, dest[:, sl], axis=0, mode="clip")
        res = _moe_out(h, y8, wk[sl], shared_w13, shared_w2, lg, lb, alpha, part, MOE_PARTS, res)
    return res


def _pad_slots(w, scale=1.0):
    dm = w.shape[0]
    nh = w.shape[1] // HEAD_DIM
    w = (w * scale).reshape(dm, nh, HEAD_DIM)
    return jnp.concatenate([w, jnp.zeros_like(w)], axis=-1).reshape(dm, nh * SLOT)


def _alibi(n):
    return np.exp2(-8.0 * np.arange(1, n + 1, dtype=np.float64) / n).astype(np.float32)


def _key_features(seq):
    p = np.arange(seq)
    slc = np.zeros((seq, SLOT), np.float32)
    slc[:, HEAD_DIM] = p % SLC_BLOCK
    blk = p // SLC_BLOCK
    nz = blk > 0
    slc[p[nz], HEAD_DIM + blk[nz]] = 1.0
    win = np.zeros((seq, SLOT), np.float32)
    win[:, HEAD_DIM] = p // 64
    win[:, HEAD_DIM + 1] = p % 64
    mob = np.zeros((seq, SLOT), np.float32)
    mob[p, HEAD_DIM + p // MOBA_BLOCK] = 1.0
    mob[:, HEAD_DIM + 16] = p % MOBA_BLOCK
    return np.concatenate([slc] * 2 + [win] * 2 + [mob] * MOBA_HEADS, axis=1)


def _query_features():
    sl = _alibi(NSA_HEADS)
    nsa = np.zeros((NSA_KV_HEADS, SLOT, 8), np.float32)
    for g in range(NSA_KV_HEADS):
        for r in range(NSA_GROUP):
            s = sl[g * NSA_GROUP + r]
            nsa[g, HEAD_DIM, r] = s
            nsa[g, HEAD_DIM + 1:, r] = s * SLC_BLOCK * np.arange(1, 64)
            nsa[g, HEAD_DIM, NSA_GROUP + r] = s * 64
            nsa[g, HEAD_DIM + 1, NSA_GROUP + r] = s
    sm = _alibi(MOBA_HEADS)
    mob = np.zeros((MOBA_HEADS, SLOT, 8), np.float32)
    for h in range(MOBA_HEADS):
        mob[h, HEAD_DIM:HEAD_DIM + 16, 0] = sm[h] * MOBA_BLOCK * np.arange(16)
        mob[h, HEAD_DIM + 16, 0] = sm[h]
    return nsa, mob


def _mask_biases():
    t = np.arange(TQ)[None, :]
    p2 = np.arange(2 * TQ)[:, None]
    cb = np.stack([p2 <= t, p2 <= TQ + t])
    p3 = np.arange(3 * TQ)[:, None]
    dist = np.stack([c * TQ + t - p3 for c in range(3)])
    wb = (dist >= 0) & (dist < WINDOW)
    to_bias = lambda m: np.where(m, 0.0, NEG).astype(np.float32)
    return to_bias(cb), to_bias(wb)


def _overlap_t(seq):
    nc = seq // CMP_STRIDE
    c_start = np.arange(nc) * CMP_STRIDE
    b_start = np.arange(64) * SLC_BLOCK
    ov = ((c_start[None, :] < (b_start + SLC_BLOCK)[:, None])
          & ((c_start + CMP_BLOCK)[None, :] > b_start[:, None])
          & (np.arange(nc) < nc - 1)[None, :] & (b_start < seq)[:, None])
    return ov.astype(np.float32)


def _compress_weights(pe, w1, w2):
    ty = np.array([0, 0, 1, 1])
    eye = jnp.eye(4, dtype=F32)
    w1r = w1.reshape(2, CMP_BLOCK, HEAD_DIM, CMP_HIDDEN)[ty]
    top = jnp.einsum('spdj,sS->psdSj', w1r[:, :CMP_STRIDE], eye).reshape(CMP_STRIDE * 256, 4 * CMP_HIDDEN)
    bot = jnp.einsum('spdj,sS->psdSj', w1r[:, CMP_STRIDE:], eye).reshape(CMP_STRIDE * 256, 4 * CMP_HIDDEN)
    per = pe[ty]
    pet = jnp.transpose(per[:, :CMP_STRIDE], (1, 0, 2)).reshape(1, CMP_STRIDE * 256)
    peb = jnp.transpose(per[:, CMP_STRIDE:], (1, 0, 2)).reshape(1, CMP_STRIDE * 256)
    w2p = jnp.concatenate([w2[ty], jnp.zeros((4, CMP_HIDDEN, SLOT - HEAD_DIM), F32)], axis=-1)
    w2b = jnp.einsum('sjd,sS->sjSd', w2p, eye).reshape(4 * CMP_HIDDEN, 4 * SLOT)
    w2k = w2b[:, :2 * SLOT].astype(BF16)
    w2vt = w2b[:, 2 * SLOT:].T.astype(BF16)
    return pet, peb, top.astype(BF16), bot.astype(BF16), w2k, w2vt


def kernel(x, mem, w_in, cmp_pe, cmp_w1, cmp_w2, conv_w, w_branch, w_out, ln1_g, ln1_b,
           xattn_wq, xattn_wkv, xattn_wo, ln2_g, ln2_b, router_w, router_b, exp_w13, exp_w2,
           shared_w13, shared_w2, ln3_g, ln3_b):
    bsz, seq, d = x.shape
    mlen = mem.shape[1]
    depth = w_in.shape[0]
    alpha = (2.0 * depth) ** 0.25
    t = bsz * seq
    scale = HEAD_DIM ** -0.5

    kfeat = jnp.asarray(_key_features(seq))
    fq_nsa, fq_moba = (jnp.asarray(a) for a in _query_features())
    cbias, wbias = (jnp.asarray(a) for a in _mask_biases())
    ovt = jnp.asarray(_overlap_t(seq)).astype(BF16)
    memf = mem.reshape(bsz * mlen, d)

    h = x.reshape(t, d)
    hb = h.astype(BF16)
    for l in range(depth):
        ht = hb.T
        wi = w_in[l]
        kv6 = wi[:, _OFF_NSA_KV:_OFF_NSA_G].reshape(d, 6, NSA_KV_HEADS * HEAD_DIM)
        mq, mk, mv = (wi[:, _OFF_MOBA + j * 512:_OFF_MOBA + (j + 1) * 512] for j in range(3))
        w_at = jnp.concatenate([wi[:, :512] * scale, kv6[:, 3], kv6[:, 5], mq * scale, mv],
                               axis=1).T.astype(BF16)
        w_b = jnp.concatenate([_pad_slots(kv6[:, 2]), _pad_slots(kv6[:, 4]), _pad_slots(mk)], axis=1).astype(BF16)
        wg = wi[:, _OFF_NSA_G:_OFF_MOBA].reshape(d, NSA_KV_HEADS, 12)
        wgt = jnp.concatenate([wg, jnp.zeros((d, NSA_KV_HEADS, SLOT - 12), F32)], axis=-1)
        wgt = wgt.reshape(d, NSA_KV_HEADS * SLOT).T.astype(BF16)
        w_c = jnp.concatenate([wi[:, _OFF_NSA_KV:_OFF_NSA_KV + 256], jnp.zeros((d, 256), F32),
                               wi[:, _OFF_CONV:]], axis=1).astype(BF16)

        value_slots = tuple(range(_SLOT_SLC_V, _SLOT_MOBA_Q)) + tuple(range(_SLOT_MOBA_V, _N_SLOTS_T))
        zat = _proj_t(w_at, ht, value_slots)
        gt = _mm(wgt, ht, F32, tm=256, tn=1024, name="proj_gate")
        zb = _mm(hb, w_b, BF16, tn=768, feats=kfeat, name="proj_b")
        zc = _mm(hb, w_c, BF16, tn=1024, name="proj_c")

        pet, peb, wt, wb, w2k, w2vt = _compress_weights(cmp_pe[l], cmp_w1[l], cmp_w2[l])
        sub = zc[:, :256].reshape(bsz, seq // CMP_STRIDE, CMP_STRIDE * 256)
        kc, vct = _compress(sub, pet, peb, wt, wb, w2k, w2vt)

        ya = _nsa_attention(zat, zb, gt, kc, vct, ovt, fq_nsa, cbias, wbias, bsz, seq)
        yb = _moba_attention(zat, zb, fq_moba, cbias, bsz, seq)

        cw = jnp.concatenate([conv_w[l], jnp.zeros((5, CONV_CH), F32)], axis=0)
        h = _merge(ya, yb, zc, h, cw, w_branch[l, 0].astype(BF16),
                   w_branch[l, 1].astype(BF16), w_branch[l, 2].astype(BF16),
                   w_out[l].astype(BF16), ln1_g[l][None], ln1_b[l][None], seq, alpha)

        kv = _mm(memf, xattn_wkv[l].astype(BF16), BF16, tm=512, name="xattn_kv")
        h, hp = _xattn(h, kv, xattn_wq[l].astype(BF16), xattn_wo[l].astype(BF16),
                       ln2_g[l][None], ln2_b[l][None], seq, mlen, alpha)

        h, hb = _moe(h, hp, router_w[l].astype(BF16), router_b[l], exp_w13, exp_w2, l,
                     shared_w13[l].astype(BF16), shared_w2[l].astype(BF16), ln3_g[l][None], ln3_b[l][None], alpha)
    return h.reshape(bsz, seq, d)
```
